```python
import jax, jax.numpy as jnp
from jax import lax
import numpy as np

D_MODEL = 1024
BATCH = 4
SEQ = 8192
DEPTH = 2

F32 = jnp.float32
CHUNK = 64
MIX_HALF = D_MODEL // 2
A_HEADS = 4
A_DK = MIX_HALF // A_HEADS
A_DV = MIX_HALF // A_HEADS
B_HEADS = 4
B_DK = MIX_HALF // (2 * B_HEADS)
B_DV = MIX_HALF // B_HEADS
GLA_LOWRANK = 16
GLA_GATE_NORMALIZER = 16.0
C_HEADS = 4
C_DK = MIX_HALF // (2 * C_HEADS)
C_DV = MIX_HALF // C_HEADS
ROPE_THETA = 10000.0
D_HEADS = 4
D_DK = MIX_HALF // D_HEADS
D_DV = MIX_HALF // D_HEADS
CONV_WIDTH = 3
N_EXPERTS = 16
N_GROUPS = 4
EXPERTS_PER_GROUP = N_EXPERTS // N_GROUPS
TOP_K = 2
D_EXPERT = D_MODEL
DISPATCH_BLOCK = 128
ALPHA = (2.0 * DEPTH) ** 0.25
BETA = (8.0 * DEPTH) ** -0.25
N_AB = (DEPTH + 1) // 2
N_CD = DEPTH // 2
LN_EPS = 1e-5
NORM_EPS = 1e-6

AB_SPLITS = (A_HEADS * A_DK, A_HEADS * A_DK, A_HEADS * A_DK, A_HEADS * A_DV, A_HEADS * A_DV,
             B_HEADS * B_DK, B_HEADS * B_DK, B_HEADS * B_DV, B_HEADS * B_DV, GLA_LOWRANK, GLA_LOWRANK)
AB_VALUE_COLS = (3, 7)
CD_SPLITS = (C_HEADS * C_DK, C_HEADS * C_DK, C_HEADS * C_DV, C_HEADS * C_DV,
             2 * D_HEADS * D_DK, D_HEADS * D_DV, D_HEADS * D_DV, 4 * D_HEADS)
CD_VALUE_COLS = (2, 5)
P_AB = sum(AB_SPLITS)
P_CD = sum(CD_SPLITS)
MIX_AB = A_HEADS * A_DV + B_HEADS * B_DV
MIX_CD = C_HEADS * C_DV + D_HEADS * D_DV

kernel_name = 'bidir_hybrid_hgrn2_gla_retnet_mlstm_moe'


def _split(p, sizes):
    return jnp.split(p, [int(c) for c in np.cumsum(sizes)[:-1]], axis=-1)


def _heads(a, n_heads):
    b, t, w = a.shape
    return a.reshape(b, t, n_heads, w // n_heads).transpose(0, 2, 1, 3)


def _merge_heads(a):
    b, h, t, d = a.shape
    return a.transpose(0, 2, 1, 3).reshape(b, t, h * d)


def _flip(a):
    return jnp.flip(a, axis=2)


def _chunk(a):
    b, h, t = a.shape[:3]
    return jnp.moveaxis(a.reshape(b, h, t // CHUNK, CHUNK, *a.shape[3:]), 2, 0)


def _unchunk(a):
    n, b, h, l = a.shape[:4]
    return jnp.moveaxis(a, 0, 2).reshape(b, h, n * l, *a.shape[4:])


def _lower_mask():
    return jnp.tril(jnp.ones((CHUNK, CHUNK), dtype=bool))


def _head_norm(o, gain):
    o = o * lax.rsqrt(jnp.mean(o * o, axis=-1, keepdims=True) + NORM_EPS)
    return _merge_heads(o) * gain.astype(F32)


def _layer_norm(x, gain, bias):
    xf = x.astype(F32)
    xc = xf - jnp.mean(xf, axis=-1, keepdims=True)
    var = jnp.mean(xc * xc, axis=-1, keepdims=True)
    return (xc * lax.rsqrt(var + LN_EPS) * gain.astype(F32) + bias.astype(F32)).astype(x.dtype)


def _rotary(a):
    t, d = a.shape[2], a.shape[3]
    half = d // 2
    freqs = ROPE_THETA ** (-jnp.arange(half, dtype=F32) / half)
    ang = jnp.arange(t, dtype=F32)[:, None] * freqs[None, :]
    cos, sin = jnp.cos(ang), jnp.sin(ang)
    a1, a2 = a[..., :half], a[..., half:]
    return jnp.concatenate([a1 * cos - a2 * sin, a1 * sin + a2 * cos], axis=-1)


def _conv_centred(a, w, b):
    c = a.shape[-1]
    out = lax.conv_general_dilated(a, w.astype(a.dtype)[:, None, :], window_strides=(1,), padding='SAME',
                                   dimension_numbers=('NWC', 'WIO', 'NWC'), feature_group_count=c)
    return out + b.astype(a.dtype)


def _gated_chunk_scan(q, k, v, log_f):
    b, h, _, dk = q.shape
    dv = v.shape[-1]
    qc, kc, vc, gc = _chunk(q), _chunk(k), _chunk(v), _chunk(log_f)
    g_cum = jnp.cumsum(gc, axis=-2)
    g_ref = g_cum[..., CHUNK // 2 - 1:CHUNK // 2, :]
    scores = jnp.einsum('nbhid,nbhjd->nbhij', qc * jnp.exp(g_cum - g_ref), kc * jnp.exp(g_ref - g_cum))
    scores = jnp.where(_lower_mask(), scores, 0.0)
    o_intra = jnp.einsum('nbhij,nbhjv->nbhiv', scores, vc)
    q_inter = qc * jnp.exp(g_cum)
    k_state = kc * jnp.exp(g_cum[..., -1:, :] - g_cum)
    chunk_decay = jnp.exp(g_cum[..., -1, :])

    def step(state, xs):
        qn, kn, vn, dn = xs
        out = jnp.einsum('bhid,bhdv->bhiv', qn, state)
        state = dn[..., None] * state + jnp.einsum('bhjd,bhjv->bhdv', kn, vn)
        return state, out

    state0 = jnp.zeros((b, h, dk, dv), F32)
    _, o_inter = lax.scan(step, state0, (q_inter, k_state, vc, chunk_decay))
    return _unchunk(o_intra + o_inter)


def _retention_log_decay(direction):
    exps = 5.0 + 2.0 * jnp.arange(C_HEADS, dtype=F32) + direction
    return jnp.log1p(-jnp.exp2(-exps))


def _retention_chunk_scan(q, k, v, log_gamma):
    b, h, _, dk = q.shape
    dv = v.shape[-1]
    qc, kc, vc = _chunk(q), _chunk(k), _chunk(v)
    pos = jnp.arange(CHUNK, dtype=F32)
    lg = log_gamma[:, None, None]
    dist = pos[:, None] - pos[None, :]
    decay_intra = jnp.where(_lower_mask(), jnp.exp(jnp.maximum(dist, 0.0)[None] * lg), 0.0)
    scores = jnp.einsum('nbhid,nbhjd->nbhij', qc, kc) * decay_intra
    o_intra = jnp.einsum('nbhij,nbhjv->nbhiv', scores, vc)
    q_inter = qc * jnp.exp((pos + 1.0)[None, :, None] * lg)
    k_state = kc * jnp.exp((CHUNK - 1.0 - pos)[None, :, None] * lg)
    chunk_decay = jnp.exp(CHUNK * log_gamma)[:, None, None]

    def step(state, xs):
        qn, kn, vn = xs
        out = jnp.einsum('bhid,bhdv->bhiv', qn, state)
        state = chunk_decay * state + jnp.einsum('bhjd,bhjv->bhdv', kn, vn)
        return state, out

    state0 = jnp.zeros((b, h, dk, dv), F32)
    _, o_inter = lax.scan(step, state0, (q_inter, k_state, vc))
    return _unchunk(o_intra + o_inter)


def _mlstm_chunk_scan(q, k, v, log_i, log_f):
    b, h, _, dk = q.shape
    dv = v.shape[-1]
    qc, kc, vc = _chunk(q), _chunk(k), _chunk(v)
    ic, fc = _chunk(log_i), _chunk(log_f)
    b_cum = jnp.cumsum(fc, axis=-1)
    log_d = jnp.where(_lower_mask(), b_cum[..., :, None] - b_cum[..., None, :] + ic[..., None, :], -jnp.inf)
    m_intra = jnp.max(log_d, axis=-1)
    qk = jnp.einsum('nbhid,nbhjd->nbhij', qc, kc)
    log_w = b_cum[..., -1:] - b_cum + ic

    def step(carry, xs):
        c_st, n_st, m_st = carry
        qn, kn, vn, bn, ldn, min_, qkn, lwn = xs
        a = bn + m_st[..., None]
        m_i = jnp.maximum(a, min_)
        w = qkn * jnp.exp(ldn - m_i[..., None])
        s_inter = jnp.exp(a - m_i)
        num = s_inter[..., None] * jnp.einsum('bhid,bhdv->bhiv', qn, c_st) + jnp.einsum('bhij,bhjv->bhiv', w, vn)
        den = s_inter * jnp.einsum('bhid,bhd->bhi', qn, n_st) + jnp.sum(w, axis=-1)
        h_out = num / jnp.maximum(jnp.abs(den), jnp.exp(-m_i))[..., None]
        m_new = jnp.maximum(bn[..., -1] + m_st, jnp.max(lwn, axis=-1))
        s_old = jnp.exp(bn[..., -1] + m_st - m_new)
        wk = kn * jnp.exp(lwn - m_new[..., None])[..., None]
        c_st = s_old[..., None, None] * c_st + jnp.einsum('bhjd,bhjv->bhdv', wk, vn)
        n_st = s_old[..., None] * n_st + jnp.sum(wk, axis=-2)
        return (c_st, n_st, m_new), h_out

    carry0 = (jnp.zeros((b, h, dk, dv), F32), jnp.zeros((b, h, dk), F32), jnp.zeros((b, h), F32))
    _, h_all = lax.scan(step, carry0, (qc, kc, vc, b_cum, log_d, m_intra, qk, log_w))
    return _unchunk(h_all)


def _mixer_ab(x, w_in, b_in, lb, gk_up, gk_b, norm_a, norm_b, w_out):
    p = (x @ w_in + b_in).astype(F32)
    qa_r, fa_fwd, fa_bwd, ia_r, ga_r, qb_r, kb_r, vb_r, gb_r, rb_fwd, rb_bwd = _split(p, AB_SPLITS)
    q_a = _heads(jax.nn.silu(qa_r), A_HEADS)
    v_a = _heads(ia_r, A_HEADS)

    def hgrn_gates(f_logit):
        f = lb + (1.0 - lb) * jax.nn.sigmoid(f_logit)
        return _heads(1.0 - f, A_HEADS), _heads(jnp.log(f), A_HEADS)

    k_af, lf_af = hgrn_gates(fa_fwd)
    k_ab, lf_ab = hgrn_gates(fa_bwd)
    o_a = (_gated_chunk_scan(q_a, k_af, v_a, lf_af)
           + _flip(_gated_chunk_scan(_flip(q_a), _flip(k_ab), _flip(v_a), _flip(lf_ab))))
    o_a = _head_norm(o_a, norm_a) * jax.nn.silu(ga_r)
    q_b = _heads(qb_r, B_HEADS) * (B_DK ** -0.5)
    k_b = _heads(kb_r, B_HEADS)
    v_b = _heads(vb_r, B_HEADS)
    lf_bf = _heads(jax.nn.log_sigmoid(rb_fwd @ gk_up[0] + gk_b[0]) / GLA_GATE_NORMALIZER, B_HEADS)
    lf_bb = _heads(jax.nn.log_sigmoid(rb_bwd @ gk_up[1] + gk_b[1]) / GLA_GATE_NORMALIZER, B_HEADS)
    o_b = (_gated_chunk_scan(q_b, k_b, v_b, lf_bf)
           + _flip(_gated_chunk_scan(_flip(q_b), _flip(k_b), _flip(v_b), _flip(lf_bb))))
    o_b = _head_norm(o_b, norm_b) * jax.nn.silu(gb_r)
    mixed = jnp.concatenate([o_a, o_b], axis=-1).astype(x.dtype)
    return mixed @ w_out


def _mixer_cd(x, w_in, b_in, conv_w, conv_b, fgate_b, norm_c, norm_d, w_out):
    p = (x @ w_in + b_in).astype(F32)
    qc_r, kc_r, vc_r, gc_r, qkd_r, vd_r, od_r, gates = _split(p, CD_SPLITS)
    q_c = _rotary(_heads(qc_r, C_HEADS)) * (C_DK ** -0.5)
    k_c = _rotary(_heads(kc_r, C_HEADS))
    v_c = _heads(vc_r, C_HEADS)
    o_c = (_retention_chunk_scan(q_c, k_c, v_c, _retention_log_decay(0.0))
           + _flip(_retention_chunk_scan(_flip(q_c), _flip(k_c), _flip(v_c), _retention_log_decay(1.0))))
    o_c = _head_norm(o_c, norm_c) * jax.nn.silu(gc_r)
    qk_d = jax.nn.silu(_conv_centred(qkd_r, conv_w, conv_b))
    qd_r, kd_r = jnp.split(qk_d, 2, axis=-1)
    q_d = _heads(qd_r, D_HEADS) * (D_DK ** -0.5)
    k_d = _heads(kd_r, D_HEADS)
    v_d = _heads(vd_r, D_HEADS)
    gates = gates.transpose(0, 2, 1)
    li_f, lf_f_raw, li_b, lf_b_raw = jnp.split(gates, 4, axis=1)
    lf_f = jax.nn.log_sigmoid(lf_f_raw + fgate_b[0][:, None])
    lf_b = jax.nn.log_sigmoid(lf_b_raw + fgate_b[1][:, None])
    h_d = (_mlstm_chunk_scan(q_d, k_d, v_d, li_f, lf_f)
           + _flip(_mlstm_chunk_scan(_flip(q_d), _flip(k_d), _flip(v_d), _flip(li_b), _flip(lf_b))))
    o_d = _head_norm(h_d, norm_d) * jax.nn.sigmoid(od_r)
    mixed = jnp.concatenate([o_c, o_d], axis=-1).astype(x.dtype)
    return mixed @ w_out


def _route(xt, router_w, router_b):
    scores = jax.nn.softmax((xt @ router_w).astype(F32), axis=-1)
    biased = scores + router_b.astype(F32)
    per_group = biased.reshape(-1, N_GROUPS, EXPERTS_PER_GROUP)
    group_score = jnp.sum(lax.top_k(per_group, TOP_K)[0], axis=-1)
    g_sel = jnp.argmax(group_score, axis=-1)
    in_group = jnp.take_along_axis(per_group, g_sel[:, None, None], axis=1)[:, 0]
    _, local = lax.top_k(in_group, TOP_K)
    expert_idx = g_sel[:, None] * EXPERTS_PER_GROUP + local
    sel = jnp.take_along_axis(scores, expert_idx, axis=1)
    return expert_idx, sel / jnp.sum(sel, axis=-1, keepdims=True)


def _moe(x, router_w, router_b, w1, w3, w2):
    bsz, t, d = x.shape
    xt = x.reshape(-1, d)
    n_tok = xt.shape[0]
    expert_idx, gates = _route(xt, router_w, router_b)
    n_assign = n_tok * TOP_K
    flat_e = expert_idx.reshape(-1)
    flat_tok = jnp.repeat(jnp.arange(n_tok, dtype=jnp.int32), TOP_K)
    flat_g = gates.reshape(-1)
    order = jnp.argsort(flat_e)
    e_sorted = flat_e[order]
    counts = jnp.bincount(flat_e, length=N_EXPERTS)
    padded = (counts + DISPATCH_BLOCK - 1) // DISPATCH_BLOCK * DISPATCH_BLOCK
    start_sorted = jnp.cumsum(counts) - counts
    pad_end = jnp.cumsum(padded)
    start_padded = pad_end - padded
    dest = start_padded[e_sorted] + (jnp.arange(n_assign) - start_sorted[e_sorted])
    n_rows = -(-n_assign // DISPATCH_BLOCK) * DISPATCH_BLOCK + N_EXPERTS * DISPATCH_BLOCK
    n_blocks = n_rows // DISPATCH_BLOCK
    buf_tok = jnp.full((n_rows,), n_tok, jnp.int32).at[dest].set(flat_tok[order])
    buf_gate = jnp.zeros((n_rows,), F32).at[dest].set(flat_g[order])
    block_expert = jnp.minimum(jnp.searchsorted(pad_end, jnp.arange(n_blocks) * DISPATCH_BLOCK, side='right'),
                               N_EXPERTS - 1)
    x_pad = jnp.concatenate([xt, jnp.zeros((1, d), xt.dtype)], axis=0)

    def expert_block(args):
        e, tok = args
        xb = x_pad[tok]
        hid = jax.nn.silu(xb @ w1[e]) * (xb @ w3[e])
        return hid @ w2[e]

    y_blocks = lax.map(expert_block, (block_expert, buf_tok.reshape(n_blocks, DISPATCH_BLOCK)))
    y = y_blocks.reshape(n_rows, d).astype(F32) * buf_gate[:, None]
    out = jax.ops.segment_sum(y, buf_tok, num_segments=n_tok + 1)[:n_tok]
    return out.reshape(bsz, t, d).astype(x.dtype)


def setup_inputs(seed: int = 0) -> dict:
    key = jax.random.key(seed)
    ks = jax.random.split(key, 32)

    def nrm(k, shape, scale):
        return jax.random.normal(k, shape, F32) * scale

    ab_scale = np.concatenate([np.full((s,), BETA if i in AB_VALUE_COLS else 1.0, np.float32)
                               for i, s in enumerate(AB_SPLITS)])
    cd_scale = np.concatenate([np.full((s,), BETA if i in CD_VALUE_COLS else 1.0, np.float32)
                               for i, s in enumerate(CD_SPLITS)])
    return {
        'x': nrm(ks[0], (BATCH, SEQ, D_MODEL), 1.0),
        'ab_w_in': nrm(ks[1], (N_AB, D_MODEL, P_AB), D_MODEL ** -0.5) * jnp.asarray(ab_scale),
        'ab_b_in': nrm(ks[2], (N_AB, P_AB), 0.02),
        'hgrn_lb': nrm(ks[3], (DEPTH + 1, A_HEADS * A_DK), 0.1),
        'gla_gk_up': nrm(ks[4], (N_AB, 2, GLA_LOWRANK, B_HEADS * B_DK), GLA_LOWRANK ** -0.5),
        'gla_gk_b': nrm(ks[5], (N_AB, 2, B_HEADS * B_DK), 0.1),
        'hgrn_norm': 1.0 + nrm(ks[6], (N_AB, A_HEADS * A_DV), 0.02),
        'gla_norm': 1.0 + nrm(ks[7], (N_AB, B_HEADS * B_DV), 0.02),
        'ab_w_out': nrm(ks[8], (N_AB, MIX_AB, D_MODEL), MIX_AB ** -0.5 * BETA),
        'cd_w_in': nrm(ks[9], (N_CD, D_MODEL, P_CD), D_MODEL ** -0.5) * jnp.asarray(cd_scale),
        'cd_b_in': nrm(ks[10], (N_CD, P_CD), 0.02),
        'mlstm_conv_w': nrm(ks[11], (N_CD, CONV_WIDTH, 2 * D_HEADS * D_DK), CONV_WIDTH ** -0.5),
        'mlstm_conv_b': nrm(ks[12], (N_CD, 2 * D_HEADS * D_DK), 0.02),
        'mlstm_fgate_b': jnp.linspace(3.0, 6.0, D_HEADS, dtype=F32)[None, None, :] + nrm(ks[13], (N_CD, 2, D_HEADS), 0.1),
        'ret_norm': 1.0 + nrm(ks[14], (N_CD, C_HEADS * C_DV), 0.02),
        'mlstm_norm': 1.0 + nrm(ks[15], (N_CD, D_HEADS * D_DV), 0.02),
        'cd_w_out': nrm(ks[16], (N_CD, MIX_CD, D_MODEL), MIX_CD ** -0.5 * BETA),
        'ln_mix_g': 1.0 + nrm(ks[17], (DEPTH, D_MODEL), 0.02),
        'ln_mix_b': nrm(ks[18], (DEPTH, D_MODEL), 0.02),
        'ln_ffn_g': 1.0 + nrm(ks[19], (DEPTH, D_MODEL), 0.02),
        'ln_ffn_b': nrm(ks[20], (DEPTH, D_MODEL), 0.02),
        'router_w': nrm(ks[21], (D_MODEL, N_EXPERTS), D_MODEL ** -0.5),
        'router_b': nrm(ks[22], (N_EXPERTS,), 0.01),
        'moe_w1': nrm(ks[23], (DEPTH, N_EXPERTS, D_MODEL, D_EXPERT), D_MODEL ** -0.5 * BETA),
        'moe_w3': nrm(ks[24], (DEPTH, N_EXPERTS, D_MODEL, D_EXPERT), D_MODEL ** -0.5 * BETA),
        'moe_w2': nrm(ks[25], (DEPTH, N_EXPERTS, D_EXPERT, D_MODEL), D_EXPERT ** -0.5 * BETA),
    }


def reference(x, ab_w_in, ab_b_in, hgrn_lb, gla_gk_up, gla_gk_b, hgrn_norm, gla_norm, ab_w_out,
              cd_w_in, cd_b_in, mlstm_conv_w, mlstm_conv_b, mlstm_fgate_b, ret_norm, mlstm_norm, cd_w_out,
              ln_mix_g, ln_mix_b, ln_ffn_g, ln_ffn_b, router_w, router_b, moe_w1, moe_w3, moe_w2):
    lower_bounds = jnp.cumsum(jax.nn.softmax(hgrn_lb.astype(F32), axis=0), axis=0)
    h = x
    for layer in range(DEPTH):
        j = layer // 2
        if layer % 2 == 0:
            y = _mixer_ab(h, ab_w_in[j], ab_b_in[j], lower_bounds[layer], gla_gk_up[j], gla_gk_b[j],
                          hgrn_norm[j], gla_norm[j], ab_w_out[j])
        else:
            y = _mixer_cd(h, cd_w_in[j], cd_b_in[j], mlstm_conv_w[j], mlstm_conv_b[j], mlstm_fgate_b[j],
                          ret_norm[j], mlstm_norm[j], cd_w_out[j])
        h = _layer_norm(ALPHA * h + y, ln_mix_g[layer], ln_mix_b[layer])
        y = _moe(h, router_w, router_b, moe_w1[layer], moe_w3[layer], moe_w2[layer])
        h = _layer_norm(ALPHA * h + y, ln_ffn_g[layer], ln_ffn_b[layer])
    return h
```

```python
import functools

import jax
import jax.numpy as jnp
import numpy as np
from jax import lax
from jax.experimental import pallas as pl
from jax.experimental.pallas import tpu as pltpu

F32 = jnp.float32
BF16 = jnp.bfloat16
HIGHEST = lax.Precision.HIGHEST

D_MODEL = 1024
DEPTH = 2
CHUNK = 64
MIX_HALF = D_MODEL // 2
N_HEADS = 4
HEAD_DV = MIX_HALF // N_HEADS
NARROW_DK = 64
GLA_LOWRANK = 16
GLA_GATE_NORMALIZER = 16.0
ROPE_THETA = 10000.0
N_EXPERTS = 16
N_GROUPS = 4
EXPERTS_PER_GROUP = N_EXPERTS // N_GROUPS
TOP_K = 2
ALPHA = (2.0 * DEPTH) ** 0.25
LN_EPS = 1e-5
NORM_EPS = 1e-6

LANES = 128
SUBLANES = 8
VMEM_LIMIT = 52 * 1024 * 1024

ROW_TILE = 256
SCAN_ROWS = 256
ROUTER_TILE = 1024
EXPERT_ROWS = 256

AB_QA, AB_FF, AB_FB, AB_IA, AB_GA, AB_QKB, AB_VB, AB_GB = range(8)
AB_LR_BLOCK = 32
AB_WIDTH = 8 * 512 + LANES
CD_QKC, CD_VC, CD_GC, CD_QD, CD_KD, CD_VD, CD_OD = range(7)
CD_GATE_BLOCK = 28
CD_WIDTH = 7 * 512 + LANES


def _params(n_axes, vmem=VMEM_LIMIT):
    return pltpu.CompilerParams(dimension_semantics=("arbitrary",) * n_axes, vmem_limit_bytes=vmem)


def _nt(a, b):
    return lax.dot_general(a, b, (((1,), (1,)), ((), ())), preferred_element_type=F32)


def _tn(a, b):
    return lax.dot_general(a, b, (((0,), (0,)), ((), ())), preferred_element_type=F32)


def _mm(a, b):
    return jnp.dot(a, b, preferred_element_type=F32)


def _sigmoid(x):
    return 1.0 / (1.0 + jnp.exp(-x))


def _silu(x):
    return x * _sigmoid(x)


def _log_sigmoid(x):
    return jnp.minimum(x, 0.0) - jnp.log(1.0 + jnp.exp(-jnp.abs(x)))


def _tri(direction):
    row = lax.broadcasted_iota(jnp.int32, (CHUNK, CHUNK), 0)
    col = lax.broadcasted_iota(jnp.int32, (CHUNK, CHUNK), 1)
    return (col <= row) if direction == 0 else (col >= row)


def _ref_row(direction):
    return CHUNK // 2 - 1 if direction == 0 else CHUNK // 2


def _last_row(direction):
    return CHUNK - 1 if direction == 0 else 0


def _proj_kernel(x_ref, w_ref, b_ref, o_ref):
    o_ref[...] = _mm(x_ref[...].astype(BF16), w_ref[...]) + b_ref[...]


def _in_proj(x2d, w, b):
    n, k = x2d.shape
    m = w.shape[1]
    return pl.pallas_call(
        _proj_kernel,
        out_shape=jax.ShapeDtypeStruct((n, m), F32),
        grid=(n // ROW_TILE,),
        in_specs=[pl.BlockSpec((ROW_TILE, k), lambda i: (i, 0)),
                  pl.BlockSpec((k, m), lambda i: (0, 0)),
                  pl.BlockSpec((1, m), lambda i: (0, 0))],
        out_specs=pl.BlockSpec((ROW_TILE, m), lambda i: (i, 0)),
        compiler_params=_params(1),
        name="in_proj",
    )(x2d, w, b)


def _scan_unit(qe, ke, qi, ks, v, dec, intra, st_ref, idx):
    s = intra(_nt(qe.astype(BF16), ke.astype(BF16)))
    st = st_ref[idx]
    vb = v.astype(BF16)
    o = _mm(s.astype(BF16), vb) + _nt(qi.astype(BF16), st.astype(BF16))
    st_ref[idx] = st * dec + _tn(vb, ks.astype(BF16))
    return o


def _gate_decays(lf, direction):
    tri = _tri(direction).astype(F32)
    gc = jnp.dot(tri, lf, precision=HIGHEST, preferred_element_type=F32)
    r, l = _ref_row(direction), _last_row(direction)
    g_ref = gc[r:r + 1, :]
    g_last = gc[l:l + 1, :]
    return (jnp.exp(gc - g_ref), jnp.exp(g_ref - gc), jnp.exp(gc), jnp.exp(g_last - gc), jnp.exp(g_last))


def _narrow_head_mask(h):
    lane = lax.broadcasted_iota(jnp.int32, (1, LANES), 1)
    return (lane // NARROW_DK) == (h % 2)


def _scan_ab_kernel(qa_f, fa_f, ia_f, qkb_f, vb_f, lr_f, qa_b, fa_b, ia_b, qkb_b, vb_b, lr_b,
                    lb_ref, u_ref, gkb_ref, oa_f, ob_f, oa_b, ob_b, st_ref):
    @pl.when(pl.program_id(1) == 0)
    def _():
        st_ref[...] = jnp.zeros_like(st_ref)

    n_chunks = SCAN_ROWS // CHUNK
    dirs = ((qa_f, fa_f, ia_f, qkb_f, vb_f, lr_f, oa_f, ob_f), (qa_b, fa_b, ia_b, qkb_b, vb_b, lr_b, oa_b, ob_b))

    def chunk_body(c, carry):
        for d, (qa, fa, ia, qkb, vb, lr, oa, ob) in enumerate(dirs):
            r0 = pl.multiple_of((c if d == 0 else n_chunks - 1 - c) * CHUNK, CHUNK)
            rows = pl.ds(r0, CHUNK)
            mask = _tri(d)
            intra = lambda s, mask=mask: jnp.where(mask, s, 0.0)
            lb = lb_ref[...]
            f = lb + (1.0 - lb) * _sigmoid(fa[rows, :])
            q = _silu(qa[rows, :])
            k = 1.0 - f
            v = ia[rows, :]
            eq, ek, ei, es, dec = _gate_decays(jnp.log(f), d)
            qe, ke, qi, ks = q * eq, k * ek, q * ei, k * es
            for h in range(N_HEADS):
                sl = slice(h * LANES, (h + 1) * LANES)
                oa[rows, sl] = _scan_unit(qe[:, sl], ke[:, sl], qi[:, sl], ks[:, sl], v[:, sl], dec[:, sl],
                                          intra, st_ref, d * 2 * N_HEADS + h)
            qk = qkb[rows, :]
            q = qk[:, :N_HEADS * NARROW_DK] * (NARROW_DK ** -0.5)
            k = qk[:, N_HEADS * NARROW_DK:]
            v = vb[rows, :]
            half = slice(d * N_HEADS * NARROW_DK, (d + 1) * N_HEADS * NARROW_DK)
            logits = _mm(lr[rows, :].astype(BF16), u_ref[:, half]) + gkb_ref[:, half]
            eq, ek, ei, es, dec = _gate_decays(_log_sigmoid(logits) / GLA_GATE_NORMALIZER, d)
            qe, ke, qi, ks = q * eq, k * ek, q * ei, k * es
            for h in range(N_HEADS):
                psl = slice((h // 2) * LANES, (h // 2 + 1) * LANES)
                hm = _narrow_head_mask(h)
                pick = lambda a, psl=psl, hm=hm: jnp.where(hm, a[:, psl], 0.0)
                ob[rows, h * HEAD_DV:(h + 1) * HEAD_DV] = _scan_unit(
                    pick(qe), pick(ke), pick(qi), pick(ks), v[:, h * HEAD_DV:(h + 1) * HEAD_DV], dec[:, psl],
                    intra, st_ref, d * 2 * N_HEADS + N_HEADS + h)
        return carry

    lax.fori_loop(0, n_chunks, chunk_body, 0)


def _scan_ab(p3, lb, u_pad, gkb):
    bsz, t, _ = p3.shape
    nb = t // SCAN_ROWS

    def fwd(j, w=512):
        return pl.BlockSpec((None, SCAN_ROWS, w), lambda b, n: (b, n, j))

    def bwd(j, w=512):
        return pl.BlockSpec((None, SCAN_ROWS, w), lambda b, n: (b, nb - 1 - n, j))

    def const(shape):
        return pl.BlockSpec(shape, lambda b, n: (0,) * len(shape))

    out = jax.ShapeDtypeStruct((bsz, t, MIX_HALF), F32)
    o_f = pl.BlockSpec((None, SCAN_ROWS, MIX_HALF), lambda b, n: (b, n, 0))
    o_b = pl.BlockSpec((None, SCAN_ROWS, MIX_HALF), lambda b, n: (b, nb - 1 - n, 0))
    return pl.pallas_call(
        _scan_ab_kernel,
        out_shape=(out, out, out, out),
        grid=(bsz, nb),
        in_specs=[fwd(AB_QA), fwd(AB_FF), fwd(AB_IA), fwd(AB_QKB), fwd(AB_VB), fwd(AB_LR_BLOCK, LANES),
                  bwd(AB_QA), bwd(AB_FB), bwd(AB_IA), bwd(AB_QKB), bwd(AB_VB), bwd(AB_LR_BLOCK, LANES),
                  const(lb.shape), const(u_pad.shape), const(gkb.shape)],
        out_specs=(o_f, o_f, o_b, o_b),
        scratch_shapes=[pltpu.VMEM((4 * N_HEADS, HEAD_DV, LANES), F32)],
        compiler_params=_params(2),
        name="scan_ab",
    )(*([p3] * 12), lb, u_pad, gkb)


def _rotary(a, cos, sin_signed):
    lane = lax.broadcasted_iota(jnp.int32, (1, LANES), 1)
    first_half = (lane % NARROW_DK) < (NARROW_DK // 2)
    swapped = jnp.where(first_half, pltpu.roll(a, LANES - NARROW_DK // 2, 1), pltpu.roll(a, NARROW_DK // 2, 1))
    return a * cos + swapped * sin_signed


def _scan_c_kernel(qk_f, v_f, cos_f, sin_f, qk_b, v_b, cos_b, sin_b, dmat_ref, qsc_ref, ksc_ref, cd_ref,
                   o_f, o_b, st_ref):
    @pl.when(pl.program_id(1) == 0)
    def _():
        st_ref[...] = jnp.zeros_like(st_ref)

    n_chunks = SCAN_ROWS // CHUNK
    dirs = ((qk_f, v_f, cos_f, sin_f, o_f), (qk_b, v_b, cos_b, sin_b, o_b))

    def chunk_body(c, carry):
        for d, (qk_ref, v_ref, cos_ref, sin_ref, o_ref) in enumerate(dirs):
            r0 = pl.multiple_of((c if d == 0 else n_chunks - 1 - c) * CHUNK, CHUNK)
            rows = pl.ds(r0, CHUNK)
            cos, sin_signed = cos_ref[rows, :], sin_ref[rows, :]
            v = v_ref[rows, :]
            for h in range(N_HEADS):
                p = h // 2
                hm = _narrow_head_mask(h)
                q = _rotary(qk_ref[rows, p * LANES:(p + 1) * LANES], cos, sin_signed) * (NARROW_DK ** -0.5)
                k = _rotary(qk_ref[rows, (2 + p) * LANES:(3 + p) * LANES], cos, sin_signed)
                q = jnp.where(hm, q, 0.0)
                k = jnp.where(hm, k, 0.0)
                dmat = dmat_ref[d, h]
                o_ref[rows, h * HEAD_DV:(h + 1) * HEAD_DV] = _scan_unit(
                    q, k, q * qsc_ref[d, h], k * ksc_ref[d, h], v[:, h * HEAD_DV:(h + 1) * HEAD_DV],
                    cd_ref[d, h][0:1, :], lambda s, dmat=dmat: s * dmat, st_ref, d * N_HEADS + h)
        return carry

    lax.fori_loop(0, n_chunks, chunk_body, 0)


def _retention_tables():
    pos = np.arange(CHUNK, dtype=np.float64)
    dmat = np.zeros((2, N_HEADS, CHUNK, CHUNK), np.float32)
    qsc = np.zeros((2, N_HEADS, CHUNK, LANES), np.float32)
    ksc = np.zeros((2, N_HEADS, CHUNK, LANES), np.float32)
    cd = np.zeros((2, N_HEADS, SUBLANES, LANES), np.float32)
    for d in range(2):
        for h in range(N_HEADS):
            lg = np.log1p(-np.exp2(-(5.0 + 2.0 * h + d)))
            dist = pos[:, None] - pos[None, :]
            if d == 0:
                dmat[d, h] = np.where(dist >= 0, np.exp(np.maximum(dist, 0.0) * lg), 0.0)
                qsc[d, h] = np.exp((pos + 1.0) * lg)[:, None]
                ksc[d, h] = np.exp((CHUNK - 1.0 - pos) * lg)[:, None]
            else:
                dmat[d, h] = np.where(dist <= 0, np.exp(np.maximum(-dist, 0.0) * lg), 0.0)
                qsc[d, h] = np.exp((CHUNK - pos) * lg)[:, None]
                ksc[d, h] = np.exp(pos * lg)[:, None]
            cd[d, h] = np.exp(CHUNK * lg)
    return jnp.asarray(dmat), jnp.asarray(qsc), jnp.asarray(ksc), jnp.asarray(cd)


def _rotary_tables(t):
    half = NARROW_DK // 2
    freqs = ROPE_THETA ** (-jnp.arange(half, dtype=F32) / half)
    ang = jnp.arange(t, dtype=F32)[:, None] * freqs[None, :]
    cos, sin = jnp.cos(ang), jnp.sin(ang)
    cos_t = jnp.tile(cos, (1, LANES // half))
    sin_t = jnp.tile(jnp.concatenate([-sin, sin], axis=1), (1, LANES // NARROW_DK))
    return cos_t.astype(F32), sin_t.astype(F32)


def _scan_c(p3):
    bsz, t, _ = p3.shape
    nb = t // SCAN_ROWS
    cos_t, sin_t = _rotary_tables(t)
    tables = _retention_tables()

    def fwd(j):
        return pl.BlockSpec((None, SCAN_ROWS, 512), lambda b, n: (b, n, j))

    def bwd(j):
        return pl.BlockSpec((None, SCAN_ROWS, 512), lambda b, n: (b, nb - 1 - n, j))

    tab_f = pl.BlockSpec((SCAN_ROWS, LANES), lambda b, n: (n, 0))
    tab_b = pl.BlockSpec((SCAN_ROWS, LANES), lambda b, n: (nb - 1 - n, 0))

    def const(shape):
        return pl.BlockSpec(shape, lambda b, n: (0,) * len(shape))

    out = jax.ShapeDtypeStruct((bsz, t, MIX_HALF), F32)
    return pl.pallas_call(
        _scan_c_kernel,
        out_shape=(out, out),
        grid=(bsz, nb),
        in_specs=[fwd(CD_QKC), fwd(CD_VC), tab_f, tab_f, bwd(CD_QKC), bwd(CD_VC), tab_b, tab_b]
        + [const(a.shape) for a in tables],
        out_specs=(pl.BlockSpec((None, SCAN_ROWS, MIX_HALF), lambda b, n: (b, n, 0)),
                   pl.BlockSpec((None, SCAN_ROWS, MIX_HALF), lambda b, n: (b, nb - 1 - n, 0))),
        scratch_shapes=[pltpu.VMEM((2 * N_HEADS, HEAD_DV, LANES), F32)],
        compiler_params=_params(2),
        name="scan_c",
    )(p3, p3, cos_t, sin_t, p3, p3, cos_t, sin_t, *tables)


def _conv_kernel(prev_ref, cur_ref, next_ref, w_ref, b_ref, o_ref, buf_ref):
    i = pl.program_id(1)
    last = pl.num_programs(1) - 1
    rows = cur_ref.shape[0]
    buf_ref[0:SUBLANES, :] = jnp.where(i == 0, 0.0, prev_ref[...])
    buf_ref[SUBLANES:SUBLANES + rows, :] = cur_ref[...]
    buf_ref[SUBLANES + rows:2 * SUBLANES + rows, :] = jnp.where(i == last, 0.0, next_ref[...])
    y = (buf_ref[SUBLANES - 1:SUBLANES - 1 + rows, :] * w_ref[0:1, :]
         + buf_ref[SUBLANES:SUBLANES + rows, :] * w_ref[1:2, :]
         + buf_ref[SUBLANES + 1:SUBLANES + 1 + rows, :] * w_ref[2:3, :]
         + b_ref[...])
    o_ref[...] = _silu(y)


def _conv_silu(p3, conv_w, conv_b):
    bsz, t, _ = p3.shape
    nb = t // ROW_TILE
    per = ROW_TILE // SUBLANES
    outs = []
    for j in (CD_QD, CD_KD):
        lo = (j - CD_QD) * 512
        outs.append(pl.pallas_call(
            _conv_kernel,
            out_shape=jax.ShapeDtypeStruct((bsz, t, 512), F32),
            grid=(bsz, nb),
            in_specs=[pl.BlockSpec((None, SUBLANES, 512), lambda b, i, j=j: (b, jnp.maximum(i * per - 1, 0), j)),
                      pl.BlockSpec((None, ROW_TILE, 512), lambda b, i, j=j: (b, i, j)),
                      pl.BlockSpec((None, SUBLANES, 512),
                                   lambda b, i, j=j: (b, jnp.minimum((i + 1) * per, t // SUBLANES - 1), j)),
                      pl.BlockSpec((3, 512), lambda b, i: (0, 0)),
                      pl.BlockSpec((1, 512), lambda b, i: (0, 0))],
            out_specs=pl.BlockSpec((None, ROW_TILE, 512), lambda b, i: (b, i, 0)),
            scratch_shapes=[pltpu.VMEM((ROW_TILE + 2 * SUBLANES, 512), F32)],
            compiler_params=_params(2),
            name="conv_silu",
        )(p3, p3, p3, conv_w[:, lo:lo + 512], conv_b[None, lo:lo + 512]))
    return outs


def _scan_d_kernel(q_f, k_f, v_f, g_f, q_b, k_b, v_b, g_b, fgb_ref, o_f, o_b, ct_ref, m_ref):
    @pl.when(pl.program_id(1) == 0)
    def _():
        ct_ref[...] = jnp.zeros_like(ct_ref)
        m_ref[...] = jnp.zeros_like(m_ref)

    n_chunks = SCAN_ROWS // CHUNK
    dirs = ((q_f, k_f, v_f, g_f, o_f), (q_b, k_b, v_b, g_b, o_b))
    lane = lax.broadcasted_iota(jnp.int32, (CHUNK, LANES), 1)
    ones_col = jnp.where(lane == 0, 1.0, 0.0).astype(BF16)

    def chunk_body(c, carry):
        for d, (q_ref, k_ref, v_ref, g_ref, o_ref) in enumerate(dirs):
            r0 = pl.multiple_of((c if d == 0 else n_chunks - 1 - c) * CHUNK, CHUNK)
            rows = pl.ds(r0, CHUNK)
            mask = _tri(d)
            last = _last_row(d)
            g = g_ref[rows, :]
            lf = _log_sigmoid(g + fgb_ref[...])
            bc = jnp.dot(mask.astype(F32), lf, precision=HIGHEST, preferred_element_type=F32)
            g_t = g.T
            bc_t = bc.T
            for h in range(N_HEADS):
                ci = d * 2 * N_HEADS + h
                cf = ci + N_HEADS
                sl = slice(h * HEAD_DV, (h + 1) * HEAD_DV)
                i_col, b_col = g[:, ci:ci + 1], bc[:, cf:cf + 1]
                i_row, b_row = g_t[ci:ci + 1, :], bc_t[cf:cf + 1, :]
                b_last = b_col[last:last + 1, :]
                m_st = m_ref[d * N_HEADS + h][0:1, 0:1]
                log_d = jnp.where(mask, b_col - b_row + i_row, -jnp.inf)
                m_intra = jnp.max(log_d, axis=-1, keepdims=True)
                a = b_col + m_st
                m_i = jnp.maximum(a, m_intra)
                q = (q_ref[rows, sl] * (HEAD_DV ** -0.5)).astype(BF16)
                k = k_ref[rows, sl]
                v_ext = jnp.concatenate([v_ref[rows, sl].astype(BF16), ones_col], axis=1)
                w = _nt(q, k.astype(BF16)) * jnp.exp(log_d - m_i)
                ct = ct_ref[d * N_HEADS + h]
                num = jnp.exp(a - m_i) * _nt(q, ct.astype(BF16)) + _mm(w.astype(BF16), v_ext)
                den = num[:, HEAD_DV:HEAD_DV + 1]
                o_ref[rows, sl] = num[:, :HEAD_DV] / jnp.maximum(jnp.abs(den), jnp.exp(-m_i))
                lw = b_last - b_col + i_col
                m_new = jnp.maximum(b_last + m_st, jnp.max(lw, axis=0, keepdims=True))
                s_old = jnp.exp(b_last + m_st - m_new)
                wk = (k * jnp.exp(lw - m_new)).astype(BF16)
                ct_ref[d * N_HEADS + h] = s_old * ct + _tn(v_ext, wk)
                m_ref[d * N_HEADS + h] = jnp.broadcast_to(m_new, (SUBLANES, LANES))
        return carry

    lax.fori_loop(0, n_chunks, chunk_body, 0)


def _scan_d(qd, kd, p3, fgb_row):
    bsz, t, _ = p3.shape
    nb = t // SCAN_ROWS

    def spec(j, w, flip):
        if flip:
            return pl.BlockSpec((None, SCAN_ROWS, w), lambda b, n: (b, nb - 1 - n, j))
        return pl.BlockSpec((None, SCAN_ROWS, w), lambda b, n: (b, n, j))

    out = jax.ShapeDtypeStruct((bsz, t, MIX_HALF), F32)
    in_specs = []
    for flip in (False, True):
        in_specs += [spec(0, 512, flip), spec(0, 512, flip), spec(CD_VD, 512, flip), spec(CD_GATE_BLOCK, LANES, flip)]
    in_specs.append(pl.BlockSpec((1, LANES), lambda b, n: (0, 0)))
    return pl.pallas_call(
        _scan_d_kernel,
        out_shape=(out, out),
        grid=(bsz, nb),
        in_specs=in_specs,
        out_specs=(spec(0, MIX_HALF, False), spec(0, MIX_HALF, True)),
        scratch_shapes=[pltpu.VMEM((2 * N_HEADS, 2 * HEAD_DV, LANES), F32),
                        pltpu.VMEM((2 * N_HEADS, SUBLANES, LANES), F32)],
        compiler_params=_params(2),
        name="scan_d",
    )(qd, kd, p3, p3, qd, kd, p3, p3, fgb_row)


def _layer_norm(z, g, b):
    zc = z - jnp.mean(z, axis=-1, keepdims=True)
    var = jnp.mean(zc * zc, axis=-1, keepdims=True)
    return zc * lax.rsqrt(var + LN_EPS) * g + b


def _mix_out_kernel(o1f, o1b, o2f, o2b, g1, g2, x_ref, norm_ref, w_ref, lng, lnb, h_ref, hb_ref, *, second_gate):
    parts = []
    for of, ob, g_ref, gate_fn, base in ((o1f, o1b, g1, _silu, 0), (o2f, o2b, g2, second_gate, MIX_HALF)):
        o = of[...] + ob[...]
        gate = gate_fn(g_ref[...])
        for h in range(N_HEADS):
            sl = slice(h * HEAD_DV, (h + 1) * HEAD_DV)
            oh = o[:, sl]
            oh = oh * lax.rsqrt(jnp.mean(oh * oh, axis=-1, keepdims=True) + NORM_EPS)
            parts.append((oh * norm_ref[:, base + h * HEAD_DV:base + (h + 1) * HEAD_DV] * gate[:, sl]).astype(BF16))
    mixed = jnp.concatenate(parts, axis=1)
    z = ALPHA * x_ref[...] + _mm(mixed, w_ref[...])
    hn = _layer_norm(z, lng[...], lnb[...])
    h_ref[...] = hn
    hb_ref[...] = hn.astype(BF16)


def _mix_out(o1f, o1b, o2f, o2b, p2d, gate_blocks, x2d, norm, w_out, ln_g, ln_b, second_gate):
    n = x2d.shape[0]
    half = pl.BlockSpec((ROW_TILE, MIX_HALF), lambda i: (i, 0))
    full = pl.BlockSpec((ROW_TILE, D_MODEL), lambda i: (i, 0))
    row = pl.BlockSpec((1, D_MODEL), lambda i: (0, 0))
    ga, gb = gate_blocks
    return pl.pallas_call(
        functools.partial(_mix_out_kernel, second_gate=second_gate),
        out_shape=(jax.ShapeDtypeStruct((n, D_MODEL), F32), jax.ShapeDtypeStruct((n, D_MODEL), BF16)),
        grid=(n // ROW_TILE,),
        in_specs=[half, half, half, half,
                  pl.BlockSpec((ROW_TILE, MIX_HALF), lambda i: (i, ga)),
                  pl.BlockSpec((ROW_TILE, MIX_HALF), lambda i: (i, gb)),
                  full, row, pl.BlockSpec((D_MODEL, D_MODEL), lambda i: (0, 0)), row, row],
        out_specs=(full, full),
        compiler_params=_params(1),
        name="mix_out",
    )(o1f, o1b, o2f, o2b, p2d, p2d, x2d, norm, w_out, ln_g, ln_b)


def _router_kernel(h_ref, rw_ref, rb_ref, idx_ref, gate_ref):
    logits = lax.dot_general(rw_ref[...], h_ref[...], (((1,), (1,)), ((), ())),
                             precision=HIGHEST, preferred_element_type=F32)
    rows = [logits[e:e + 1, :] for e in range(N_EXPERTS)]
    mx = functools.reduce(jnp.maximum, rows)
    ex = [jnp.exp(r - mx) for r in rows]
    tot = functools.reduce(lambda a, b: a + b, ex)
    score = [e / tot for e in ex]
    biased = [score[e] + rb_ref[e:e + 1, :] for e in range(N_EXPERTS)]

    def argmax_first(vals):
        best, idx = vals[0], jnp.zeros(vals[0].shape, jnp.int32)
        for j in range(1, len(vals)):
            upd = vals[j] > best
            best = jnp.where(upd, vals[j], best)
            idx = jnp.where(upd, j, idx)
        return best, idx

    def pick(vals, idx):
        out = vals[0]
        for j in range(1, len(vals)):
            out = jnp.where(idx == j, vals[j], out)
        return out

    group_scores = []
    for g in range(N_GROUPS):
        a = biased[g * EXPERTS_PER_GROUP:(g + 1) * EXPERTS_PER_GROUP]
        pairs = [a[i] + a[j] for i in range(EXPERTS_PER_GROUP) for j in range(i + 1, EXPERTS_PER_GROUP)]
        group_scores.append(functools.reduce(jnp.maximum, pairs))
    _, g_sel = argmax_first(group_scores)
    in_b = [pick([biased[g * EXPERTS_PER_GROUP + k] for g in range(N_GROUPS)], g_sel) for k in range(EXPERTS_PER_GROUP)]
    in_s = [pick([score[g * EXPERTS_PER_GROUP + k] for g in range(N_GROUPS)], g_sel) for k in range(EXPERTS_PER_GROUP)]
    _, i0 = argmax_first(in_b)
    _, i1 = argmax_first([jnp.where(i0 == k, -jnp.inf, in_b[k]) for k in range(EXPERTS_PER_GROUP)])
    s0, s1 = pick(in_s, i0), pick(in_s, i1)
    den = s0 + s1
    idx_ref[0:1, :] = g_sel * EXPERTS_PER_GROUP + i0
    idx_ref[1:2, :] = g_sel * EXPERTS_PER_GROUP + i1
    gate_ref[0:1, :] = s0 / den
    gate_ref[1:2, :] = s1 / den


def _router(h2d, rw_t, rb_col):
    n = h2d.shape[0]
    tile = min(ROUTER_TILE, n)
    return pl.pallas_call(
        _router_kernel,
        out_shape=(jax.ShapeDtypeStruct((TOP_K, n), jnp.int32), jax.ShapeDtypeStruct((TOP_K, n), F32)),
        grid=(n // tile,),
        in_specs=[pl.BlockSpec((tile, D_MODEL), lambda i: (i, 0)),
                  pl.BlockSpec((N_EXPERTS, D_MODEL), lambda i: (0, 0)),
                  pl.BlockSpec((N_EXPERTS, 1), lambda i: (0, 0))],
        out_specs=(pl.BlockSpec((TOP_K, tile), lambda i: (0, i)), pl.BlockSpec((TOP_K, tile), lambda i: (0, i))),
        compiler_params=_params(1),
        name="router",
    )(h2d, rw_t, rb_col)


def _expert_kernel(be_ref, nv_ref, x_ref, g_ref, w1_ref, w3_ref, w2_ref, y_ref):
    i = pl.program_id(0)

    @pl.when(i < nv_ref[0])
    def _():
        x = x_ref[...]
        hid = _silu(_mm(x, w1_ref[...])) * _mm(x, w3_ref[...])
        y_ref[...] = _mm(hid.astype(BF16), w2_ref[...]) * g_ref[...]

    @pl.when(i >= nv_ref[0])
    def _():
        y_ref[...] = jnp.zeros_like(y_ref)


def _experts(block_expert, n_valid, xs, gate_col, w1, w3, w2):
    n_rows = xs.shape[0]
    wspec = pl.BlockSpec((None, D_MODEL, D_MODEL), lambda i, be, nv: (be[i], 0, 0))
    return pl.pallas_call(
        _expert_kernel,
        out_shape=jax.ShapeDtypeStruct((n_rows, D_MODEL), F32),
        grid_spec=pltpu.PrefetchScalarGridSpec(
            num_scalar_prefetch=2,
            grid=(n_rows // EXPERT_ROWS,),
            in_specs=[pl.BlockSpec((EXPERT_ROWS, D_MODEL), lambda i, be, nv: (i, 0)),
                      pl.BlockSpec((EXPERT_ROWS, 1), lambda i, be, nv: (i, 0)),
                      wspec, wspec, wspec],
            out_specs=pl.BlockSpec((EXPERT_ROWS, D_MODEL), lambda i, be, nv: (i, 0))),
        compiler_params=_params(1),
        name="experts",
    )(block_expert, n_valid, xs, gate_col, w1, w3, w2)


def _combine_kernel(h_ref, y_ref, lng, lnb, o_ref):
    y = y_ref[...]
    z = ALPHA * h_ref[...] + (y[:, :D_MODEL] + y[:, D_MODEL:])
    o_ref[...] = _layer_norm(z, lng[...], lnb[...])


def _combine(h2d, y_pairs, ln_g, ln_b):
    n = h2d.shape[0]
    full = pl.BlockSpec((ROW_TILE, D_MODEL), lambda i: (i, 0))
    row = pl.BlockSpec((1, D_MODEL), lambda i: (0, 0))
    return pl.pallas_call(
        _combine_kernel,
        out_shape=jax.ShapeDtypeStruct((n, D_MODEL), F32),
        grid=(n // ROW_TILE,),
        in_specs=[full, pl.BlockSpec((ROW_TILE, TOP_K * D_MODEL), lambda i: (i, 0)), row, row],
        out_specs=full,
        compiler_params=_params(1),
        name="moe_combine",
    )(h2d, y_pairs, ln_g, ln_b)


def _moe(h2d, h_bf16, rw_t, rb_col, w1, w3, w2, ln_g, ln_b):
    n = h2d.shape[0]
    idx, gates = _router(h2d, rw_t, rb_col)
    e_flat = idx.T.reshape(-1)
    g_flat = gates.T.reshape(-1)
    n_assign = n * TOP_K
    onehot = (e_flat[:, None] == jnp.arange(N_EXPERTS, dtype=jnp.int32)[None, :]).astype(jnp.int32)
    csum = jnp.cumsum(onehot, axis=0)
    counts = csum[-1]
    rank = jnp.take_along_axis(csum, e_flat[:, None], axis=1)[:, 0] - 1
    padded = (counts + EXPERT_ROWS - 1) // EXPERT_ROWS * EXPERT_ROWS
    pad_end = jnp.cumsum(padded)
    dest = (pad_end - padded)[e_flat] + rank
    n_rows = n_assign + N_EXPERTS * EXPERT_ROWS
    n_blocks = n_rows // EXPERT_ROWS
    tok = jnp.arange(n_assign, dtype=jnp.int32) // TOP_K
    buf_tok = jnp.full((n_rows,), n, jnp.int32).at[dest].set(tok)
    buf_gate = jnp.zeros((n_rows,), F32).at[dest].set(g_flat)
    block_expert = jnp.minimum(
        jnp.searchsorted(pad_end, jnp.arange(n_blocks, dtype=jnp.int32) * EXPERT_ROWS, side='right'),
        N_EXPERTS - 1).astype(jnp.int32)
    n_valid = (pad_end[-1:] // EXPERT_ROWS).astype(jnp.int32)
    x_pad = jnp.concatenate([h_bf16, jnp.zeros((1, D_MODEL), BF16)], axis=0)
    xs = x_pad[buf_tok]
    y = _experts(block_expert, n_valid, xs, buf_gate[:, None], w1, w3, w2)
    y_pairs = y[dest].reshape(n, TOP_K * D_MODEL)
    return _combine(h2d, y_pairs, ln_g, ln_b)


def _pad_cols(w, b, main, width):
    k = w.shape[0]
    tail = w.shape[1] - main
    w_p = jnp.concatenate([w[:, :main], w[:, main:], jnp.zeros((k, width - main - tail), w.dtype)], axis=1)
    b_p = jnp.concatenate([b[:main], b[main:], jnp.zeros((width - main - tail,), b.dtype)])
    return w_p.astype(BF16), b_p[None, :].astype(F32)


def kernel(x, ab_w_in, ab_b_in, hgrn_lb, gla_gk_up, gla_gk_b, hgrn_norm, gla_norm, ab_w_out, cd_w_in, cd_b_in,
           mlstm_conv_w, mlstm_conv_b, mlstm_fgate_b, ret_norm, mlstm_norm, cd_w_out, ln_mix_g, ln_mix_b, ln_ffn_g,
           ln_ffn_b, router_w, router_b, moe_w1, moe_w3, moe_w2):
    bsz, t, d = x.shape
    n = bsz * t
    lower_bounds = jnp.cumsum(jax.nn.softmax(hgrn_lb.astype(F32), axis=0), axis=0)
    rw_t = router_w.T.astype(F32)
    rb_col = router_b.astype(F32)[:, None]
    h = x.reshape(n, d)
    for layer in range(DEPTH):
        j = layer // 2
        if layer % 2 == 0:
            w_p, b_p = _pad_cols(ab_w_in[j], ab_b_in[j], 8 * 512, AB_WIDTH)
            p = _in_proj(h, w_p, b_p)
            p3 = p.reshape(bsz, t, AB_WIDTH)
            nk = N_HEADS * NARROW_DK
            u_pad = jnp.zeros((LANES, 2 * nk), F32)
            u_pad = u_pad.at[:GLA_LOWRANK, :nk].set(gla_gk_up[j, 0])
            u_pad = u_pad.at[GLA_LOWRANK:2 * GLA_LOWRANK, nk:].set(gla_gk_up[j, 1]).astype(BF16)
            gkb = jnp.concatenate([gla_gk_b[j, 0], gla_gk_b[j, 1]])[None, :].astype(F32)
            oa_f, ob_f, oa_b, ob_b = _scan_ab(p3, lower_bounds[layer][None, :], u_pad, gkb)
            norm = jnp.concatenate([hgrn_norm[j], gla_norm[j]])[None, :].astype(F32)
            outs = [a.reshape(n, MIX_HALF) for a in (oa_f, oa_b, ob_f, ob_b)]
            h, h_bf16 = _mix_out(*outs, p, (AB_GA, AB_GB), h, norm, ab_w_out[j].astype(BF16),
                                 ln_mix_g[layer][None, :], ln_mix_b[layer][None, :], _silu)
        else:
            w_p, b_p = _pad_cols(cd_w_in[j], cd_b_in[j], 7 * 512, CD_WIDTH)
            p = _in_proj(h, w_p, b_p)
            p3 = p.reshape(bsz, t, CD_WIDTH)
            oc_f, oc_b = _scan_c(p3)
            qd, kd = _conv_silu(p3, mlstm_conv_w[j].astype(F32), mlstm_conv_b[j].astype(F32))
            fgb_row = jnp.zeros((LANES,), F32)
            fgb_row = fgb_row.at[N_HEADS:2 * N_HEADS].set(mlstm_fgate_b[j, 0])
            fgb_row = fgb_row.at[3 * N_HEADS:4 * N_HEADS].set(mlstm_fgate_b[j, 1])[None, :]
            od_f, od_b = _scan_d(qd, kd, p3, fgb_row)
            norm = jnp.concatenate([ret_norm[j], mlstm_norm[j]])[None, :].astype(F32)
            outs = [a.reshape(n, MIX_HALF) for a in (oc_f, oc_b, od_f, od_b)]
            h, h_bf16 = _mix_out(*outs, p, (CD_GC, CD_OD), h, norm, cd_w_out[j].astype(BF16),
                                 ln_mix_g[layer][None, :], ln_mix_b[layer][None, :], _sigmoid)
        h = _moe(h, h_bf16, rw_t, rb_col, moe_w1[layer].astype(BF16), moe_w3[layer].astype(BF16),
                 moe_w2[layer].astype(BF16), ln_ffn_g[layer][None, :], ln_ffn_b[layer][None, :])
    return h.reshape(bsz, t, d)
```

```python
import functools

import jax
import jax.numpy as jnp
import numpy as np
from jax import lax
from jax.experimental import pallas as pl
from jax.experimental.pallas import tpu as pltpu

F32 = jnp.float32
BF16 = jnp.bfloat16
HIGHEST = lax.Precision.HIGHEST

D_MODEL = 1024
DEPTH = 2
CHUNK = 64
MIX_HALF = D_MODEL // 2
N_HEADS = 4
HEAD_DV = MIX_HALF // N_HEADS
NARROW_DK = 64
GLA_LOWRANK = 16
GLA_GATE_NORMALIZER = 16.0
ROPE_THETA = 10000.0
N_EXPERTS = 16
N_GROUPS = 4
EXPERTS_PER_GROUP = N_EXPERTS // N_GROUPS
TOP_K = 2
ALPHA = (2.0 * DEPTH) ** 0.25
LN_EPS = 1e-5
NORM_EPS = 1e-6

LANES = 128
SUBLANES = 8
VMEM_LIMIT = 52 * 1024 * 1024

ROW_TILE = 256
SCAN_ROWS = 256
ROUTER_TILE = 1024
EXPERT_ROWS = 256

AB_QA, AB_FF, AB_FB, AB_IA, AB_GA, AB_QKB, AB_VB, AB_GB = range(8)
AB_LR_BLOCK = 32
AB_WIDTH = 8 * 512 + LANES
CD_QKC, CD_VC, CD_GC, CD_QD, CD_KD, CD_VD, CD_OD = range(7)
CD_GATE_BLOCK = 28
CD_WIDTH = 7 * 512 + LANES


def _params(n_axes, vmem=VMEM_LIMIT):
    return pltpu.CompilerParams(dimension_semantics=("arbitrary",) * n_axes, vmem_limit_bytes=vmem)


def _nt(a, b):
    return lax.dot_general(a, b, (((1,), (1,)), ((), ())), preferred_element_type=F32)


def _tn(a, b):
    return lax.dot_general(a, b, (((0,), (0,)), ((), ())), preferred_element_type=F32)


def _mm(a, b):
    return jnp.dot(a, b, preferred_element_type=F32)


def _sigmoid(x):
    return 1.0 / (1.0 + jnp.exp(-x))


def _silu(x):
    return x * _sigmoid(x)


def _log_sigmoid(x):
    return jnp.minimum(x, 0.0) - jnp.log(1.0 + jnp.exp(-jnp.abs(x)))


def _tri(direction):
    row = lax.broadcasted_iota(jnp.int32, (CHUNK, CHUNK), 0)
    col = lax.broadcasted_iota(jnp.int32, (CHUNK, CHUNK), 1)
    return (col <= row) if direction == 0 else (col >= row)


def _ref_row(direction):
    return CHUNK // 2 - 1 if direction == 0 else CHUNK // 2


def _last_row(direction):
    return CHUNK - 1 if direction == 0 else 0


def _proj_kernel(x_ref, w_ref, b_ref, o_ref):
    o_ref[...] = _mm(x_ref[...].astype(BF16), w_ref[...]) + b_ref[...]


def _in_proj(x2d, w, b):
    n, k = x2d.shape
    m = w.shape[1]
    return pl.pallas_call(
        _proj_kernel,
        out_shape=jax.ShapeDtypeStruct((n, m), F32),
        grid=(n // ROW_TILE,),
        in_specs=[pl.BlockSpec((ROW_TILE, k), lambda i: (i, 0)),
                  pl.BlockSpec((k, m), lambda i: (0, 0)),
                  pl.BlockSpec((1, m), lambda i: (0, 0))],
        out_specs=pl.BlockSpec((ROW_TILE, m), lambda i: (i, 0)),
        compiler_params=_params(1),
        name="in_proj",
    )(x2d, w, b)


def _scan_unit(qe, ke, qi, ks, v, dec, intra, st_ref, idx):
    s = intra(_nt(qe.astype(BF16), ke.astype(BF16)))
    st = st_ref[idx]
    vb = v.astype(BF16)
    o = _mm(s.astype(BF16), vb) + _nt(qi.astype(BF16), st.astype(BF16))
    st_ref[idx] = st * dec + _tn(vb, ks.astype(BF16))
    return o


def _gate_decays(lf, direction):
    tri = _tri(direction).astype(F32)
    gc = jnp.dot(tri, lf, precision=HIGHEST, preferred_element_type=F32)
    r, l = _ref_row(direction), _last_row(direction)
    g_ref = gc[r:r + 1, :]
    g_last = gc[l:l + 1, :]
    return (jnp.exp(gc - g_ref), jnp.exp(g_ref - gc), jnp.exp(gc), jnp.exp(g_last - gc), jnp.exp(g_last))


def _narrow_head_mask(h):
    lane = lax.broadcasted_iota(jnp.int32, (1, LANES), 1)
    return (lane // NARROW_DK) == (h % 2)


def _scan_ab_kernel(qa_f, fa_f, ia_f, qkb_f, vb_f, lr_f, qa_b, fa_b, ia_b, qkb_b, vb_b, lr_b,
                    lb_ref, u_ref, gkb_ref, oa_f, ob_f, oa_b, ob_b, st_ref):
    @pl.when(pl.program_id(1) == 0)
    def _():
        st_ref[...] = jnp.zeros_like(st_ref)

    n_chunks = SCAN_ROWS // CHUNK
    dirs = ((qa_f, fa_f, ia_f, qkb_f, vb_f, lr_f, oa_f, ob_f), (qa_b, fa_b, ia_b, qkb_b, vb_b, lr_b, oa_b, ob_b))

    def chunk_body(c, carry):
        for d, (qa, fa, ia, qkb, vb, lr, oa, ob) in enumerate(dirs):
            r0 = pl.multiple_of((c if d == 0 else n_chunks - 1 - c) * CHUNK, CHUNK)
            rows = pl.ds(r0, CHUNK)
            mask = _tri(d)
            intra = lambda s, mask=mask: jnp.where(mask, s, 0.0)
            lb = lb_ref[...]
            f = lb + (1.0 - lb) * _sigmoid(fa[rows, :])
            q = _silu(qa[rows, :])
            k = 1.0 - f
            v = ia[rows, :]
            eq, ek, ei, es, dec = _gate_decays(jnp.log(f), d)
            qe, ke, qi, ks = q * eq, k * ek, q * ei, k * es
            for h in range(N_HEADS):
                sl = slice(h * LANES, (h + 1) * LANES)
                oa[rows, sl] = _scan_unit(qe[:, sl], ke[:, sl], qi[:, sl], ks[:, sl], v[:, sl], dec[:, sl],
                                          intra, st_ref, d * 2 * N_HEADS + h)
            qk = qkb[rows, :]
            q = qk[:, :N_HEADS * NARROW_DK] * (NARROW_DK ** -0.5)
            k = qk[:, N_HEADS * NARROW_DK:]
            v = vb[rows, :]
            half = slice(d * N_HEADS * NARROW_DK, (d + 1) * N_HEADS * NARROW_DK)
            logits = _mm(lr[rows, :].astype(BF16), u_ref[:, half]) + gkb_ref[:, half]
            eq, ek, ei, es, dec = _gate_decays(_log_sigmoid(logits) / GLA_GATE_NORMALIZER, d)
            qe, ke, qi, ks = q * eq, k * ek, q * ei, k * es
            for h in range(N_HEADS):
                psl = slice((h // 2) * LANES, (h // 2 + 1) * LANES)
                hm = _narrow_head_mask(h)
                pick = lambda a, psl=psl, hm=hm: jnp.where(hm, a[:, psl], 0.0)
                ob[rows, h * HEAD_DV:(h + 1) * HEAD_DV] = _scan_unit(
                    pick(qe), pick(ke), pick(qi), pick(ks), v[:, h * HEAD_DV:(h + 1) * HEAD_DV], dec[:, psl],
                    intra, st_ref, d * 2 * N_HEADS + N_HEADS + h)
        return carry

    lax.fori_loop(0, n_chunks, chunk_body, 0)


def _scan_ab(p3, lb, u_pad, gkb):
    bsz, t, _ = p3.shape
    nb = t // SCAN_ROWS

    def fwd(j, w=512):
        return pl.BlockSpec((None, SCAN_ROWS, w), lambda b, n: (b, n, j))

    def bwd(j, w=512):
        return pl.BlockSpec((None, SCAN_ROWS, w), lambda b, n: (b, nb - 1 - n, j))

    def const(shape):
        return pl.BlockSpec(shape, lambda b, n: (0,) * len(shape))

    out = jax.ShapeDtypeStruct((bsz, t, MIX_HALF), F32)
    o_f = pl.BlockSpec((None, SCAN_ROWS, MIX_HALF), lambda b, n: (b, n, 0))
    o_b = pl.BlockSpec((None, SCAN_ROWS, MIX_HALF), lambda b, n: (b, nb - 1 - n, 0))
    return pl.pallas_call(
        _scan_ab_kernel,
        out_shape=(out, out, out, out),
        grid=(bsz, nb),
        in_specs=[fwd(AB_QA), fwd(AB_FF), fwd(AB_IA), fwd(AB_QKB), fwd(AB_VB), fwd(AB_LR_BLOCK, LANES),
                  bwd(AB_QA), bwd(AB_FB), bwd(AB_IA), bwd(AB_QKB), bwd(AB_VB), bwd(AB_LR_BLOCK, LANES),
                  const(lb.shape), const(u_pad.shape), const(gkb.shape)],
        out_specs=(o_f, o_f, o_b, o_b),
        scratch_shapes=[pltpu.VMEM((4 * N_HEADS, HEAD_DV, LANES), F32)],
        compiler_params=_params(2),
        name="scan_ab",
    )(*([p3] * 12), lb, u_pad, gkb)


def _rotary(a, cos, sin_signed):
    lane = lax.broadcasted_iota(jnp.int32, (1, LANES), 1)
    first_half = (lane % NARROW_DK) < (NARROW_DK // 2)
    swapped = jnp.where(first_half, pltpu.roll(a, LANES - NARROW_DK // 2, 1), pltpu.roll(a, NARROW_DK // 2, 1))
    return a * cos + swapped * sin_signed


def _scan_c_kernel(qk_f, v_f, cos_f, sin_f, qk_b, v_b, cos_b, sin_b, dmat_ref, qsc_ref, ksc_ref, cd_ref,
                   o_f, o_b, st_ref):
    @pl.when(pl.program_id(1) == 0)
    def _():
        st_ref[...] = jnp.zeros_like(st_ref)

    n_chunks = SCAN_ROWS // CHUNK
    dirs = ((qk_f, v_f, cos_f, sin_f, o_f), (qk_b, v_b, cos_b, sin_b, o_b))

    def chunk_body(c, carry):
        for d, (qk_ref, v_ref, cos_ref, sin_ref, o_ref) in enumerate(dirs):
            r0 = pl.multiple_of((c if d == 0 else n_chunks - 1 - c) * CHUNK, CHUNK)
            rows = pl.ds(r0, CHUNK)
            cos, sin_signed = cos_ref[rows, :], sin_ref[rows, :]
            v = v_ref[rows, :]
            for h in range(N_HEADS):
                p = h // 2
                hm = _narrow_head_mask(h)
                q = _rotary(qk_ref[rows, p * LANES:(p + 1) * LANES], cos, sin_signed) * (NARROW_DK ** -0.5)
                k = _rotary(qk_ref[rows, (2 + p) * LANES:(3 + p) * LANES], cos, sin_signed)
                q = jnp.where(hm, q, 0.0)
                k = jnp.where(hm, k, 0.0)
                dmat = dmat_ref[d, h]
                o_ref[rows, h * HEAD_DV:(h + 1) * HEAD_DV] = _scan_unit(
                    q, k, q * qsc_ref[d, h], k * ksc_ref[d, h], v[:, h * HEAD_DV:(h + 1) * HEAD_DV],
                    cd_ref[d, h][0:1, :], lambda s, dmat=dmat: s * dmat, st_ref, d * N_HEADS + h)
        return carry

    lax.fori_loop(0, n_chunks, chunk_body, 0)


def _retention_tables():
    pos = np.arange(CHUNK, dtype=np.float64)
    dmat = np.zeros((2, N_HEADS, CHUNK, CHUNK), np.float32)
    qsc = np.zeros((2, N_HEADS, CHUNK, LANES), np.float32)
    ksc = np.zeros((2, N_HEADS, CHUNK, LANES), np.float32)
    cd = np.zeros((2, N_HEADS, SUBLANES, LANES), np.float32)
    for d in range(2):
        for h in range(N_HEADS):
            lg = np.log1p(-np.exp2(-(5.0 + 2.0 * h + d)))
            dist = pos[:, None] - pos[None, :]
            if d == 0:
                dmat[d, h] = np.where(dist >= 0, np.exp(np.maximum(dist, 0.0) * lg), 0.0)
                qsc[d, h] = np.exp((pos + 1.0) * lg)[:, None]
                ksc[d, h] = np.exp((CHUNK - 1.0 - pos) * lg)[:, None]
            else:
                dmat[d, h] = np.where(dist <= 0, np.exp(np.maximum(-dist, 0.0) * lg), 0.0)
                qsc[d, h] = np.exp((CHUNK - pos) * lg)[:, None]
                ksc[d, h] = np.exp(pos * lg)[:, None]
            cd[d, h] = np.exp(CHUNK * lg)
    return jnp.asarray(dmat), jnp.asarray(qsc), jnp.asarray(ksc), jnp.asarray(cd)


def _rotary_tables(t):
    half = NARROW_DK // 2
    freqs = ROPE_THETA ** (-jnp.arange(half, dtype=F32) / half)
    ang = jnp.arange(t, dtype=F32)[:, None] * freqs[None, :]
    cos, sin = jnp.cos(ang), jnp.sin(ang)
    cos_t = jnp.tile(cos, (1, LANES // half))
    sin_t = jnp.tile(jnp.concatenate([-sin, sin], axis=1), (1, LANES // NARROW_DK))
    return cos_t.astype(F32), sin_t.astype(F32)


def _scan_c(p3):
    bsz, t, _ = p3.shape
    nb = t // SCAN_ROWS
    cos_t, sin_t = _rotary_tables(t)
    tables = _retention_tables()

    def fwd(j):
        return pl.BlockSpec((None, SCAN_ROWS, 512), lambda b, n: (b, n, j))

    def bwd(j):
        return pl.BlockSpec((None, SCAN_ROWS, 512), lambda b, n: (b, nb - 1 - n, j))

    tab_f = pl.BlockSpec((SCAN_ROWS, LANES), lambda b, n: (n, 0))
    tab_b = pl.BlockSpec((SCAN_ROWS, LANES), lambda b, n: (nb - 1 - n, 0))

    def const(shape):
        return pl.BlockSpec(shape, lambda b, n: (0,) * len(shape))

    out = jax.ShapeDtypeStruct((bsz, t, MIX_HALF), F32)
    return pl.pallas_call(
        _scan_c_kernel,
        out_shape=(out, out),
        grid=(bsz, nb),
        in_specs=[fwd(CD_QKC), fwd(CD_VC), tab_f, tab_f, bwd(CD_QKC), bwd(CD_VC), tab_b, tab_b]
        + [const(a.shape) for a in tables],
        out_specs=(pl.BlockSpec((None, SCAN_ROWS, MIX_HALF), lambda b, n: (b, n, 0)),
                   pl.BlockSpec((None, SCAN_ROWS, MIX_HALF), lambda b, n: (b, nb - 1 - n, 0))),
        scratch_shapes=[pltpu.VMEM((2 * N_HEADS, HEAD_DV, LANES), F32)],
        compiler_params=_params(2),
        name="scan_c",
    )(p3, p3, cos_t, sin_t, p3, p3, cos_t, sin_t, *tables)


def _conv_kernel(prev_ref, cur_ref, next_ref, w_ref, b_ref, o_ref, buf_ref):
    i = pl.program_id(1)
    last = pl.num_programs(1) - 1
    rows = cur_ref.shape[0]
    buf_ref[0:SUBLANES, :] = jnp.where(i == 0, 0.0, prev_ref[...])
    buf_ref[SUBLANES:SUBLANES + rows, :] = cur_ref[...]
    buf_ref[SUBLANES + rows:2 * SUBLANES + rows, :] = jnp.where(i == last, 0.0, next_ref[...])
    y = (buf_ref[SUBLANES - 1:SUBLANES - 1 + rows, :] * w_ref[0:1, :]
         + buf_ref[SUBLANES:SUBLANES + rows, :] * w_ref[1:2, :]
         + buf_ref[SUBLANES + 1:SUBLANES + 1 + rows, :] * w_ref[2:3, :]
         + b_ref[...])
    o_ref[...] = _silu(y)


def _conv_silu(p3, conv_w, conv_b):
    bsz, t, _ = p3.shape
    nb = t // ROW_TILE
    per = ROW_TILE // SUBLANES
    outs = []
    for j in (CD_QD, CD_KD):
        lo = (j - CD_QD) * 512
        outs.append(pl.pallas_call(
            _conv_kernel,
            out_shape=jax.ShapeDtypeStruct((bsz, t, 512), F32),
            grid=(bsz, nb),
            in_specs=[pl.BlockSpec((None, SUBLANES, 512), lambda b, i, j=j: (b, jnp.maximum(i * per - 1, 0), j)),
                      pl.BlockSpec((None, ROW_TILE, 512), lambda b, i, j=j: (b, i, j)),
                      pl.BlockSpec((None, SUBLANES, 512),
                                   lambda b, i, j=j: (b, jnp.minimum((i + 1) * per, t // SUBLANES - 1), j)),
                      pl.BlockSpec((3, 512), lambda b, i: (0, 0)),
                      pl.BlockSpec((1, 512), lambda b, i: (0, 0))],
            out_specs=pl.BlockSpec((None, ROW_TILE, 512), lambda b, i: (b, i, 0)),
            scratch_shapes=[pltpu.VMEM((ROW_TILE + 2 * SUBLANES, 512), F32)],
            compiler_params=_params(2),
            name="conv_silu",
        )(p3, p3, p3, conv_w[:, lo:lo + 512], conv_b[None, lo:lo + 512]))
    return outs


def _scan_d_kernel(q_f, k_f, v_f, g_f, q_b, k_b, v_b, g_b, fgb_ref, o_f, o_b, ct_ref, m_ref):
    @pl.when(pl.program_id(1) == 0)
    def _():
        ct_ref[...] = jnp.zeros_like(ct_ref)
        m_ref[...] = jnp.zeros_like(m_ref)

    n_chunks = SCAN_ROWS // CHUNK
    dirs = ((q_f, k_f, v_f, g_f, o_f), (q_b, k_b, v_b, g_b, o_b))
    lane = lax.broadcasted_iota(jnp.int32, (CHUNK, LANES), 1)
    ones_col = jnp.where(lane == 0, 1.0, 0.0).astype(BF16)

    def chunk_body(c, carry):
        for d, (q_ref, k_ref, v_ref, g_ref, o_ref) in enumerate(dirs):
            r0 = pl.multiple_of((c if d == 0 else n_chunks - 1 - c) * CHUNK, CHUNK)
            rows = pl.ds(r0, CHUNK)
            mask = _tri(d)
            last = _last_row(d)
            g = g_ref[rows, :]
            lf = _log_sigmoid(g + fgb_ref[...])
            bc = jnp.dot(mask.astype(F32), lf, precision=HIGHEST, preferred_element_type=F32)
            g_t = g.T
            bc_t = bc.T
            for h in range(N_HEADS):
                ci = d * 2 * N_HEADS + h
                cf = ci + N_HEADS
                sl = slice(h * HEAD_DV, (h + 1) * HEAD_DV)
                i_col, b_col = g[:, ci:ci + 1], bc[:, cf:cf + 1]
                i_row, b_row = g_t[ci:ci + 1, :], bc_t[cf:cf + 1, :]
                b_last = b_col[last:last + 1, :]
                m_st = m_ref[d * N_HEADS + h][0:1, 0:1]
                log_d = jnp.where(mask, b_col - b_row + i_row, -jnp.inf)
                m_intra = jnp.max(log_d, axis=-1, keepdims=True)
                a = b_col + m_st
                m_i = jnp.maximum(a, m_intra)
                q = (q_ref[rows, sl] * (HEAD_DV ** -0.5)).astype(BF16)
                k = k_ref[rows, sl]
                v_ext = jnp.concatenate([v_ref[rows, sl].astype(BF16), ones_col], axis=1)
                w = _nt(q, k.astype(BF16)) * jnp.exp(log_d - m_i)
                ct = ct_ref[d * N_HEADS + h]
                num = jnp.exp(a - m_i) * _nt(q, ct.astype(BF16)) + _mm(w.astype(BF16), v_ext)
                den = num[:, HEAD_DV:HEAD_DV + 1]
                o_ref[rows, sl] = num[:, :HEAD_DV] / jnp.maximum(jnp.abs(den), jnp.exp(-m_i))
                lw = b_last - b_col + i_col
                m_new = jnp.maximum(b_last + m_st, jnp.max(lw, axis=0, keepdims=True))
                s_old = jnp.exp(b_last + m_st - m_new)
                wk = (k * jnp.exp(lw - m_new)).astype(BF16)
                ct_ref[d * N_HEADS + h] = s_old * ct + _tn(v_ext, wk)
                m_ref[d * N_HEADS + h] = jnp.broadcast_to(m_new, (SUBLANES, LANES))
        return carry

    lax.fori_loop(0, n_chunks, chunk_body, 0)


def _scan_d(qd, kd, p3, fgb_row):
    bsz, t, _ = p3.shape
    nb = t // SCAN_ROWS

    def spec(j, w, flip):
        if flip:
            return pl.BlockSpec((None, SCAN_ROWS, w), lambda b, n: (b, nb - 1 - n, j))
        return pl.BlockSpec((None, SCAN_ROWS, w), lambda b, n: (b, n, j))

    out = jax.ShapeDtypeStruct((bsz, t, MIX_HALF), F32)
    in_specs = []
    for flip in (False, True):
        in_specs += [spec(0, 512, flip), spec(0, 512, flip), spec(CD_VD, 512, flip), spec(CD_GATE_BLOCK, LANES, flip)]
    in_specs.append(pl.BlockSpec((1, LANES), lambda b, n: (0, 0)))
    return pl.pallas_call(
        _scan_d_kernel,
        out_shape=(out, out),
        grid=(bsz, nb),
        in_specs=in_specs,
        out_specs=(spec(0, MIX_HALF, False), spec(0, MIX_HALF, True)),
        scratch_shapes=[pltpu.VMEM((2 * N_HEADS, 2 * HEAD_DV, LANES), F32),
                        pltpu.VMEM((2 * N_HEADS, SUBLANES, LANES), F32)],
        compiler_params=_params(2),
        name="scan_d",
    )(qd, kd, p3, p3, qd, kd, p3, p3, fgb_row)


def _layer_norm(z, g, b):
    zc = z - jnp.mean(z, axis=-1, keepdims=True)
    var = jnp.mean(zc * zc, axis=-1, keepdims=True)
    return zc * lax.rsqrt(var + LN_EPS) * g + b


def _mix_out_kernel(o1f, o1b, o2f, o2b, g1, g2, x_ref, norm_ref, w_ref, lng, lnb, h_ref, hb_ref, *, second_gate):
    parts = []
    for of, ob, g_ref, gate_fn, base in ((o1f, o1b, g1, _silu, 0), (o2f, o2b, g2, second_gate, MIX_HALF)):
        o = of[...] + ob[...]
        gate = gate_fn(g_ref[...])
        for h in range(N_HEADS):
            sl = slice(h * HEAD_DV, (h + 1) * HEAD_DV)
            oh = o[:, sl]
            oh = oh * lax.rsqrt(jnp.mean(oh * oh, axis=-1, keepdims=True) + NORM_EPS)
            parts.append((oh * norm_ref[:, base + h * HEAD_DV:base + (h + 1) * HEAD_DV] * gate[:, sl]).astype(BF16))
    mixed = jnp.concatenate(parts, axis=1)
    z = ALPHA * x_ref[...] + _mm(mixed, w_ref[...])
    hn = _layer_norm(z, lng[...], lnb[...])
    h_ref[...] = hn
    hb_ref[...] = hn.astype(BF16)


def _mix_out(o1f, o1b, o2f, o2b, p2d, gate_blocks, x2d, norm, w_out, ln_g, ln_b, second_gate):
    n = x2d.shape[0]
    half = pl.BlockSpec((ROW_TILE, MIX_HALF), lambda i: (i, 0))
    full = pl.BlockSpec((ROW_TILE, D_MODEL), lambda i: (i, 0))
    row = pl.BlockSpec((1, D_MODEL), lambda i: (0, 0))
    ga, gb = gate_blocks
    return pl.pallas_call(
        functools.partial(_mix_out_kernel, second_gate=second_gate),
        out_shape=(jax.ShapeDtypeStruct((n, D_MODEL), F32), jax.ShapeDtypeStruct((n, D_MODEL), BF16)),
        grid=(n // ROW_TILE,),
        in_specs=[half, half, half, half,
                  pl.BlockSpec((ROW_TILE, MIX_HALF), lambda i: (i, ga)),
                  pl.BlockSpec((ROW_TILE, MIX_HALF), lambda i: (i, gb)),
                  full, row, pl.BlockSpec((D_MODEL, D_MODEL), lambda i: (0, 0)), row, row],
        out_specs=(full, full),
        compiler_params=_params(1),
        name="mix_out",
    )(o1f, o1b, o2f, o2b, p2d, p2d, x2d, norm, w_out, ln_g, ln_b)


def _router_kernel(h_ref, rw_ref, rb_ref, idx_ref, gate_ref):
    logits = lax.dot_general(rw_ref[...], h_ref[...], (((1,), (1,)), ((), ())),
                             precision=HIGHEST, preferred_element_type=F32)
    rows = [logits[e:e + 1, :] for e in range(N_EXPERTS)]
    mx = functools.reduce(jnp.maximum, rows)
    ex = [jnp.exp(r - mx) for r in rows]
    tot = functools.reduce(lambda a, b: a + b, ex)
    score = [e / tot for e in ex]
    biased = [score[e] + rb_ref[e:e + 1, :] for e in range(N_EXPERTS)]

    def argmax_first(vals):
        best, idx = vals[0], jnp.zeros(vals[0].shape, jnp.int32)
        for j in range(1, len(vals)):
            upd = vals[j] > best
            best = jnp.where(upd, vals[j], best)
            idx = jnp.where(upd, j, idx)
        return best, idx

    def pick(vals, idx):
        out = vals[0]
        for j in range(1, len(vals)):
            out = jnp.where(idx == j, vals[j], out)
        return out

    group_scores = []
    for g in range(N_GROUPS):
        a = biased[g * EXPERTS_PER_GROUP:(g + 1) * EXPERTS_PER_GROUP]
        pairs = [a[i] + a[j] for i in range(EXPERTS_PER_GROUP) for j in range(i + 1, EXPERTS_PER_GROUP)]
        group_scores.append(functools.reduce(jnp.maximum, pairs))
    _, g_sel = argmax_first(group_scores)
    in_b = [pick([biased[g * EXPERTS_PER_GROUP + k] for g in range(N_GROUPS)], g_sel) for k in range(EXPERTS_PER_GROUP)]
    in_s = [pick([score[g * EXPERTS_PER_GROUP + k] for g in range(N_GROUPS)], g_sel) for k in range(EXPERTS_PER_GROUP)]
    _, i0 = argmax_first(in_b)
    _, i1 = argmax_first([jnp.where(i0 == k, -jnp.inf, in_b[k]) for k in range(EXPERTS_PER_GROUP)])
    s0, s1 = pick(in_s, i0), pick(in_s, i1)
    den = s0 + s1
    idx_ref[0:1, :] = g_sel * EXPERTS_PER_GROUP + i0
    idx_ref[1:2, :] = g_sel * EXPERTS_PER_GROUP + i1
    gate_ref[0:1, :] = s0 / den
    gate_ref[1:2, :] = s1 / den


def _router(h2d, rw_t, rb_col):
    n = h2d.shape[0]
    tile = min(ROUTER_TILE, n)
    return pl.pallas_call(
        _router_kernel,
        out_shape=(jax.ShapeDtypeStruct((TOP_K, n), jnp.int32), jax.ShapeDtypeStruct((TOP_K, n), F32)),
        grid=(n // tile,),
        in_specs=[pl.BlockSpec((tile, D_MODEL), lambda i: (i, 0)),
                  pl.BlockSpec((N_EXPERTS, D_MODEL), lambda i: (0, 0)),
                  pl.BlockSpec((N_EXPERTS, 1), lambda i: (0, 0))],
        out_specs=(pl.BlockSpec((TOP_K, tile), lambda i: (0, i)), pl.BlockSpec((TOP_K, tile), lambda i: (0, i))),
        compiler_params=_params(1),
        name="router",
    )(h2d, rw_t, rb_col)


def _plan_kernel(idx_ref, dest_ref, cnt_ref, su_ref, run_ref, start_ref):
    phase, i = pl.program_id(0), pl.program_id(1)
    tile = idx_ref.shape[1]
    e_iota = lax.broadcasted_iota(jnp.int32, (N_EXPERTS, tile), 0)
    oh0 = idx_ref[0:1, :] == e_iota
    oh1 = idx_ref[1:2, :] == e_iota
    member = jnp.where(oh0 | oh1, 1.0, 0.0)
    tile_count = jnp.broadcast_to(jnp.sum(member, axis=1, keepdims=True), (N_EXPERTS, LANES))

    @pl.when((phase == 0) & (i == 0))
    def _():
        run_ref[...] = jnp.zeros_like(run_ref)
        r = lax.broadcasted_iota(jnp.int32, (tile, tile), 0)
        c = lax.broadcasted_iota(jnp.int32, (tile, tile), 1)
        su_ref[...] = jnp.where(r < c, 1.0, 0.0).astype(BF16)

    @pl.when((phase == 1) & (i == 0))
    def _():
        counts = run_ref[...]
        cnt_ref[...] = counts.astype(jnp.int32)
        padded = jnp.ceil(counts * (1.0 / EXPERT_ROWS)) * EXPERT_ROWS
        r = lax.broadcasted_iota(jnp.int32, (N_EXPERTS, N_EXPERTS), 0)
        c = lax.broadcasted_iota(jnp.int32, (N_EXPERTS, N_EXPERTS), 1)
        start_ref[...] = jnp.dot(jnp.where(c < r, 1.0, 0.0), padded, precision=HIGHEST, preferred_element_type=F32)
        run_ref[...] = jnp.zeros_like(run_ref)

    @pl.when(phase == 1)
    def _():
        before = _mm(member.astype(BF16), su_ref[...])
        pos = start_ref[:, 0:1] + run_ref[:, 0:1] + before
        dest_ref[0:1, :] = jnp.sum(jnp.where(oh0, pos, 0.0), axis=0, keepdims=True).astype(jnp.int32)
        dest_ref[1:2, :] = jnp.sum(jnp.where(oh1, pos, 0.0), axis=0, keepdims=True).astype(jnp.int32)

    run_ref[...] += tile_count


def _plan(idx):
    n = idx.shape[1]
    tile = min(ROUTER_TILE, n)
    return pl.pallas_call(
        _plan_kernel,
        out_shape=(jax.ShapeDtypeStruct((TOP_K, n), jnp.int32), jax.ShapeDtypeStruct((N_EXPERTS, LANES), jnp.int32)),
        grid=(2, n // tile),
        in_specs=[pl.BlockSpec((TOP_K, tile), lambda p, i: (0, i))],
        out_specs=(pl.BlockSpec((TOP_K, tile), lambda p, i: (0, i * p)),
                   pl.BlockSpec((N_EXPERTS, LANES), lambda p, i: (0, 0))),
        scratch_shapes=[pltpu.VMEM((tile, tile), BF16), pltpu.VMEM((N_EXPERTS, LANES), F32),
                        pltpu.VMEM((N_EXPERTS, LANES), F32)],
        compiler_params=_params(2),
        name="moe_plan",
    )(idx)


def _invert_kernel(dest_ref, tok_ref):
    n_assign = dest_ref.shape[0]
    n = n_assign // TOP_K

    def clear(r, carry):
        tok_ref[r] = 0
        return carry

    lax.fori_loop(0, tok_ref.shape[0], clear, 0, unroll=8)

    def place(a, carry):
        tok_ref[dest_ref[a]] = jnp.where(a >= n, a - n, a)
        return carry

    lax.fori_loop(0, n_assign, place, 0, unroll=8)


def _invert(dest_flat, n_rows):
    return pl.pallas_call(
        _invert_kernel,
        out_shape=jax.ShapeDtypeStruct((n_rows,), jnp.int32),
        in_specs=[pl.BlockSpec(memory_space=pltpu.SMEM)],
        out_specs=pl.BlockSpec(memory_space=pltpu.SMEM),
        name="moe_invert",
    )(dest_flat)


def _expert_kernel(be_ref, nv_ref, x_ref, w1_ref, w3_ref, w2_ref, y_ref, wb_ref):
    i = pl.program_id(0)
    valid = i < nv_ref[0]
    new_expert = (i == 0) | (be_ref[i] != be_ref[jnp.maximum(i - 1, 0)])

    @pl.when(valid & new_expert)
    def _():
        wb_ref[0] = w1_ref[...].astype(BF16)
        wb_ref[1] = w3_ref[...].astype(BF16)
        wb_ref[2] = w2_ref[...].astype(BF16)

    @pl.when(valid)
    def _():
        x = x_ref[...]
        hid = _silu(_mm(x, wb_ref[0])) * _mm(x, wb_ref[1])
        y_ref[...] = _mm(hid.astype(BF16), wb_ref[2]).astype(BF16)

    @pl.when(jnp.logical_not(valid))
    def _():
        y_ref[...] = jnp.zeros_like(y_ref)


def _experts(block_expert, n_valid, xs, w1, w3, w2, layer):
    n_rows = xs.shape[0]
    wspec = pl.BlockSpec((None, None, D_MODEL, D_MODEL), lambda i, be, nv: (layer, be[i], 0, 0))
    return pl.pallas_call(
        _expert_kernel,
        out_shape=jax.ShapeDtypeStruct((n_rows, D_MODEL), BF16),
        grid_spec=pltpu.PrefetchScalarGridSpec(
            num_scalar_prefetch=2,
            grid=(n_rows // EXPERT_ROWS,),
            in_specs=[pl.BlockSpec((EXPERT_ROWS, D_MODEL), lambda i, be, nv: (i, 0)), wspec, wspec, wspec],
            out_specs=pl.BlockSpec((EXPERT_ROWS, D_MODEL), lambda i, be, nv: (i, 0)),
            scratch_shapes=[pltpu.VMEM((3, D_MODEL, D_MODEL), BF16)]),
        compiler_params=_params(1),
        name="experts",
    )(block_expert, n_valid, xs, w1, w3, w2)


def _combine_kernel(h_ref, y0_ref, y1_ref, g_ref, lng, lnb, o_ref):
    g = g_ref[...]
    z = ALPHA * h_ref[...] + (g[:, 0:1] * y0_ref[...].astype(F32) + g[:, 1:2] * y1_ref[...].astype(F32))
    o_ref[...] = _layer_norm(z, lng[...], lnb[...])


def _combine(h2d, y_rows, gates_t, ln_g, ln_b):
    n = h2d.shape[0]
    nt = n // ROW_TILE
    full = pl.BlockSpec((ROW_TILE, D_MODEL), lambda i: (i, 0))
    row = pl.BlockSpec((1, D_MODEL), lambda i: (0, 0))
    return pl.pallas_call(
        _combine_kernel,
        out_shape=jax.ShapeDtypeStruct((n, D_MODEL), F32),
        grid=(nt,),
        in_specs=[full, full, pl.BlockSpec((ROW_TILE, D_MODEL), lambda i: (nt + i, 0)),
                  pl.BlockSpec((ROW_TILE, TOP_K), lambda i: (i, 0)), row, row],
        out_specs=full,
        compiler_params=_params(1),
        name="moe_combine",
    )(h2d, y_rows, y_rows, gates_t, ln_g, ln_b)


def _moe(h2d, h_bf16, rw_t, rb_col, w1, w3, w2, layer, ln_g, ln_b):
    n = h2d.shape[0]
    idx, gates = _router(h2d, rw_t, rb_col)
    dest, counts = _plan(idx)
    n_rows = n * TOP_K + N_EXPERTS * EXPERT_ROWS
    n_blocks = n_rows // EXPERT_ROWS
    counts = counts[:, 0]
    pad_end = jnp.cumsum((counts + EXPERT_ROWS - 1) // EXPERT_ROWS * EXPERT_ROWS)
    block_start = jnp.arange(n_blocks, dtype=jnp.int32) * EXPERT_ROWS
    block_expert = jnp.minimum(jnp.sum((pad_end[None, :] <= block_start[:, None]).astype(jnp.int32), axis=1),
                               N_EXPERTS - 1)
    n_valid = (pad_end[-1:] // EXPERT_ROWS).astype(jnp.int32)
    dest_flat = dest.reshape(-1)
    xs = h_bf16[_invert(dest_flat, n_rows)]
    y = _experts(block_expert, n_valid, xs, w1, w3, w2, layer)
    return _combine(h2d, y[dest_flat], gates.T, ln_g, ln_b)


def _pad_cols(w, b, main, width):
    k = w.shape[0]
    tail = w.shape[1] - main
    w_p = jnp.concatenate([w[:, :main], w[:, main:], jnp.zeros((k, width - main - tail), w.dtype)], axis=1)
    b_p = jnp.concatenate([b[:main], b[main:], jnp.zeros((width - main - tail,), b.dtype)])
    return w_p.astype(BF16), b_p[None, :].astype(F32)


def kernel(x, ab_w_in, ab_b_in, hgrn_lb, gla_gk_up, gla_gk_b, hgrn_norm, gla_norm, ab_w_out, cd_w_in, cd_b_in,
           mlstm_conv_w, mlstm_conv_b, mlstm_fgate_b, ret_norm, mlstm_norm, cd_w_out, ln_mix_g, ln_mix_b, ln_ffn_g,
           ln_ffn_b, router_w, router_b, moe_w1, moe_w3, moe_w2):
    bsz, t, d = x.shape
    n = bsz * t
    lower_bounds = jnp.cumsum(jax.nn.softmax(hgrn_lb.astype(F32), axis=0), axis=0)
    rw_t = router_w.T.astype(F32)
    rb_col = router_b.astype(F32)[:, None]
    h = x.reshape(n, d)
    for layer in range(DEPTH):
        j = layer // 2
        if layer % 2 == 0:
            w_p, b_p = _pad_cols(ab_w_in[j], ab_b_in[j], 8 * 512, AB_WIDTH)
            p = _in_proj(h, w_p, b_p)
            p3 = p.reshape(bsz, t, AB_WIDTH)
            nk = N_HEADS * NARROW_DK
            u_pad = jnp.zeros((LANES, 2 * nk), F32)
            u_pad = u_pad.at[:GLA_LOWRANK, :nk].set(gla_gk_up[j, 0])
            u_pad = u_pad.at[GLA_LOWRANK:2 * GLA_LOWRANK, nk:].set(gla_gk_up[j, 1]).astype(BF16)
            gkb = jnp.concatenate([gla_gk_b[j, 0], gla_gk_b[j, 1]])[None, :].astype(F32)
            oa_f, ob_f, oa_b, ob_b = _scan_ab(p3, lower_bounds[layer][None, :], u_pad, gkb)
            norm = jnp.concatenate([hgrn_norm[j], gla_norm[j]])[None, :].astype(F32)
            outs = [a.reshape(n, MIX_HALF) for a in (oa_f, oa_b, ob_f, ob_b)]
            h, h_bf16 = _mix_out(*outs, p, (AB_GA, AB_GB), h, norm, ab_w_out[j].astype(BF16),
                                 ln_mix_g[layer][None, :], ln_mix_b[layer][None, :], _silu)
        else:
            w_p, b_p = _pad_cols(cd_w_in[j], cd_b_in[j], 7 * 512, CD_WIDTH)
            p = _in_proj(h, w_p, b_p)
            p3 = p.reshape(bsz, t, CD_WIDTH)
            oc_f, oc_b = _scan_c(p3)
            qd, kd = _conv_silu(p3, mlstm_conv_w[j].astype(F32), mlstm_conv_b[j].astype(F32))
            fgb_row = jnp.zeros((LANES,), F32)
            fgb_row = fgb_row.at[N_HEADS:2 * N_HEADS].set(mlstm_fgate_b[j, 0])
            fgb_row = fgb_row.at[3 * N_HEADS:4 * N_HEADS].set(mlstm_fgate_b[j, 1])[None, :]
            od_f, od_b = _scan_d(qd, kd, p3, fgb_row)
            norm = jnp.concatenate([ret_norm[j], mlstm_norm[j]])[None, :].astype(F32)
            outs = [a.reshape(n, MIX_HALF) for a in (oc_f, oc_b, od_f, od_b)]
            h, h_bf16 = _mix_out(*outs, p, (CD_GC, CD_OD), h, norm, cd_w_out[j].astype(BF16),
                                 ln_mix_g[layer][None, :], ln_mix_b[layer][None, :], _sigmoid)
        h = _moe(h, h_bf16, rw_t, rb_col, moe_w1, moe_w3, moe_w2, layer,
                 ln_ffn_g[layer][None, :], ln_ffn_b[layer][None, :])
    return h.reshape(bsz, t, d)
```

```python
import functools

import jax
import jax.numpy as jnp
import numpy as np
from jax import lax
from jax.experimental import pallas as pl
from jax.experimental.pallas import tpu as pltpu

F32 = jnp.float32
BF16 = jnp.bfloat16
HIGHEST = lax.Precision.HIGHEST

D_MODEL = 1024
DEPTH = 2
CHUNK = 64
MIX_HALF = D_MODEL // 2
N_HEADS = 4
HEAD_DV = MIX_HALF // N_HEADS
NARROW_DK = 64
GLA_LOWRANK = 16
GLA_GATE_NORMALIZER = 16.0
ROPE_THETA = 10000.0
N_EXPERTS = 16
N_GROUPS = 4
EXPERTS_PER_GROUP = N_EXPERTS // N_GROUPS
TOP_K = 2
ALPHA = (2.0 * DEPTH) ** 0.25
LN_EPS = 1e-5
NORM_EPS = 1e-6

LANES = 128
SUBLANES = 8
VMEM_LIMIT = 52 * 1024 * 1024

ROW_TILE = 256
SCAN_ROWS = 256
ROUTER_TILE = 1024
EXPERT_ROWS = 256

AB_QA, AB_FF, AB_FB, AB_IA, AB_GA, AB_QKB, AB_VB, AB_GB = range(8)
AB_LR_BLOCK = 32
AB_WIDTH = 8 * 512 + LANES
CD_QKC, CD_VC, CD_GC, CD_QD, CD_KD, CD_VD, CD_OD = range(7)
CD_GATE_BLOCK = 28
CD_WIDTH = 7 * 512 + LANES


def _params(n_axes, vmem=VMEM_LIMIT):
    return pltpu.CompilerParams(dimension_semantics=("arbitrary",) * n_axes, vmem_limit_bytes=vmem)


def _nt(a, b):
    return lax.dot_general(a, b, (((1,), (1,)), ((), ())), preferred_element_type=F32)


def _tn(a, b):
    return lax.dot_general(a, b, (((0,), (0,)), ((), ())), preferred_element_type=F32)


def _mm(a, b):
    return jnp.dot(a, b, preferred_element_type=F32)


def _sigmoid(x):
    return 1.0 / (1.0 + jnp.exp(-x))


def _silu(x):
    return x * _sigmoid(x)


def _log_sigmoid(x):
    return jnp.minimum(x, 0.0) - jnp.log(1.0 + jnp.exp(-jnp.abs(x)))


def _tri(direction):
    row = lax.broadcasted_iota(jnp.int32, (CHUNK, CHUNK), 0)
    col = lax.broadcasted_iota(jnp.int32, (CHUNK, CHUNK), 1)
    return (col <= row) if direction == 0 else (col >= row)


def _ref_row(direction):
    return CHUNK // 2 - 1 if direction == 0 else CHUNK // 2


def _last_row(direction):
    return CHUNK - 1 if direction == 0 else 0


def _proj_kernel(x_ref, w_ref, b_ref, o_ref):
    o_ref[...] = _mm(x_ref[...].astype(BF16), w_ref[...]) + b_ref[...]


def _in_proj(x2d, w, b):
    n, k = x2d.shape
    m = w.shape[1]
    return pl.pallas_call(
        _proj_kernel,
        out_shape=jax.ShapeDtypeStruct((n, m), F32),
        grid=(n // ROW_TILE,),
        in_specs=[pl.BlockSpec((ROW_TILE, k), lambda i: (i, 0)),
                  pl.BlockSpec((k, m), lambda i: (0, 0)),
                  pl.BlockSpec((1, m), lambda i: (0, 0))],
        out_specs=pl.BlockSpec((ROW_TILE, m), lambda i: (i, 0)),
        compiler_params=_params(1),
        name="in_proj",
    )(x2d, w, b)


def _run_units(units, st_ref):
    first = [(_nt(u["qe"], u["ke"]), _tn(u["v"], u["ks"]), _nt(u["qi"], st_ref[u["idx"]].astype(BF16)))
             for u in units]
    for u, (s, upd, inter) in zip(units, first):
        u["store"](_mm(u["intra"](s).astype(BF16), u["v"]) + inter)
        st_ref[u["idx"]] = st_ref[u["idx"]] * u["dec"] + upd


def _chunk_cumsum(parts):
    cat = jnp.concatenate([x for x, _ in parts], axis=1)
    hi = cat.astype(BF16)
    lo = (cat - hi.astype(F32)).astype(BF16)
    width = cat.shape[1]
    inc = _mm(_tri(0).astype(BF16), jnp.concatenate([hi, lo], axis=1))
    inc = inc[:, :width] + inc[:, width:]
    out, off = [], 0
    for x, direction in parts:
        p = inc[:, off:off + x.shape[1]]
        off += x.shape[1]
        out.append(p if direction == 0 else p[CHUNK - 1:CHUNK, :] - p + x)
    return out


def _gate_decays(gc, direction):
    r, l = _ref_row(direction), _last_row(direction)
    g_ref = gc[r:r + 1, :]
    g_last = gc[l:l + 1, :]
    return (jnp.exp(gc - g_ref), jnp.exp(g_ref - gc), jnp.exp(gc), jnp.exp(g_last - gc), jnp.exp(g_last))


def _narrow_head_mask(h):
    lane = lax.broadcasted_iota(jnp.int32, (1, LANES), 1)
    return (lane // NARROW_DK) == (h % 2)


def _scan_ab_kernel(qa_f, fa_f, ia_f, qkb_f, vb_f, lr_f, qa_b, fa_b, ia_b, qkb_b, vb_b, lr_b,
                    lb_ref, u_ref, gkb_ref, oa_f, ob_f, oa_b, ob_b, st_ref):
    @pl.when(pl.program_id(1) == 0)
    def _():
        st_ref[...] = jnp.zeros_like(st_ref)

    n_chunks = SCAN_ROWS // CHUNK
    dirs = ((qa_f, fa_f, ia_f, qkb_f, vb_f, lr_f, oa_f, ob_f), (qa_b, fa_b, ia_b, qkb_b, vb_b, lr_b, oa_b, ob_b))

    nk = N_HEADS * NARROW_DK

    def chunk_body(c, carry):
        pre = []
        for d, (qa, fa, ia, qkb, vb, lr, oa, ob) in enumerate(dirs):
            r0 = pl.multiple_of((c if d == 0 else n_chunks - 1 - c) * CHUNK, CHUNK)
            rows = pl.ds(r0, CHUNK)
            lb = lb_ref[...]
            f = lb + (1.0 - lb) * _sigmoid(fa[rows, :])
            logits = _mm(lr[rows, :].astype(BF16), u_ref[:, d * nk:(d + 1) * nk]) + gkb_ref[:, d * nk:(d + 1) * nk]
            qk = qkb[rows, :]
            pre.append(dict(rows=rows, f=f, lf_a=jnp.log(f), lf_b=_log_sigmoid(logits) / GLA_GATE_NORMALIZER,
                            q_a=_silu(qa[rows, :]), v_a=ia[rows, :].astype(BF16),
                            q_b=qk[:, :nk] * (NARROW_DK ** -0.5), k_b=qk[:, nk:], v_b=vb[rows, :].astype(BF16)))
        gcs = _chunk_cumsum([(pre[0]["lf_a"], 0), (pre[0]["lf_b"], 0), (pre[1]["lf_a"], 1), (pre[1]["lf_b"], 1)])
        units = []
        for d, (qa, fa, ia, qkb, vb, lr, oa, ob) in enumerate(dirs):
            p = pre[d]
            rows = p["rows"]
            mask = _tri(d)
            intra = lambda s, mask=mask: jnp.where(mask, s, 0.0)
            eq, ek, ei, es, dec = _gate_decays(gcs[2 * d], d)
            k = 1.0 - p["f"]
            qe, ke, qi, ks = [a.astype(BF16) for a in (p["q_a"] * eq, k * ek, p["q_a"] * ei, k * es)]
            for h in range(N_HEADS):
                sl = slice(h * LANES, (h + 1) * LANES)

                def store(o, oa=oa, rows=rows, sl=sl):
                    oa[rows, sl] = o

                units.append(dict(qe=qe[:, sl], ke=ke[:, sl], qi=qi[:, sl], ks=ks[:, sl], v=p["v_a"][:, sl],
                                  dec=dec[:, sl], intra=intra, idx=d * 2 * N_HEADS + h, store=store))
            eq, ek, ei, es, dec = _gate_decays(gcs[2 * d + 1], d)
            qe, ke, qi, ks = [a.astype(BF16) for a in (p["q_b"] * eq, p["k_b"] * ek, p["q_b"] * ei, p["k_b"] * es)]
            for h in range(N_HEADS):
                psl = slice((h // 2) * LANES, (h // 2 + 1) * LANES)
                hm = _narrow_head_mask(h)
                pick = lambda a, psl=psl, hm=hm: jnp.where(hm, a[:, psl], jnp.zeros((), BF16))
                vsl = slice(h * HEAD_DV, (h + 1) * HEAD_DV)

                def store(o, ob=ob, rows=rows, vsl=vsl):
                    ob[rows, vsl] = o

                units.append(dict(qe=pick(qe), ke=pick(ke), qi=pick(qi), ks=pick(ks), v=p["v_b"][:, vsl],
                                  dec=dec[:, psl], intra=intra, idx=d * 2 * N_HEADS + N_HEADS + h, store=store))
        _run_units(units, st_ref)
        return carry

    lax.fori_loop(0, n_chunks, chunk_body, 0)


def _scan_ab(p3, lb, u_pad, gkb):
    bsz, t, _ = p3.shape
    nb = t // SCAN_ROWS

    def fwd(j, w=512):
        return pl.BlockSpec((None, SCAN_ROWS, w), lambda b, n: (b, n, j))

    def bwd(j, w=512):
        return pl.BlockSpec((None, SCAN_ROWS, w), lambda b, n: (b, nb - 1 - n, j))

    def const(shape):
        return pl.BlockSpec(shape, lambda b, n: (0,) * len(shape))

    out = jax.ShapeDtypeStruct((bsz, t, MIX_HALF), F32)
    o_f = pl.BlockSpec((None, SCAN_ROWS, MIX_HALF), lambda b, n: (b, n, 0))
    o_b = pl.BlockSpec((None, SCAN_ROWS, MIX_HALF), lambda b, n: (b, nb - 1 - n, 0))
    return pl.pallas_call(
        _scan_ab_kernel,
        out_shape=(out, out, out, out),
        grid=(bsz, nb),
        in_specs=[fwd(AB_QA), fwd(AB_FF), fwd(AB_IA), fwd(AB_QKB), fwd(AB_VB), fwd(AB_LR_BLOCK, LANES),
                  bwd(AB_QA), bwd(AB_FB), bwd(AB_IA), bwd(AB_QKB), bwd(AB_VB), bwd(AB_LR_BLOCK, LANES),
                  const(lb.shape), const(u_pad.shape), const(gkb.shape)],
        out_specs=(o_f, o_f, o_b, o_b),
        scratch_shapes=[pltpu.VMEM((4 * N_HEADS, HEAD_DV, LANES), F32)],
        compiler_params=_params(2),
        name="scan_ab",
    )(*([p3] * 12), lb, u_pad, gkb)


def _rotary(a, cos, sin_signed):
    lane = lax.broadcasted_iota(jnp.int32, (1, LANES), 1)
    first_half = (lane % NARROW_DK) < (NARROW_DK // 2)
    swapped = jnp.where(first_half, pltpu.roll(a, LANES - NARROW_DK // 2, 1), pltpu.roll(a, NARROW_DK // 2, 1))
    return a * cos + swapped * sin_signed


def _scan_c_kernel(qk_f, v_f, cos_f, sin_f, qk_b, v_b, cos_b, sin_b, dmat_ref, qsc_ref, ksc_ref, cd_ref,
                   o_f, o_b, st_ref):
    @pl.when(pl.program_id(1) == 0)
    def _():
        st_ref[...] = jnp.zeros_like(st_ref)

    n_chunks = SCAN_ROWS // CHUNK
    dirs = ((qk_f, v_f, cos_f, sin_f, o_f), (qk_b, v_b, cos_b, sin_b, o_b))

    def chunk_body(c, carry):
        units = []
        for d, (qk_ref, v_ref, cos_ref, sin_ref, o_ref) in enumerate(dirs):
            r0 = pl.multiple_of((c if d == 0 else n_chunks - 1 - c) * CHUNK, CHUNK)
            rows = pl.ds(r0, CHUNK)
            cos, sin_signed = cos_ref[rows, :], sin_ref[rows, :]
            v = v_ref[rows, :].astype(BF16)
            rot = [_rotary(qk_ref[rows, j * LANES:(j + 1) * LANES], cos, sin_signed) for j in range(4)]
            for h in range(N_HEADS):
                hm = _narrow_head_mask(h)
                q = jnp.where(hm, rot[h // 2] * (NARROW_DK ** -0.5), 0.0)
                k = jnp.where(hm, rot[2 + h // 2], 0.0)
                vsl = slice(h * HEAD_DV, (h + 1) * HEAD_DV)

                def store(o, o_ref=o_ref, rows=rows, vsl=vsl):
                    o_ref[rows, vsl] = o

                units.append(dict(qe=q.astype(BF16), ke=k.astype(BF16), qi=(q * qsc_ref[d, h]).astype(BF16),
                                  ks=(k * ksc_ref[d, h]).astype(BF16), v=v[:, vsl], dec=cd_ref[d, h][0:1, :],
                                  intra=lambda s, d=d, h=h: s * dmat_ref[d, h], idx=d * N_HEADS + h, store=store))
        _run_units(units, st_ref)
        return carry

    lax.fori_loop(0, n_chunks, chunk_body, 0)


def _retention_tables():
    pos = np.arange(CHUNK, dtype=np.float64)
    dmat = np.zeros((2, N_HEADS, CHUNK, CHUNK), np.float32)
    qsc = np.zeros((2, N_HEADS, CHUNK, LANES), np.float32)
    ksc = np.zeros((2, N_HEADS, CHUNK, LANES), np.float32)
    cd = np.zeros((2, N_HEADS, SUBLANES, LANES), np.float32)
    for d in range(2):
        for h in range(N_HEADS):
            lg = np.log1p(-np.exp2(-(5.0 + 2.0 * h + d)))
            dist = pos[:, None] - pos[None, :]
            if d == 0:
                dmat[d, h] = np.where(dist >= 0, np.exp(np.maximum(dist, 0.0) * lg), 0.0)
                qsc[d, h] = np.exp((pos + 1.0) * lg)[:, None]
                ksc[d, h] = np.exp((CHUNK - 1.0 - pos) * lg)[:, None]
            else:
                dmat[d, h] = np.where(dist <= 0, np.exp(np.maximum(-dist, 0.0) * lg), 0.0)
                qsc[d, h] = np.exp((CHUNK - pos) * lg)[:, None]
                ksc[d, h] = np.exp(pos * lg)[:, None]
            cd[d, h] = np.exp(CHUNK * lg)
    return jnp.asarray(dmat), jnp.asarray(qsc), jnp.asarray(ksc), jnp.asarray(cd)


def _rotary_tables(t):
    half = NARROW_DK // 2
    freqs = ROPE_THETA ** (-jnp.arange(half, dtype=F32) / half)
    ang = jnp.arange(t, dtype=F32)[:, None] * freqs[None, :]
    cos, sin = jnp.cos(ang), jnp.sin(ang)
    cos_t = jnp.tile(cos, (1, LANES // half))
    sin_t = jnp.tile(jnp.concatenate([-sin, sin], axis=1), (1, LANES // NARROW_DK))
    return cos_t.astype(F32), sin_t.astype(F32)


def _scan_c(p3):
    bsz, t, _ = p3.shape
    nb = t // SCAN_ROWS
    cos_t, sin_t = _rotary_tables(t)
    tables = _retention_tables()

    def fwd(j):
        return pl.BlockSpec((None, SCAN_ROWS, 512), lambda b, n: (b, n, j))

    def bwd(j):
        return pl.BlockSpec((None, SCAN_ROWS, 512), lambda b, n: (b, nb - 1 - n, j))

    tab_f = pl.BlockSpec((SCAN_ROWS, LANES), lambda b, n: (n, 0))
    tab_b = pl.BlockSpec((SCAN_ROWS, LANES), lambda b, n: (nb - 1 - n, 0))

    def const(shape):
        return pl.BlockSpec(shape, lambda b, n: (0,) * len(shape))

    out = jax.ShapeDtypeStruct((bsz, t, MIX_HALF), F32)
    return pl.pallas_call(
        _scan_c_kernel,
        out_shape=(out, out),
        grid=(bsz, nb),
        in_specs=[fwd(CD_QKC), fwd(CD_VC), tab_f, tab_f, bwd(CD_QKC), bwd(CD_VC), tab_b, tab_b]
        + [const(a.shape) for a in tables],
        out_specs=(pl.BlockSpec((None, SCAN_ROWS, MIX_HALF), lambda b, n: (b, n, 0)),
                   pl.BlockSpec((None, SCAN_ROWS, MIX_HALF), lambda b, n: (b, nb - 1 - n, 0))),
        scratch_shapes=[pltpu.VMEM((2 * N_HEADS, HEAD_DV, LANES), F32)],
        compiler_params=_params(2),
        name="scan_c",
    )(p3, p3, cos_t, sin_t, p3, p3, cos_t, sin_t, *tables)


def _conv_kernel(prev_ref, cur_ref, next_ref, w_ref, b_ref, o_ref, buf_ref):
    i = pl.program_id(1)
    last = pl.num_programs(1) - 1
    rows = cur_ref.shape[0]
    buf_ref[0:SUBLANES, :] = jnp.where(i == 0, 0.0, prev_ref[...])
    buf_ref[SUBLANES:SUBLANES + rows, :] = cur_ref[...]
    buf_ref[SUBLANES + rows:2 * SUBLANES + rows, :] = jnp.where(i == last, 0.0, next_ref[...])
    y = (buf_ref[SUBLANES - 1:SUBLANES - 1 + rows, :] * w_ref[0:1, :]
         + buf_ref[SUBLANES:SUBLANES + rows, :] * w_ref[1:2, :]
         + buf_ref[SUBLANES + 1:SUBLANES + 1 + rows, :] * w_ref[2:3, :]
         + b_ref[...])
    o_ref[...] = _silu(y)


def _conv_silu(p3, conv_w, conv_b):
    bsz, t, _ = p3.shape
    nb = t // ROW_TILE
    per = ROW_TILE // SUBLANES
    outs = []
    for j in (CD_QD, CD_KD):
        lo = (j - CD_QD) * 512
        outs.append(pl.pallas_call(
            _conv_kernel,
            out_shape=jax.ShapeDtypeStruct((bsz, t, 512), F32),
            grid=(bsz, nb),
            in_specs=[pl.BlockSpec((None, SUBLANES, 512), lambda b, i, j=j: (b, jnp.maximum(i * per - 1, 0), j)),
                      pl.BlockSpec((None, ROW_TILE, 512), lambda b, i, j=j: (b, i, j)),
                      pl.BlockSpec((None, SUBLANES, 512),
                                   lambda b, i, j=j: (b, jnp.minimum((i + 1) * per, t // SUBLANES - 1), j)),
                      pl.BlockSpec((3, 512), lambda b, i: (0, 0)),
                      pl.BlockSpec((1, 512), lambda b, i: (0, 0))],
            out_specs=pl.BlockSpec((None, ROW_TILE, 512), lambda b, i: (b, i, 0)),
            scratch_shapes=[pltpu.VMEM((ROW_TILE + 2 * SUBLANES, 512), F32)],
            compiler_params=_params(2),
            name="conv_silu",
        )(p3, p3, p3, conv_w[:, lo:lo + 512], conv_b[None, lo:lo + 512]))
    return outs


def _scan_d_kernel(q_f, k_f, v_f, g_f, q_b, k_b, v_b, g_b, fgb_ref, o_f, o_b, ct_ref, m_ref):
    @pl.when(pl.program_id(1) == 0)
    def _():
        ct_ref[...] = jnp.zeros_like(ct_ref)
        m_ref[...] = jnp.zeros_like(m_ref)

    n_chunks = SCAN_ROWS // CHUNK
    dirs = ((q_f, k_f, v_f, g_f, o_f), (q_b, k_b, v_b, g_b, o_b))
    lane = lax.broadcasted_iota(jnp.int32, (CHUNK, LANES), 1)
    ones_col = jnp.where(lane == 0, 1.0, 0.0).astype(BF16)

    def chunk_body(c, carry):
        pre = []
        for d, (q_ref, k_ref, v_ref, g_ref, o_ref) in enumerate(dirs):
            r0 = pl.multiple_of((c if d == 0 else n_chunks - 1 - c) * CHUNK, CHUNK)
            rows = pl.ds(r0, CHUNK)
            g = g_ref[rows, :]
            pre.append((rows, g, _log_sigmoid(g + fgb_ref[...])))
        bcs = _chunk_cumsum([(pre[0][2], 0), (pre[1][2], 1)])
        units = []
        for d, (q_ref, k_ref, v_ref, g_ref, o_ref) in enumerate(dirs):
            rows, g, _ = pre[d]
            bc = bcs[d]
            mask = _tri(d)
            last = _last_row(d)
            g_t = g.T
            bc_t = bc.T
            for h in range(N_HEADS):
                u = d * N_HEADS + h
                ci = d * 2 * N_HEADS + h
                cf = ci + N_HEADS
                sl = slice(h * HEAD_DV, (h + 1) * HEAD_DV)
                i_col, b_col = g[:, ci:ci + 1], bc[:, cf:cf + 1]
                i_row, b_row = g_t[ci:ci + 1, :], bc_t[cf:cf + 1, :]
                b_last = b_col[last:last + 1, :]
                m_st = m_ref[u][0:1, 0:1]
                log_d = jnp.where(mask, b_col - b_row + i_row, -jnp.inf)
                a = b_col + m_st
                m_i = jnp.maximum(a, jnp.max(log_d, axis=-1, keepdims=True))
                lw = b_last - b_col + i_col
                m_new = jnp.maximum(b_last + m_st, jnp.max(lw, axis=0, keepdims=True))
                q = (q_ref[rows, sl] * (HEAD_DV ** -0.5)).astype(BF16)
                k = k_ref[rows, sl]
                v_ext = jnp.concatenate([v_ref[rows, sl].astype(BF16), ones_col], axis=1)
                wk = (k * jnp.exp(lw - m_new)).astype(BF16)
                units.append(dict(u=u, rows=rows, sl=sl, o_ref=o_ref, v_ext=v_ext, m_i=m_i, m_new=m_new,
                                  d_intra=jnp.exp(log_d - m_i), s_inter=jnp.exp(a - m_i),
                                  s_old=jnp.exp(b_last + m_st - m_new),
                                  qk=_nt(q, k.astype(BF16)), inter=_nt(q, ct_ref[u].astype(BF16)),
                                  upd=_tn(v_ext, wk)))
        for t in units:
            num = t["s_inter"] * t["inter"] + _mm((t["qk"] * t["d_intra"]).astype(BF16), t["v_ext"])
            den = num[:, HEAD_DV:HEAD_DV + 1]
            t["o_ref"][t["rows"], t["sl"]] = num[:, :HEAD_DV] / jnp.maximum(jnp.abs(den), jnp.exp(-t["m_i"]))
            ct_ref[t["u"]] = t["s_old"] * ct_ref[t["u"]] + t["upd"]
            m_ref[t["u"]] = jnp.broadcast_to(t["m_new"], (SUBLANES, LANES))
        return carry

    lax.fori_loop(0, n_chunks, chunk_body, 0)


def _scan_d(qd, kd, p3, fgb_row):
    bsz, t, _ = p3.shape
    nb = t // SCAN_ROWS

    def spec(j, w, flip):
        if flip:
            return pl.BlockSpec((None, SCAN_ROWS, w), lambda b, n: (b, nb - 1 - n, j))
        return pl.BlockSpec((None, SCAN_ROWS, w), lambda b, n: (b, n, j))

    out = jax.ShapeDtypeStruct((bsz, t, MIX_HALF), F32)
    in_specs = []
    for flip in (False, True):
        in_specs += [spec(0, 512, flip), spec(0, 512, flip), spec(CD_VD, 512, flip), spec(CD_GATE_BLOCK, LANES, flip)]
    in_specs.append(pl.BlockSpec((1, LANES), lambda b, n: (0, 0)))
    return pl.pallas_call(
        _scan_d_kernel,
        out_shape=(out, out),
        grid=(bsz, nb),
        in_specs=in_specs,
        out_specs=(spec(0, MIX_HALF, False), spec(0, MIX_HALF, True)),
        scratch_shapes=[pltpu.VMEM((2 * N_HEADS, 2 * HEAD_DV, LANES), F32),
                        pltpu.VMEM((2 * N_HEADS, SUBLANES, LANES), F32)],
        compiler_params=_params(2),
        name="scan_d",
    )(qd, kd, p3, p3, qd, kd, p3, p3, fgb_row)


def _layer_norm(z, g, b):
    zc = z - jnp.mean(z, axis=-1, keepdims=True)
    var = jnp.mean(zc * zc, axis=-1, keepdims=True)
    return zc * lax.rsqrt(var + LN_EPS) * g + b


def _mix_out_kernel(o1f, o1b, o2f, o2b, g1, g2, x_ref, norm_ref, w_ref, lng, lnb, h_ref, *, second_gate):
    parts = []
    for of, ob, g_ref, gate_fn, base in ((o1f, o1b, g1, _silu, 0), (o2f, o2b, g2, second_gate, MIX_HALF)):
        o = of[...] + ob[...]
        gate = gate_fn(g_ref[...])
        for h in range(N_HEADS):
            sl = slice(h * HEAD_DV, (h + 1) * HEAD_DV)
            oh = o[:, sl]
            oh = oh * lax.rsqrt(jnp.mean(oh * oh, axis=-1, keepdims=True) + NORM_EPS)
            parts.append((oh * norm_ref[:, base + h * HEAD_DV:base + (h + 1) * HEAD_DV] * gate[:, sl]).astype(BF16))
    mixed = jnp.concatenate(parts, axis=1)
    z = ALPHA * x_ref[...] + _mm(mixed, w_ref[...])
    hn = _layer_norm(z, lng[...], lnb[...])
    h_ref[...] = hn


def _mix_out(o1f, o1b, o2f, o2b, p2d, gate_blocks, x2d, norm, w_out, ln_g, ln_b, second_gate):
    n = x2d.shape[0]
    half = pl.BlockSpec((ROW_TILE, MIX_HALF), lambda i: (i, 0))
    full = pl.BlockSpec((ROW_TILE, D_MODEL), lambda i: (i, 0))
    row = pl.BlockSpec((1, D_MODEL), lambda i: (0, 0))
    ga, gb = gate_blocks
    return pl.pallas_call(
        functools.partial(_mix_out_kernel, second_gate=second_gate),
        out_shape=jax.ShapeDtypeStruct((n, D_MODEL), F32),
        grid=(n // ROW_TILE,),
        in_specs=[half, half, half, half,
                  pl.BlockSpec((ROW_TILE, MIX_HALF), lambda i: (i, ga)),
                  pl.BlockSpec((ROW_TILE, MIX_HALF), lambda i: (i, gb)),
                  full, row, pl.BlockSpec((D_MODEL, D_MODEL), lambda i: (0, 0)), row, row],
        out_specs=full,
        compiler_params=_params(1),
        name="mix_out",
    )(o1f, o1b, o2f, o2b, p2d, p2d, x2d, norm, w_out, ln_g, ln_b)


def _router_kernel(h_ref, rw_ref, rb_ref, idx_ref, gate_ref):
    logits = lax.dot_general(rw_ref[...], h_ref[...], (((1,), (1,)), ((), ())),
                             precision=HIGHEST, preferred_element_type=F32)
    rows = [logits[e:e + 1, :] for e in range(N_EXPERTS)]
    mx = functools.reduce(jnp.maximum, rows)
    ex = [jnp.exp(r - mx) for r in rows]
    tot = functools.reduce(lambda a, b: a + b, ex)
    score = [e / tot for e in ex]
    biased = [score[e] + rb_ref[e:e + 1, :] for e in range(N_EXPERTS)]

    def argmax_first(vals):
        best, idx = vals[0], jnp.zeros(vals[0].shape, jnp.int32)
        for j in range(1, len(vals)):
            upd = vals[j] > best
            best = jnp.where(upd, vals[j], best)
            idx = jnp.where(upd, j, idx)
        return best, idx

    def pick(vals, idx):
        out = vals[0]
        for j in range(1, len(vals)):
            out = jnp.where(idx == j, vals[j], out)
        return out

    group_scores = []
    for g in range(N_GROUPS):
        a = biased[g * EXPERTS_PER_GROUP:(g + 1) * EXPERTS_PER_GROUP]
        pairs = [a[i] + a[j] for i in range(EXPERTS_PER_GROUP) for j in range(i + 1, EXPERTS_PER_GROUP)]
        group_scores.append(functools.reduce(jnp.maximum, pairs))
    _, g_sel = argmax_first(group_scores)
    in_b = [pick([biased[g * EXPERTS_PER_GROUP + k] for g in range(N_GROUPS)], g_sel) for k in range(EXPERTS_PER_GROUP)]
    in_s = [pick([score[g * EXPERTS_PER_GROUP + k] for g in range(N_GROUPS)], g_sel) for k in range(EXPERTS_PER_GROUP)]
    _, i0 = argmax_first(in_b)
    _, i1 = argmax_first([jnp.where(i0 == k, -jnp.inf, in_b[k]) for k in range(EXPERTS_PER_GROUP)])
    s0, s1 = pick(in_s, i0), pick(in_s, i1)
    den = s0 + s1
    idx_ref[0:1, :] = g_sel * EXPERTS_PER_GROUP + i0
    idx_ref[1:2, :] = g_sel * EXPERTS_PER_GROUP + i1
    gate_ref[0:1, :] = s0 / den
    gate_ref[1:2, :] = s1 / den


def _router(h2d, rw_t, rb_col):
    n = h2d.shape[0]
    tile = min(ROUTER_TILE, n)
    return pl.pallas_call(
        _router_kernel,
        out_shape=(jax.ShapeDtypeStruct((TOP_K, n), jnp.int32), jax.ShapeDtypeStruct((TOP_K, n), F32)),
        grid=(n // tile,),
        in_specs=[pl.BlockSpec((tile, D_MODEL), lambda i: (i, 0)),
                  pl.BlockSpec((N_EXPERTS, D_MODEL), lambda i: (0, 0)),
                  pl.BlockSpec((N_EXPERTS, 1), lambda i: (0, 0))],
        out_specs=(pl.BlockSpec((TOP_K, tile), lambda i: (0, i)), pl.BlockSpec((TOP_K, tile), lambda i: (0, i))),
        compiler_params=_params(1),
        name="router",
    )(h2d, rw_t, rb_col)


def _plan_kernel(idx_ref, dest_ref, cnt_ref, su_ref, run_ref, start_ref):
    phase, i = pl.program_id(0), pl.program_id(1)
    tile = idx_ref.shape[1]
    e_iota = lax.broadcasted_iota(jnp.int32, (N_EXPERTS, tile), 0)
    oh0 = idx_ref[0:1, :] == e_iota
    oh1 = idx_ref[1:2, :] == e_iota
    member = jnp.where(oh0 | oh1, 1.0, 0.0)
    tile_count = jnp.broadcast_to(jnp.sum(member, axis=1, keepdims=True), (N_EXPERTS, LANES))

    @pl.when((phase == 0) & (i == 0))
    def _():
        run_ref[...] = jnp.zeros_like(run_ref)
        r = lax.broadcasted_iota(jnp.int32, (tile, tile), 0)
        c = lax.broadcasted_iota(jnp.int32, (tile, tile), 1)
        su_ref[...] = jnp.where(r < c, 1.0, 0.0).astype(BF16)

    @pl.when((phase == 1) & (i == 0))
    def _():
        counts = run_ref[...]
        cnt_ref[...] = counts.astype(jnp.int32)
        padded = jnp.ceil(counts * (1.0 / EXPERT_ROWS)) * EXPERT_ROWS
        r = lax.broadcasted_iota(jnp.int32, (N_EXPERTS, N_EXPERTS), 0)
        c = lax.broadcasted_iota(jnp.int32, (N_EXPERTS, N_EXPERTS), 1)
        start_ref[...] = jnp.dot(jnp.where(c < r, 1.0, 0.0), padded, precision=HIGHEST, preferred_element_type=F32)
        run_ref[...] = jnp.zeros_like(run_ref)

    @pl.when(phase == 1)
    def _():
        before = _mm(member.astype(BF16), su_ref[...])
        pos = start_ref[:, 0:1] + run_ref[:, 0:1] + before
        dest_ref[0:1, :] = jnp.sum(jnp.where(oh0, pos, 0.0), axis=0, keepdims=True).astype(jnp.int32)
        dest_ref[1:2, :] = jnp.sum(jnp.where(oh1, pos, 0.0), axis=0, keepdims=True).astype(jnp.int32)

    run_ref[...] += tile_count


def _plan(idx):
    n = idx.shape[1]
    tile = min(ROUTER_TILE, n)
    return pl.pallas_call(
        _plan_kernel,
        out_shape=(jax.ShapeDtypeStruct((TOP_K, n), jnp.int32), jax.ShapeDtypeStruct((N_EXPERTS, LANES), jnp.int32)),
        grid=(2, n // tile),
        in_specs=[pl.BlockSpec((TOP_K, tile), lambda p, i: (0, i))],
        out_specs=(pl.BlockSpec((TOP_K, tile), lambda p, i: (0, i * p)),
                   pl.BlockSpec((N_EXPERTS, LANES), lambda p, i: (0, 0))),
        scratch_shapes=[pltpu.VMEM((tile, tile), BF16), pltpu.VMEM((N_EXPERTS, LANES), F32),
                        pltpu.VMEM((N_EXPERTS, LANES), F32)],
        compiler_params=_params(2),
        name="moe_plan",
    )(idx)


def _invert_kernel(pad_ref, dest_ref, tok_ref):
    n = dest_ref.shape[0] // TOP_K

    def clear(r, carry):
        tok_ref[r] = 0
        return carry

    for e in range(N_EXPERTS + 1):
        lax.fori_loop(pad_ref[0, e], pad_ref[1, e], clear, 0)

    for k in range(TOP_K):
        def place(t, carry, k=k):
            tok_ref[dest_ref[k * n + t]] = t
            return carry

        lax.fori_loop(0, n, place, 0, unroll=16)


def _invert(pad_ranges, dest_flat, n_rows):
    smem = pl.BlockSpec(memory_space=pltpu.SMEM)
    return pl.pallas_call(
        _invert_kernel,
        out_shape=jax.ShapeDtypeStruct((n_rows,), jnp.int32),
        in_specs=[smem, smem],
        out_specs=smem,
        name="moe_invert",
    )(pad_ranges, dest_flat)


def _expert_kernel(be_ref, nv_ref, x_ref, w1_ref, w3_ref, w2_ref, y_ref, wb_ref):
    i = pl.program_id(0)
    valid = i < nv_ref[0]
    new_expert = (i == 0) | (be_ref[i] != be_ref[jnp.maximum(i - 1, 0)])

    @pl.when(valid & new_expert)
    def _():
        wb_ref[0] = w1_ref[...].astype(BF16)
        wb_ref[1] = w3_ref[...].astype(BF16)
        wb_ref[2] = w2_ref[...].astype(BF16)

    @pl.when(valid)
    def _():
        x = x_ref[...].astype(BF16)
        hid = _silu(_mm(x, wb_ref[0])) * _mm(x, wb_ref[1])
        y_ref[...] = _mm(hid.astype(BF16), wb_ref[2])

    @pl.when(jnp.logical_not(valid))
    def _():
        y_ref[...] = jnp.zeros_like(y_ref)


def _experts(block_expert, n_valid, xs, w1, w3, w2, layer):
    n_rows = xs.shape[0]
    wspec = pl.BlockSpec((None, None, D_MODEL, D_MODEL), lambda i, be, nv: (layer, be[i], 0, 0))
    return pl.pallas_call(
        _expert_kernel,
        out_shape=jax.ShapeDtypeStruct((n_rows, D_MODEL), F32),
        grid_spec=pltpu.PrefetchScalarGridSpec(
            num_scalar_prefetch=2,
            grid=(n_rows // EXPERT_ROWS,),
            in_specs=[pl.BlockSpec((EXPERT_ROWS, D_MODEL), lambda i, be, nv: (i, 0)), wspec, wspec, wspec],
            out_specs=pl.BlockSpec((EXPERT_ROWS, D_MODEL), lambda i, be, nv: (i, 0)),
            scratch_shapes=[pltpu.VMEM((3, D_MODEL, D_MODEL), BF16)]),
        compiler_params=_params(1),
        name="experts",
    )(block_expert, n_valid, xs, w1, w3, w2)


def _combine_kernel(h_ref, y0_ref, y1_ref, g_ref, lng, lnb, o_ref):
    g = g_ref[...]
    z = ALPHA * h_ref[...] + (g[:, 0:1] * y0_ref[...] + g[:, 1:2] * y1_ref[...])
    o_ref[...] = _layer_norm(z, lng[...], lnb[...])


def _combine(h2d, y_rows, gates_t, ln_g, ln_b):
    n = h2d.shape[0]
    nt = n // ROW_TILE
    full = pl.BlockSpec((ROW_TILE, D_MODEL), lambda i: (i, 0))
    row = pl.BlockSpec((1, D_MODEL), lambda i: (0, 0))
    return pl.pallas_call(
        _combine_kernel,
        out_shape=jax.ShapeDtypeStruct((n, D_MODEL), F32),
        grid=(nt,),
        in_specs=[full, full, pl.BlockSpec((ROW_TILE, D_MODEL), lambda i: (nt + i, 0)),
                  pl.BlockSpec((ROW_TILE, TOP_K), lambda i: (i, 0)), row, row],
        out_specs=full,
        compiler_params=_params(1),
        name="moe_combine",
    )(h2d, y_rows, y_rows, gates_t, ln_g, ln_b)


def _moe(h2d, rw_t, rb_col, w1, w3, w2, layer, ln_g, ln_b):
    n = h2d.shape[0]
    idx, gates = _router(h2d, rw_t, rb_col)
    dest, counts = _plan(idx)
    n_rows = n * TOP_K + N_EXPERTS * EXPERT_ROWS
    n_blocks = n_rows // EXPERT_ROWS
    counts = counts[:, 0]
    padded = (counts + EXPERT_ROWS - 1) // EXPERT_ROWS * EXPERT_ROWS
    pad_end = jnp.cumsum(padded)
    block_start = jnp.arange(n_blocks, dtype=jnp.int32) * EXPERT_ROWS
    block_expert = jnp.minimum(jnp.sum((pad_end[None, :] <= block_start[:, None]).astype(jnp.int32), axis=1),
                               N_EXPERTS - 1)
    n_valid = (pad_end[-1:] // EXPERT_ROWS).astype(jnp.int32)
    pad_ranges = jnp.stack([jnp.concatenate([pad_end - padded + counts, pad_end[-1:]]),
                            jnp.concatenate([pad_end, jnp.full((1,), n_rows, jnp.int32)])]).astype(jnp.int32)
    dest_flat = dest.reshape(-1)
    xs = h2d[_invert(pad_ranges, dest_flat, n_rows)]
    y = _experts(block_expert, n_valid, xs, w1, w3, w2, layer)
    return _combine(h2d, y[dest_flat], gates.T, ln_g, ln_b)


def _pad_cols(w, b, main, width):
    k = w.shape[0]
    tail = w.shape[1] - main
    w_p = jnp.concatenate([w[:, :main], w[:, main:], jnp.zeros((k, width - main - tail), w.dtype)], axis=1)
    b_p = jnp.concatenate([b[:main], b[main:], jnp.zeros((width - main - tail,), b.dtype)])
    return w_p.astype(BF16), b_p[None, :].astype(F32)


def kernel(x, ab_w_in, ab_b_in, hgrn_lb, gla_gk_up, gla_gk_b, hgrn_norm, gla_norm, ab_w_out, cd_w_in, cd_b_in,
           mlstm_conv_w, mlstm_conv_b, mlstm_fgate_b, ret_norm, mlstm_norm, cd_w_out, ln_mix_g, ln_mix_b, ln_ffn_g,
           ln_ffn_b, router_w, router_b, moe_w1, moe_w3, moe_w2):
    bsz, t, d = x.shape
    n = bsz * t
    lower_bounds = jnp.cumsum(jax.nn.softmax(hgrn_lb.astype(F32), axis=0), axis=0)
    rw_t = router_w.T.astype(F32)
    rb_col = router_b.astype(F32)[:, None]
    h = x.reshape(n, d)
    for layer in range(DEPTH):
        j = layer // 2
        if layer % 2 == 0:
            w_p, b_p = _pad_cols(ab_w_in[j], ab_b_in[j], 8 * 512, AB_WIDTH)
            p = _in_proj(h, w_p, b_p)
            p3 = p.reshape(bsz, t, AB_WIDTH)
            nk = N_HEADS * NARROW_DK
            u_pad = jnp.zeros((LANES, 2 * nk), F32)
            u_pad = u_pad.at[:GLA_LOWRANK, :nk].set(gla_gk_up[j, 0])
            u_pad = u_pad.at[GLA_LOWRANK:2 * GLA_LOWRANK, nk:].set(gla_gk_up[j, 1]).astype(BF16)
            gkb = jnp.concatenate([gla_gk_b[j, 0], gla_gk_b[j, 1]])[None, :].astype(F32)
            oa_f, ob_f, oa_b, ob_b = _scan_ab(p3, lower_bounds[layer][None, :], u_pad, gkb)
            norm = jnp.concatenate([hgrn_norm[j], gla_norm[j]])[None, :].astype(F32)
            outs = [a.reshape(n, MIX_HALF) for a in (oa_f, oa_b, ob_f, ob_b)]
            h = _mix_out(*outs, p, (AB_GA, AB_GB), h, norm, ab_w_out[j].astype(BF16),
                                 ln_mix_g[layer][None, :], ln_mix_b[layer][None, :], _silu)
        else:
            w_p, b_p = _pad_cols(cd_w_in[j], cd_b_in[j], 7 * 512, CD_WIDTH)
            p = _in_proj(h, w_p, b_p)
            p3 = p.reshape(bsz, t, CD_WIDTH)
            oc_f, oc_b = _scan_c(p3)
            qd, kd = _conv_silu(p3, mlstm_conv_w[j].astype(F32), mlstm_conv_b[j].astype(F32))
            fgb_row = jnp.zeros((LANES,), F32)
            fgb_row = fgb_row.at[N_HEADS:2 * N_HEADS].set(mlstm_fgate_b[j, 0])
            fgb_row = fgb_row.at[3 * N_HEADS:4 * N_HEADS].set(mlstm_fgate_b[j, 1])[None, :]
            od_f, od_b = _scan_d(qd, kd, p3, fgb_row)
            norm = jnp.concatenate([ret_norm[j], mlstm_norm[j]])[None, :].astype(F32)
            outs = [a.reshape(n, MIX_HALF) for a in (oc_f, oc_b, od_f, od_b)]
            h = _mix_out(*outs, p, (CD_GC, CD_OD), h, norm, cd_w_out[j].astype(BF16),
                                 ln_mix_g[layer][None, :], ln_mix_b[layer][None, :], _sigmoid)
        h = _moe(h, rw_t, rb_col, moe_w1, moe_w3, moe_w2, layer,
                 ln_ffn_g[layer][None, :], ln_ffn_b[layer][None, :])
    return h.reshape(bsz, t, d)
```

```python
import functools

import jax
import jax.numpy as jnp
import numpy as np
from jax import lax
from jax.experimental import pallas as pl
from jax.experimental.pallas import tpu as pltpu

F32 = jnp.float32
BF16 = jnp.bfloat16
HIGHEST = lax.Precision.HIGHEST

D_MODEL = 1024
DEPTH = 2
CHUNK = 64
MIX_HALF = D_MODEL // 2
N_HEADS = 4
HEAD_DV = MIX_HALF // N_HEADS
NARROW_DK = 64
GLA_LOWRANK = 16
GLA_GATE_NORMALIZER = 16.0
ROPE_THETA = 10000.0
N_EXPERTS = 16
N_GROUPS = 4
EXPERTS_PER_GROUP = N_EXPERTS // N_GROUPS
TOP_K = 2
ALPHA = (2.0 * DEPTH) ** 0.25
LN_EPS = 1e-5
NORM_EPS = 1e-6

LANES = 128
SUBLANES = 8
VMEM_LIMIT = 52 * 1024 * 1024

ROW_TILE = 256
SCAN_ROWS = 256
ROUTER_TILE = 1024
EXPERT_ROWS = 256

AB_QA, AB_FF, AB_FB, AB_IA, AB_GA, AB_QKB, AB_VB, AB_GB = range(8)
AB_LR_BLOCK = 32
AB_WIDTH = 8 * 512 + LANES
CD_QKC, CD_VC, CD_GC, CD_QD, CD_KD, CD_VD, CD_OD = range(7)
CD_GATE_I_BLOCK = 28
CD_GATE_F_BLOCK = 29
CD_WIDTH = 7 * 512 + 2 * LANES
GATE_LANES = 16


def _params(n_axes, vmem=VMEM_LIMIT):
    return pltpu.CompilerParams(dimension_semantics=("arbitrary",) * n_axes, vmem_limit_bytes=vmem)


def _nt(a, b):
    return lax.dot_general(a, b, (((1,), (1,)), ((), ())), preferred_element_type=F32)


def _tn(a, b):
    return lax.dot_general(a, b, (((0,), (0,)), ((), ())), preferred_element_type=F32)


def _mm(a, b):
    return jnp.dot(a, b, preferred_element_type=F32)


def _sigmoid(x):
    return 1.0 / (1.0 + jnp.exp(-x))


def _silu(x):
    return x * _sigmoid(x)


def _log_sigmoid(x):
    return jnp.minimum(x, 0.0) - jnp.log(1.0 + jnp.exp(-jnp.abs(x)))


def _tri(direction):
    row = lax.broadcasted_iota(jnp.int32, (CHUNK, CHUNK), 0)
    col = lax.broadcasted_iota(jnp.int32, (CHUNK, CHUNK), 1)
    return (col <= row) if direction == 0 else (col >= row)


def _ref_row(direction):
    return CHUNK // 2 - 1 if direction == 0 else CHUNK // 2


def _last_row(direction):
    return CHUNK - 1 if direction == 0 else 0


def _proj_kernel(x_ref, w_ref, b_ref, o_ref):
    o_ref[...] = _mm(x_ref[...].astype(BF16), w_ref[...]) + b_ref[...]


def _in_proj(x2d, w, b):
    n, k = x2d.shape
    m = w.shape[1]
    return pl.pallas_call(
        _proj_kernel,
        out_shape=jax.ShapeDtypeStruct((n, m), F32),
        grid=(n // ROW_TILE,),
        in_specs=[pl.BlockSpec((ROW_TILE, k), lambda i: (i, 0)),
                  pl.BlockSpec((k, m), lambda i: (0, 0)),
                  pl.BlockSpec((1, m), lambda i: (0, 0))],
        out_specs=pl.BlockSpec((ROW_TILE, m), lambda i: (i, 0)),
        compiler_params=_params(1),
        name="in_proj",
    )(x2d, w, b)


def _run_units(units, st_ref):
    first = [(_nt(u["qe"], u["ke"]), _tn(u["v"], u["ks"]), _nt(u["qi"], st_ref[u["idx"]].astype(BF16)))
             for u in units]
    for u, (s, upd, inter) in zip(units, first):
        u["store"](_mm(u["intra"](s).astype(BF16), u["v"]) + inter)
        st_ref[u["idx"]] = st_ref[u["idx"]] * u["dec"] + upd


def _chunk_cumsum(parts):
    cat = jnp.concatenate([x for x, _ in parts], axis=1)
    hi = cat.astype(BF16)
    lo = (cat - hi.astype(F32)).astype(BF16)
    width = cat.shape[1]
    inc = _mm(_tri(0).astype(BF16), jnp.concatenate([hi, lo], axis=1))
    inc = inc[:, :width] + inc[:, width:]
    out, off = [], 0
    for x, direction in parts:
        p = inc[:, off:off + x.shape[1]]
        off += x.shape[1]
        out.append(p if direction == 0 else p[CHUNK - 1:CHUNK, :] - p + x)
    return out


def _gate_decays(gc, direction):
    r, l = _ref_row(direction), _last_row(direction)
    g_ref = gc[r:r + 1, :]
    g_last = gc[l:l + 1, :]
    return (jnp.exp(gc - g_ref), jnp.exp(g_ref - gc), jnp.exp(gc), jnp.exp(g_last - gc), jnp.exp(g_last))


def _narrow_head_mask(h):
    lane = lax.broadcasted_iota(jnp.int32, (1, LANES), 1)
    return (lane // NARROW_DK) == (h % 2)


def _scan_ab_kernel(qa_f, fa_f, ia_f, qkb_f, vb_f, lr_f, qa_b, fa_b, ia_b, qkb_b, vb_b, lr_b,
                    lb_ref, u_ref, gkb_ref, oa_f, ob_f, oa_b, ob_b, st_ref):
    @pl.when(pl.program_id(1) == 0)
    def _():
        st_ref[...] = jnp.zeros_like(st_ref)

    n_chunks = SCAN_ROWS // CHUNK
    dirs = ((qa_f, fa_f, ia_f, qkb_f, vb_f, lr_f, oa_f, ob_f), (qa_b, fa_b, ia_b, qkb_b, vb_b, lr_b, oa_b, ob_b))

    nk = N_HEADS * NARROW_DK

    def chunk_body(c, carry):
        pre = []
        for d, (qa, fa, ia, qkb, vb, lr, oa, ob) in enumerate(dirs):
            r0 = pl.multiple_of((c if d == 0 else n_chunks - 1 - c) * CHUNK, CHUNK)
            rows = pl.ds(r0, CHUNK)
            lb = lb_ref[...]
            f = lb + (1.0 - lb) * _sigmoid(fa[rows, :])
            logits = _mm(lr[rows, :].astype(BF16), u_ref[:, d * nk:(d + 1) * nk]) + gkb_ref[:, d * nk:(d + 1) * nk]
            qk = qkb[rows, :]
            pre.append(dict(rows=rows, f=f, lf_a=jnp.log(f), lf_b=_log_sigmoid(logits) / GLA_GATE_NORMALIZER,
                            q_a=_silu(qa[rows, :]), v_a=ia[rows, :].astype(BF16),
                            q_b=qk[:, :nk] * (NARROW_DK ** -0.5), k_b=qk[:, nk:], v_b=vb[rows, :].astype(BF16)))
        gcs = _chunk_cumsum([(pre[0]["lf_a"], 0), (pre[0]["lf_b"], 0), (pre[1]["lf_a"], 1), (pre[1]["lf_b"], 1)])
        units = []
        for d, (qa, fa, ia, qkb, vb, lr, oa, ob) in enumerate(dirs):
            p = pre[d]
            rows = p["rows"]
            mask = _tri(d)
            intra = lambda s, mask=mask: jnp.where(mask, s, 0.0)
            eq, ek, ei, es, dec = _gate_decays(gcs[2 * d], d)
            k = 1.0 - p["f"]
            qe, ke, qi, ks = [a.astype(BF16) for a in (p["q_a"] * eq, k * ek, p["q_a"] * ei, k * es)]
            for h in range(N_HEADS):
                sl = slice(h * LANES, (h + 1) * LANES)

                def store(o, oa=oa, rows=rows, sl=sl):
                    oa[rows, sl] = o

                units.append(dict(qe=qe[:, sl], ke=ke[:, sl], qi=qi[:, sl], ks=ks[:, sl], v=p["v_a"][:, sl],
                                  dec=dec[:, sl], intra=intra, idx=d * 2 * N_HEADS + h, store=store))
            eq, ek, ei, es, dec = _gate_decays(gcs[2 * d + 1], d)
            qe, ke, qi, ks = [a.astype(BF16) for a in (p["q_b"] * eq, p["k_b"] * ek, p["q_b"] * ei, p["k_b"] * es)]
            for h in range(N_HEADS):
                psl = slice((h // 2) * LANES, (h // 2 + 1) * LANES)
                hm = _narrow_head_mask(h)
                pick = lambda a, psl=psl, hm=hm: jnp.where(hm, a[:, psl], jnp.zeros((), BF16))
                vsl = slice(h * HEAD_DV, (h + 1) * HEAD_DV)

                def store(o, ob=ob, rows=rows, vsl=vsl):
                    ob[rows, vsl] = o

                units.append(dict(qe=pick(qe), ke=pick(ke), qi=pick(qi), ks=pick(ks), v=p["v_b"][:, vsl],
                                  dec=dec[:, psl], intra=intra, idx=d * 2 * N_HEADS + N_HEADS + h, store=store))
        _run_units(units, st_ref)
        return carry

    lax.fori_loop(0, n_chunks, chunk_body, 0)


def _scan_ab(p3, lb, u_pad, gkb):
    bsz, t, _ = p3.shape
    nb = t // SCAN_ROWS

    def fwd(j, w=512):
        return pl.BlockSpec((None, SCAN_ROWS, w), lambda b, n: (b, n, j))

    def bwd(j, w=512):
        return pl.BlockSpec((None, SCAN_ROWS, w), lambda b, n: (b, nb - 1 - n, j))

    def const(shape):
        return pl.BlockSpec(shape, lambda b, n: (0,) * len(shape))

    out = jax.ShapeDtypeStruct((bsz, t, MIX_HALF), F32)
    o_f = pl.BlockSpec((None, SCAN_ROWS, MIX_HALF), lambda b, n: (b, n, 0))
    o_b = pl.BlockSpec((None, SCAN_ROWS, MIX_HALF), lambda b, n: (b, nb - 1 - n, 0))
    return pl.pallas_call(
        _scan_ab_kernel,
        out_shape=(out, out, out, out),
        grid=(bsz, nb),
        in_specs=[fwd(AB_QA), fwd(AB_FF), fwd(AB_IA), fwd(AB_QKB), fwd(AB_VB), fwd(AB_LR_BLOCK, LANES),
                  bwd(AB_QA), bwd(AB_FB), bwd(AB_IA), bwd(AB_QKB), bwd(AB_VB), bwd(AB_LR_BLOCK, LANES),
                  const(lb.shape), const(u_pad.shape), const(gkb.shape)],
        out_specs=(o_f, o_f, o_b, o_b),
        scratch_shapes=[pltpu.VMEM((4 * N_HEADS, HEAD_DV, LANES), F32)],
        compiler_params=_params(2),
        name="scan_ab",
    )(*([p3] * 12), lb, u_pad, gkb)


def _rotary(a, cos, sin_signed):
    lane = lax.broadcasted_iota(jnp.int32, (1, LANES), 1)
    first_half = (lane % NARROW_DK) < (NARROW_DK // 2)
    swapped = jnp.where(first_half, pltpu.roll(a, LANES - NARROW_DK // 2, 1), pltpu.roll(a, NARROW_DK // 2, 1))
    return a * cos + swapped * sin_signed


def _scan_c_kernel(qk_f, v_f, cos_f, sin_f, qk_b, v_b, cos_b, sin_b, dmat_ref, qsc_ref, ksc_ref, cd_ref,
                   o_f, o_b, st_ref):
    @pl.when(pl.program_id(1) == 0)
    def _():
        st_ref[...] = jnp.zeros_like(st_ref)

    n_chunks = SCAN_ROWS // CHUNK
    dirs = ((qk_f, v_f, cos_f, sin_f, o_f), (qk_b, v_b, cos_b, sin_b, o_b))

    def chunk_body(c, carry):
        units = []
        for d, (qk_ref, v_ref, cos_ref, sin_ref, o_ref) in enumerate(dirs):
            r0 = pl.multiple_of((c if d == 0 else n_chunks - 1 - c) * CHUNK, CHUNK)
            rows = pl.ds(r0, CHUNK)
            cos, sin_signed = cos_ref[rows, :], sin_ref[rows, :]
            v = v_ref[rows, :].astype(BF16)
            rot = [_rotary(qk_ref[rows, j * LANES:(j + 1) * LANES], cos, sin_signed) for j in range(4)]
            for h in range(N_HEADS):
                hm = _narrow_head_mask(h)
                q = jnp.where(hm, rot[h // 2] * (NARROW_DK ** -0.5), 0.0)
                k = jnp.where(hm, rot[2 + h // 2], 0.0)
                vsl = slice(h * HEAD_DV, (h + 1) * HEAD_DV)

                def store(o, o_ref=o_ref, rows=rows, vsl=vsl):
                    o_ref[rows, vsl] = o

                units.append(dict(qe=q.astype(BF16), ke=k.astype(BF16), qi=(q * qsc_ref[d, h]).astype(BF16),
                                  ks=(k * ksc_ref[d, h]).astype(BF16), v=v[:, vsl], dec=cd_ref[d, h][0:1, :],
                                  intra=lambda s, d=d, h=h: s * dmat_ref[d, h], idx=d * N_HEADS + h, store=store))
        _run_units(units, st_ref)
        return carry

    lax.fori_loop(0, n_chunks, chunk_body, 0)


def _retention_tables():
    pos = np.arange(CHUNK, dtype=np.float64)
    dmat = np.zeros((2, N_HEADS, CHUNK, CHUNK), np.float32)
    qsc = np.zeros((2, N_HEADS, CHUNK, LANES), np.float32)
    ksc = np.zeros((2, N_HEADS, CHUNK, LANES), np.float32)
    cd = np.zeros((2, N_HEADS, SUBLANES, LANES), np.float32)
    for d in range(2):
        for h in range(N_HEADS):
            lg = np.log1p(-np.exp2(-(5.0 + 2.0 * h + d)))
            dist = pos[:, None] - pos[None, :]
            if d == 0:
                dmat[d, h] = np.where(dist >= 0, np.exp(np.maximum(dist, 0.0) * lg), 0.0)
                qsc[d, h] = np.exp((pos + 1.0) * lg)[:, None]
                ksc[d, h] = np.exp((CHUNK - 1.0 - pos) * lg)[:, None]
            else:
                dmat[d, h] = np.where(dist <= 0, np.exp(np.maximum(-dist, 0.0) * lg), 0.0)
                qsc[d, h] = np.exp((CHUNK - pos) * lg)[:, None]
                ksc[d, h] = np.exp(pos * lg)[:, None]
            cd[d, h] = np.exp(CHUNK * lg)
    return jnp.asarray(dmat), jnp.asarray(qsc), jnp.asarray(ksc), jnp.asarray(cd)


def _rotary_tables(t):
    half = NARROW_DK // 2
    freqs = ROPE_THETA ** (-jnp.arange(half, dtype=F32) / half)
    ang = jnp.arange(t, dtype=F32)[:, None] * freqs[None, :]
    cos, sin = jnp.cos(ang), jnp.sin(ang)
    cos_t = jnp.tile(cos, (1, LANES // half))
    sin_t = jnp.tile(jnp.concatenate([-sin, sin], axis=1), (1, LANES // NARROW_DK))
    return cos_t.astype(F32), sin_t.astype(F32)


def _scan_c(p3):
    bsz, t, _ = p3.shape
    nb = t // SCAN_ROWS
    cos_t, sin_t = _rotary_tables(t)
    tables = _retention_tables()

    def fwd(j):
        return pl.BlockSpec((None, SCAN_ROWS, 512), lambda b, n: (b, n, j))

    def bwd(j):
        return pl.BlockSpec((None, SCAN_ROWS, 512), lambda b, n: (b, nb - 1 - n, j))

    tab_f = pl.BlockSpec((SCAN_ROWS, LANES), lambda b, n: (n, 0))
    tab_b = pl.BlockSpec((SCAN_ROWS, LANES), lambda b, n: (nb - 1 - n, 0))

    def const(shape):
        return pl.BlockSpec(shape, lambda b, n: (0,) * len(shape))

    out = jax.ShapeDtypeStruct((bsz, t, MIX_HALF), F32)
    return pl.pallas_call(
        _scan_c_kernel,
        out_shape=(out, out),
        grid=(bsz, nb),
        in_specs=[fwd(CD_QKC), fwd(CD_VC), tab_f, tab_f, bwd(CD_QKC), bwd(CD_VC), tab_b, tab_b]
        + [const(a.shape) for a in tables],
        out_specs=(pl.BlockSpec((None, SCAN_ROWS, MIX_HALF), lambda b, n: (b, n, 0)),
                   pl.BlockSpec((None, SCAN_ROWS, MIX_HALF), lambda b, n: (b, nb - 1 - n, 0))),
        scratch_shapes=[pltpu.VMEM((2 * N_HEADS, HEAD_DV, LANES), F32)],
        compiler_params=_params(2),
        name="scan_c",
    )(p3, p3, cos_t, sin_t, p3, p3, cos_t, sin_t, *tables)


def _conv_kernel(prev_ref, cur_ref, next_ref, w_ref, b_ref, o_ref, buf_ref):
    i = pl.program_id(1)
    last = pl.num_programs(1) - 1
    rows = cur_ref.shape[0]
    buf_ref[0:SUBLANES, :] = jnp.where(i == 0, 0.0, prev_ref[...])
    buf_ref[SUBLANES:SUBLANES + rows, :] = cur_ref[...]
    buf_ref[SUBLANES + rows:2 * SUBLANES + rows, :] = jnp.where(i == last, 0.0, next_ref[...])
    y = (buf_ref[SUBLANES - 1:SUBLANES - 1 + rows, :] * w_ref[0:1, :]
         + buf_ref[SUBLANES:SUBLANES + rows, :] * w_ref[1:2, :]
         + buf_ref[SUBLANES + 1:SUBLANES + 1 + rows, :] * w_ref[2:3, :]
         + b_ref[...])
    o_ref[...] = _silu(y)


def _conv_silu(p3, conv_w, conv_b):
    bsz, t, _ = p3.shape
    nb = t // ROW_TILE
    per = ROW_TILE // SUBLANES
    outs = []
    for j in (CD_QD, CD_KD):
        lo = (j - CD_QD) * 512
        outs.append(pl.pallas_call(
            _conv_kernel,
            out_shape=jax.ShapeDtypeStruct((bsz, t, 512), F32),
            grid=(bsz, nb),
            in_specs=[pl.BlockSpec((None, SUBLANES, 512), lambda b, i, j=j: (b, jnp.maximum(i * per - 1, 0), j)),
                      pl.BlockSpec((None, ROW_TILE, 512), lambda b, i, j=j: (b, i, j)),
                      pl.BlockSpec((None, SUBLANES, 512),
                                   lambda b, i, j=j: (b, jnp.minimum((i + 1) * per, t // SUBLANES - 1), j)),
                      pl.BlockSpec((3, 512), lambda b, i: (0, 0)),
                      pl.BlockSpec((1, 512), lambda b, i: (0, 0))],
            out_specs=pl.BlockSpec((None, ROW_TILE, 512), lambda b, i: (b, i, 0)),
            scratch_shapes=[pltpu.VMEM((ROW_TILE + 2 * SUBLANES, 512), F32)],
            compiler_params=_params(2),
            name="conv_silu",
        )(p3, p3, p3, conv_w[:, lo:lo + 512], conv_b[None, lo:lo + 512]))
    return outs


def _gate_lane_layout():
    lane = np.arange(LANES)
    r = lane % GATE_LANES
    return lane // (N_HEADS * GATE_LANES), (lane // GATE_LANES) % N_HEADS, r // 3, r % 3


def _gate_const_rows():
    _, _, cls, _ = _gate_lane_layout()
    return jnp.asarray(np.concatenate([np.tile((cls == k + 1).astype(np.float32)[None, :], (LANES, 1))
                                       for k in range(3)], axis=0), BF16)


def _pieces(x, piece):
    hi = x.astype(BF16).astype(F32)
    r1 = x - hi
    mid = r1.astype(BF16).astype(F32)
    lo = (r1 - mid).astype(BF16).astype(F32)
    return jnp.where(piece == 0, hi, jnp.where(piece == 1, mid, lo))


def _running_max(x, direction):
    row = lax.broadcasted_iota(jnp.int32, x.shape, 0)
    step = 1
    while step < CHUNK:
        if direction == 0:
            shifted = jnp.where(row >= step, pltpu.roll(x, step, 0), -jnp.inf)
        else:
            shifted = jnp.where(row < CHUNK - step, pltpu.roll(x, CHUNK - step, 0), -jnp.inf)
        x = jnp.maximum(x, shifted)
        step *= 2
    return x


def _scan_d_kernel(q_f, k_f, v_f, gi_f, gf_f, q_b, k_b, v_b, gi_b, gf_b, fgb_ref, const_ref, o_f, o_b, c_ref, m_ref):
    @pl.when(pl.program_id(1) == 0)
    def _():
        c_ref[...] = jnp.zeros_like(c_ref)
        m_ref[...] = jnp.zeros_like(m_ref)

    n_chunks = SCAN_ROWS // CHUNK
    dirs = ((q_f, k_f, v_f, gi_f, gf_f, o_f), (q_b, k_b, v_b, gi_b, gf_b, o_b))
    lane = lax.broadcasted_iota(jnp.int32, (1, LANES), 1)
    lane_dir = lane // (N_HEADS * GATE_LANES)
    lane_head = (lane // GATE_LANES) % N_HEADS
    cls = (lane % GATE_LANES) // 3
    piece = (lane % GATE_LANES) % 3
    ones_block = jnp.ones((CHUNK, HEAD_DV), BF16)

    def chunk_body(c, carry):
        pre = []
        for d, (q_ref, k_ref, v_ref, gi_ref, gf_ref, o_ref) in enumerate(dirs):
            r0 = pl.multiple_of((c if d == 0 else n_chunks - 1 - c) * CHUNK, CHUNK)
            rows = pl.ds(r0, CHUNK)
            pre.append((rows, gi_ref[rows, :], _log_sigmoid(gf_ref[rows, :] + fgb_ref[...])))
        bcs = _chunk_cumsum([(pre[0][2], 0), (pre[1][2], 1)])
        units = []
        for d, (q_ref, k_ref, v_ref, gi_ref, gf_ref, o_ref) in enumerate(dirs):
            rows, gi, _ = pre[d]
            b = bcs[d]
            last = _last_row(d)
            mask = _tri(d)
            m_st = m_ref[d][0:1, :]
            b_last = b[last:last + 1, :]
            cj = gi - b
            run = _running_max(cj, d)
            m_i = jnp.maximum(b + m_st, b + run)
            u = b - m_i
            m_new = b_last + jnp.maximum(m_st, run[last:last + 1, :])
            s_old = jnp.exp(b_last + m_st - m_new)
            m_ref[d] = jnp.broadcast_to(m_new, (SUBLANES, LANES))
            quantity = jnp.where(cls == 0, u, jnp.where(cls == 1, u + m_st, jnp.where(
                cls == 2, -m_i, cj + (b_last - m_new))))
            lhs_all = jnp.where(cls == 4, 1.0, jnp.where(cls < 4, _pieces(quantity, piece), 0.0))
            lhs = jnp.concatenate([jnp.where((lane_dir == d) & (lane_head == h), lhs_all, 0.0)
                                   for h in range(N_HEADS)], axis=0).astype(BF16)
            key_rows = jnp.where(cls == 4, _pieces(cj, piece), jnp.where(cls == 0, 1.0, 0.0)).astype(BF16)
            expo = _nt(lhs, jnp.concatenate([const_ref[...], key_rows], axis=0))
            for h in range(N_HEADS):
                sl = slice(h * HEAD_DV, (h + 1) * HEAD_DV)
                q = q_ref[rows, sl] * (HEAD_DV ** -0.5)
                k = k_ref[rows, sl]
                v_ext = jnp.concatenate([v_ref[rows, sl].astype(BF16), ones_block], axis=1)
                lane0 = (d * N_HEADS + h) * GATE_LANES
                units.append(dict(u=d * N_HEADS + h, rows=rows, sl=sl, o_ref=o_ref, q=q, k=k, v_ext=v_ext,
                                  mask=mask, expo=expo[h * CHUNK:(h + 1) * CHUNK, :],
                                  s_old=s_old[:, lane0:lane0 + 1], qk=_nt(q.astype(BF16), k.astype(BF16))))
        for t in units:
            e = t["expo"]
            w = t["qk"] * jnp.exp(jnp.where(t["mask"], e[:, 3 * LANES:3 * LANES + CHUNK], -jnp.inf))
            lhs2 = jnp.concatenate([(t["q"] * jnp.exp(e[:, :LANES])).astype(BF16), w.astype(BF16)], axis=1)
            rhs2 = jnp.concatenate([c_ref[t["u"]].astype(BF16), t["v_ext"]], axis=0)
            t["num"] = _mm(lhs2, rhs2)
            t["upd"] = _tn((t["k"] * jnp.exp(e[:, 2 * LANES:3 * LANES])).astype(BF16), t["v_ext"])
        for t in units:
            num = t["num"]
            den = jnp.maximum(jnp.abs(num[:, HEAD_DV:]), jnp.exp(t["expo"][:, LANES:2 * LANES]))
            t["o_ref"][t["rows"], t["sl"]] = num[:, :HEAD_DV] / den
            c_ref[t["u"]] = t["s_old"] * c_ref[t["u"]] + t["upd"]
        return carry

    lax.fori_loop(0, n_chunks, chunk_body, 0)


def _scan_d(qd, kd, p3, fgb_row):
    bsz, t, _ = p3.shape
    nb = t // SCAN_ROWS
    const_rows = _gate_const_rows()

    def spec(j, w, flip):
        if flip:
            return pl.BlockSpec((None, SCAN_ROWS, w), lambda b, n: (b, nb - 1 - n, j))
        return pl.BlockSpec((None, SCAN_ROWS, w), lambda b, n: (b, n, j))

    out = jax.ShapeDtypeStruct((bsz, t, MIX_HALF), F32)
    in_specs = []
    for flip in (False, True):
        in_specs += [spec(0, 512, flip), spec(0, 512, flip), spec(CD_VD, 512, flip),
                     spec(CD_GATE_I_BLOCK, LANES, flip), spec(CD_GATE_F_BLOCK, LANES, flip)]
    in_specs += [pl.BlockSpec((1, LANES), lambda b, n: (0, 0)), pl.BlockSpec(const_rows.shape, lambda b, n: (0, 0))]
    return pl.pallas_call(
        _scan_d_kernel,
        out_shape=(out, out),
        grid=(bsz, nb),
        in_specs=in_specs,
        out_specs=(spec(0, MIX_HALF, False), spec(0, MIX_HALF, True)),
        scratch_shapes=[pltpu.VMEM((2 * N_HEADS, HEAD_DV, 2 * HEAD_DV), F32),
                        pltpu.VMEM((2, SUBLANES, LANES), F32)],
        compiler_params=_params(2),
        name="scan_d",
    )(qd, kd, p3, p3, p3, qd, kd, p3, p3, p3, fgb_row, const_rows)


def _layer_norm(z, g, b):
    zc = z - jnp.mean(z, axis=-1, keepdims=True)
    var = jnp.mean(zc * zc, axis=-1, keepdims=True)
    return zc * lax.rsqrt(var + LN_EPS) * g + b


def _mix_out_kernel(o1f, o1b, o2f, o2b, g1, g2, x_ref, norm_ref, w_ref, lng, lnb, rw_ref, rb_ref,
                    h_ref, idx_ref, gate_ref, *, second_gate):
    parts = []
    for of, ob, g_ref, gate_fn, base in ((o1f, o1b, g1, _silu, 0), (o2f, o2b, g2, second_gate, MIX_HALF)):
        o = of[...] + ob[...]
        gate = gate_fn(g_ref[...])
        for h in range(N_HEADS):
            sl = slice(h * HEAD_DV, (h + 1) * HEAD_DV)
            oh = o[:, sl]
            oh = oh * lax.rsqrt(jnp.mean(oh * oh, axis=-1, keepdims=True) + NORM_EPS)
            parts.append((oh * norm_ref[:, base + h * HEAD_DV:base + (h + 1) * HEAD_DV] * gate[:, sl]).astype(BF16))
    mixed = jnp.concatenate(parts, axis=1)
    z = ALPHA * x_ref[...] + _mm(mixed, w_ref[...])
    hn = _layer_norm(z, lng[...], lnb[...])
    h_ref[...] = hn
    _route(hn, rw_ref, rb_ref, idx_ref, gate_ref)


def _mix_out(o1f, o1b, o2f, o2b, p2d, gate_blocks, x2d, norm, w_out, ln_g, ln_b, rw_t, rb_col, second_gate):
    n = x2d.shape[0]
    half = pl.BlockSpec((ROW_TILE, MIX_HALF), lambda i: (i, 0))
    full = pl.BlockSpec((ROW_TILE, D_MODEL), lambda i: (i, 0))
    row = pl.BlockSpec((1, D_MODEL), lambda i: (0, 0))
    routed = pl.BlockSpec((TOP_K, ROW_TILE), lambda i: (0, i))
    ga, gb = gate_blocks
    return pl.pallas_call(
        functools.partial(_mix_out_kernel, second_gate=second_gate),
        out_shape=(jax.ShapeDtypeStruct((n, D_MODEL), F32), jax.ShapeDtypeStruct((TOP_K, n), jnp.int32),
                   jax.ShapeDtypeStruct((TOP_K, n), F32)),
        grid=(n // ROW_TILE,),
        in_specs=[half, half, half, half,
                  pl.BlockSpec((ROW_TILE, MIX_HALF), lambda i: (i, ga)),
                  pl.BlockSpec((ROW_TILE, MIX_HALF), lambda i: (i, gb)),
                  full, row, pl.BlockSpec((D_MODEL, D_MODEL), lambda i: (0, 0)), row, row,
                  pl.BlockSpec((N_EXPERTS, D_MODEL), lambda i: (0, 0)),
                  pl.BlockSpec((N_EXPERTS, 1), lambda i: (0, 0))],
        out_specs=(full, routed, routed),
        compiler_params=_params(1),
        name="mix_out",
    )(o1f, o1b, o2f, o2b, p2d, p2d, x2d, norm, w_out, ln_g, ln_b, rw_t, rb_col)


def _route(h, rw_ref, rb_ref, idx_ref, gate_ref):
    logits = lax.dot_general(rw_ref[...], h, (((1,), (1,)), ((), ())),
                             precision=HIGHEST, preferred_element_type=F32)
    rows = [logits[e:e + 1, :] for e in range(N_EXPERTS)]
    mx = functools.reduce(jnp.maximum, rows)
    ex = [jnp.exp(r - mx) for r in rows]
    tot = functools.reduce(lambda a, b: a + b, ex)
    score = [e / tot for e in ex]
    biased = [score[e] + rb_ref[e:e + 1, :] for e in range(N_EXPERTS)]

    def argmax_first(vals):
        best, idx = vals[0], jnp.zeros(vals[0].shape, jnp.int32)
        for j in range(1, len(vals)):
            upd = vals[j] > best
            best = jnp.where(upd, vals[j], best)
            idx = jnp.where(upd, j, idx)
        return best, idx

    def pick(vals, idx):
        out = vals[0]
        for j in range(1, len(vals)):
            out = jnp.where(idx == j, vals[j], out)
        return out

    group_scores = []
    for g in range(N_GROUPS):
        a = biased[g * EXPERTS_PER_GROUP:(g + 1) * EXPERTS_PER_GROUP]
        pairs = [a[i] + a[j] for i in range(EXPERTS_PER_GROUP) for j in range(i + 1, EXPERTS_PER_GROUP)]
        group_scores.append(functools.reduce(jnp.maximum, pairs))
    _, g_sel = argmax_first(group_scores)
    in_b = [pick([biased[g * EXPERTS_PER_GROUP + k] for g in range(N_GROUPS)], g_sel) for k in range(EXPERTS_PER_GROUP)]
    in_s = [pick([score[g * EXPERTS_PER_GROUP + k] for g in range(N_GROUPS)], g_sel) for k in range(EXPERTS_PER_GROUP)]
    _, i0 = argmax_first(in_b)
    _, i1 = argmax_first([jnp.where(i0 == k, -jnp.inf, in_b[k]) for k in range(EXPERTS_PER_GROUP)])
    s0, s1 = pick(in_s, i0), pick(in_s, i1)
    den = s0 + s1
    idx_ref[0:1, :] = g_sel * EXPERTS_PER_GROUP + i0
    idx_ref[1:2, :] = g_sel * EXPERTS_PER_GROUP + i1
    gate_ref[0:1, :] = s0 / den
    gate_ref[1:2, :] = s1 / den


def _plan_kernel(idx_ref, dest_ref, cnt_ref, su_ref, run_ref, start_ref):
    phase, i = pl.program_id(0), pl.program_id(1)
    tile = idx_ref.shape[1]
    e_iota = lax.broadcasted_iota(jnp.int32, (N_EXPERTS, tile), 0)
    oh0 = idx_ref[0:1, :] == e_iota
    oh1 = idx_ref[1:2, :] == e_iota
    member = jnp.where(oh0 | oh1, 1.0, 0.0)
    tile_count = jnp.broadcast_to(jnp.sum(member, axis=1, keepdims=True), (N_EXPERTS, LANES))

    @pl.when((phase == 0) & (i == 0))
    def _():
        run_ref[...] = jnp.zeros_like(run_ref)
        r = lax.broadcasted_iota(jnp.int32, (tile, tile), 0)
        c = lax.broadcasted_iota(jnp.int32, (tile, tile), 1)
        su_ref[...] = jnp.where(r < c, 1.0, 0.0).astype(BF16)

    @pl.when((phase == 1) & (i == 0))
    def _():
        counts = run_ref[...]
        cnt_ref[...] = counts.astype(jnp.int32)
        padded = jnp.ceil(counts * (1.0 / EXPERT_ROWS)) * EXPERT_ROWS
        r = lax.broadcasted_iota(jnp.int32, (N_EXPERTS, N_EXPERTS), 0)
        c = lax.broadcasted_iota(jnp.int32, (N_EXPERTS, N_EXPERTS), 1)
        start_ref[...] = jnp.dot(jnp.where(c < r, 1.0, 0.0), padded, precision=HIGHEST, preferred_element_type=F32)
        run_ref[...] = jnp.zeros_like(run_ref)

    @pl.when(phase == 1)
    def _():
        before = _mm(member.astype(BF16), su_ref[...])
        pos = start_ref[:, 0:1] + run_ref[:, 0:1] + before
        dest_ref[0:1, :] = jnp.sum(jnp.where(oh0, pos, 0.0), axis=0, keepdims=True).astype(jnp.int32)
        dest_ref[1:2, :] = jnp.sum(jnp.where(oh1, pos, 0.0), axis=0, keepdims=True).astype(jnp.int32)

    run_ref[...] += tile_count


def _plan(idx):
    n = idx.shape[1]
    tile = min(ROUTER_TILE, n)
    return pl.pallas_call(
        _plan_kernel,
        out_shape=(jax.ShapeDtypeStruct((TOP_K, n), jnp.int32), jax.ShapeDtypeStruct((N_EXPERTS, LANES), jnp.int32)),
        grid=(2, n // tile),
        in_specs=[pl.BlockSpec((TOP_K, tile), lambda p, i: (0, i))],
        out_specs=(pl.BlockSpec((TOP_K, tile), lambda p, i: (0, i * p)),
                   pl.BlockSpec((N_EXPERTS, LANES), lambda p, i: (0, 0))),
        scratch_shapes=[pltpu.VMEM((tile, tile), BF16), pltpu.VMEM((N_EXPERTS, LANES), F32),
                        pltpu.VMEM((N_EXPERTS, LANES), F32)],
        compiler_params=_params(2),
        name="moe_plan",
    )(idx)


def _invert_kernel(pad_ref, dest_ref, tok_ref):
    n = dest_ref.shape[0] // TOP_K

    def clear(r, carry):
        tok_ref[r] = lax.rem(r, n)
        return carry

    for e in range(N_EXPERTS + 1):
        lax.fori_loop(pad_ref[0, e], pad_ref[1, e], clear, 0)

    for k in range(TOP_K):
        def place(t, carry, k=k):
            tok_ref[dest_ref[k * n + t]] = t
            return carry

        lax.fori_loop(0, n, place, 0, unroll=16)


def _invert(pad_ranges, dest_flat, n_rows):
    smem = pl.BlockSpec(memory_space=pltpu.SMEM)
    return pl.pallas_call(
        _invert_kernel,
        out_shape=jax.ShapeDtypeStruct((n_rows,), jnp.int32),
        in_specs=[smem, smem],
        out_specs=smem,
        name="moe_invert",
    )(pad_ranges, dest_flat)


def _expert_kernel(be_ref, nv_ref, x_ref, w1_ref, w3_ref, w2_ref, y_ref, wb_ref):
    i = pl.program_id(0)
    valid = i < nv_ref[0]
    new_expert = (i == 0) | (be_ref[i] != be_ref[jnp.maximum(i - 1, 0)])

    @pl.when(valid & new_expert)
    def _():
        wb_ref[0] = w1_ref[...].astype(BF16)
        wb_ref[1] = w3_ref[...].astype(BF16)
        wb_ref[2] = w2_ref[...].astype(BF16)

    @pl.when(valid)
    def _():
        x = x_ref[...].astype(BF16)
        hid = _silu(_mm(x, wb_ref[0])) * _mm(x, wb_ref[1])
        y_ref[...] = _mm(hid.astype(BF16), wb_ref[2])

    @pl.when(jnp.logical_not(valid))
    def _():
        y_ref[...] = jnp.zeros_like(y_ref)


def _experts(block_expert, n_valid, xs, w1, w3, w2, layer):
    n_rows = xs.shape[0]
    wspec = pl.BlockSpec((None, None, D_MODEL, D_MODEL), lambda i, be, nv: (layer, be[i], 0, 0))
    return pl.pallas_call(
        _expert_kernel,
        out_shape=jax.ShapeDtypeStruct((n_rows, D_MODEL), F32),
        grid_spec=pltpu.PrefetchScalarGridSpec(
            num_scalar_prefetch=2,
            grid=(n_rows // EXPERT_ROWS,),
            in_specs=[pl.BlockSpec((EXPERT_ROWS, D_MODEL), lambda i, be, nv: (i, 0)), wspec, wspec, wspec],
            out_specs=pl.BlockSpec((EXPERT_ROWS, D_MODEL), lambda i, be, nv: (i, 0)),
            scratch_shapes=[pltpu.VMEM((3, D_MODEL, D_MODEL), BF16)]),
        compiler_params=_params(1),
        name="experts",
    )(block_expert, n_valid, xs, w1, w3, w2)


def _combine_kernel(h_ref, y0_ref, y1_ref, g_ref, lng, lnb, o_ref):
    g = g_ref[...]
    z = ALPHA * h_ref[...] + (g[:, 0:1] * y0_ref[...] + g[:, 1:2] * y1_ref[...])
    o_ref[...] = _layer_norm(z, lng[...], lnb[...])


def _combine(h2d, y_rows, gates_t, ln_g, ln_b):
    n = h2d.shape[0]
    nt = n // ROW_TILE
    full = pl.BlockSpec((ROW_TILE, D_MODEL), lambda i: (i, 0))
    row = pl.BlockSpec((1, D_MODEL), lambda i: (0, 0))
    return pl.pallas_call(
        _combine_kernel,
        out_shape=jax.ShapeDtypeStruct((n, D_MODEL), F32),
        grid=(nt,),
        in_specs=[full, full, pl.BlockSpec((ROW_TILE, D_MODEL), lambda i: (nt + i, 0)),
                  pl.BlockSpec((ROW_TILE, TOP_K), lambda i: (i, 0)), row, row],
        out_specs=full,
        compiler_params=_params(1),
        name="moe_combine",
    )(h2d, y_rows, y_rows, gates_t, ln_g, ln_b)


def _moe(h2d, idx, gates, w1, w3, w2, layer, ln_g, ln_b):
    n = h2d.shape[0]
    dest, counts = _plan(idx)
    n_rows = n * TOP_K + N_EXPERTS * EXPERT_ROWS
    n_blocks = n_rows // EXPERT_ROWS
    counts = counts[:, 0]
    padded = (counts + EXPERT_ROWS - 1) // EXPERT_ROWS * EXPERT_ROWS
    pad_end = jnp.cumsum(padded)
    block_start = jnp.arange(n_blocks, dtype=jnp.int32) * EXPERT_ROWS
    block_expert = jnp.minimum(jnp.sum((pad_end[None, :] <= block_start[:, None]).astype(jnp.int32), axis=1),
                               N_EXPERTS - 1)
    n_valid = (pad_end[-1:] // EXPERT_ROWS).astype(jnp.int32)
    pad_ranges = jnp.stack([jnp.concatenate([pad_end - padded + counts, pad_end[-1:]]),
                            jnp.concatenate([pad_end, jnp.full((1,), n_rows, jnp.int32)])]).astype(jnp.int32)
    dest_flat = dest.reshape(-1)
    xs = h2d[_invert(pad_ranges, dest_flat, n_rows)]
    y = _experts(block_expert, n_valid, xs, w1, w3, w2, layer)
    return _combine(h2d, y[dest_flat], gates.T, ln_g, ln_b)


def _pad_cols(w, b, main, width):
    k = w.shape[0]
    tail = w.shape[1] - main
    w_p = jnp.concatenate([w[:, :main], w[:, main:], jnp.zeros((k, width - main - tail), w.dtype)], axis=1)
    b_p = jnp.concatenate([b[:main], b[main:], jnp.zeros((width - main - tail,), b.dtype)])
    return w_p.astype(BF16), b_p[None, :].astype(F32)


def kernel(x, ab_w_in, ab_b_in, hgrn_lb, gla_gk_up, gla_gk_b, hgrn_norm, gla_norm, ab_w_out, cd_w_in, cd_b_in,
           mlstm_conv_w, mlstm_conv_b, mlstm_fgate_b, ret_norm, mlstm_norm, cd_w_out, ln_mix_g, ln_mix_b, ln_ffn_g,
           ln_ffn_b, router_w, router_b, moe_w1, moe_w3, moe_w2):
    bsz, t, d = x.shape
    n = bsz * t
    lower_bounds = jnp.cumsum(jax.nn.softmax(hgrn_lb.astype(F32), axis=0), axis=0)
    rw_t = router_w.T.astype(F32)
    rb_col = router_b.astype(F32)[:, None]
    h = x.reshape(n, d)
    for layer in range(DEPTH):
        j = layer // 2
        if layer % 2 == 0:
            w_p, b_p = _pad_cols(ab_w_in[j], ab_b_in[j], 8 * 512, AB_WIDTH)
            p = _in_proj(h, w_p, b_p)
            p3 = p.reshape(bsz, t, AB_WIDTH)
            nk = N_HEADS * NARROW_DK
            u_pad = jnp.zeros((LANES, 2 * nk), F32)
            u_pad = u_pad.at[:GLA_LOWRANK, :nk].set(gla_gk_up[j, 0])
            u_pad = u_pad.at[GLA_LOWRANK:2 * GLA_LOWRANK, nk:].set(gla_gk_up[j, 1]).astype(BF16)
            gkb = jnp.concatenate([gla_gk_b[j, 0], gla_gk_b[j, 1]])[None, :].astype(F32)
            oa_f, ob_f, oa_b, ob_b = _scan_ab(p3, lower_bounds[layer][None, :], u_pad, gkb)
            norm = jnp.concatenate([hgrn_norm[j], gla_norm[j]])[None, :].astype(F32)
            outs = [a.reshape(n, MIX_HALF) for a in (oa_f, oa_b, ob_f, ob_b)]
            h, idx, gates = _mix_out(*outs, p, (AB_GA, AB_GB), h, norm, ab_w_out[j].astype(BF16),
                                     ln_mix_g[layer][None, :], ln_mix_b[layer][None, :], rw_t, rb_col, _silu)
        else:
            lane_dir, lane_head, _, _ = _gate_lane_layout()
            col_i = 7 * 512 + lane_dir * 2 * N_HEADS + lane_head
            cols = np.concatenate([np.arange(7 * 512), col_i, col_i + N_HEADS])
            w_p, b_p = cd_w_in[j][:, cols].astype(BF16), cd_b_in[j][cols][None, :].astype(F32)
            p = _in_proj(h, w_p, b_p)
            p3 = p.reshape(bsz, t, CD_WIDTH)
            oc_f, oc_b = _scan_c(p3)
            qd, kd = _conv_silu(p3, mlstm_conv_w[j].astype(F32), mlstm_conv_b[j].astype(F32))
            fgb_row = mlstm_fgate_b[j].astype(F32)[lane_dir, lane_head][None, :]
            od_f, od_b = _scan_d(qd, kd, p3, fgb_row)
            norm = jnp.concatenate([ret_norm[j], mlstm_norm[j]])[None, :].astype(F32)
            outs = [a.reshape(n, MIX_HALF) for a in (oc_f, oc_b, od_f, od_b)]
            h, idx, gates = _mix_out(*outs, p, (CD_GC, CD_OD), h, norm, cd_w_out[j].astype(BF16),
                                     ln_mix_g[layer][None, :], ln_mix_b[layer][None, :], rw_t, rb_col, _sigmoid)
        h = _moe(h, idx, gates, moe_w1, moe_w3, moe_w2, layer,
                 ln_ffn_g[layer][None, :], ln_ffn_b[layer][None, :])
    return h.reshape(bsz, t, d)
```

```python
import functools

import jax
import jax.numpy as jnp
import numpy as np
from jax import lax
from jax.experimental import pallas as pl
from jax.experimental.pallas import tpu as pltpu

F32 = jnp.float32
BF16 = jnp.bfloat16
HIGHEST = lax.Precision.HIGHEST

D_MODEL = 1024
DEPTH = 2
CHUNK = 64
MIX_HALF = D_MODEL // 2
N_HEADS = 4
HEAD_DV = MIX_HALF // N_HEADS
NARROW_DK = 64
GLA_LOWRANK = 16
GLA_GATE_NORMALIZER = 16.0
ROPE_THETA = 10000.0
N_EXPERTS = 16
N_GROUPS = 4
EXPERTS_PER_GROUP = N_EXPERTS // N_GROUPS
TOP_K = 2
ALPHA = (2.0 * DEPTH) ** 0.25
LN_EPS = 1e-5
NORM_EPS = 1e-6

LANES = 128
SUBLANES = 8
VMEM_LIMIT = 52 * 1024 * 1024

ROW_TILE = 256
SCAN_ROWS = 256
ROUTER_TILE = 1024
EXPERT_ROWS = 256

AB_QA, AB_FF, AB_FB, AB_IA, AB_GA, AB_QKB, AB_VB, AB_GB = range(8)
AB_LR_BLOCK = 32
AB_WIDTH = 8 * 512 + LANES
CD_QKC, CD_VC, CD_GC, CD_QD, CD_KD, CD_VD, CD_OD = range(7)
CD_GATE_I_BLOCK = 28
CD_GATE_F_BLOCK = 29
CD_WIDTH = 7 * 512 + 2 * LANES
GATE_LANES = 16


def _params(n_axes, vmem=VMEM_LIMIT):
    return pltpu.CompilerParams(dimension_semantics=("arbitrary",) * n_axes, vmem_limit_bytes=vmem)


def _nt(a, b):
    return lax.dot_general(a, b, (((1,), (1,)), ((), ())), preferred_element_type=F32)


def _tn(a, b):
    return lax.dot_general(a, b, (((0,), (0,)), ((), ())), preferred_element_type=F32)


def _mm(a, b):
    return jnp.dot(a, b, preferred_element_type=F32)


def _sigmoid(x):
    return 1.0 / (1.0 + jnp.exp(-x))


def _silu(x):
    return x * _sigmoid(x)


def _log_sigmoid(x):
    return jnp.minimum(x, 0.0) - jnp.log(1.0 + jnp.exp(-jnp.abs(x)))


def _tri(direction):
    row = lax.broadcasted_iota(jnp.int32, (CHUNK, CHUNK), 0)
    col = lax.broadcasted_iota(jnp.int32, (CHUNK, CHUNK), 1)
    return (col <= row) if direction == 0 else (col >= row)


def _ref_row(direction):
    return CHUNK // 2 - 1 if direction == 0 else CHUNK // 2


def _last_row(direction):
    return CHUNK - 1 if direction == 0 else 0


def _proj_ab_kernel(x_ref, w_ref, b_ref, lb_ref, o_ref):
    p = _mm(x_ref[...].astype(BF16), w_ref[...]) + b_ref[...]
    o_ref[:, :512] = _silu(p[:, :512])
    lb = jnp.concatenate([lb_ref[...], lb_ref[...]], axis=1)
    o_ref[:, 512:1536] = lb + (1.0 - lb) * _sigmoid(p[:, 512:1536])
    o_ref[:, 1536:] = p[:, 1536:]


def _proj_cd_kernel(x_ref, w_ref, b_ref, cos_ref, sin_ref, o_ref):
    p = _mm(x_ref[...].astype(BF16), w_ref[...]) + b_ref[...]
    cos, sin_signed = cos_ref[...], sin_ref[...]
    for j in range(4):
        rot = _rotary(p[:, j * LANES:(j + 1) * LANES], cos, sin_signed)
        o_ref[:, j * LANES:(j + 1) * LANES] = rot * (NARROW_DK ** -0.5) if j < 2 else rot
    o_ref[:, 512:] = p[:, 512:]


def _in_proj(x2d, w, b, body, extras, extra_specs):
    n, k = x2d.shape
    m = w.shape[1]
    return pl.pallas_call(
        body,
        out_shape=jax.ShapeDtypeStruct((n, m), F32),
        grid=(n // ROW_TILE,),
        in_specs=[pl.BlockSpec((ROW_TILE, k), lambda i: (i, 0)),
                  pl.BlockSpec((k, m), lambda i: (0, 0)),
                  pl.BlockSpec((1, m), lambda i: (0, 0))] + extra_specs,
        out_specs=pl.BlockSpec((ROW_TILE, m), lambda i: (i, 0)),
        compiler_params=_params(1),
        name="in_proj",
    )(x2d, w, b, *extras)


def _pipelined_chunks(n_chunks, prepare, finish):
    pending = prepare(0)
    for c in range(n_chunks):
        following = prepare(c + 1) if c + 1 < n_chunks else None
        finish(pending)
        pending = following


def _units_prepare(units):
    scores = [_nt(u["qe"], u["ke"]) for u in units]
    for u, s in zip(units, scores):
        u["upd"] = _tn(u["v"], u["ks"])
        u["o_intra"] = _mm(u["intra"](s).astype(BF16), u["v"])
    return units


def _units_finish(units, st_ref):
    inter = [_nt(u["qi"], st_ref[u["idx"]].astype(BF16)) for u in units]
    for u, o_inter in zip(units, inter):
        u["store"](u["o_intra"] + o_inter)
        st_ref[u["idx"]] = st_ref[u["idx"]] * u["dec"] + u["upd"]


def _chunk_cumsum(parts):
    cat = jnp.concatenate([x for x, _ in parts], axis=1)
    hi = cat.astype(BF16)
    lo = (cat - hi.astype(F32)).astype(BF16)
    width = cat.shape[1]
    inc = _mm(_tri(0).astype(BF16), jnp.concatenate([hi, lo], axis=1))
    inc = inc[:, :width] + inc[:, width:]
    out, off = [], 0
    for x, direction in parts:
        p = inc[:, off:off + x.shape[1]]
        off += x.shape[1]
        out.append(p if direction == 0 else p[CHUNK - 1:CHUNK, :] - p + x)
    return out


def _gated_operands(q, k, gc, direction):
    r, l = _ref_row(direction), _last_row(direction)
    g_ref = gc[r:r + 1, :]
    g_last = gc[l:l + 1, :]
    qe = q * jnp.exp(gc - g_ref)
    ke = k * jnp.exp(g_ref - gc)
    qi = qe * jnp.exp(g_ref)
    ks = ke * jnp.exp(g_last - g_ref)
    return [a.astype(BF16) for a in (qe, ke, qi, ks)], jnp.exp(g_last)


def _narrow_head_mask(h):
    lane = lax.broadcasted_iota(jnp.int32, (1, LANES), 1)
    return (lane // NARROW_DK) == (h % 2)


def _scan_ab_kernel(qa_f, fa_f, ia_f, qkb_f, vb_f, lr_f, qa_b, fa_b, ia_b, qkb_b, vb_b, lr_b,
                    u_ref, gkb_ref, oa_f, ob_f, oa_b, ob_b, st_ref):
    @pl.when(pl.program_id(1) == 0)
    def _():
        st_ref[...] = jnp.zeros_like(st_ref)

    n_chunks = SCAN_ROWS // CHUNK
    dirs = ((qa_f, fa_f, ia_f, qkb_f, vb_f, lr_f, oa_f, ob_f), (qa_b, fa_b, ia_b, qkb_b, vb_b, lr_b, oa_b, ob_b))

    nk = N_HEADS * NARROW_DK

    def prepare(c):
        pre = []
        for d, (qa, fa, ia, qkb, vb, lr, oa, ob) in enumerate(dirs):
            rows = pl.ds((c if d == 0 else n_chunks - 1 - c) * CHUNK, CHUNK)
            f = fa[rows, :]
            logits = _mm(lr[rows, :].astype(BF16), u_ref[:, d * nk:(d + 1) * nk]) + gkb_ref[:, d * nk:(d + 1) * nk]
            qk = qkb[rows, :]
            pre.append(dict(rows=rows, f=f, lf_a=jnp.log(f), lf_b=_log_sigmoid(logits) / GLA_GATE_NORMALIZER,
                            q_a=qa[rows, :], v_a=ia[rows, :].astype(BF16),
                            q_b=qk[:, :nk] * (NARROW_DK ** -0.5), k_b=qk[:, nk:], v_b=vb[rows, :].astype(BF16)))
        gcs = _chunk_cumsum([(pre[0]["lf_a"], 0), (pre[0]["lf_b"], 0), (pre[1]["lf_a"], 1), (pre[1]["lf_b"], 1)])
        units = []
        for d, (qa, fa, ia, qkb, vb, lr, oa, ob) in enumerate(dirs):
            p = pre[d]
            rows = p["rows"]
            mask = _tri(d)
            intra = lambda s, mask=mask: jnp.where(mask, s, 0.0)
            (qe, ke, qi, ks), dec = _gated_operands(p["q_a"], 1.0 - p["f"], gcs[2 * d], d)
            for h in range(N_HEADS):
                sl = slice(h * LANES, (h + 1) * LANES)

                def store(o, oa=oa, rows=rows, sl=sl):
                    oa[rows, sl] = o

                units.append(dict(qe=qe[:, sl], ke=ke[:, sl], qi=qi[:, sl], ks=ks[:, sl], v=p["v_a"][:, sl],
                                  dec=dec[:, sl], intra=intra, idx=d * 2 * N_HEADS + h, store=store))
            (qe, ke, qi, ks), dec = _gated_operands(p["q_b"], p["k_b"], gcs[2 * d + 1], d)
            for h in range(N_HEADS):
                psl = slice((h // 2) * LANES, (h // 2 + 1) * LANES)
                hm = _narrow_head_mask(h)
                pick = lambda a, psl=psl, hm=hm: jnp.where(hm, a[:, psl], jnp.zeros((), BF16))
                vsl = slice(h * HEAD_DV, (h + 1) * HEAD_DV)

                def store(o, ob=ob, rows=rows, vsl=vsl):
                    ob[rows, vsl] = o

                units.append(dict(qe=pick(qe), ke=pick(ke), qi=pick(qi), ks=pick(ks), v=p["v_b"][:, vsl],
                                  dec=dec[:, psl], intra=intra, idx=d * 2 * N_HEADS + N_HEADS + h, store=store))
        return _units_prepare(units)

    _pipelined_chunks(n_chunks, prepare, functools.partial(_units_finish, st_ref=st_ref))


def _scan_ab(p3, u_pad, gkb):
    bsz, t, _ = p3.shape
    nb = t // SCAN_ROWS

    def fwd(j, w=512):
        return pl.BlockSpec((None, SCAN_ROWS, w), lambda b, n: (b, n, j))

    def bwd(j, w=512):
        return pl.BlockSpec((None, SCAN_ROWS, w), lambda b, n: (b, nb - 1 - n, j))

    def const(shape):
        return pl.BlockSpec(shape, lambda b, n: (0,) * len(shape))

    out = jax.ShapeDtypeStruct((bsz, t, MIX_HALF), F32)
    o_f = pl.BlockSpec((None, SCAN_ROWS, MIX_HALF), lambda b, n: (b, n, 0))
    o_b = pl.BlockSpec((None, SCAN_ROWS, MIX_HALF), lambda b, n: (b, nb - 1 - n, 0))
    return pl.pallas_call(
        _scan_ab_kernel,
        out_shape=(out, out, out, out),
        grid=(bsz, nb),
        in_specs=[fwd(AB_QA), fwd(AB_FF), fwd(AB_IA), fwd(AB_QKB), fwd(AB_VB), fwd(AB_LR_BLOCK, LANES),
                  bwd(AB_QA), bwd(AB_FB), bwd(AB_IA), bwd(AB_QKB), bwd(AB_VB), bwd(AB_LR_BLOCK, LANES),
                  const(u_pad.shape), const(gkb.shape)],
        out_specs=(o_f, o_f, o_b, o_b),
        scratch_shapes=[pltpu.VMEM((4 * N_HEADS, HEAD_DV, LANES), F32)],
        compiler_params=_params(2),
        name="scan_ab",
    )(*([p3] * 12), u_pad, gkb)


def _rotary(a, cos, sin_signed):
    lane = lax.broadcasted_iota(jnp.int32, (1, LANES), 1)
    first_half = (lane % NARROW_DK) < (NARROW_DK // 2)
    swapped = jnp.where(first_half, pltpu.roll(a, LANES - NARROW_DK // 2, 1), pltpu.roll(a, NARROW_DK // 2, 1))
    return a * cos + swapped * sin_signed


def _scan_c_kernel(qk_f, v_f, qk_b, v_b, dmat_ref, qsc_ref, ksc_ref, cd_ref, o_f, o_b, st_ref):
    @pl.when(pl.program_id(1) == 0)
    def _():
        st_ref[...] = jnp.zeros_like(st_ref)

    n_chunks = SCAN_ROWS // CHUNK
    dirs = ((qk_f, v_f, o_f), (qk_b, v_b, o_b))

    def prepare(c):
        units = []
        for d, (qk_ref, v_ref, o_ref) in enumerate(dirs):
            rows = pl.ds((c if d == 0 else n_chunks - 1 - c) * CHUNK, CHUNK)
            v = v_ref[rows, :].astype(BF16)
            for h in range(N_HEADS):
                hm = _narrow_head_mask(h)
                q = jnp.where(hm, qk_ref[rows, (h // 2) * LANES:(h // 2 + 1) * LANES], 0.0)
                k = jnp.where(hm, qk_ref[rows, (2 + h // 2) * LANES:(3 + h // 2) * LANES], 0.0)
                vsl = slice(h * HEAD_DV, (h + 1) * HEAD_DV)

                def store(o, o_ref=o_ref, rows=rows, vsl=vsl):
                    o_ref[rows, vsl] = o

                units.append(dict(qe=q.astype(BF16), ke=k.astype(BF16), qi=(q * qsc_ref[d, h]).astype(BF16),
                                  ks=(k * ksc_ref[d, h]).astype(BF16), v=v[:, vsl], dec=cd_ref[d, h][0:1, :],
                                  intra=lambda s, d=d, h=h: s * dmat_ref[d, h], idx=d * N_HEADS + h, store=store))
        return _units_prepare(units)

    _pipelined_chunks(n_chunks, prepare, functools.partial(_units_finish, st_ref=st_ref))


def _retention_tables():
    pos = np.arange(CHUNK, dtype=np.float64)
    dmat = np.zeros((2, N_HEADS, CHUNK, CHUNK), np.float32)
    qsc = np.zeros((2, N_HEADS, CHUNK, LANES), np.float32)
    ksc = np.zeros((2, N_HEADS, CHUNK, LANES), np.float32)
    cd = np.zeros((2, N_HEADS, SUBLANES, LANES), np.float32)
    for d in range(2):
        for h in range(N_HEADS):
            lg = np.log1p(-np.exp2(-(5.0 + 2.0 * h + d)))
            dist = pos[:, None] - pos[None, :]
            if d == 0:
                dmat[d, h] = np.where(dist >= 0, np.exp(np.maximum(dist, 0.0) * lg), 0.0)
                qsc[d, h] = np.exp((pos + 1.0) * lg)[:, None]
                ksc[d, h] = np.exp((CHUNK - 1.0 - pos) * lg)[:, None]
            else:
                dmat[d, h] = np.where(dist <= 0, np.exp(np.maximum(-dist, 0.0) * lg), 0.0)
                qsc[d, h] = np.exp((CHUNK - pos) * lg)[:, None]
                ksc[d, h] = np.exp(pos * lg)[:, None]
            cd[d, h] = np.exp(CHUNK * lg)
    return jnp.asarray(dmat), jnp.asarray(qsc), jnp.asarray(ksc), jnp.asarray(cd)


def _rotary_tables(t):
    half = NARROW_DK // 2
    freqs = ROPE_THETA ** (-jnp.arange(half, dtype=F32) / half)
    ang = jnp.arange(t, dtype=F32)[:, None] * freqs[None, :]
    cos, sin = jnp.cos(ang), jnp.sin(ang)
    cos_t = jnp.tile(cos, (1, LANES // half))
    sin_t = jnp.tile(jnp.concatenate([-sin, sin], axis=1), (1, LANES // NARROW_DK))
    return cos_t.astype(F32), sin_t.astype(F32)


def _scan_c(p3):
    bsz, t, _ = p3.shape
    nb = t // SCAN_ROWS
    tables = _retention_tables()

    def fwd(j):
        return pl.BlockSpec((None, SCAN_ROWS, 512), lambda b, n: (b, n, j))

    def bwd(j):
        return pl.BlockSpec((None, SCAN_ROWS, 512), lambda b, n: (b, nb - 1 - n, j))

    def const(shape):
        return pl.BlockSpec(shape, lambda b, n: (0,) * len(shape))

    out = jax.ShapeDtypeStruct((bsz, t, MIX_HALF), F32)
    return pl.pallas_call(
        _scan_c_kernel,
        out_shape=(out, out),
        grid=(bsz, nb),
        in_specs=[fwd(CD_QKC), fwd(CD_VC), bwd(CD_QKC), bwd(CD_VC)] + [const(a.shape) for a in tables],
        out_specs=(pl.BlockSpec((None, SCAN_ROWS, MIX_HALF), lambda b, n: (b, n, 0)),
                   pl.BlockSpec((None, SCAN_ROWS, MIX_HALF), lambda b, n: (b, nb - 1 - n, 0))),
        scratch_shapes=[pltpu.VMEM((2 * N_HEADS, HEAD_DV, LANES), F32)],
        compiler_params=_params(2),
        name="scan_c",
    )(p3, p3, p3, p3, *tables)


def _conv_kernel(prev_ref, cur_ref, next_ref, w_ref, b_ref, o_ref, buf_ref):
    i = pl.program_id(1)
    last = pl.num_programs(1) - 1
    rows = cur_ref.shape[0]
    buf_ref[0:SUBLANES, :] = jnp.where(i == 0, 0.0, prev_ref[...])
    buf_ref[SUBLANES:SUBLANES + rows, :] = cur_ref[...]
    buf_ref[SUBLANES + rows:2 * SUBLANES + rows, :] = jnp.where(i == last, 0.0, next_ref[...])
    y = (buf_ref[SUBLANES - 1:SUBLANES - 1 + rows, :] * w_ref[0:1, :]
         + buf_ref[SUBLANES:SUBLANES + rows, :] * w_ref[1:2, :]
         + buf_ref[SUBLANES + 1:SUBLANES + 1 + rows, :] * w_ref[2:3, :]
         + b_ref[...])
    o_ref[...] = _silu(y)


def _conv_silu(p3, conv_w, conv_b):
    bsz, t, _ = p3.shape
    nb = t // ROW_TILE
    per = ROW_TILE // SUBLANES
    outs = []
    for j in (CD_QD, CD_KD):
        lo = (j - CD_QD) * 512
        outs.append(pl.pallas_call(
            _conv_kernel,
            out_shape=jax.ShapeDtypeStruct((bsz, t, 512), F32),
            grid=(bsz, nb),
            in_specs=[pl.BlockSpec((None, SUBLANES, 512), lambda b, i, j=j: (b, jnp.maximum(i * per - 1, 0), j)),
                      pl.BlockSpec((None, ROW_TILE, 512), lambda b, i, j=j: (b, i, j)),
                      pl.BlockSpec((None, SUBLANES, 512),
                                   lambda b, i, j=j: (b, jnp.minimum((i + 1) * per, t // SUBLANES - 1), j)),
                      pl.BlockSpec((3, 512), lambda b, i: (0, 0)),
                      pl.BlockSpec((1, 512), lambda b, i: (0, 0))],
            out_specs=pl.BlockSpec((None, ROW_TILE, 512), lambda b, i: (b, i, 0)),
            scratch_shapes=[pltpu.VMEM((ROW_TILE + 2 * SUBLANES, 512), F32)],
            compiler_params=_params(2),
            name="conv_silu",
        )(p3, p3, p3, conv_w[:, lo:lo + 512], conv_b[None, lo:lo + 512]))
    return outs


def _gate_lane_layout():
    lane = np.arange(LANES)
    r = lane % GATE_LANES
    return lane // (N_HEADS * GATE_LANES), (lane // GATE_LANES) % N_HEADS, r // 3, r % 3


def _gate_const_rows():
    _, _, cls, _ = _gate_lane_layout()
    return jnp.asarray(np.concatenate([np.tile((cls == k + 1).astype(np.float32)[None, :], (LANES, 1))
                                       for k in range(3)], axis=0), BF16)


def _pieces(x, piece):
    hi = x.astype(BF16).astype(F32)
    r1 = x - hi
    mid = r1.astype(BF16).astype(F32)
    lo = (r1 - mid).astype(BF16).astype(F32)
    return jnp.where(piece == 0, hi, jnp.where(piece == 1, mid, lo))


def _running_max(x, direction):
    row = lax.broadcasted_iota(jnp.int32, x.shape, 0)
    step = 1
    while step < CHUNK:
        if direction == 0:
            shifted = jnp.where(row >= step, pltpu.roll(x, step, 0), -jnp.inf)
        else:
            shifted = jnp.where(row < CHUNK - step, pltpu.roll(x, CHUNK - step, 0), -jnp.inf)
        x = jnp.maximum(x, shifted)
        step *= 2
    return x


def _scan_d_kernel(q_f, k_f, v_f, gi_f, gf_f, q_b, k_b, v_b, gi_b, gf_b, fgb_ref, const_ref, o_f, o_b, c_ref, m_ref):
    @pl.when(pl.program_id(1) == 0)
    def _():
        c_ref[...] = jnp.zeros_like(c_ref)
        m_ref[...] = jnp.zeros_like(m_ref)

    n_chunks = SCAN_ROWS // CHUNK
    dirs = ((q_f, k_f, v_f, gi_f, gf_f, o_f), (q_b, k_b, v_b, gi_b, gf_b, o_b))
    lane = lax.broadcasted_iota(jnp.int32, (1, LANES), 1)
    lane_dir = lane // (N_HEADS * GATE_LANES)
    lane_head = (lane // GATE_LANES) % N_HEADS
    cls = (lane % GATE_LANES) // 3
    piece = (lane % GATE_LANES) % 3
    ones_block = jnp.ones((CHUNK, HEAD_DV), BF16)

    def prepare(c, m_state):
        pre = []
        for d, (q_ref, k_ref, v_ref, gi_ref, gf_ref, o_ref) in enumerate(dirs):
            rows = pl.ds((c if d == 0 else n_chunks - 1 - c) * CHUNK, CHUNK)
            pre.append((rows, gi_ref[rows, :], _log_sigmoid(gf_ref[rows, :] + fgb_ref[...])))
        bcs = _chunk_cumsum([(pre[0][2], 0), (pre[1][2], 1)])
        units, m_next = [], []
        for d, (q_ref, k_ref, v_ref, gi_ref, gf_ref, o_ref) in enumerate(dirs):
            rows, gi, _ = pre[d]
            b = bcs[d]
            last = _last_row(d)
            mask = _tri(d)
            m_st = m_state[d]
            b_last = b[last:last + 1, :]
            cj = gi - b
            run = _running_max(cj, d)
            m_i = jnp.maximum(b + m_st, b + run)
            u = b - m_i
            m_new = b_last + jnp.maximum(m_st, run[last:last + 1, :])
            s_old = jnp.exp(b_last + m_st - m_new)
            m_next.append(m_new)
            quantity =jnp.where(cls == 0, u, jnp.where(cls == 1, u + m_st, jnp.where(
                cls == 2, -m_i, cj + (b_last - m_new))))
            lhs_all = jnp.where(cls == 4, 1.0, jnp.where(cls < 4, _pieces(quantity, piece), 0.0))
            lhs = jnp.concatenate([jnp.where((lane_dir == d) & (lane_head == h), lhs_all, 0.0)
                                   for h in range(N_HEADS)], axis=0).astype(BF16)
            key_rows = jnp.where(cls == 4, _pieces(cj, piece), jnp.where(cls == 0, 1.0, 0.0)).astype(BF16)
            expo = _nt(lhs, jnp.concatenate([const_ref[...], key_rows], axis=0))
            for h in range(N_HEADS):
                sl = slice(h * HEAD_DV, (h + 1) * HEAD_DV)
                q = q_ref[rows, sl] * (HEAD_DV ** -0.5)
                k = k_ref[rows, sl]
                v_ext = jnp.concatenate([v_ref[rows, sl].astype(BF16), ones_block], axis=1)
                lane0 = (d * N_HEADS + h) * GATE_LANES
                units.append(dict(u=d * N_HEADS + h, rows=rows, sl=sl, o_ref=o_ref, q=q, k=k, v_ext=v_ext,
                                  mask=mask, expo=expo[h * CHUNK:(h + 1) * CHUNK, :],
                                  s_old=s_old[:, lane0:lane0 + 1], qk=_nt(q.astype(BF16), k.astype(BF16))))
        for t in units:
            e = t.pop("expo")
            w = t.pop("qk") * jnp.exp(jnp.where(t.pop("mask"), e[:, 3 * LANES:3 * LANES + CHUNK], -jnp.inf))
            t["lhs2"] = jnp.concatenate([(t.pop("q") * jnp.exp(e[:, :LANES])).astype(BF16), w.astype(BF16)], axis=1)
            t["floor"] = jnp.exp(e[:, LANES:2 * LANES])
            t["upd"] = _tn((t.pop("k") * jnp.exp(e[:, 2 * LANES:3 * LANES])).astype(BF16), t["v_ext"])
        return units, m_next

    def finish(units):
        nums = [_mm(t["lhs2"], jnp.concatenate([c_ref[t["u"]].astype(BF16), t["v_ext"]], axis=0)) for t in units]
        for t, num in zip(units, nums):
            den = jnp.maximum(jnp.abs(num[:, HEAD_DV:]), t["floor"])
            t["o_ref"][t["rows"], t["sl"]] = num[:, :HEAD_DV] / den
            c_ref[t["u"]] = t["s_old"] * c_ref[t["u"]] + t["upd"]

    m_state = [m_ref[d][0:1, :] for d in range(2)]
    pending, m_state = prepare(0, m_state)
    for c in range(n_chunks):
        following = None
        if c + 1 < n_chunks:
            following, m_state = prepare(c + 1, m_state)
        finish(pending)
        pending = following
    for d in range(2):
        m_ref[d] = jnp.broadcast_to(m_state[d], (SUBLANES, LANES))


def _scan_d(qd, kd, p3, fgb_row):
    bsz, t, _ = p3.shape
    nb = t // SCAN_ROWS
    const_rows = _gate_const_rows()

    def spec(j, w, flip):
        if flip:
            return pl.BlockSpec((None, SCAN_ROWS, w), lambda b, n: (b, nb - 1 - n, j))
        return pl.BlockSpec((None, SCAN_ROWS, w), lambda b, n: (b, n, j))

    out = jax.ShapeDtypeStruct((bsz, t, MIX_HALF), F32)
    in_specs = []
    for flip in (False, True):
        in_specs += [spec(0, 512, flip), spec(0, 512, flip), spec(CD_VD, 512, flip),
                     spec(CD_GATE_I_BLOCK, LANES, flip), spec(CD_GATE_F_BLOCK, LANES, flip)]
    in_specs += [pl.BlockSpec((1, LANES), lambda b, n: (0, 0)), pl.BlockSpec(const_rows.shape, lambda b, n: (0, 0))]
    return pl.pallas_call(
        _scan_d_kernel,
        out_shape=(out, out),
        grid=(bsz, nb),
        in_specs=in_specs,
        out_specs=(spec(0, MIX_HALF, False), spec(0, MIX_HALF, True)),
        scratch_shapes=[pltpu.VMEM((2 * N_HEADS, HEAD_DV, 2 * HEAD_DV), F32),
                        pltpu.VMEM((2, SUBLANES, LANES), F32)],
        compiler_params=_params(2),
        name="scan_d",
    )(qd, kd, p3, p3, p3, qd, kd, p3, p3, p3, fgb_row, const_rows)


def _layer_norm(z, g, b):
    zc = z - jnp.mean(z, axis=-1, keepdims=True)
    var = jnp.mean(zc * zc, axis=-1, keepdims=True)
    return zc * lax.rsqrt(var + LN_EPS) * g + b


def _mix_out_kernel(o1f, o1b, o2f, o2b, g1, g2, x_ref, norm_ref, w_ref, lng, lnb, rw_ref, rb_ref,
                    h_ref, hb_ref, idx_ref, gate_ref, *, second_gate):
    parts = []
    for of, ob, g_ref, gate_fn, base in ((o1f, o1b, g1, _silu, 0), (o2f, o2b, g2, second_gate, MIX_HALF)):
        o = of[...] + ob[...]
        gate = gate_fn(g_ref[...])
        for h in range(N_HEADS):
            sl = slice(h * HEAD_DV, (h + 1) * HEAD_DV)
            oh = o[:, sl]
            oh = oh * lax.rsqrt(jnp.mean(oh * oh, axis=-1, keepdims=True) + NORM_EPS)
            parts.append((oh * norm_ref[:, base + h * HEAD_DV:base + (h + 1) * HEAD_DV] * gate[:, sl]).astype(BF16))
    mixed = jnp.concatenate(parts, axis=1)
    z = ALPHA * x_ref[...] + _mm(mixed, w_ref[...])
    hn = _layer_norm(z, lng[...], lnb[...])
    h_ref[...] = hn
    hb_ref[...] = hn.astype(BF16)
    _route(hn, rw_ref, rb_ref, idx_ref, gate_ref)


def _mix_out(o1f, o1b, o2f, o2b, p2d, gate_blocks, x2d, norm, w_out, ln_g, ln_b, rw_t, rb_col, second_gate):
    n = x2d.shape[0]
    half = pl.BlockSpec((ROW_TILE, MIX_HALF), lambda i: (i, 0))
    full = pl.BlockSpec((ROW_TILE, D_MODEL), lambda i: (i, 0))
    row = pl.BlockSpec((1, D_MODEL), lambda i: (0, 0))
    routed = pl.BlockSpec((TOP_K, ROW_TILE), lambda i: (0, i))
    ga, gb = gate_blocks
    return pl.pallas_call(
        functools.partial(_mix_out_kernel, second_gate=second_gate),
        out_shape=(jax.ShapeDtypeStruct((n, D_MODEL), F32), jax.ShapeDtypeStruct((n, D_MODEL), BF16),
                   jax.ShapeDtypeStruct((TOP_K, n), jnp.int32), jax.ShapeDtypeStruct((TOP_K, n), F32)),
        grid=(n // ROW_TILE,),
        in_specs=[half, half, half, half,
                  pl.BlockSpec((ROW_TILE, MIX_HALF), lambda i: (i, ga)),
                  pl.BlockSpec((ROW_TILE, MIX_HALF), lambda i: (i, gb)),
                  full, row, pl.BlockSpec((D_MODEL, D_MODEL), lambda i: (0, 0)), row, row,
                  pl.BlockSpec((N_EXPERTS, D_MODEL), lambda i: (0, 0)),
                  pl.BlockSpec((N_EXPERTS, 1), lambda i: (0, 0))],
        out_specs=(full, full, routed, routed),
        compiler_params=_params(1),
        name="mix_out",
    )(o1f, o1b, o2f, o2b, p2d, p2d, x2d, norm, w_out, ln_g, ln_b, rw_t, rb_col)


def _route(h, rw_ref, rb_ref, idx_ref, gate_ref):
    logits = lax.dot_general(rw_ref[...], h, (((1,), (1,)), ((), ())),
                             precision=HIGHEST, preferred_element_type=F32)
    rows = [logits[e:e + 1, :] for e in range(N_EXPERTS)]
    mx = functools.reduce(jnp.maximum, rows)
    ex = [jnp.exp(r - mx) for r in rows]
    tot = functools.reduce(lambda a, b: a + b, ex)
    score = [e / tot for e in ex]
    biased = [score[e] + rb_ref[e:e + 1, :] for e in range(N_EXPERTS)]

    def argmax_first(vals):
        best, idx = vals[0], jnp.zeros(vals[0].shape, jnp.int32)
        for j in range(1, len(vals)):
            upd = vals[j] > best
            best = jnp.where(upd, vals[j], best)
            idx = jnp.where(upd, j, idx)
        return best, idx

    def pick(vals, idx):
        out = vals[0]
        for j in range(1, len(vals)):
            out = jnp.where(idx == j, vals[j], out)
        return out

    group_scores = []
    for g in range(N_GROUPS):
        a = biased[g * EXPERTS_PER_GROUP:(g + 1) * EXPERTS_PER_GROUP]
        pairs = [a[i] + a[j] for i in range(EXPERTS_PER_GROUP) for j in range(i + 1, EXPERTS_PER_GROUP)]
        group_scores.append(functools.reduce(jnp.maximum, pairs))
    _, g_sel = argmax_first(group_scores)
    in_b = [pick([biased[g * EXPERTS_PER_GROUP + k] for g in range(N_GROUPS)], g_sel) for k in range(EXPERTS_PER_GROUP)]
    in_s = [pick([score[g * EXPERTS_PER_GROUP + k] for g in range(N_GROUPS)], g_sel) for k in range(EXPERTS_PER_GROUP)]
    _, i0 = argmax_first(in_b)
    _, i1 = argmax_first([jnp.where(i0 == k, -jnp.inf, in_b[k]) for k in range(EXPERTS_PER_GROUP)])
    s0, s1 = pick(in_s, i0), pick(in_s, i1)
    den = s0 + s1
    idx_ref[0:1, :] = g_sel * EXPERTS_PER_GROUP + i0
    idx_ref[1:2, :] = g_sel * EXPERTS_PER_GROUP + i1
    gate_ref[0:1, :] = s0 / den
    gate_ref[1:2, :] = s1 / den


def _plan_kernel(idx_ref, dest_ref, cnt_ref, su_ref, run_ref, start_ref):
    phase, i = pl.program_id(0), pl.program_id(1)
    tile = idx_ref.shape[1]
    e_iota = lax.broadcasted_iota(jnp.int32, (N_EXPERTS, tile), 0)
    oh0 = idx_ref[0:1, :] == e_iota
    oh1 = idx_ref[1:2, :] == e_iota
    member = jnp.where(oh0 | oh1, 1.0, 0.0)
    tile_count = jnp.broadcast_to(jnp.sum(member, axis=1, keepdims=True), (N_EXPERTS, LANES))

    @pl.when((phase == 0) & (i == 0))
    def _():
        run_ref[...] = jnp.zeros_like(run_ref)
        r = lax.broadcasted_iota(jnp.int32, (tile, tile), 0)
        c = lax.broadcasted_iota(jnp.int32, (tile, tile), 1)
        su_ref[...] = jnp.where(r < c, 1.0, 0.0).astype(BF16)

    @pl.when((phase == 1) & (i == 0))
    def _():
        counts = run_ref[...]
        cnt_ref[...] = counts.astype(jnp.int32)
        padded = jnp.ceil(counts * (1.0 / EXPERT_ROWS)) * EXPERT_ROWS
        r = lax.broadcasted_iota(jnp.int32, (N_EXPERTS, N_EXPERTS), 0)
        c = lax.broadcasted_iota(jnp.int32, (N_EXPERTS, N_EXPERTS), 1)
        start_ref[...] = jnp.dot(jnp.where(c < r, 1.0, 0.0), padded, precision=HIGHEST, preferred_element_type=F32)
        run_ref[...] = jnp.zeros_like(run_ref)

    @pl.when(phase == 1)
    def _():
        before = _mm(member.astype(BF16), su_ref[...])
        pos = start_ref[:, 0:1] + run_ref[:, 0:1] + before
        dest_ref[0:1, :] = jnp.sum(jnp.where(oh0, pos, 0.0), axis=0, keepdims=True).astype(jnp.int32)
        dest_ref[1:2, :] = jnp.sum(jnp.where(oh1, pos, 0.0), axis=0, keepdims=True).astype(jnp.int32)

    run_ref[...] += tile_count


def _plan(idx):
    n = idx.shape[1]
    tile = min(ROUTER_TILE, n)
    return pl.pallas_call(
        _plan_kernel,
        out_shape=(jax.ShapeDtypeStruct((TOP_K, n), jnp.int32), jax.ShapeDtypeStruct((N_EXPERTS, LANES), jnp.int32)),
        grid=(2, n // tile),
        in_specs=[pl.BlockSpec((TOP_K, tile), lambda p, i: (0, i))],
        out_specs=(pl.BlockSpec((TOP_K, tile), lambda p, i: (0, i * p)),
                   pl.BlockSpec((N_EXPERTS, LANES), lambda p, i: (0, 0))),
        scratch_shapes=[pltpu.VMEM((tile, tile), BF16), pltpu.VMEM((N_EXPERTS, LANES), F32),
                        pltpu.VMEM((N_EXPERTS, LANES), F32)],
        compiler_params=_params(2),
        name="moe_plan",
    )(idx)


def _invert_kernel(pad_ref, dest_ref, tok_ref):
    n = dest_ref.shape[0] // TOP_K

    def clear(r, carry):
        tok_ref[r] = lax.rem(r, n)
        return carry

    for e in range(N_EXPERTS + 1):
        lax.fori_loop(pad_ref[0, e], pad_ref[1, e], clear, 0)

    for k in range(TOP_K):
        def place(t, carry, k=k):
            tok_ref[dest_ref[k * n + t]] = t
            return carry

        lax.fori_loop(0, n, place, 0, unroll=16)


def _invert(pad_ranges, dest_flat, n_rows):
    smem = pl.BlockSpec(memory_space=pltpu.SMEM)
    return pl.pallas_call(
        _invert_kernel,
        out_shape=jax.ShapeDtypeStruct((n_rows,), jnp.int32),
        in_specs=[smem, smem],
        out_specs=smem,
        name="moe_invert",
    )(pad_ranges, dest_flat)


def _expert_kernel(be_ref, nv_ref, x_ref, w1_ref, w3_ref, w2_ref, y_ref, wb_ref):
    i = pl.program_id(0)
    valid = i < nv_ref[0]
    new_expert = (i == 0) | (be_ref[i] != be_ref[jnp.maximum(i - 1, 0)])

    @pl.when(valid & new_expert)
    def _():
        wb_ref[0] = w1_ref[...].astype(BF16)
        wb_ref[1] = w3_ref[...].astype(BF16)
        wb_ref[2] = w2_ref[...].astype(BF16)

    @pl.when(valid)
    def _():
        x = x_ref[...]
        hid = _silu(_mm(x, wb_ref[0])) * _mm(x, wb_ref[1])
        y_ref[...] = _mm(hid.astype(BF16), wb_ref[2]).astype(BF16)

    @pl.when(jnp.logical_not(valid))
    def _():
        y_ref[...] = jnp.zeros_like(y_ref)


def _experts(block_expert, n_valid, xs, w1, w3, w2, layer):
    n_rows = xs.shape[0]
    wspec = pl.BlockSpec((None, None, D_MODEL, D_MODEL), lambda i, be, nv: (layer, be[i], 0, 0))
    return pl.pallas_call(
        _expert_kernel,
        out_shape=jax.ShapeDtypeStruct((n_rows, D_MODEL), BF16),
        grid_spec=pltpu.PrefetchScalarGridSpec(
            num_scalar_prefetch=2,
            grid=(n_rows // EXPERT_ROWS,),
            in_specs=[pl.BlockSpec((EXPERT_ROWS, D_MODEL), lambda i, be, nv: (i, 0)), wspec, wspec, wspec],
            out_specs=pl.BlockSpec((EXPERT_ROWS, D_MODEL), lambda i, be, nv: (i, 0)),
            scratch_shapes=[pltpu.VMEM((3, D_MODEL, D_MODEL), BF16)]),
        compiler_params=_params(1),
        name="experts",
    )(block_expert, n_valid, xs, w1, w3, w2)


def _combine_kernel(h_ref, y0_ref, y1_ref, g_ref, lng, lnb, o_ref):
    g = g_ref[...]
    z = ALPHA * h_ref[...] + (g[:, 0:1] * y0_ref[...].astype(F32) + g[:, 1:2] * y1_ref[...].astype(F32))
    o_ref[...] = _layer_norm(z, lng[...], lnb[...])


def _combine(h2d, y_rows, gates_t, ln_g, ln_b):
    n = h2d.shape[0]
    nt = n // ROW_TILE
    full = pl.BlockSpec((ROW_TILE, D_MODEL), lambda i: (i, 0))
    row = pl.BlockSpec((1, D_MODEL), lambda i: (0, 0))
    return pl.pallas_call(
        _combine_kernel,
        out_shape=jax.ShapeDtypeStruct((n, D_MODEL), F32),
        grid=(nt,),
        in_specs=[full, full, pl.BlockSpec((ROW_TILE, D_MODEL), lambda i: (nt + i, 0)),
                  pl.BlockSpec((ROW_TILE, TOP_K), lambda i: (i, 0)), row, row],
        out_specs=full,
        compiler_params=_params(1),
        name="moe_combine",
    )(h2d, y_rows, y_rows, gates_t, ln_g, ln_b)


def _moe(h2d, h_bf16, idx, gates, w1, w3, w2, layer, ln_g, ln_b):
    n = h2d.shape[0]
    dest, counts = _plan(idx)
    n_rows = n * TOP_K + N_EXPERTS * EXPERT_ROWS
    n_blocks = n_rows // EXPERT_ROWS
    counts = counts[:, 0]
    padded = (counts + EXPERT_ROWS - 1) // EXPERT_ROWS * EXPERT_ROWS
    pad_end = jnp.cumsum(padded)
    block_start = jnp.arange(n_blocks, dtype=jnp.int32) * EXPERT_ROWS
    block_expert = jnp.minimum(jnp.sum((pad_end[None, :] <= block_start[:, None]).astype(jnp.int32), axis=1),
                               N_EXPERTS - 1)
    n_valid = (pad_end[-1:] // EXPERT_ROWS).astype(jnp.int32)
    pad_ranges = jnp.stack([jnp.concatenate([pad_end - padded + counts, pad_end[-1:]]),
                            jnp.concatenate([pad_end, jnp.full((1,), n_rows, jnp.int32)])]).astype(jnp.int32)
    dest_flat = dest.reshape(-1)
    xs = h_bf16[_invert(pad_ranges, dest_flat, n_rows)]
    y = _experts(block_expert, n_valid, xs, w1, w3, w2, layer)
    return _combine(h2d, y[dest_flat], gates.T, ln_g, ln_b)


def _pad_cols(w, b, main, width):
    k = w.shape[0]
    tail = w.shape[1] - main
    w_p = jnp.concatenate([w[:, :main], w[:, main:], jnp.zeros((k, width - main - tail), w.dtype)], axis=1)
    b_p = jnp.concatenate([b[:main], b[main:], jnp.zeros((width - main - tail,), b.dtype)])
    return w_p.astype(BF16), b_p[None, :].astype(F32)


def kernel(x, ab_w_in, ab_b_in, hgrn_lb, gla_gk_up, gla_gk_b, hgrn_norm, gla_norm, ab_w_out, cd_w_in, cd_b_in,
           mlstm_conv_w, mlstm_conv_b, mlstm_fgate_b, ret_norm, mlstm_norm, cd_w_out, ln_mix_g, ln_mix_b, ln_ffn_g,
           ln_ffn_b, router_w, router_b, moe_w1, moe_w3, moe_w2):
    bsz, t, d = x.shape
    n = bsz * t
    lower_bounds = jnp.cumsum(jax.nn.softmax(hgrn_lb.astype(F32), axis=0), axis=0)
    rw_t = router_w.T.astype(F32)
    rb_col = router_b.astype(F32)[:, None]
    h = x.reshape(n, d)
    for layer in range(DEPTH):
        j = layer // 2
        if layer % 2 == 0:
            w_p, b_p = _pad_cols(ab_w_in[j], ab_b_in[j], 8 * 512, AB_WIDTH)
            lb = lower_bounds[layer][None, :]
            p = _in_proj(h, w_p, b_p, _proj_ab_kernel, [lb], [pl.BlockSpec(lb.shape, lambda i: (0, 0))])
            p3 = p.reshape(bsz, t, AB_WIDTH)
            nk = N_HEADS * NARROW_DK
            u_pad = jnp.zeros((LANES, 2 * nk), F32)
            u_pad = u_pad.at[:GLA_LOWRANK, :nk].set(gla_gk_up[j, 0])
            u_pad = u_pad.at[GLA_LOWRANK:2 * GLA_LOWRANK, nk:].set(gla_gk_up[j, 1]).astype(BF16)
            gkb = jnp.concatenate([gla_gk_b[j, 0], gla_gk_b[j, 1]])[None, :].astype(F32)
            oa_f, ob_f, oa_b, ob_b = _scan_ab(p3, u_pad, gkb)
            norm = jnp.concatenate([hgrn_norm[j], gla_norm[j]])[None, :].astype(F32)
            outs = [a.reshape(n, MIX_HALF) for a in (oa_f, oa_b, ob_f, ob_b)]
            h, h_bf16, idx, gates = _mix_out(*outs, p, (AB_GA, AB_GB), h, norm, ab_w_out[j].astype(BF16),
                                     ln_mix_g[layer][None, :], ln_mix_b[layer][None, :], rw_t, rb_col, _silu)
        else:
            lane_dir, lane_head, _, _ = _gate_lane_layout()
            col_i = 7 * 512 + lane_dir * 2 * N_HEADS + lane_head
            cols = np.concatenate([np.arange(7 * 512), col_i, col_i + N_HEADS])
            w_p, b_p = cd_w_in[j][:, cols].astype(BF16), cd_b_in[j][cols][None, :].astype(F32)
            pos = pl.BlockSpec((ROW_TILE, LANES), lambda i: (i % (t // ROW_TILE), 0))
            p = _in_proj(h, w_p, b_p, _proj_cd_kernel, list(_rotary_tables(t)), [pos, pos])
            p3 = p.reshape(bsz, t, CD_WIDTH)
            oc_f, oc_b = _scan_c(p3)
            qd, kd = _conv_silu(p3, mlstm_conv_w[j].astype(F32), mlstm_conv_b[j].astype(F32))
            fgb_row = mlstm_fgate_b[j].astype(F32)[lane_dir, lane_head][None, :]
            od_f, od_b = _scan_d(qd, kd, p3, fgb_row)
            norm = jnp.concatenate([ret_norm[j], mlstm_norm[j]])[None, :].astype(F32)
            outs = [a.reshape(n, MIX_HALF) for a in (oc_f, oc_b, od_f, od_b)]
            h, h_bf16, idx, gates = _mix_out(*outs, p, (CD_GC, CD_OD), h, norm, cd_w_out[j].astype(BF16),
                                     ln_mix_g[layer][None, :], ln_mix_b[layer][None, :], rw_t, rb_col, _sigmoid)
        h = _moe(h, h_bf16, idx, gates, moe_w1, moe_w3, moe_w2, layer,
                 ln_ffn_g[layer][None, :], ln_ffn_b[layer][None, :])
    return h.reshape(bsz, t, d)
```

```python
import functools

import jax
import jax.numpy as jnp
import numpy as np
from jax import lax
from jax.experimental import pallas as pl
from jax.experimental.pallas import tpu as pltpu

F32 = jnp.float32
BF16 = jnp.bfloat16
HIGHEST = lax.Precision.HIGHEST

D_MODEL = 1024
DEPTH = 2
CHUNK = 64
MIX_HALF = D_MODEL // 2
N_HEADS = 4
HEAD_DV = MIX_HALF // N_HEADS
NARROW_DK = 64
GLA_LOWRANK = 16
GLA_GATE_NORMALIZER = 16.0
ROPE_THETA = 10000.0
N_EXPERTS = 16
N_GROUPS = 4
EXPERTS_PER_GROUP = N_EXPERTS // N_GROUPS
TOP_K = 2
ALPHA = (2.0 * DEPTH) ** 0.25
LN_EPS = 1e-5
NORM_EPS = 1e-6

LANES = 128
SUBLANES = 8
VMEM_LIMIT = 52 * 1024 * 1024

ROW_TILE = 256
SCAN_ROWS = 256
ROUTER_TILE = 2048
PLAN_TILE = 1024
EXPERT_ROWS = 512

AB_QA, AB_FF, AB_FB, AB_IA, AB_GA, AB_QKB, AB_VB, AB_GB = range(8)
AB_LR_BLOCK = 32
AB_WIDTH = 8 * 512 + LANES
CD_QKC, CD_VC, CD_GC, CD_QD, CD_KD, CD_VD, CD_OD = range(7)
CD_GATE_I_BLOCK = 28
CD_GATE_F_BLOCK = 29
CD_WIDTH = 7 * 512 + 2 * LANES
GATE_LANES = 16


def _params(n_axes, vmem=VMEM_LIMIT):
    return pltpu.CompilerParams(dimension_semantics=("arbitrary",) * n_axes, vmem_limit_bytes=vmem)


def _nt(a, b):
    return lax.dot_general(a, b, (((1,), (1,)), ((), ())), preferred_element_type=F32)


def _tn(a, b):
    return lax.dot_general(a, b, (((0,), (0,)), ((), ())), preferred_element_type=F32)


def _mm(a, b):
    return jnp.dot(a, b, preferred_element_type=F32)


def _sigmoid(x):
    return 1.0 / (1.0 + jnp.exp(-x))


def _silu(x):
    return x * _sigmoid(x)


def _log_sigmoid(x):
    return jnp.minimum(x, 0.0) - jnp.log(1.0 + jnp.exp(-jnp.abs(x)))


def _tri(direction):
    row = lax.broadcasted_iota(jnp.int32, (CHUNK, CHUNK), 0)
    col = lax.broadcasted_iota(jnp.int32, (CHUNK, CHUNK), 1)
    return (col <= row) if direction == 0 else (col >= row)


def _ref_row(direction):
    return CHUNK // 2 - 1 if direction == 0 else CHUNK // 2


def _last_row(direction):
    return CHUNK - 1 if direction == 0 else 0


def _proj_ab_kernel(x_ref, w_ref, b_ref, lb_ref, o_ref):
    p = _mm(x_ref[...].astype(BF16), w_ref[...]) + b_ref[...]
    o_ref[:, :512] = _silu(p[:, :512])
    lb = jnp.concatenate([lb_ref[...], lb_ref[...]], axis=1)
    o_ref[:, 512:1536] = lb + (1.0 - lb) * _sigmoid(p[:, 512:1536])
    o_ref[:, 1536:] = p[:, 1536:]
    for blk in (AB_GA, AB_GB):
        o_ref[:, blk * 512:(blk + 1) * 512] = _silu(p[:, blk * 512:(blk + 1) * 512])


def _proj_cd_kernel(x_ref, w_ref, b_ref, cos_ref, sin_ref, o_ref):
    p = _mm(x_ref[...].astype(BF16), w_ref[...]) + b_ref[...]
    cos, sin_signed = cos_ref[...], sin_ref[...]
    for j in range(4):
        rot = _rotary(p[:, j * LANES:(j + 1) * LANES], cos, sin_signed)
        o_ref[:, j * LANES:(j + 1) * LANES] = rot * (NARROW_DK ** -0.5) if j < 2 else rot
    o_ref[:, 512:] = p[:, 512:]
    o_ref[:, CD_GC * 512:(CD_GC + 1) * 512] = _silu(p[:, CD_GC * 512:(CD_GC + 1) * 512])
    o_ref[:, CD_OD * 512:(CD_OD + 1) * 512] = _sigmoid(p[:, CD_OD * 512:(CD_OD + 1) * 512])


def _in_proj(x2d, w, b, body, extras, extra_specs):
    n, k = x2d.shape
    m = w.shape[1]
    return pl.pallas_call(
        body,
        out_shape=jax.ShapeDtypeStruct((n, m), F32),
        grid=(n // ROW_TILE,),
        in_specs=[pl.BlockSpec((ROW_TILE, k), lambda i: (i, 0)),
                  pl.BlockSpec((k, m), lambda i: (0, 0)),
                  pl.BlockSpec((1, m), lambda i: (0, 0))] + extra_specs,
        out_specs=pl.BlockSpec((ROW_TILE, m), lambda i: (i, 0)),
        compiler_params=_params(1),
        name="in_proj",
    )(x2d, w, b, *extras)


def _pipelined_chunks(n_chunks, prepare, finish):
    pending = prepare(0)
    for c in range(n_chunks):
        following = prepare(c + 1) if c + 1 < n_chunks else None
        finish(pending)
        pending = following


def _units_prepare(units):
    scores = [_nt(u["qe"], u["ke"]) for u in units]
    for u, s in zip(units, scores):
        u["upd"] = _tn(u["v"], u["ks"])
        u["o_intra"] = _mm(u["intra"](s).astype(BF16), u["v"])
    return units


def _units_finish(units, st_ref):
    inter = [_nt(u["qi"], st_ref[u["idx"]].astype(BF16)) for u in units]
    for u, o_inter in zip(units, inter):
        u["store"](u["o_intra"] + o_inter)
        st_ref[u["idx"]] = st_ref[u["idx"]] * u["dec"] + u["upd"]


def _chunk_cumsum(parts):
    cat = jnp.concatenate([x for x, _ in parts], axis=1)
    hi = cat.astype(BF16)
    lo = (cat - hi.astype(F32)).astype(BF16)
    width = cat.shape[1]
    inc = _mm(_tri(0).astype(BF16), jnp.concatenate([hi, lo], axis=1))
    inc = inc[:, :width] + inc[:, width:]
    out, off = [], 0
    for x, direction in parts:
        p = inc[:, off:off + x.shape[1]]
        off += x.shape[1]
        out.append(p if direction == 0 else p[CHUNK - 1:CHUNK, :] - p + x)
    return out


def _gated_operands(q, k, gc, direction):
    r, l = _ref_row(direction), _last_row(direction)
    g_ref = gc[r:r + 1, :]
    g_last = gc[l:l + 1, :]
    qe = q * jnp.exp(gc - g_ref)
    ke = k * jnp.exp(g_ref - gc)
    qi = qe * jnp.exp(g_ref)
    ks = ke * jnp.exp(g_last - g_ref)
    return [a.astype(BF16) for a in (qe, ke, qi, ks)], jnp.exp(g_last)


def _narrow_head_mask(h):
    lane = lax.broadcasted_iota(jnp.int32, (1, LANES), 1)
    return (lane // NARROW_DK) == (h % 2)


def _scan_ab_kernel(qa_f, fa_f, ia_f, qkb_f, vb_f, lr_f, qa_b, fa_b, ia_b, qkb_b, vb_b, lr_b,
                    u_ref, gkb_ref, oa_f, ob_f, oa_b, ob_b, st_ref):
    @pl.when(pl.program_id(1) == 0)
    def _():
        st_ref[...] = jnp.zeros_like(st_ref)

    n_chunks = SCAN_ROWS // CHUNK
    dirs = ((qa_f, fa_f, ia_f, qkb_f, vb_f, lr_f, oa_f, ob_f), (qa_b, fa_b, ia_b, qkb_b, vb_b, lr_b, oa_b, ob_b))

    nk = N_HEADS * NARROW_DK

    def prepare(c):
        pre = []
        for d, (qa, fa, ia, qkb, vb, lr, oa, ob) in enumerate(dirs):
            rows = pl.ds((c if d == 0 else n_chunks - 1 - c) * CHUNK, CHUNK)
            f = fa[rows, :]
            logits = _mm(lr[rows, :].astype(BF16), u_ref[:, d * nk:(d + 1) * nk]) + gkb_ref[:, d * nk:(d + 1) * nk]
            qk = qkb[rows, :]
            pre.append(dict(rows=rows, f=f, lf_a=jnp.log(f), lf_b=_log_sigmoid(logits) / GLA_GATE_NORMALIZER,
                            q_a=qa[rows, :], v_a=ia[rows, :].astype(BF16),
                            q_b=qk[:, :nk] * (NARROW_DK ** -0.5), k_b=qk[:, nk:], v_b=vb[rows, :].astype(BF16)))
        gcs = _chunk_cumsum([(pre[0]["lf_a"], 0), (pre[0]["lf_b"], 0), (pre[1]["lf_a"], 1), (pre[1]["lf_b"], 1)])
        units = []
        for d, (qa, fa, ia, qkb, vb, lr, oa, ob) in enumerate(dirs):
            p = pre[d]
            rows = p["rows"]
            mask = _tri(d)
            intra = lambda s, mask=mask: jnp.where(mask, s, 0.0)
            (qe, ke, qi, ks), dec = _gated_operands(p["q_a"], 1.0 - p["f"], gcs[2 * d], d)
            for h in range(N_HEADS):
                sl = slice(h * LANES, (h + 1) * LANES)

                def store(o, oa=oa, rows=rows, sl=sl):
                    oa[rows, sl] = o

                units.append(dict(qe=qe[:, sl], ke=ke[:, sl], qi=qi[:, sl], ks=ks[:, sl], v=p["v_a"][:, sl],
                                  dec=dec[:, sl], intra=intra, idx=d * 2 * N_HEADS + h, store=store))
            (qe, ke, qi, ks), dec = _gated_operands(p["q_b"], p["k_b"], gcs[2 * d + 1], d)
            for h in range(N_HEADS):
                psl = slice((h // 2) * LANES, (h // 2 + 1) * LANES)
                hm = _narrow_head_mask(h)
                pick = lambda a, psl=psl, hm=hm: jnp.where(hm, a[:, psl], jnp.zeros((), BF16))
                vsl = slice(h * HEAD_DV, (h + 1) * HEAD_DV)

                def store(o, ob=ob, rows=rows, vsl=vsl):
                    ob[rows, vsl] = o

                units.append(dict(qe=pick(qe), ke=pick(ke), qi=pick(qi), ks=pick(ks), v=p["v_b"][:, vsl],
                                  dec=dec[:, psl], intra=intra, idx=d * 2 * N_HEADS + N_HEADS + h, store=store))
        return _units_prepare(units)

    _pipelined_chunks(n_chunks, prepare, functools.partial(_units_finish, st_ref=st_ref))


def _scan_ab(p3, u_pad, gkb):
    bsz, t, _ = p3.shape
    nb = t // SCAN_ROWS

    def fwd(j, w=512):
        return pl.BlockSpec((None, SCAN_ROWS, w), lambda b, n: (b, n, j))

    def bwd(j, w=512):
        return pl.BlockSpec((None, SCAN_ROWS, w), lambda b, n: (b, nb - 1 - n, j))

    def const(shape):
        return pl.BlockSpec(shape, lambda b, n: (0,) * len(shape))

    out = jax.ShapeDtypeStruct((bsz, t, MIX_HALF), F32)
    o_f = pl.BlockSpec((None, SCAN_ROWS, MIX_HALF), lambda b, n: (b, n, 0))
    o_b = pl.BlockSpec((None, SCAN_ROWS, MIX_HALF), lambda b, n: (b, nb - 1 - n, 0))
    return pl.pallas_call(
        _scan_ab_kernel,
        out_shape=(out, out, out, out),
        grid=(bsz, nb),
        in_specs=[fwd(AB_QA), fwd(AB_FF), fwd(AB_IA), fwd(AB_QKB), fwd(AB_VB), fwd(AB_LR_BLOCK, LANES),
                  bwd(AB_QA), bwd(AB_FB), bwd(AB_IA), bwd(AB_QKB), bwd(AB_VB), bwd(AB_LR_BLOCK, LANES),
                  const(u_pad.shape), const(gkb.shape)],
        out_specs=(o_f, o_f, o_b, o_b),
        scratch_shapes=[pltpu.VMEM((4 * N_HEADS, HEAD_DV, LANES), F32)],
        compiler_params=_params(2),
        name="scan_ab",
    )(*([p3] * 12), u_pad, gkb)


def _rotary(a, cos, sin_signed):
    lane = lax.broadcasted_iota(jnp.int32, (1, LANES), 1)
    first_half = (lane % NARROW_DK) < (NARROW_DK // 2)
    swapped = jnp.where(first_half, pltpu.roll(a, LANES - NARROW_DK // 2, 1), pltpu.roll(a, NARROW_DK // 2, 1))
    return a * cos + swapped * sin_signed


def _scan_c_kernel(qk_f, v_f, qk_b, v_b, dmat_ref, qsc_ref, ksc_ref, cd_ref, o_f, o_b, st_ref):
    @pl.when(pl.program_id(1) == 0)
    def _():
        st_ref[...] = jnp.zeros_like(st_ref)

    n_chunks = SCAN_ROWS // CHUNK
    dirs = ((qk_f, v_f, o_f), (qk_b, v_b, o_b))

    def prepare(c):
        units = []
        for d, (qk_ref, v_ref, o_ref) in enumerate(dirs):
            rows = pl.ds((c if d == 0 else n_chunks - 1 - c) * CHUNK, CHUNK)
            v = v_ref[rows, :].astype(BF16)
            for h in range(N_HEADS):
                hm = _narrow_head_mask(h)
                q = jnp.where(hm, qk_ref[rows, (h // 2) * LANES:(h // 2 + 1) * LANES], 0.0)
                k = jnp.where(hm, qk_ref[rows, (2 + h // 2) * LANES:(3 + h // 2) * LANES], 0.0)
                vsl = slice(h * HEAD_DV, (h + 1) * HEAD_DV)

                def store(o, o_ref=o_ref, rows=rows, vsl=vsl):
                    o_ref[rows, vsl] = o

                units.append(dict(qe=q.astype(BF16), ke=k.astype(BF16), qi=(q * qsc_ref[d, h]).astype(BF16),
                                  ks=(k * ksc_ref[d, h]).astype(BF16), v=v[:, vsl], dec=cd_ref[d, h][0:1, :],
                                  intra=lambda s, d=d, h=h: s * dmat_ref[d, h], idx=d * N_HEADS + h, store=store))
        return _units_prepare(units)

    _pipelined_chunks(n_chunks, prepare, functools.partial(_units_finish, st_ref=st_ref))


def _retention_tables():
    pos = np.arange(CHUNK, dtype=np.float64)
    dmat = np.zeros((2, N_HEADS, CHUNK, CHUNK), np.float32)
    qsc = np.zeros((2, N_HEADS, CHUNK, LANES), np.float32)
    ksc = np.zeros((2, N_HEADS, CHUNK, LANES), np.float32)
    cd = np.zeros((2, N_HEADS, SUBLANES, LANES), np.float32)
    for d in range(2):
        for h in range(N_HEADS):
            lg = np.log1p(-np.exp2(-(5.0 + 2.0 * h + d)))
            dist = pos[:, None] - pos[None, :]
            if d == 0:
                dmat[d, h] = np.where(dist >= 0, np.exp(np.maximum(dist, 0.0) * lg), 0.0)
                qsc[d, h] = np.exp((pos + 1.0) * lg)[:, None]
                ksc[d, h] = np.exp((CHUNK - 1.0 - pos) * lg)[:, None]
            else:
                dmat[d, h] = np.where(dist <= 0, np.exp(np.maximum(-dist, 0.0) * lg), 0.0)
                qsc[d, h] = np.exp((CHUNK - pos) * lg)[:, None]
                ksc[d, h] = np.exp(pos * lg)[:, None]
            cd[d, h] = np.exp(CHUNK * lg)
    return jnp.asarray(dmat), jnp.asarray(qsc), jnp.asarray(ksc), jnp.asarray(cd)


def _rotary_tables(t):
    half = NARROW_DK // 2
    freqs = ROPE_THETA ** (-jnp.arange(half, dtype=F32) / half)
    ang = jnp.arange(t, dtype=F32)[:, None] * freqs[None, :]
    cos, sin = jnp.cos(ang), jnp.sin(ang)
    cos_t = jnp.tile(cos, (1, LANES // half))
    sin_t = jnp.tile(jnp.concatenate([-sin, sin], axis=1), (1, LANES // NARROW_DK))
    return cos_t.astype(F32), sin_t.astype(F32)


def _scan_c(p3):
    bsz, t, _ = p3.shape
    nb = t // SCAN_ROWS
    tables = _retention_tables()

    def fwd(j):
        return pl.BlockSpec((None, SCAN_ROWS, 512), lambda b, n: (b, n, j))

    def bwd(j):
        return pl.BlockSpec((None, SCAN_ROWS, 512), lambda b, n: (b, nb - 1 - n, j))

    def const(shape):
        return pl.BlockSpec(shape, lambda b, n: (0,) * len(shape))

    out = jax.ShapeDtypeStruct((bsz, t, MIX_HALF), F32)
    return pl.pallas_call(
        _scan_c_kernel,
        out_shape=(out, out),
        grid=(bsz, nb),
        in_specs=[fwd(CD_QKC), fwd(CD_VC), bwd(CD_QKC), bwd(CD_VC)] + [const(a.shape) for a in tables],
        out_specs=(pl.BlockSpec((None, SCAN_ROWS, MIX_HALF), lambda b, n: (b, n, 0)),
                   pl.BlockSpec((None, SCAN_ROWS, MIX_HALF), lambda b, n: (b, nb - 1 - n, 0))),
        scratch_shapes=[pltpu.VMEM((2 * N_HEADS, HEAD_DV, LANES), F32)],
        compiler_params=_params(2),
        name="scan_c",
    )(p3, p3, p3, p3, *tables)


def _conv_kernel(prev_ref, cur_ref, next_ref, w_ref, b_ref, o_ref, buf_ref):
    i = pl.program_id(1)
    last = pl.num_programs(1) - 1
    rows = cur_ref.shape[0]
    buf_ref[0:SUBLANES, :] = jnp.where(i == 0, 0.0, prev_ref[...])
    buf_ref[SUBLANES:SUBLANES + rows, :] = cur_ref[...]
    buf_ref[SUBLANES + rows:2 * SUBLANES + rows, :] = jnp.where(i == last, 0.0, next_ref[...])
    y = (buf_ref[SUBLANES - 1:SUBLANES - 1 + rows, :] * w_ref[0:1, :]
         + buf_ref[SUBLANES:SUBLANES + rows, :] * w_ref[1:2, :]
         + buf_ref[SUBLANES + 1:SUBLANES + 1 + rows, :] * w_ref[2:3, :]
         + b_ref[...])
    o_ref[...] = _silu(y)


def _conv_silu(p3, conv_w, conv_b):
    bsz, t, _ = p3.shape
    nb = t // ROW_TILE
    per = ROW_TILE // SUBLANES
    outs = []
    for j in (CD_QD, CD_KD):
        lo = (j - CD_QD) * 512
        outs.append(pl.pallas_call(
            _conv_kernel,
            out_shape=jax.ShapeDtypeStruct((bsz, t, 512), F32),
            grid=(bsz, nb),
            in_specs=[pl.BlockSpec((None, SUBLANES, 512), lambda b, i, j=j: (b, jnp.maximum(i * per - 1, 0), j)),
                      pl.BlockSpec((None, ROW_TILE, 512), lambda b, i, j=j: (b, i, j)),
                      pl.BlockSpec((None, SUBLANES, 512),
                                   lambda b, i, j=j: (b, jnp.minimum((i + 1) * per, t // SUBLANES - 1), j)),
                      pl.BlockSpec((3, 512), lambda b, i: (0, 0)),
                      pl.BlockSpec((1, 512), lambda b, i: (0, 0))],
            out_specs=pl.BlockSpec((None, ROW_TILE, 512), lambda b, i: (b, i, 0)),
            scratch_shapes=[pltpu.VMEM((ROW_TILE + 2 * SUBLANES, 512), F32)],
            compiler_params=_params(2),
            name="conv_silu",
        )(p3, p3, p3, conv_w[:, lo:lo + 512], conv_b[None, lo:lo + 512]))
    return outs


def _gate_lane_layout():
    lane = np.arange(LANES)
    r = lane % GATE_LANES
    return lane // (N_HEADS * GATE_LANES), (lane // GATE_LANES) % N_HEADS, r // 3, r % 3


def _gate_const_rows():
    _, _, cls, _ = _gate_lane_layout()
    return jnp.asarray(np.concatenate([np.tile((cls == k + 1).astype(np.float32)[None, :], (LANES, 1))
                                       for k in range(3)], axis=0), BF16)


def _pieces(x, piece):
    hi = x.astype(BF16).astype(F32)
    r1 = x - hi
    mid = r1.astype(BF16).astype(F32)
    lo = (r1 - mid).astype(BF16).astype(F32)
    return jnp.where(piece == 0, hi, jnp.where(piece == 1, mid, lo))


def _running_max(x, direction):
    row = lax.broadcasted_iota(jnp.int32, x.shape, 0)
    step = 1
    while step < CHUNK:
        if direction == 0:
            shifted = jnp.where(row >= step, pltpu.roll(x, step, 0), -jnp.inf)
        else:
            shifted = jnp.where(row < CHUNK - step, pltpu.roll(x, CHUNK - step, 0), -jnp.inf)
        x = jnp.maximum(x, shifted)
        step *= 2
    return x


def _scan_d_kernel(q_f, k_f, v_f, gi_f, gf_f, q_b, k_b, v_b, gi_b, gf_b, fgb_ref, const_ref, o_f, o_b, c_ref, m_ref):
    @pl.when(pl.program_id(1) == 0)
    def _():
        c_ref[...] = jnp.zeros_like(c_ref)
        m_ref[...] = jnp.zeros_like(m_ref)

    n_chunks = SCAN_ROWS // CHUNK
    dirs = ((q_f, k_f, v_f, gi_f, gf_f, o_f), (q_b, k_b, v_b, gi_b, gf_b, o_b))
    lane = lax.broadcasted_iota(jnp.int32, (1, LANES), 1)
    lane_dir = lane // (N_HEADS * GATE_LANES)
    lane_head = (lane // GATE_LANES) % N_HEADS
    cls = (lane % GATE_LANES) // 3
    piece = (lane % GATE_LANES) % 3
    ones_block = jnp.ones((CHUNK, HEAD_DV), BF16)

    def prepare(c, m_state):
        pre = []
        for d, (q_ref, k_ref, v_ref, gi_ref, gf_ref, o_ref) in enumerate(dirs):
            rows = pl.ds((c if d == 0 else n_chunks - 1 - c) * CHUNK, CHUNK)
            pre.append((rows, gi_ref[rows, :], _log_sigmoid(gf_ref[rows, :] + fgb_ref[...])))
        bcs = _chunk_cumsum([(pre[0][2], 0), (pre[1][2], 1)])
        units, m_next = [], []
        for d, (q_ref, k_ref, v_ref, gi_ref, gf_ref, o_ref) in enumerate(dirs):
            rows, gi, _ = pre[d]
            b = bcs[d]
            last = _last_row(d)
            mask = _tri(d)
            m_st = m_state[d]
            b_last = b[last:last + 1, :]
            cj = gi - b
            run = _running_max(cj, d)
            m_i = jnp.maximum(b + m_st, b + run)
            u = b - m_i
            m_new = b_last + jnp.maximum(m_st, run[last:last + 1, :])
            s_old = jnp.exp(b_last + m_st - m_new)
            m_next.append(m_new)
            quantity =jnp.where(cls == 0, u, jnp.where(cls == 1, u + m_st, jnp.where(
                cls == 2, -m_i, cj + (b_last - m_new))))
            lhs_all = jnp.where(cls == 4, 1.0, jnp.where(cls < 4, _pieces(quantity, piece), 0.0))
            lhs = jnp.concatenate([jnp.where((lane_dir == d) & (lane_head == h), lhs_all, 0.0)
                                   for h in range(N_HEADS)], axis=0).astype(BF16)
            key_rows = jnp.where(cls == 4, _pieces(cj, piece), jnp.where(cls == 0, 1.0, 0.0)).astype(BF16)
            expo = _nt(lhs, jnp.concatenate([const_ref[...], key_rows], axis=0))
            for h in range(N_HEADS):
                sl = slice(h * HEAD_DV, (h + 1) * HEAD_DV)
                q = q_ref[rows, sl] * (HEAD_DV ** -0.5)
                k = k_ref[rows, sl]
                v_ext = jnp.concatenate([v_ref[rows, sl].astype(BF16), ones_block], axis=1)
                lane0 = (d * N_HEADS + h) * GATE_LANES
                units.append(dict(u=d * N_HEADS + h, rows=rows, sl=sl, o_ref=o_ref, q=q, k=k, v_ext=v_ext,
                                  mask=mask, expo=expo[h * CHUNK:(h + 1) * CHUNK, :],
                                  s_old=s_old[:, lane0:lane0 + 1], qk=_nt(q.astype(BF16), k.astype(BF16))))
        for t in units:
            e = t.pop("expo")
            w = t.pop("qk") * jnp.exp(jnp.where(t.pop("mask"), e[:, 3 * LANES:3 * LANES + CHUNK], -jnp.inf))
            t["lhs2"] = jnp.concatenate([(t.pop("q") * jnp.exp(e[:, :LANES])).astype(BF16), w.astype(BF16)], axis=1)
            t["floor"] = jnp.exp(e[:, LANES:2 * LANES])
            t["upd"] = _tn((t.pop("k") * jnp.exp(e[:, 2 * LANES:3 * LANES])).astype(BF16), t["v_ext"])
        return units, m_next

    def finish(units):
        nums = [_mm(t["lhs2"], jnp.concatenate([c_ref[t["u"]].astype(BF16), t["v_ext"]], axis=0)) for t in units]
        for t, num in zip(units, nums):
            den = jnp.maximum(jnp.abs(num[:, HEAD_DV:]), t["floor"])
            t["o_ref"][t["rows"], t["sl"]] = num[:, :HEAD_DV] / den
            c_ref[t["u"]] = t["s_old"] * c_ref[t["u"]] + t["upd"]

    m_state = [m_ref[d][0:1, :] for d in range(2)]
    pending, m_state = prepare(0, m_state)
    for c in range(n_chunks):
        following = None
        if c + 1 < n_chunks:
            following, m_state = prepare(c + 1, m_state)
        finish(pending)
        pending = following
    for d in range(2):
        m_ref[d] = jnp.broadcast_to(m_state[d], (SUBLANES, LANES))


def _scan_d(qd, kd, p3, fgb_row):
    bsz, t, _ = p3.shape
    nb = t // SCAN_ROWS
    const_rows = _gate_const_rows()

    def spec(j, w, flip):
        if flip:
            return pl.BlockSpec((None, SCAN_ROWS, w), lambda b, n: (b, nb - 1 - n, j))
        return pl.BlockSpec((None, SCAN_ROWS, w), lambda b, n: (b, n, j))

    out = jax.ShapeDtypeStruct((bsz, t, MIX_HALF), F32)
    in_specs = []
    for flip in (False, True):
        in_specs += [spec(0, 512, flip), spec(0, 512, flip), spec(CD_VD, 512, flip),
                     spec(CD_GATE_I_BLOCK, LANES, flip), spec(CD_GATE_F_BLOCK, LANES, flip)]
    in_specs += [pl.BlockSpec((1, LANES), lambda b, n: (0, 0)), pl.BlockSpec(const_rows.shape, lambda b, n: (0, 0))]
    return pl.pallas_call(
        _scan_d_kernel,
        out_shape=(out, out),
        grid=(bsz, nb),
        in_specs=in_specs,
        out_specs=(spec(0, MIX_HALF, False), spec(0, MIX_HALF, True)),
        scratch_shapes=[pltpu.VMEM((2 * N_HEADS, HEAD_DV, 2 * HEAD_DV), F32),
                        pltpu.VMEM((2, SUBLANES, LANES), F32)],
        compiler_params=_params(2),
        name="scan_d",
    )(qd, kd, p3, p3, p3, qd, kd, p3, p3, p3, fgb_row, const_rows)


def _layer_norm(z, g, b):
    zc = z - jnp.mean(z, axis=-1, keepdims=True)
    var = jnp.mean(zc * zc, axis=-1, keepdims=True)
    return zc * lax.rsqrt(var + LN_EPS) * g + b


def _mix_out_kernel(o1f, o1b, o2f, o2b, g1, g2, x_ref, norm_ref, w_ref, lng, lnb, h_ref, hb_ref):
    parts = []
    for of, ob, g_ref, base in ((o1f, o1b, g1, 0), (o2f, o2b, g2, MIX_HALF)):
        o = of[...] + ob[...]
        gate = g_ref[...]
        for h in range(N_HEADS):
            sl = slice(h * HEAD_DV, (h + 1) * HEAD_DV)
            oh = o[:, sl]
            oh = oh * lax.rsqrt(jnp.mean(oh * oh, axis=-1, keepdims=True) + NORM_EPS)
            parts.append((oh * norm_ref[:, base + h * HEAD_DV:base + (h + 1) * HEAD_DV] * gate[:, sl]).astype(BF16))
    mixed = jnp.concatenate(parts, axis=1)
    z = ALPHA * x_ref[...] + _mm(mixed, w_ref[...])
    hn = _layer_norm(z, lng[...], lnb[...])
    h_ref[...] = hn
    hb_ref[...] = hn.astype(BF16)


def _mix_out(o1f, o1b, o2f, o2b, p2d, gate_blocks, x2d, norm, w_out, ln_g, ln_b):
    n = x2d.shape[0]
    half = pl.BlockSpec((ROW_TILE, MIX_HALF), lambda i: (i, 0))
    full = pl.BlockSpec((ROW_TILE, D_MODEL), lambda i: (i, 0))
    row = pl.BlockSpec((1, D_MODEL), lambda i: (0, 0))
    ga, gb = gate_blocks
    return pl.pallas_call(
        _mix_out_kernel,
        out_shape=(jax.ShapeDtypeStruct((n, D_MODEL), F32), jax.ShapeDtypeStruct((n, D_MODEL), BF16)),
        grid=(n // ROW_TILE,),
        in_specs=[half, half, half, half,
                  pl.BlockSpec((ROW_TILE, MIX_HALF), lambda i: (i, ga)),
                  pl.BlockSpec((ROW_TILE, MIX_HALF), lambda i: (i, gb)),
                  full, row, pl.BlockSpec((D_MODEL, D_MODEL), lambda i: (0, 0)), row, row],
        out_specs=(full, full),
        compiler_params=_params(1),
        name="mix_out",
    )(o1f, o1b, o2f, o2b, p2d, p2d, x2d, norm, w_out, ln_g, ln_b)


def _route(h, rw_ref, rb_ref, idx_ref, gate_ref):
    rw = rw_ref[...]
    rw_hi = rw.astype(BF16)
    rw_r = rw - rw_hi.astype(F32)
    rw_mid = rw_r.astype(BF16)
    rw3 = jnp.concatenate([rw_hi, rw_mid, (rw_r - rw_mid.astype(F32)).astype(BF16)], axis=0)
    h_hi = h.astype(BF16)
    h_lo = (h - h_hi.astype(F32)).astype(BF16)
    terms = _nt(rw3, h_hi) + _nt(rw3, h_lo)
    logits = terms[:N_EXPERTS] + terms[N_EXPERTS:2 * N_EXPERTS] + terms[2 * N_EXPERTS:]
    rows = [logits[e:e + 1, :] for e in range(N_EXPERTS)]
    mx = functools.reduce(jnp.maximum, rows)
    ex = [jnp.exp(r - mx) for r in rows]
    tot = functools.reduce(lambda a, b: a + b, ex)
    score = [e / tot for e in ex]
    biased = [score[e] + rb_ref[e:e + 1, :] for e in range(N_EXPERTS)]

    def argmax_first(vals):
        best, idx = vals[0], jnp.zeros(vals[0].shape, jnp.int32)
        for j in range(1, len(vals)):
            upd = vals[j] > best
            best = jnp.where(upd, vals[j], best)
            idx = jnp.where(upd, j, idx)
        return best, idx

    def pick(vals, idx):
        out = vals[0]
        for j in range(1, len(vals)):
            out = jnp.where(idx == j, vals[j], out)
        return out

    group_scores = []
    for g in range(N_GROUPS):
        a = biased[g * EXPERTS_PER_GROUP:(g + 1) * EXPERTS_PER_GROUP]
        pairs = [a[i] + a[j] for i in range(EXPERTS_PER_GROUP) for j in range(i + 1, EXPERTS_PER_GROUP)]
        group_scores.append(functools.reduce(jnp.maximum, pairs))
    _, g_sel = argmax_first(group_scores)
    in_b = [pick([biased[g * EXPERTS_PER_GROUP + k] for g in range(N_GROUPS)], g_sel) for k in range(EXPERTS_PER_GROUP)]
    in_s = [pick([score[g * EXPERTS_PER_GROUP + k] for g in range(N_GROUPS)], g_sel) for k in range(EXPERTS_PER_GROUP)]
    _, i0 = argmax_first(in_b)
    _, i1 = argmax_first([jnp.where(i0 == k, -jnp.inf, in_b[k]) for k in range(EXPERTS_PER_GROUP)])
    s0, s1 = pick(in_s, i0), pick(in_s, i1)
    den = s0 + s1
    idx_ref[0:1, :] = g_sel * EXPERTS_PER_GROUP + i0
    idx_ref[1:2, :] = g_sel * EXPERTS_PER_GROUP + i1
    gate_ref[0:1, :] = s0 / den
    gate_ref[1:2, :] = s1 / den


def _router_kernel(h_ref, rw_ref, rb_ref, idx_ref, gate_ref):
    _route(h_ref[...], rw_ref, rb_ref, idx_ref, gate_ref)


def _router(h2d, rw_t, rb_col):
    n = h2d.shape[0]
    tile = min(ROUTER_TILE, n)
    routed = pl.BlockSpec((TOP_K, tile), lambda i: (0, i))
    return pl.pallas_call(
        _router_kernel,
        out_shape=(jax.ShapeDtypeStruct((TOP_K, n), jnp.int32), jax.ShapeDtypeStruct((TOP_K, n), F32)),
        grid=(n // tile,),
        in_specs=[pl.BlockSpec((tile, D_MODEL), lambda i: (i, 0)),
                  pl.BlockSpec(rw_t.shape, lambda i: (0, 0)),
                  pl.BlockSpec((N_EXPERTS, 1), lambda i: (0, 0))],
        out_specs=(routed, routed),
        compiler_params=_params(1),
        name="router",
    )(h2d, rw_t, rb_col)


def _plan_kernel(idx_ref, dest_ref, cnt_ref, su_ref, run_ref, start_ref):
    phase, i = pl.program_id(0), pl.program_id(1)
    tile = idx_ref.shape[1]
    e_iota = lax.broadcasted_iota(jnp.int32, (N_EXPERTS, tile), 0)
    oh0 = idx_ref[0:1, :] == e_iota
    oh1 = idx_ref[1:2, :] == e_iota
    member = jnp.where(oh0 | oh1, 1.0, 0.0)
    tile_count = jnp.broadcast_to(jnp.sum(member, axis=1, keepdims=True), (N_EXPERTS, LANES))

    @pl.when((phase == 0) & (i == 0))
    def _():
        run_ref[...] = jnp.zeros_like(run_ref)
        r = lax.broadcasted_iota(jnp.int32, (tile, tile), 0)
        c = lax.broadcasted_iota(jnp.int32, (tile, tile), 1)
        su_ref[...] = jnp.where(r < c, 1.0, 0.0).astype(BF16)

    @pl.when((phase == 1) & (i == 0))
    def _():
        counts = run_ref[...]
        cnt_ref[...] = counts.astype(jnp.int32)
        padded = jnp.ceil(counts * (1.0 / EXPERT_ROWS)) * EXPERT_ROWS
        r = lax.broadcasted_iota(jnp.int32, (N_EXPERTS, N_EXPERTS), 0)
        c = lax.broadcasted_iota(jnp.int32, (N_EXPERTS, N_EXPERTS), 1)
        start_ref[...] = jnp.dot(jnp.where(c < r, 1.0, 0.0), padded, precision=HIGHEST, preferred_element_type=F32)
        run_ref[...] = jnp.zeros_like(run_ref)

    @pl.when(phase == 1)
    def _():
        before = _mm(member.astype(BF16), su_ref[...])
        pos = start_ref[:, 0:1] + run_ref[:, 0:1] + before
        dest_ref[0:1, :] = jnp.sum(jnp.where(oh0, pos, 0.0), axis=0, keepdims=True).astype(jnp.int32)
        dest_ref[1:2, :] = jnp.sum(jnp.where(oh1, pos, 0.0), axis=0, keepdims=True).astype(jnp.int32)

    run_ref[...] += tile_count


def _plan(idx):
    n = idx.shape[1]
    tile = min(PLAN_TILE, n)
    return pl.pallas_call(
        _plan_kernel,
        out_shape=(jax.ShapeDtypeStruct((TOP_K, n), jnp.int32), jax.ShapeDtypeStruct((N_EXPERTS, LANES), jnp.int32)),
        grid=(2, n // tile),
        in_specs=[pl.BlockSpec((TOP_K, tile), lambda p, i: (0, i))],
        out_specs=(pl.BlockSpec((TOP_K, tile), lambda p, i: (0, i * p)),
                   pl.BlockSpec((N_EXPERTS, LANES), lambda p, i: (0, 0))),
        scratch_shapes=[pltpu.VMEM((tile, tile), BF16), pltpu.VMEM((N_EXPERTS, LANES), F32),
                        pltpu.VMEM((N_EXPERTS, LANES), F32)],
        compiler_params=_params(2),
        name="moe_plan",
    )(idx)


def _invert_kernel(pad_ref, dest_ref, tok_ref):
    n = dest_ref.shape[0] // TOP_K

    def clear(r, carry):
        tok_ref[r] = lax.rem(r, n)
        return carry

    for e in range(N_EXPERTS + 1):
        lax.fori_loop(pad_ref[0, e], pad_ref[1, e], clear, 0)

    for k in range(TOP_K):
        def place(t, carry, k=k):
            tok_ref[dest_ref[k * n + t]] = t
            return carry

        lax.fori_loop(0, n, place, 0, unroll=16)


def _invert(pad_ranges, dest_flat, n_rows):
    smem = pl.BlockSpec(memory_space=pltpu.SMEM)
    return pl.pallas_call(
        _invert_kernel,
        out_shape=jax.ShapeDtypeStruct((n_rows,), jnp.int32),
        in_specs=[smem, smem],
        out_specs=smem,
        name="moe_invert",
    )(pad_ranges, dest_flat)


def _expert_kernel(be_ref, nv_ref, x_ref, w1_ref, w3_ref, w2_ref, y_ref, wb_ref):
    i = pl.program_id(0)
    valid = i < nv_ref[0]
    new_expert = (i == 0) | (be_ref[i] != be_ref[jnp.maximum(i - 1, 0)])

    @pl.when(valid & new_expert)
    def _():
        wb_ref[0] = w1_ref[...].astype(BF16)
        wb_ref[1] = w3_ref[...].astype(BF16)
        wb_ref[2] = w2_ref[...].astype(BF16)

    @pl.when(valid)
    def _():
        x = x_ref[...]
        hid = _silu(_mm(x, wb_ref[0])) * _mm(x, wb_ref[1])
        y_ref[...] = _mm(hid.astype(BF16), wb_ref[2]).astype(BF16)

    @pl.when(jnp.logical_not(valid))
    def _():
        y_ref[...] = jnp.zeros_like(y_ref)


def _experts(block_expert, n_valid, xs, w1, w3, w2, layer):
    n_rows = xs.shape[0]
    wspec = pl.BlockSpec((None, None, D_MODEL, D_MODEL), lambda i, be, nv: (layer, be[i], 0, 0))
    return pl.pallas_call(
        _expert_kernel,
        out_shape=jax.ShapeDtypeStruct((n_rows, D_MODEL), BF16),
        grid_spec=pltpu.PrefetchScalarGridSpec(
            num_scalar_prefetch=2,
            grid=(n_rows // EXPERT_ROWS,),
            in_specs=[pl.BlockSpec((EXPERT_ROWS, D_MODEL), lambda i, be, nv: (i, 0)), wspec, wspec, wspec],
            out_specs=pl.BlockSpec((EXPERT_ROWS, D_MODEL), lambda i, be, nv: (i, 0)),
            scratch_shapes=[pltpu.VMEM((3, D_MODEL, D_MODEL), BF16)]),
        compiler_params=_params(1),
        name="experts",
    )(block_expert, n_valid, xs, w1, w3, w2)


def _combine_kernel(h_ref, y0_ref, y1_ref, g_ref, lng, lnb, o_ref):
    g = g_ref[...]
    z = ALPHA * h_ref[...] + (g[:, 0:1] * y0_ref[...].astype(F32) + g[:, 1:2] * y1_ref[...].astype(F32))
    o_ref[...] = _layer_norm(z, lng[...], lnb[...])


def _combine(h2d, y_rows, gates_t, ln_g, ln_b):
    n = h2d.shape[0]
    nt = n // ROW_TILE
    full = pl.BlockSpec((ROW_TILE, D_MODEL), lambda i: (i, 0))
    row = pl.BlockSpec((1, D_MODEL), lambda i: (0, 0))
    return pl.pallas_call(
        _combine_kernel,
        out_shape=jax.ShapeDtypeStruct((n, D_MODEL), F32),
        grid=(nt,),
        in_specs=[full, full, pl.BlockSpec((ROW_TILE, D_MODEL), lambda i: (nt + i, 0)),
                  pl.BlockSpec((ROW_TILE, TOP_K), lambda i: (i, 0)), row, row],
        out_specs=full,
        compiler_params=_params(1),
        name="moe_combine",
    )(h2d, y_rows, y_rows, gates_t, ln_g, ln_b)


def _moe(h2d, h_bf16, idx, gates, w1, w3, w2, layer, ln_g, ln_b):
    n = h2d.shape[0]
    dest, counts = _plan(idx)
    n_rows = n * TOP_K + N_EXPERTS * EXPERT_ROWS
    n_blocks = n_rows // EXPERT_ROWS
    counts = counts[:, 0]
    padded = (counts + EXPERT_ROWS - 1) // EXPERT_ROWS * EXPERT_ROWS
    pad_end = jnp.cumsum(padded)
    block_start = jnp.arange(n_blocks, dtype=jnp.int32) * EXPERT_ROWS
    block_expert = jnp.minimum(jnp.sum((pad_end[None, :] <= block_start[:, None]).astype(jnp.int32), axis=1),
                               N_EXPERTS - 1)
    n_valid = (pad_end[-1:] // EXPERT_ROWS).astype(jnp.int32)
    pad_ranges = jnp.stack([jnp.concatenate([pad_end - padded + counts, pad_end[-1:]]),
                            jnp.concatenate([pad_end, jnp.full((1,), n_rows, jnp.int32)])]).astype(jnp.int32)
    dest_flat = dest.reshape(-1)
    xs = h_bf16[_invert(pad_ranges, dest_flat, n_rows)]
    y = _experts(block_expert, n_valid, xs, w1, w3, w2, layer)
    return _combine(h2d, y[dest_flat], gates.T, ln_g, ln_b)


def _pad_cols(w, b, main, width):
    k = w.shape[0]
    tail = w.shape[1] - main
    w_p = jnp.concatenate([w[:, :main], w[:, main:], jnp.zeros((k, width - main - tail), w.dtype)], axis=1)
    b_p = jnp.concatenate([b[:main], b[main:], jnp.zeros((width - main - tail,), b.dtype)])
    return w_p.astype(BF16), b_p[None, :].astype(F32)


def kernel(x, ab_w_in, ab_b_in, hgrn_lb, gla_gk_up, gla_gk_b, hgrn_norm, gla_norm, ab_w_out, cd_w_in, cd_b_in,
           mlstm_conv_w, mlstm_conv_b, mlstm_fgate_b, ret_norm, mlstm_norm, cd_w_out, ln_mix_g, ln_mix_b, ln_ffn_g,
           ln_ffn_b, router_w, router_b, moe_w1, moe_w3, moe_w2):
    bsz, t, d = x.shape
    n = bsz * t
    lower_bounds = jnp.cumsum(jax.nn.softmax(hgrn_lb.astype(F32), axis=0), axis=0)
    rw_t = router_w.T.astype(F32)
    rb_col = router_b.astype(F32)[:, None]
    h = x.reshape(n, d)
    for layer in range(DEPTH):
        j = layer // 2
        if layer % 2 == 0:
            w_p, b_p = _pad_cols(ab_w_in[j], ab_b_in[j], 8 * 512, AB_WIDTH)
            lb = lower_bounds[layer][None, :]
            p = _in_proj(h, w_p, b_p, _proj_ab_kernel, [lb], [pl.BlockSpec(lb.shape, lambda i: (0, 0))])
            p3 = p.reshape(bsz, t, AB_WIDTH)
            nk = N_HEADS * NARROW_DK
            u_pad = jnp.zeros((LANES, 2 * nk), F32)
            u_pad = u_pad.at[:GLA_LOWRANK, :nk].set(gla_gk_up[j, 0])
            u_pad = u_pad.at[GLA_LOWRANK:2 * GLA_LOWRANK, nk:].set(gla_gk_up[j, 1]).astype(BF16)
            gkb = jnp.concatenate([gla_gk_b[j, 0], gla_gk_b[j, 1]])[None, :].astype(F32)
            oa_f, ob_f, oa_b, ob_b = _scan_ab(p3, u_pad, gkb)
            norm = jnp.concatenate([hgrn_norm[j], gla_norm[j]])[None, :].astype(F32)
            outs = [a.reshape(n, MIX_HALF) for a in (oa_f, oa_b, ob_f, ob_b)]
            h, h_bf16 = _mix_out(*outs, p, (AB_GA, AB_GB), h, norm, ab_w_out[j].astype(BF16),
                                 ln_mix_g[layer][None, :], ln_mix_b[layer][None, :])
        else:
            lane_dir, lane_head, _, _ = _gate_lane_layout()
            col_i = 7 * 512 + lane_dir * 2 * N_HEADS + lane_head
            cols = np.concatenate([np.arange(7 * 512), col_i, col_i + N_HEADS])
            w_p, b_p = cd_w_in[j][:, cols].astype(BF16), cd_b_in[j][cols][None, :].astype(F32)
            pos = pl.BlockSpec((ROW_TILE, LANES), lambda i: (i % (t // ROW_TILE), 0))
            p = _in_proj(h, w_p, b_p, _proj_cd_kernel, list(_rotary_tables(t)), [pos, pos])
            p3 = p.reshape(bsz, t, CD_WIDTH)
            oc_f, oc_b = _scan_c(p3)
            qd, kd = _conv_silu(p3, mlstm_conv_w[j].astype(F32), mlstm_conv_b[j].astype(F32))
            fgb_row = mlstm_fgate_b[j].astype(F32)[lane_dir, lane_head][None, :]
            od_f, od_b = _scan_d(qd, kd, p3, fgb_row)
            norm = jnp.concatenate([ret_norm[j], mlstm_norm[j]])[None, :].astype(F32)
            outs = [a.reshape(n, MIX_HALF) for a in (oc_f, oc_b, od_f, od_b)]
            h, h_bf16 = _mix_out(*outs, p, (CD_GC, CD_OD), h, norm, cd_w_out[j].astype(BF16),
                                 ln_mix_g[layer][None, :], ln_mix_b[layer][None, :])
        idx, gates = _router(h, rw_t, rb_col)
        h = _moe(h, h_bf16, idx, gates, moe_w1, moe_w3, moe_w2, layer,
                 ln_ffn_g[layer][None, :], ln_ffn_b[layer][None, :])
    return h.reshape(bsz, t, d)
```

```python
import functools

import jax
import jax.numpy as jnp
import numpy as np
from jax import lax
from jax.experimental import pallas as pl
from jax.experimental.pallas import tpu as pltpu

F32 = jnp.float32
BF16 = jnp.bfloat16
HIGHEST = lax.Precision.HIGHEST

D_MODEL = 1024
DEPTH = 2
CHUNK = 64
MIX_HALF = D_MODEL // 2
N_HEADS = 4
HEAD_DV = MIX_HALF // N_HEADS
NARROW_DK = 64
GLA_LOWRANK = 16
GLA_GATE_NORMALIZER = 16.0
ROPE_THETA = 10000.0
N_EXPERTS = 16
N_GROUPS = 4
EXPERTS_PER_GROUP = N_EXPERTS // N_GROUPS
TOP_K = 2
ALPHA = (2.0 * DEPTH) ** 0.25
LN_EPS = 1e-5
NORM_EPS = 1e-6

LANES = 128
SUBLANES = 8
VMEM_LIMIT = 52 * 1024 * 1024

ROW_TILE = 256
WIDE_TILE = 512
CONV_TILE = 1024
SCAN_ROWS = 256
ROUTER_TILE = 2048
PLAN_TILE = 1024
EXPERT_ROWS = 512

AB_QA, AB_FF, AB_FB, AB_IA, AB_GA, AB_QKB, AB_VB, AB_GB = range(8)
AB_LR_BLOCK = 32
AB_WIDTH = 8 * 512 + LANES
CD_QKC, CD_VC, CD_GC, CD_QD, CD_KD, CD_VD, CD_OD = range(7)
CD_GATE_I_BLOCK = 28
CD_GATE_F_BLOCK = 29
CD_WIDTH = 7 * 512 + 2 * LANES
GATE_LANES = 16


def _params(n_axes, vmem=VMEM_LIMIT):
    return pltpu.CompilerParams(dimension_semantics=("arbitrary",) * n_axes, vmem_limit_bytes=vmem)


def _nt(a, b):
    return lax.dot_general(a, b, (((1,), (1,)), ((), ())), preferred_element_type=F32)


def _tn(a, b):
    return lax.dot_general(a, b, (((0,), (0,)), ((), ())), preferred_element_type=F32)


def _mm(a, b):
    return jnp.dot(a, b, preferred_element_type=F32)


def _sigmoid(x):
    return 1.0 / (1.0 + jnp.exp(-x))


def _silu(x):
    return x * _sigmoid(x)


def _log_sigmoid(x):
    return jnp.minimum(x, 0.0) - jnp.log(1.0 + jnp.exp(-jnp.abs(x)))


def _tri(direction):
    row = lax.broadcasted_iota(jnp.int32, (CHUNK, CHUNK), 0)
    col = lax.broadcasted_iota(jnp.int32, (CHUNK, CHUNK), 1)
    return (col <= row) if direction == 0 else (col >= row)


def _ref_row(direction):
    return CHUNK // 2 - 1 if direction == 0 else CHUNK // 2


def _last_row(direction):
    return CHUNK - 1 if direction == 0 else 0


def _proj_ab_kernel(x_ref, w_ref, b_ref, lb_ref, o_ref):
    p = _mm(x_ref[...].astype(BF16), w_ref[...]) + b_ref[...]
    o_ref[:, :512] = _silu(p[:, :512])
    lb = jnp.concatenate([lb_ref[...], lb_ref[...]], axis=1)
    o_ref[:, 512:1536] = lb + (1.0 - lb) * _sigmoid(p[:, 512:1536])
    o_ref[:, 1536:] = p[:, 1536:]
    for blk in (AB_GA, AB_GB):
        o_ref[:, blk * 512:(blk + 1) * 512] = _silu(p[:, blk * 512:(blk + 1) * 512])


def _proj_cd_kernel(x_ref, w_ref, b_ref, cos_ref, sin_ref, o_ref):
    p = _mm(x_ref[...].astype(BF16), w_ref[...]) + b_ref[...]
    cos, sin_signed = cos_ref[...], sin_ref[...]
    for j in range(4):
        rot = _rotary(p[:, j * LANES:(j + 1) * LANES], cos, sin_signed)
        o_ref[:, j * LANES:(j + 1) * LANES] = rot * (NARROW_DK ** -0.5) if j < 2 else rot
    o_ref[:, 512:] = p[:, 512:]
    o_ref[:, CD_GC * 512:(CD_GC + 1) * 512] = _silu(p[:, CD_GC * 512:(CD_GC + 1) * 512])
    o_ref[:, CD_OD * 512:(CD_OD + 1) * 512] = _sigmoid(p[:, CD_OD * 512:(CD_OD + 1) * 512])


def _in_proj(x2d, w, b, body, extras, extra_specs):
    n, k = x2d.shape
    m = w.shape[1]
    return pl.pallas_call(
        body,
        out_shape=jax.ShapeDtypeStruct((n, m), F32),
        grid=(n // ROW_TILE,),
        in_specs=[pl.BlockSpec((ROW_TILE, k), lambda i: (i, 0)),
                  pl.BlockSpec((k, m), lambda i: (0, 0)),
                  pl.BlockSpec((1, m), lambda i: (0, 0))] + extra_specs,
        out_specs=pl.BlockSpec((ROW_TILE, m), lambda i: (i, 0)),
        compiler_params=_params(1),
        name="in_proj",
    )(x2d, w, b, *extras)


def _pipelined_chunks(n_chunks, prepare, finish):
    pending = prepare(0)
    for c in range(n_chunks):
        following = prepare(c + 1) if c + 1 < n_chunks else None
        finish(pending)
        pending = following


def _units_prepare(units):
    scores = [_nt(u["qe"], u["ke"]) for u in units]
    for u, s in zip(units, scores):
        u["upd"] = _tn(u["v"], u["ks"])
        u["o_intra"] = _mm(u["intra"](s).astype(BF16), u["v"])
    return units


def _units_finish(units, st_ref):
    inter = [_nt(u["qi"], st_ref[u["idx"]].astype(BF16)) for u in units]
    for u, o_inter in zip(units, inter):
        u["store"]((u["o_intra"] + o_inter).astype(BF16))
        st_ref[u["idx"]] = st_ref[u["idx"]] * u["dec"] + u["upd"]


def _chunk_cumsum(parts):
    cat = jnp.concatenate([x for x, _ in parts], axis=1)
    hi = cat.astype(BF16)
    lo = (cat - hi.astype(F32)).astype(BF16)
    width = cat.shape[1]
    inc = _mm(_tri(0).astype(BF16), jnp.concatenate([hi, lo], axis=1))
    inc = inc[:, :width] + inc[:, width:]
    out, off = [], 0
    for x, direction in parts:
        p = inc[:, off:off + x.shape[1]]
        off += x.shape[1]
        out.append(p if direction == 0 else p[CHUNK - 1:CHUNK, :] - p + x)
    return out


def _gated_operands(q, k, gc, direction):
    r, l = _ref_row(direction), _last_row(direction)
    g_ref = gc[r:r + 1, :]
    g_last = gc[l:l + 1, :]
    qe = q * jnp.exp(gc - g_ref)
    ke = k * jnp.exp(g_ref - gc)
    qi = qe * jnp.exp(g_ref)
    ks = ke * jnp.exp(g_last - g_ref)
    return [a.astype(BF16) for a in (qe, ke, qi, ks)], jnp.exp(g_last)


def _narrow_head_mask(h):
    lane = lax.broadcasted_iota(jnp.int32, (1, LANES), 1)
    return (lane // NARROW_DK) == (h % 2)


def _scan_ab_kernel(qa_f, fa_f, ia_f, qkb_f, vb_f, lr_f, qa_b, fa_b, ia_b, qkb_b, vb_b, lr_b,
                    u_ref, gkb_ref, oa_f, ob_f, oa_b, ob_b, st_ref):
    @pl.when(pl.program_id(1) == 0)
    def _():
        st_ref[...] = jnp.zeros_like(st_ref)

    n_chunks = SCAN_ROWS // CHUNK
    dirs = ((qa_f, fa_f, ia_f, qkb_f, vb_f, lr_f, oa_f, ob_f), (qa_b, fa_b, ia_b, qkb_b, vb_b, lr_b, oa_b, ob_b))

    nk = N_HEADS * NARROW_DK

    def prepare(c):
        pre = []
        for d, (qa, fa, ia, qkb, vb, lr, oa, ob) in enumerate(dirs):
            rows = pl.ds((c if d == 0 else n_chunks - 1 - c) * CHUNK, CHUNK)
            f = fa[rows, :]
            logits = _mm(lr[rows, :].astype(BF16), u_ref[:, d * nk:(d + 1) * nk]) + gkb_ref[:, d * nk:(d + 1) * nk]
            qk = qkb[rows, :]
            pre.append(dict(rows=rows, f=f, lf_a=jnp.log(f), lf_b=_log_sigmoid(logits) / GLA_GATE_NORMALIZER,
                            q_a=qa[rows, :], v_a=ia[rows, :].astype(BF16),
                            q_b=qk[:, :nk] * (NARROW_DK ** -0.5), k_b=qk[:, nk:], v_b=vb[rows, :].astype(BF16)))
        gcs = _chunk_cumsum([(pre[0]["lf_a"], 0), (pre[0]["lf_b"], 0), (pre[1]["lf_a"], 1), (pre[1]["lf_b"], 1)])
        units = []
        for d, (qa, fa, ia, qkb, vb, lr, oa, ob) in enumerate(dirs):
            p = pre[d]
            rows = p["rows"]
            mask = _tri(d)
            intra = lambda s, mask=mask: jnp.where(mask, s, 0.0)
            (qe, ke, qi, ks), dec = _gated_operands(p["q_a"], 1.0 - p["f"], gcs[2 * d], d)
            for h in range(N_HEADS):
                sl = slice(h * LANES, (h + 1) * LANES)

                def store(o, oa=oa, rows=rows, sl=sl):
                    oa[rows, sl] = o

                units.append(dict(qe=qe[:, sl], ke=ke[:, sl], qi=qi[:, sl], ks=ks[:, sl], v=p["v_a"][:, sl],
                                  dec=dec[:, sl], intra=intra, idx=d * 2 * N_HEADS + h, store=store))
            (qe, ke, qi, ks), dec = _gated_operands(p["q_b"], p["k_b"], gcs[2 * d + 1], d)
            for h in range(N_HEADS):
                psl = slice((h // 2) * LANES, (h // 2 + 1) * LANES)
                hm = _narrow_head_mask(h)
                pick = lambda a, psl=psl, hm=hm: jnp.where(hm, a[:, psl], jnp.zeros((), BF16))
                vsl = slice(h * HEAD_DV, (h + 1) * HEAD_DV)

                def store(o, ob=ob, rows=rows, vsl=vsl):
                    ob[rows, vsl] = o

                units.append(dict(qe=pick(qe), ke=pick(ke), qi=pick(qi), ks=pick(ks), v=p["v_b"][:, vsl],
                                  dec=dec[:, psl], intra=intra, idx=d * 2 * N_HEADS + N_HEADS + h, store=store))
        return _units_prepare(units)

    _pipelined_chunks(n_chunks, prepare, functools.partial(_units_finish, st_ref=st_ref))


def _scan_ab(p3, u_pad, gkb):
    bsz, t, _ = p3.shape
    nb = t // SCAN_ROWS

    def fwd(j, w=512):
        return pl.BlockSpec((None, SCAN_ROWS, w), lambda b, n: (b, n, j))

    def bwd(j, w=512):
        return pl.BlockSpec((None, SCAN_ROWS, w), lambda b, n: (b, nb - 1 - n, j))

    def const(shape):
        return pl.BlockSpec(shape, lambda b, n: (0,) * len(shape))

    out = jax.ShapeDtypeStruct((bsz, t, MIX_HALF), BF16)
    o_f = pl.BlockSpec((None, SCAN_ROWS, MIX_HALF), lambda b, n: (b, n, 0))
    o_b = pl.BlockSpec((None, SCAN_ROWS, MIX_HALF), lambda b, n: (b, nb - 1 - n, 0))
    return pl.pallas_call(
        _scan_ab_kernel,
        out_shape=(out, out, out, out),
        grid=(bsz, nb),
        in_specs=[fwd(AB_QA), fwd(AB_FF), fwd(AB_IA), fwd(AB_QKB), fwd(AB_VB), fwd(AB_LR_BLOCK, LANES),
                  bwd(AB_QA), bwd(AB_FB), bwd(AB_IA), bwd(AB_QKB), bwd(AB_VB), bwd(AB_LR_BLOCK, LANES),
                  const(u_pad.shape), const(gkb.shape)],
        out_specs=(o_f, o_f, o_b, o_b),
        scratch_shapes=[pltpu.VMEM((4 * N_HEADS, HEAD_DV, LANES), F32)],
        compiler_params=_params(2),
        name="scan_ab",
    )(*([p3] * 12), u_pad, gkb)


def _rotary(a, cos, sin_signed):
    lane = lax.broadcasted_iota(jnp.int32, (1, LANES), 1)
    first_half = (lane % NARROW_DK) < (NARROW_DK // 2)
    swapped = jnp.where(first_half, pltpu.roll(a, LANES - NARROW_DK // 2, 1), pltpu.roll(a, NARROW_DK // 2, 1))
    return a * cos + swapped * sin_signed


def _scan_c_kernel(qk_f, v_f, qk_b, v_b, dmat_ref, qsc_ref, ksc_ref, cd_ref, o_f, o_b, st_ref):
    @pl.when(pl.program_id(1) == 0)
    def _():
        st_ref[...] = jnp.zeros_like(st_ref)

    n_chunks = SCAN_ROWS // CHUNK
    dirs = ((qk_f, v_f, o_f), (qk_b, v_b, o_b))

    def prepare(c):
        units = []
        for d, (qk_ref, v_ref, o_ref) in enumerate(dirs):
            rows = pl.ds((c if d == 0 else n_chunks - 1 - c) * CHUNK, CHUNK)
            v = v_ref[rows, :].astype(BF16)
            for h in range(N_HEADS):
                hm = _narrow_head_mask(h)
                q = jnp.where(hm, qk_ref[rows, (h // 2) * LANES:(h // 2 + 1) * LANES], 0.0)
                k = jnp.where(hm, qk_ref[rows, (2 + h // 2) * LANES:(3 + h // 2) * LANES], 0.0)
                vsl = slice(h * HEAD_DV, (h + 1) * HEAD_DV)

                def store(o, o_ref=o_ref, rows=rows, vsl=vsl):
                    o_ref[rows, vsl] = o

                units.append(dict(qe=q.astype(BF16), ke=k.astype(BF16), qi=(q * qsc_ref[d, h]).astype(BF16),
                                  ks=(k * ksc_ref[d, h]).astype(BF16), v=v[:, vsl], dec=cd_ref[d, h][0:1, :],
                                  intra=lambda s, d=d, h=h: s * dmat_ref[d, h], idx=d * N_HEADS + h, store=store))
        return _units_prepare(units)

    _pipelined_chunks(n_chunks, prepare, functools.partial(_units_finish, st_ref=st_ref))


def _retention_tables():
    pos = np.arange(CHUNK, dtype=np.float64)
    dmat = np.zeros((2, N_HEADS, CHUNK, CHUNK), np.float32)
    qsc = np.zeros((2, N_HEADS, CHUNK, LANES), np.float32)
    ksc = np.zeros((2, N_HEADS, CHUNK, LANES), np.float32)
    cd = np.zeros((2, N_HEADS, SUBLANES, LANES), np.float32)
    for d in range(2):
        for h in range(N_HEADS):
            lg = np.log1p(-np.exp2(-(5.0 + 2.0 * h + d)))
            dist = pos[:, None] - pos[None, :]
            if d == 0:
                dmat[d, h] = np.where(dist >= 0, np.exp(np.maximum(dist, 0.0) * lg), 0.0)
                qsc[d, h] = np.exp((pos + 1.0) * lg)[:, None]
                ksc[d, h] = np.exp((CHUNK - 1.0 - pos) * lg)[:, None]
            else:
                dmat[d, h] = np.where(dist <= 0, np.exp(np.maximum(-dist, 0.0) * lg), 0.0)
                qsc[d, h] = np.exp((CHUNK - pos) * lg)[:, None]
                ksc[d, h] = np.exp(pos * lg)[:, None]
            cd[d, h] = np.exp(CHUNK * lg)
    return jnp.asarray(dmat), jnp.asarray(qsc), jnp.asarray(ksc), jnp.asarray(cd)


def _rotary_tables(t):
    half = NARROW_DK // 2
    freqs = ROPE_THETA ** (-jnp.arange(half, dtype=F32) / half)
    ang = jnp.arange(t, dtype=F32)[:, None] * freqs[None, :]
    cos, sin = jnp.cos(ang), jnp.sin(ang)
    cos_t = jnp.tile(cos, (1, LANES // half))
    sin_t = jnp.tile(jnp.concatenate([-sin, sin], axis=1), (1, LANES // NARROW_DK))
    return cos_t.astype(F32), sin_t.astype(F32)


def _scan_c(p3):
    bsz, t, _ = p3.shape
    nb = t // SCAN_ROWS
    tables = _retention_tables()

    def fwd(j):
        return pl.BlockSpec((None, SCAN_ROWS, 512), lambda b, n: (b, n, j))

    def bwd(j):
        return pl.BlockSpec((None, SCAN_ROWS, 512), lambda b, n: (b, nb - 1 - n, j))

    def const(shape):
        return pl.BlockSpec(shape, lambda b, n: (0,) * len(shape))

    out = jax.ShapeDtypeStruct((bsz, t, MIX_HALF), BF16)
    return pl.pallas_call(
        _scan_c_kernel,
        out_shape=(out, out),
        grid=(bsz, nb),
        in_specs=[fwd(CD_QKC), fwd(CD_VC), bwd(CD_QKC), bwd(CD_VC)] + [const(a.shape) for a in tables],
        out_specs=(pl.BlockSpec((None, SCAN_ROWS, MIX_HALF), lambda b, n: (b, n, 0)),
                   pl.BlockSpec((None, SCAN_ROWS, MIX_HALF), lambda b, n: (b, nb - 1 - n, 0))),
        scratch_shapes=[pltpu.VMEM((2 * N_HEADS, HEAD_DV, LANES), F32)],
        compiler_params=_params(2),
        name="scan_c",
    )(p3, p3, p3, p3, *tables)


def _conv_kernel(prev_ref, cur_ref, next_ref, w_ref, b_ref, o_ref, buf_ref):
    i = pl.program_id(1)
    last = pl.num_programs(1) - 1
    rows = cur_ref.shape[0]
    buf_ref[0:SUBLANES, :] = jnp.where(i == 0, 0.0, prev_ref[...])
    buf_ref[SUBLANES:SUBLANES + rows, :] = cur_ref[...]
    buf_ref[SUBLANES + rows:2 * SUBLANES + rows, :] = jnp.where(i == last, 0.0, next_ref[...])
    y = (buf_ref[SUBLANES - 1:SUBLANES - 1 + rows, :] * w_ref[0:1, :]
         + buf_ref[SUBLANES:SUBLANES + rows, :] * w_ref[1:2, :]
         + buf_ref[SUBLANES + 1:SUBLANES + 1 + rows, :] * w_ref[2:3, :]
         + b_ref[...])
    o_ref[...] = _silu(y)


def _conv_silu(p3, conv_w, conv_b):
    bsz, t, _ = p3.shape
    tile = min(CONV_TILE, t)
    nb = t // tile
    per = tile // SUBLANES
    outs = []
    for j in (CD_QD, CD_KD):
        lo = (j - CD_QD) * 512
        outs.append(pl.pallas_call(
            _conv_kernel,
            out_shape=jax.ShapeDtypeStruct((bsz, t, 512), F32),
            grid=(bsz, nb),
            in_specs=[pl.BlockSpec((None, SUBLANES, 512), lambda b, i, j=j: (b, jnp.maximum(i * per - 1, 0), j)),
                      pl.BlockSpec((None, tile, 512), lambda b, i, j=j: (b, i, j)),
                      pl.BlockSpec((None, SUBLANES, 512),
                                   lambda b, i, j=j: (b, jnp.minimum((i + 1) * per, t // SUBLANES - 1), j)),
                      pl.BlockSpec((3, 512), lambda b, i: (0, 0)),
                      pl.BlockSpec((1, 512), lambda b, i: (0, 0))],
            out_specs=pl.BlockSpec((None, tile, 512), lambda b, i: (b, i, 0)),
            scratch_shapes=[pltpu.VMEM((tile + 2 * SUBLANES, 512), F32)],
            compiler_params=_params(2),
            name="conv_silu",
        )(p3, p3, p3, conv_w[:, lo:lo + 512], conv_b[None, lo:lo + 512]))
    return outs


def _gate_lane_layout():
    lane = np.arange(LANES)
    r = lane % GATE_LANES
    return lane // (N_HEADS * GATE_LANES), (lane // GATE_LANES) % N_HEADS, r // 3, r % 3


def _gate_const_rows():
    _, _, cls, _ = _gate_lane_layout()
    return jnp.asarray(np.concatenate([np.tile((cls == k + 1).astype(np.float32)[None, :], (LANES, 1))
                                       for k in range(3)], axis=0), BF16)


def _pieces(x, piece):
    hi = x.astype(BF16).astype(F32)
    r1 = x - hi
    mid = r1.astype(BF16).astype(F32)
    lo = (r1 - mid).astype(BF16).astype(F32)
    return jnp.where(piece == 0, hi, jnp.where(piece == 1, mid, lo))


def _running_max(x, direction):
    row = lax.broadcasted_iota(jnp.int32, x.shape, 0)
    step = 1
    while step < CHUNK:
        if direction == 0:
            shifted = jnp.where(row >= step, pltpu.roll(x, step, 0), -jnp.inf)
        else:
            shifted = jnp.where(row < CHUNK - step, pltpu.roll(x, CHUNK - step, 0), -jnp.inf)
        x = jnp.maximum(x, shifted)
        step *= 2
    return x


def _scan_d_kernel(q_f, k_f, v_f, gi_f, gf_f, q_b, k_b, v_b, gi_b, gf_b, fgb_ref, const_ref, o_f, o_b, c_ref, m_ref):
    @pl.when(pl.program_id(1) == 0)
    def _():
        c_ref[...] = jnp.zeros_like(c_ref)
        m_ref[...] = jnp.zeros_like(m_ref)

    n_chunks = SCAN_ROWS // CHUNK
    dirs = ((q_f, k_f, v_f, gi_f, gf_f, o_f), (q_b, k_b, v_b, gi_b, gf_b, o_b))
    lane = lax.broadcasted_iota(jnp.int32, (1, LANES), 1)
    lane_dir = lane // (N_HEADS * GATE_LANES)
    lane_head = (lane // GATE_LANES) % N_HEADS
    cls = (lane % GATE_LANES) // 3
    piece = (lane % GATE_LANES) % 3
    ones_block = jnp.ones((CHUNK, HEAD_DV), BF16)

    def prepare(c, m_state):
        pre = []
        for d, (q_ref, k_ref, v_ref, gi_ref, gf_ref, o_ref) in enumerate(dirs):
            rows = pl.ds((c if d == 0 else n_chunks - 1 - c) * CHUNK, CHUNK)
            pre.append((rows, gi_ref[rows, :], _log_sigmoid(gf_ref[rows, :] + fgb_ref[...])))
        bcs = _chunk_cumsum([(pre[0][2], 0), (pre[1][2], 1)])
        units, m_next = [], []
        for d, (q_ref, k_ref, v_ref, gi_ref, gf_ref, o_ref) in enumerate(dirs):
            rows, gi, _ = pre[d]
            b = bcs[d]
            last = _last_row(d)
            mask = _tri(d)
            m_st = m_state[d]
            b_last = b[last:last + 1, :]
            cj = gi - b
            run = _running_max(cj, d)
            m_i = jnp.maximum(b + m_st, b + run)
            u = b - m_i
            m_new = b_last + jnp.maximum(m_st, run[last:last + 1, :])
            s_old = jnp.exp(b_last + m_st - m_new)
            m_next.append(m_new)
            quantity =jnp.where(cls == 0, u, jnp.where(cls == 1, u + m_st, jnp.where(
                cls == 2, -m_i, cj + (b_last - m_new))))
            lhs_all = jnp.where(cls == 4, 1.0, jnp.where(cls < 4, _pieces(quantity, piece), 0.0))
            lhs = jnp.concatenate([jnp.where((lane_dir == d) & (lane_head == h), lhs_all, 0.0)
                                   for h in range(N_HEADS)], axis=0).astype(BF16)
            key_rows = jnp.where(cls == 4, _pieces(cj, piece), jnp.where(cls == 0, 1.0, 0.0)).astype(BF16)
            expo = _nt(lhs, jnp.concatenate([const_ref[...], key_rows], axis=0))
            for h in range(N_HEADS):
                sl = slice(h * HEAD_DV, (h + 1) * HEAD_DV)
                q = q_ref[rows, sl] * (HEAD_DV ** -0.5)
                k = k_ref[rows, sl]
                v_ext = jnp.concatenate([v_ref[rows, sl].astype(BF16), ones_block], axis=1)
                lane0 = (d * N_HEADS + h) * GATE_LANES
                units.append(dict(u=d * N_HEADS + h, rows=rows, sl=sl, o_ref=o_ref, q=q, k=k, v_ext=v_ext,
                                  mask=mask, expo=expo[h * CHUNK:(h + 1) * CHUNK, :],
                                  s_old=s_old[:, lane0:lane0 + 1], qk=_nt(q.astype(BF16), k.astype(BF16))))
        for t in units:
            e = t.pop("expo")
            w = t.pop("qk") * jnp.exp(jnp.where(t.pop("mask"), e[:, 3 * LANES:3 * LANES + CHUNK], -jnp.inf))
            t["lhs2"] = jnp.concatenate([(t.pop("q") * jnp.exp(e[:, :LANES])).astype(BF16), w.astype(BF16)], axis=1)
            t["floor"] = jnp.exp(e[:, LANES:2 * LANES])
            t["upd"] = _tn((t.pop("k") * jnp.exp(e[:, 2 * LANES:3 * LANES])).astype(BF16), t["v_ext"])
        return units, m_next

    def finish(units):
        nums = [_mm(t["lhs2"], jnp.concatenate([c_ref[t["u"]].astype(BF16), t["v_ext"]], axis=0)) for t in units]
        for t, num in zip(units, nums):
            den = jnp.maximum(jnp.abs(num[:, HEAD_DV:]), t["floor"])
            t["o_ref"][t["rows"], t["sl"]] = (num[:, :HEAD_DV] / den).astype(BF16)
            c_ref[t["u"]] = t["s_old"] * c_ref[t["u"]] + t["upd"]

    m_state = [m_ref[d][0:1, :] for d in range(2)]
    pending, m_state = prepare(0, m_state)
    for c in range(n_chunks):
        following = None
        if c + 1 < n_chunks:
            following, m_state = prepare(c + 1, m_state)
        finish(pending)
        pending = following
    for d in range(2):
        m_ref[d] = jnp.broadcast_to(m_state[d], (SUBLANES, LANES))


def _scan_d(qd, kd, p3, fgb_row):
    bsz, t, _ = p3.shape
    nb = t // SCAN_ROWS
    const_rows = _gate_const_rows()

    def spec(j, w, flip):
        if flip:
            return pl.BlockSpec((None, SCAN_ROWS, w), lambda b, n: (b, nb - 1 - n, j))
        return pl.BlockSpec((None, SCAN_ROWS, w), lambda b, n: (b, n, j))

    out = jax.ShapeDtypeStruct((bsz, t, MIX_HALF), BF16)
    in_specs = []
    for flip in (False, True):
        in_specs += [spec(0, 512, flip), spec(0, 512, flip), spec(CD_VD, 512, flip),
                     spec(CD_GATE_I_BLOCK, LANES, flip), spec(CD_GATE_F_BLOCK, LANES, flip)]
    in_specs += [pl.BlockSpec((1, LANES), lambda b, n: (0, 0)), pl.BlockSpec(const_rows.shape, lambda b, n: (0, 0))]
    return pl.pallas_call(
        _scan_d_kernel,
        out_shape=(out, out),
        grid=(bsz, nb),
        in_specs=in_specs,
        out_specs=(spec(0, MIX_HALF, False), spec(0, MIX_HALF, True)),
        scratch_shapes=[pltpu.VMEM((2 * N_HEADS, HEAD_DV, 2 * HEAD_DV), F32),
                        pltpu.VMEM((2, SUBLANES, LANES), F32)],
        compiler_params=_params(2),
        name="scan_d",
    )(qd, kd, p3, p3, p3, qd, kd, p3, p3, p3, fgb_row, const_rows)


def _layer_norm(z, g, b):
    zc = z - jnp.mean(z, axis=-1, keepdims=True)
    var = jnp.mean(zc * zc, axis=-1, keepdims=True)
    return zc * lax.rsqrt(var + LN_EPS) * g + b


def _mix_out_kernel(o1f, o1b, o2f, o2b, g1, g2, x_ref, norm_ref, w_ref, lng, lnb, h_ref, hb_ref):
    parts = []
    for of, ob, g_ref, base in ((o1f, o1b, g1, 0), (o2f, o2b, g2, MIX_HALF)):
        o = of[...].astype(F32) + ob[...].astype(F32)
        gate = g_ref[...]
        for h in range(N_HEADS):
            sl = slice(h * HEAD_DV, (h + 1) * HEAD_DV)
            oh = o[:, sl]
            oh = oh * lax.rsqrt(jnp.mean(oh * oh, axis=-1, keepdims=True) + NORM_EPS)
            parts.append((oh * norm_ref[:, base + h * HEAD_DV:base + (h + 1) * HEAD_DV] * gate[:, sl]).astype(BF16))
    mixed = jnp.concatenate(parts, axis=1)
    z = ALPHA * x_ref[...] + _mm(mixed, w_ref[...])
    hn = _layer_norm(z, lng[...], lnb[...])
    h_ref[...] = hn
    hb_ref[...] = hn.astype(BF16)


def _mix_out(o1f, o1b, o2f, o2b, p2d, gate_blocks, x2d, norm, w_out, ln_g, ln_b):
    n = x2d.shape[0]
    half = pl.BlockSpec((WIDE_TILE, MIX_HALF), lambda i: (i, 0))
    full = pl.BlockSpec((WIDE_TILE, D_MODEL), lambda i: (i, 0))
    row = pl.BlockSpec((1, D_MODEL), lambda i: (0, 0))
    ga, gb = gate_blocks
    return pl.pallas_call(
        _mix_out_kernel,
        out_shape=(jax.ShapeDtypeStruct((n, D_MODEL), F32), jax.ShapeDtypeStruct((n, D_MODEL), BF16)),
        grid=(n // WIDE_TILE,),
        in_specs=[half, half, half, half,
                  pl.BlockSpec((WIDE_TILE, MIX_HALF), lambda i: (i, ga)),
                  pl.BlockSpec((WIDE_TILE, MIX_HALF), lambda i: (i, gb)),
                  full, row, pl.BlockSpec((D_MODEL, D_MODEL), lambda i: (0, 0)), row, row],
        out_specs=(full, full),
        compiler_params=_params(1),
        name="mix_out",
    )(o1f, o1b, o2f, o2b, p2d, p2d, x2d, norm, w_out, ln_g, ln_b)


def _route(h, rw_ref, rb_ref, idx_ref, gate_ref):
    rw = rw_ref[...]
    rw_hi = rw.astype(BF16)
    rw_r = rw - rw_hi.astype(F32)
    rw_mid = rw_r.astype(BF16)
    rw3 = jnp.concatenate([rw_hi, rw_mid, (rw_r - rw_mid.astype(F32)).astype(BF16)], axis=0)
    h_hi = h.astype(BF16)
    h_lo = (h - h_hi.astype(F32)).astype(BF16)
    terms = _nt(rw3, h_hi) + _nt(rw3, h_lo)
    logits = terms[:N_EXPERTS] + terms[N_EXPERTS:2 * N_EXPERTS] + terms[2 * N_EXPERTS:]
    rows = [logits[e:e + 1, :] for e in range(N_EXPERTS)]
    mx = functools.reduce(jnp.maximum, rows)
    ex = [jnp.exp(r - mx) for r in rows]
    tot = functools.reduce(lambda a, b: a + b, ex)
    score = [e / tot for e in ex]
    biased = [score[e] + rb_ref[e:e + 1, :] for e in range(N_EXPERTS)]

    def argmax_first(vals):
        best, idx = vals[0], jnp.zeros(vals[0].shape, jnp.int32)
        for j in range(1, len(vals)):
            upd = vals[j] > best
            best = jnp.where(upd, vals[j], best)
            idx = jnp.where(upd, j, idx)
        return best, idx

    def pick(vals, idx):
        out = vals[0]
        for j in range(1, len(vals)):
            out = jnp.where(idx == j, vals[j], out)
        return out

    group_scores = []
    for g in range(N_GROUPS):
        a = biased[g * EXPERTS_PER_GROUP:(g + 1) * EXPERTS_PER_GROUP]
        pairs = [a[i] + a[j] for i in range(EXPERTS_PER_GROUP) for j in range(i + 1, EXPERTS_PER_GROUP)]
        group_scores.append(functools.reduce(jnp.maximum, pairs))
    _, g_sel = argmax_first(group_scores)
    in_b = [pick([biased[g * EXPERTS_PER_GROUP + k] for g in range(N_GROUPS)], g_sel) for k in range(EXPERTS_PER_GROUP)]
    in_s = [pick([score[g * EXPERTS_PER_GROUP + k] for g in range(N_GROUPS)], g_sel) for k in range(EXPERTS_PER_GROUP)]
    _, i0 = argmax_first(in_b)
    _, i1 = argmax_first([jnp.where(i0 == k, -jnp.inf, in_b[k]) for k in range(EXPERTS_PER_GROUP)])
    s0, s1 = pick(in_s, i0), pick(in_s, i1)
    den = s0 + s1
    idx_ref[0:1, :] = g_sel * EXPERTS_PER_GROUP + i0
    idx_ref[1:2, :] = g_sel * EXPERTS_PER_GROUP + i1
    gate_ref[0:1, :] = s0 / den
    gate_ref[1:2, :] = s1 / den


def _router_kernel(h_ref, rw_ref, rb_ref, idx_ref, gate_ref):
    _route(h_ref[...], rw_ref, rb_ref, idx_ref, gate_ref)


def _router(h2d, rw_t, rb_col):
    n = h2d.shape[0]
    tile = min(ROUTER_TILE, n)
    routed = pl.BlockSpec((TOP_K, tile), lambda i: (0, i))
    return pl.pallas_call(
        _router_kernel,
        out_shape=(jax.ShapeDtypeStruct((TOP_K, n), jnp.int32), jax.ShapeDtypeStruct((TOP_K, n), F32)),
        grid=(n // tile,),
        in_specs=[pl.BlockSpec((tile, D_MODEL), lambda i: (i, 0)),
                  pl.BlockSpec(rw_t.shape, lambda i: (0, 0)),
                  pl.BlockSpec((N_EXPERTS, 1), lambda i: (0, 0))],
        out_specs=(routed, routed),
        compiler_params=_params(1),
        name="router",
    )(h2d, rw_t, rb_col)


def _plan_kernel(idx_ref, dest_ref, cnt_ref, su_ref, run_ref, start_ref):
    phase, i = pl.program_id(0), pl.program_id(1)
    tile = idx_ref.shape[1]
    e_iota = lax.broadcasted_iota(jnp.int32, (N_EXPERTS, tile), 0)
    oh0 = idx_ref[0:1, :] == e_iota
    oh1 = idx_ref[1:2, :] == e_iota
    member = jnp.where(oh0 | oh1, 1.0, 0.0)
    tile_count = jnp.broadcast_to(jnp.sum(member, axis=1, keepdims=True), (N_EXPERTS, LANES))

    @pl.when((phase == 0) & (i == 0))
    def _():
        run_ref[...] = jnp.zeros_like(run_ref)
        r = lax.broadcasted_iota(jnp.int32, (tile, tile), 0)
        c = lax.broadcasted_iota(jnp.int32, (tile, tile), 1)
        su_ref[...] = jnp.where(r < c, 1.0, 0.0).astype(BF16)

    @pl.when((phase == 1) & (i == 0))
    def _():
        counts = run_ref[...]
        cnt_ref[...] = counts.astype(jnp.int32)
        padded = jnp.ceil(counts * (1.0 / EXPERT_ROWS)) * EXPERT_ROWS
        r = lax.broadcasted_iota(jnp.int32, (N_EXPERTS, N_EXPERTS), 0)
        c = lax.broadcasted_iota(jnp.int32, (N_EXPERTS, N_EXPERTS), 1)
        start_ref[...] = jnp.dot(jnp.where(c < r, 1.0, 0.0), padded, precision=HIGHEST, preferred_element_type=F32)
        run_ref[...] = jnp.zeros_like(run_ref)

    @pl.when(phase == 1)
    def _():
        before = _mm(member.astype(BF16), su_ref[...])
        pos = start_ref[:, 0:1] + run_ref[:, 0:1] + before
        dest_ref[0:1, :] = jnp.sum(jnp.where(oh0, pos, 0.0), axis=0, keepdims=True).astype(jnp.int32)
        dest_ref[1:2, :] = jnp.sum(jnp.where(oh1, pos, 0.0), axis=0, keepdims=True).astype(jnp.int32)

    run_ref[...] += tile_count


def _plan(idx):
    n = idx.shape[1]
    tile = min(PLAN_TILE, n)
    return pl.pallas_call(
        _plan_kernel,
        out_shape=(jax.ShapeDtypeStruct((TOP_K, n), jnp.int32), jax.ShapeDtypeStruct((N_EXPERTS, LANES), jnp.int32)),
        grid=(2, n // tile),
        in_specs=[pl.BlockSpec((TOP_K, tile), lambda p, i: (0, i))],
        out_specs=(pl.BlockSpec((TOP_K, tile), lambda p, i: (0, i * p)),
                   pl.BlockSpec((N_EXPERTS, LANES), lambda p, i: (0, 0))),
        scratch_shapes=[pltpu.VMEM((tile, tile), BF16), pltpu.VMEM((N_EXPERTS, LANES), F32),
                        pltpu.VMEM((N_EXPERTS, LANES), F32)],
        compiler_params=_params(2),
        name="moe_plan",
    )(idx)


def _invert_kernel(pad_ref, dest_ref, tok_ref):
    n = dest_ref.shape[0] // TOP_K
    n_rows = tok_ref.shape[0]

    def filler(r):
        return r - jnp.where(r >= 2 * n, 2 * n, jnp.where(r >= n, n, 0))

    for e in range(N_EXPERTS):
        def clear(i, carry, lo=pad_ref[0, e], hi=pad_ref[1, e]):
            r = jnp.minimum(jnp.maximum(hi - 1 - i, lo), n_rows - 1)
            tok_ref[r] = filler(r)
            return carry

        lax.fori_loop(0, EXPERT_ROWS, clear, 0, unroll=16)

    def clear_tail(r, carry):
        tok_ref[r] = filler(r)
        return carry

    lax.fori_loop(pad_ref[0, N_EXPERTS], pad_ref[1, N_EXPERTS], clear_tail, 0)

    for k in range(TOP_K):
        def place(t, carry, k=k):
            tok_ref[dest_ref[k * n + t]] = t
            return carry

        lax.fori_loop(0, n, place, 0, unroll=16)


def _invert(pad_ranges, dest_flat, n_rows):
    smem = pl.BlockSpec(memory_space=pltpu.SMEM)
    return pl.pallas_call(
        _invert_kernel,
        out_shape=jax.ShapeDtypeStruct((n_rows,), jnp.int32),
        in_specs=[smem, smem],
        out_specs=smem,
        name="moe_invert",
    )(pad_ranges, dest_flat)


def _expert_kernel(be_ref, nv_ref, x_ref, w1_ref, w3_ref, w2_ref, y_ref, wb_ref):
    i = pl.program_id(0)
    valid = i < nv_ref[0]
    new_expert = (i == 0) | (be_ref[i] != be_ref[jnp.maximum(i - 1, 0)])

    @pl.when(valid & new_expert)
    def _():
        wb_ref[0] = w1_ref[...].astype(BF16)
        wb_ref[1] = w3_ref[...].astype(BF16)
        wb_ref[2] = w2_ref[...].astype(BF16)

    @pl.when(valid)
    def _():
        x = x_ref[...]
        hid = _silu(_mm(x, wb_ref[0])) * _mm(x, wb_ref[1])
        y_ref[...] = _mm(hid.astype(BF16), wb_ref[2]).astype(BF16)

    @pl.when(jnp.logical_not(valid))
    def _():
        y_ref[...] = jnp.zeros_like(y_ref)


def _experts(block_expert, n_valid, xs, w1, w3, w2, layer):
    n_rows = xs.shape[0]
    wspec = pl.BlockSpec((None, None, D_MODEL, D_MODEL), lambda i, be, nv: (layer, be[i], 0, 0))
    return pl.pallas_call(
        _expert_kernel,
        out_shape=jax.ShapeDtypeStruct((n_rows, D_MODEL), BF16),
        grid_spec=pltpu.PrefetchScalarGridSpec(
            num_scalar_prefetch=2,
            grid=(n_rows // EXPERT_ROWS,),
            in_specs=[pl.BlockSpec((EXPERT_ROWS, D_MODEL), lambda i, be, nv: (i, 0)), wspec, wspec, wspec],
            out_specs=pl.BlockSpec((EXPERT_ROWS, D_MODEL), lambda i, be, nv: (i, 0)),
            scratch_shapes=[pltpu.VMEM((3, D_MODEL, D_MODEL), BF16)]),
        compiler_params=_params(1),
        name="experts",
    )(block_expert, n_valid, xs, w1, w3, w2)


def _combine_kernel(h_ref, y0_ref, y1_ref, g_ref, lng, lnb, o_ref):
    g = g_ref[...]
    z = ALPHA * h_ref[...] + (g[:, 0:1] * y0_ref[...].astype(F32) + g[:, 1:2] * y1_ref[...].astype(F32))
    o_ref[...] = _layer_norm(z, lng[...], lnb[...])


def _combine(h2d, y_rows, gates_t, ln_g, ln_b):
    n = h2d.shape[0]
    nt = n // WIDE_TILE
    full = pl.BlockSpec((WIDE_TILE, D_MODEL), lambda i: (i, 0))
    row = pl.BlockSpec((1, D_MODEL), lambda i: (0, 0))
    return pl.pallas_call(
        _combine_kernel,
        out_shape=jax.ShapeDtypeStruct((n, D_MODEL), F32),
        grid=(nt,),
        in_specs=[full, full, pl.BlockSpec((WIDE_TILE, D_MODEL), lambda i: (nt + i, 0)),
                  pl.BlockSpec((WIDE_TILE, TOP_K), lambda i: (i, 0)), row, row],
        out_specs=full,
        compiler_params=_params(1),
        name="moe_combine",
    )(h2d, y_rows, y_rows, gates_t, ln_g, ln_b)


def _moe(h2d, h_bf16, idx, gates, w1, w3, w2, layer, ln_g, ln_b):
    n = h2d.shape[0]
    dest, counts = _plan(idx)
    n_rows = n * TOP_K + N_EXPERTS * EXPERT_ROWS
    n_blocks = n_rows // EXPERT_ROWS
    counts = counts[:, 0]
    padded = (counts + EXPERT_ROWS - 1) // EXPERT_ROWS * EXPERT_ROWS
    pad_end = jnp.cumsum(padded)
    block_start = jnp.arange(n_blocks, dtype=jnp.int32) * EXPERT_ROWS
    block_expert = jnp.minimum(jnp.sum((pad_end[None, :] <= block_start[:, None]).astype(jnp.int32), axis=1),
                               N_EXPERTS - 1)
    n_valid = (pad_end[-1:] // EXPERT_ROWS).astype(jnp.int32)
    pad_ranges = jnp.stack([jnp.concatenate([pad_end - padded + counts, pad_end[-1:]]),
                            jnp.concatenate([pad_end, jnp.full((1,), n_rows, jnp.int32)])]).astype(jnp.int32)
    dest_flat = dest.reshape(-1)
    xs = h_bf16[_invert(pad_ranges, dest_flat, n_rows)]
    y = _experts(block_expert, n_valid, xs, w1, w3, w2, layer)
    return _combine(h2d, y[dest_flat], gates.T, ln_g, ln_b)


def _pad_cols(w, b, main, width):
    k = w.shape[0]
    tail = w.shape[1] - main
    w_p = jnp.concatenate([w[:, :main], w[:, main:], jnp.zeros((k, width - main - tail), w.dtype)], axis=1)
    b_p = jnp.concatenate([b[:main], b[main:], jnp.zeros((width - main - tail,), b.dtype)])
    return w_p.astype(BF16), b_p[None, :].astype(F32)


def kernel(x, ab_w_in, ab_b_in, hgrn_lb, gla_gk_up, gla_gk_b, hgrn_norm, gla_norm, ab_w_out, cd_w_in, cd_b_in,
           mlstm_conv_w, mlstm_conv_b, mlstm_fgate_b, ret_norm, mlstm_norm, cd_w_out, ln_mix_g, ln_mix_b, ln_ffn_g,
           ln_ffn_b, router_w, router_b, moe_w1, moe_w3, moe_w2):
    bsz, t, d = x.shape
    n = bsz * t
    lower_bounds = jnp.cumsum(jax.nn.softmax(hgrn_lb.astype(F32), axis=0), axis=0)
    rw_t = router_w.T.astype(F32)
    rb_col = router_b.astype(F32)[:, None]
    h = x.reshape(n, d)
    for layer in range(DEPTH):
        j = layer // 2
        if layer % 2 == 0:
            w_p, b_p = _pad_cols(ab_w_in[j], ab_b_in[j], 8 * 512, AB_WIDTH)
            lb = lower_bounds[layer][None, :]
            p = _in_proj(h, w_p, b_p, _proj_ab_kernel, [lb], [pl.BlockSpec(lb.shape, lambda i: (0, 0))])
            p3 = p.reshape(bsz, t, AB_WIDTH)
            nk = N_HEADS * NARROW_DK
            u_pad = jnp.zeros((LANES, 2 * nk), F32)
            u_pad = u_pad.at[:GLA_LOWRANK, :nk].set(gla_gk_up[j, 0])
            u_pad = u_pad.at[GLA_LOWRANK:2 * GLA_LOWRANK, nk:].set(gla_gk_up[j, 1]).astype(BF16)
            gkb = jnp.concatenate([gla_gk_b[j, 0], gla_gk_b[j, 1]])[None, :].astype(F32)
            oa_f, ob_f, oa_b, ob_b = _scan_ab(p3, u_pad, gkb)
            norm = jnp.concatenate([hgrn_norm[j], gla_norm[j]])[None, :].astype(F32)
            outs = [a.reshape(n, MIX_HALF) for a in (oa_f, oa_b, ob_f, ob_b)]
            h, h_bf16 = _mix_out(*outs, p, (AB_GA, AB_GB), h, norm, ab_w_out[j].astype(BF16),
                                 ln_mix_g[layer][None, :], ln_mix_b[layer][None, :])
        else:
            lane_dir, lane_head, _, _ = _gate_lane_layout()
            col_i = 7 * 512 + lane_dir * 2 * N_HEADS + lane_head
            cols = np.concatenate([np.arange(7 * 512), col_i, col_i + N_HEADS])
            w_p, b_p = cd_w_in[j][:, cols].astype(BF16), cd_b_in[j][cols][None, :].astype(F32)
            pos = pl.BlockSpec((ROW_TILE, LANES), lambda i: (i % (t // ROW_TILE), 0))
            p = _in_proj(h, w_p, b_p, _proj_cd_kernel, list(_rotary_tables(t)), [pos, pos])
            p3 = p.reshape(bsz, t, CD_WIDTH)
            oc_f, oc_b = _scan_c(p3)
            qd, kd = _conv_silu(p3, mlstm_conv_w[j].astype(F32), mlstm_conv_b[j].astype(F32))
            fgb_row = mlstm_fgate_b[j].astype(F32)[lane_dir, lane_head][None, :]
            od_f, od_b = _scan_d(qd, kd, p3, fgb_row)
            norm = jnp.concatenate([ret_norm[j], mlstm_norm[j]])[None, :].astype(F32)
            outs = [a.reshape(n, MIX_HALF) for a in (oc_f, oc_b, od_f, od_b)]
            h, h_bf16 = _mix_out(*outs, p, (CD_GC, CD_OD), h, norm, cd_w_out[j].astype(BF16),
                                 ln_mix_g[layer][None, :], ln_mix_b[layer][None, :])
        idx, gates = _router(h, rw_t, rb_col)
        h = _moe(h, h_bf16, idx, gates, moe_w1, moe_w3, moe_w2, layer,
                 ln_ffn_g[layer][None, :], ln_ffn_b[layer][None, :])
    return h.reshape(bsz, t, d)
```

```python
import functools

import jax
import jax.numpy as jnp
import numpy as np
from jax import lax
from jax.experimental import pallas as pl
from jax.experimental.pallas import tpu as pltpu

F32 = jnp.float32
BF16 = jnp.bfloat16
HIGHEST = lax.Precision.HIGHEST

D_MODEL = 1024
DEPTH = 2
CHUNK = 64
MIX_HALF = D_MODEL // 2
N_HEADS = 4
HEAD_DV = MIX_HALF // N_HEADS
NARROW_DK = 64
GLA_LOWRANK = 16
GLA_GATE_NORMALIZER = 16.0
ROPE_THETA = 10000.0
N_EXPERTS = 16
N_GROUPS = 4
EXPERTS_PER_GROUP = N_EXPERTS // N_GROUPS
TOP_K = 2
ALPHA = (2.0 * DEPTH) ** 0.25
LN_EPS = 1e-5
NORM_EPS = 1e-6

LANES = 128
SUBLANES = 8
VMEM_LIMIT = 52 * 1024 * 1024

ROW_TILE = 512
WIDE_TILE = 512
CONV_TILE = 1024
SCAN_ROWS = 512
ROUTER_TILE = 2048
PLAN_TILE = 1024
EXPERT_ROWS = 512

AB_QA, AB_FF, AB_FB, AB_IA, AB_GA, AB_QKB, AB_VB, AB_GB = range(8)
AB_LR_BLOCK = 32
AB_WIDTH = 8 * 512 + LANES
CD_QKC, CD_VC, CD_GC, CD_QD, CD_KD, CD_VD, CD_OD = range(7)
CD_GATE_I_BLOCK = 28
CD_GATE_F_BLOCK = 29
CD_WIDTH = 7 * 512 + 2 * LANES
GATE_LANES = 16


def _params(n_axes, vmem=VMEM_LIMIT):
    return pltpu.CompilerParams(dimension_semantics=("arbitrary",) * n_axes, vmem_limit_bytes=vmem)


def _nt(a, b):
    return lax.dot_general(a, b, (((1,), (1,)), ((), ())), preferred_element_type=F32)


def _tn(a, b):
    return lax.dot_general(a, b, (((0,), (0,)), ((), ())), preferred_element_type=F32)


def _mm(a, b):
    return jnp.dot(a, b, preferred_element_type=F32)


def _sigmoid(x):
    return 1.0 / (1.0 + jnp.exp(-x))


def _silu(x):
    return x * _sigmoid(x)


def _log_sigmoid(x):
    return jnp.minimum(x, 0.0) - jnp.log(1.0 + jnp.exp(-jnp.abs(x)))


def _tri(direction):
    row = lax.broadcasted_iota(jnp.int32, (CHUNK, CHUNK), 0)
    col = lax.broadcasted_iota(jnp.int32, (CHUNK, CHUNK), 1)
    return (col <= row) if direction == 0 else (col >= row)


def _ref_row(direction):
    return CHUNK // 2 - 1 if direction == 0 else CHUNK // 2


def _last_row(direction):
    return CHUNK - 1 if direction == 0 else 0


def _proj_ab_kernel(x_ref, w_ref, b_ref, lb_ref, o_ref):
    p = _mm(x_ref[...].astype(BF16), w_ref[...]) + b_ref[...]
    o_ref[:, :512] = _silu(p[:, :512])
    lb = jnp.concatenate([lb_ref[...], lb_ref[...]], axis=1)
    o_ref[:, 512:1536] = lb + (1.0 - lb) * _sigmoid(p[:, 512:1536])
    o_ref[:, 1536:] = p[:, 1536:]
    for blk in (AB_GA, AB_GB):
        o_ref[:, blk * 512:(blk + 1) * 512] = _silu(p[:, blk * 512:(blk + 1) * 512])


def _proj_cd_kernel(x_ref, w_ref, b_ref, cos_ref, sin_ref, o_ref):
    p = _mm(x_ref[...].astype(BF16), w_ref[...]) + b_ref[...]
    cos, sin_signed = cos_ref[...], sin_ref[...]
    for j in range(4):
        rot = _rotary(p[:, j * LANES:(j + 1) * LANES], cos, sin_signed)
        o_ref[:, j * LANES:(j + 1) * LANES] = rot * (NARROW_DK ** -0.5) if j < 2 else rot
    o_ref[:, 512:] = p[:, 512:]
    o_ref[:, CD_GC * 512:(CD_GC + 1) * 512] = _silu(p[:, CD_GC * 512:(CD_GC + 1) * 512])
    o_ref[:, CD_OD * 512:(CD_OD + 1) * 512] = _sigmoid(p[:, CD_OD * 512:(CD_OD + 1) * 512])


def _in_proj(x2d, w, b, body, extras, extra_specs):
    n, k = x2d.shape
    m = w.shape[1]
    return pl.pallas_call(
        body,
        out_shape=jax.ShapeDtypeStruct((n, m), F32),
        grid=(n // ROW_TILE,),
        in_specs=[pl.BlockSpec((ROW_TILE, k), lambda i: (i, 0)),
                  pl.BlockSpec((k, m), lambda i: (0, 0)),
                  pl.BlockSpec((1, m), lambda i: (0, 0))] + extra_specs,
        out_specs=pl.BlockSpec((ROW_TILE, m), lambda i: (i, 0)),
        compiler_params=_params(1),
        name="in_proj",
    )(x2d, w, b, *extras)


def _pipelined_chunks(n_chunks, prepare, finish):
    pending = prepare(0)
    for c in range(n_chunks):
        following = prepare(c + 1) if c + 1 < n_chunks else None
        finish(pending)
        pending = following


def _units_prepare(units):
    return _units_prepare_staggered([lambda: units])


def _units_prepare_staggered(builders):
    def second(group):
        for u in group:
            u["o_intra"] = _mm(u["intra"](u.pop("scores")).astype(BF16), u["v"])

    done, previous = [], None
    for build in builders:
        group = build()
        for u in group:
            u["scores"] = _nt(u["qe"], u["ke"])
        for u in group:
            u["upd"] = _tn(u["v"], u["ks"])
        if previous is not None:
            second(previous)
        previous = group
        done += group
    second(previous)
    return done


def _units_finish(units, st_ref, group=4):
    def close(batch):
        for u in batch:
            u["store"]((u["o_intra"] + u.pop("o_inter")).astype(BF16))
            st_ref[u["idx"]] = st_ref[u["idx"]] * u["dec"] + u["upd"]

    previous = None
    for g in range(0, len(units), group):
        batch = units[g:g + group]
        for u in batch:
            u["o_inter"] = _nt(u["qi"], st_ref[u["idx"]].astype(BF16))
        if previous is not None:
            close(previous)
        previous = batch
    close(previous)


def _chunk_cumsum(parts):
    cat = jnp.concatenate([x for x, _ in parts], axis=1)
    hi = cat.astype(BF16)
    lo = (cat - hi.astype(F32)).astype(BF16)
    width = cat.shape[1]
    inc = _mm(_tri(0).astype(BF16), jnp.concatenate([hi, lo], axis=1))
    inc = inc[:, :width] + inc[:, width:]
    out, off = [], 0
    for x, direction in parts:
        p = inc[:, off:off + x.shape[1]]
        off += x.shape[1]
        out.append(p if direction == 0 else p[CHUNK - 1:CHUNK, :] - p + x)
    return out


def _gated_operands(q, k, gc, direction):
    r, l = _ref_row(direction), _last_row(direction)
    g_ref = gc[r:r + 1, :]
    g_last = gc[l:l + 1, :]
    qe = q * jnp.exp(gc - g_ref)
    ke = k * jnp.exp(g_ref - gc)
    qi = qe * jnp.exp(g_ref)
    ks = ke * jnp.exp(g_last - g_ref)
    return [a.astype(BF16) for a in (qe, ke, qi, ks)], jnp.exp(g_last)


def _narrow_head_mask(h):
    lane = lax.broadcasted_iota(jnp.int32, (1, LANES), 1)
    return (lane // NARROW_DK) == (h % 2)


def _scan_ab_kernel(qa_f, fa_f, ia_f, qkb_f, vb_f, lr_f, qa_b, fa_b, ia_b, qkb_b, vb_b, lr_b,
                    u_ref, gkb_ref, oa_f, ob_f, oa_b, ob_b, st_ref):
    @pl.when(pl.program_id(1) == 0)
    def _():
        st_ref[...] = jnp.zeros_like(st_ref)

    n_chunks = SCAN_ROWS // CHUNK
    dirs = ((qa_f, fa_f, ia_f, qkb_f, vb_f, lr_f, oa_f, ob_f), (qa_b, fa_b, ia_b, qkb_b, vb_b, lr_b, oa_b, ob_b))

    nk = N_HEADS * NARROW_DK

    def prepare(c):
        pre = []
        for d, (qa, fa, ia, qkb, vb, lr, oa, ob) in enumerate(dirs):
            rows = pl.ds((c if d == 0 else n_chunks - 1 - c) * CHUNK, CHUNK)
            f = fa[rows, :]
            logits = _mm(lr[rows, :].astype(BF16), u_ref[:, d * nk:(d + 1) * nk]) + gkb_ref[:, d * nk:(d + 1) * nk]
            qk = qkb[rows, :]
            pre.append(dict(rows=rows, f=f, lf_a=jnp.log(f), lf_b=_log_sigmoid(logits) / GLA_GATE_NORMALIZER,
                            q_a=qa[rows, :], v_a=ia[rows, :].astype(BF16),
                            q_b=qk[:, :nk] * (NARROW_DK ** -0.5), k_b=qk[:, nk:], v_b=vb[rows, :].astype(BF16)))
        gcs = _chunk_cumsum([(pre[0]["lf_a"], 0), (pre[0]["lf_b"], 0), (pre[1]["lf_a"], 1), (pre[1]["lf_b"], 1)])
        def hgrn_group(d, oa):
            p, mask = pre[d], _tri(d)
            rows = p["rows"]
            (qe, ke, qi, ks), dec = _gated_operands(p["q_a"], 1.0 - p["f"], gcs[2 * d], d)
            group = []
            for h in range(N_HEADS):
                sl = slice(h * LANES, (h + 1) * LANES)

                def store(o, sl=sl):
                    oa[rows, sl] = o

                group.append(dict(qe=qe[:, sl], ke=ke[:, sl], qi=qi[:, sl], ks=ks[:, sl], v=p["v_a"][:, sl],
                                  dec=dec[:, sl], intra=lambda s: jnp.where(mask, s, 0.0),
                                  idx=d * 2 * N_HEADS + h, store=store))
            return group

        def gla_group(d, ob):
            p, mask = pre[d], _tri(d)
            rows = p["rows"]
            (qe, ke, qi, ks), dec = _gated_operands(p["q_b"], p["k_b"], gcs[2 * d + 1], d)
            group = []
            for h in range(N_HEADS):
                psl = slice((h // 2) * LANES, (h // 2 + 1) * LANES)
                hm = _narrow_head_mask(h)
                pick = lambda a, psl=psl, hm=hm: jnp.where(hm, a[:, psl], jnp.zeros((), BF16))
                vsl = slice(h * HEAD_DV, (h + 1) * HEAD_DV)

                def store(o, vsl=vsl):
                    ob[rows, vsl] = o

                group.append(dict(qe=pick(qe), ke=pick(ke), qi=pick(qi), ks=pick(ks), v=p["v_b"][:, vsl],
                                  dec=dec[:, psl], intra=lambda s: jnp.where(mask, s, 0.0),
                                  idx=d * 2 * N_HEADS + N_HEADS + h, store=store))
            return group

        builders = []
        for d, (qa, fa, ia, qkb, vb, lr, oa, ob) in enumerate(dirs):
            builders += [functools.partial(hgrn_group, d, oa), functools.partial(gla_group, d, ob)]
        return _units_prepare_staggered(builders)

    _pipelined_chunks(n_chunks, prepare, functools.partial(_units_finish, st_ref=st_ref))


def _scan_ab(p3, u_pad, gkb):
    bsz, t, _ = p3.shape
    nb = t // SCAN_ROWS

    def fwd(j, w=512):
        return pl.BlockSpec((None, SCAN_ROWS, w), lambda b, n: (b, n, j))

    def bwd(j, w=512):
        return pl.BlockSpec((None, SCAN_ROWS, w), lambda b, n: (b, nb - 1 - n, j))

    def const(shape):
        return pl.BlockSpec(shape, lambda b, n: (0,) * len(shape))

    out = jax.ShapeDtypeStruct((bsz, t, MIX_HALF), BF16)
    o_f = pl.BlockSpec((None, SCAN_ROWS, MIX_HALF), lambda b, n: (b, n, 0))
    o_b = pl.BlockSpec((None, SCAN_ROWS, MIX_HALF), lambda b, n: (b, nb - 1 - n, 0))
    return pl.pallas_call(
        _scan_ab_kernel,
        out_shape=(out, out, out, out),
        grid=(bsz, nb),
        in_specs=[fwd(AB_QA), fwd(AB_FF), fwd(AB_IA), fwd(AB_QKB), fwd(AB_VB), fwd(AB_LR_BLOCK, LANES),
                  bwd(AB_QA), bwd(AB_FB), bwd(AB_IA), bwd(AB_QKB), bwd(AB_VB), bwd(AB_LR_BLOCK, LANES),
                  const(u_pad.shape), const(gkb.shape)],
        out_specs=(o_f, o_f, o_b, o_b),
        scratch_shapes=[pltpu.VMEM((4 * N_HEADS, HEAD_DV, LANES), F32)],
        compiler_params=_params(2),
        name="scan_ab",
    )(*([p3] * 12), u_pad, gkb)


def _rotary(a, cos, sin_signed):
    lane = lax.broadcasted_iota(jnp.int32, (1, LANES), 1)
    first_half = (lane % NARROW_DK) < (NARROW_DK // 2)
    swapped = jnp.where(first_half, pltpu.roll(a, LANES - NARROW_DK // 2, 1), pltpu.roll(a, NARROW_DK // 2, 1))
    return a * cos + swapped * sin_signed


def _scan_c_kernel(qk_f, v_f, qk_b, v_b, dmat_ref, qsc_ref, ksc_ref, cd_ref, o_f, o_b, st_ref):
    @pl.when(pl.program_id(1) == 0)
    def _():
        st_ref[...] = jnp.zeros_like(st_ref)

    n_chunks = SCAN_ROWS // CHUNK
    dirs = ((qk_f, v_f, o_f), (qk_b, v_b, o_b))

    def prepare(c):
        units = []
        for d, (qk_ref, v_ref, o_ref) in enumerate(dirs):
            rows = pl.ds((c if d == 0 else n_chunks - 1 - c) * CHUNK, CHUNK)
            v = v_ref[rows, :].astype(BF16)
            for h in range(N_HEADS):
                hm = _narrow_head_mask(h)
                q = jnp.where(hm, qk_ref[rows, (h // 2) * LANES:(h // 2 + 1) * LANES], 0.0)
                k = jnp.where(hm, qk_ref[rows, (2 + h // 2) * LANES:(3 + h // 2) * LANES], 0.0)
                vsl = slice(h * HEAD_DV, (h + 1) * HEAD_DV)

                def store(o, o_ref=o_ref, rows=rows, vsl=vsl):
                    o_ref[rows, vsl] = o

                units.append(dict(qe=q.astype(BF16), ke=k.astype(BF16), qi=(q * qsc_ref[d, h]).astype(BF16),
                                  ks=(k * ksc_ref[d, h]).astype(BF16), v=v[:, vsl], dec=cd_ref[d, h][0:1, :],
                                  intra=lambda s, d=d, h=h: s * dmat_ref[d, h], idx=d * N_HEADS + h, store=store))
        return _units_prepare(units)

    _pipelined_chunks(n_chunks, prepare, functools.partial(_units_finish, st_ref=st_ref))


def _retention_tables():
    pos = np.arange(CHUNK, dtype=np.float64)
    dmat = np.zeros((2, N_HEADS, CHUNK, CHUNK), np.float32)
    qsc = np.zeros((2, N_HEADS, CHUNK, LANES), np.float32)
    ksc = np.zeros((2, N_HEADS, CHUNK, LANES), np.float32)
    cd = np.zeros((2, N_HEADS, SUBLANES, LANES), np.float32)
    for d in range(2):
        for h in range(N_HEADS):
            lg = np.log1p(-np.exp2(-(5.0 + 2.0 * h + d)))
            dist = pos[:, None] - pos[None, :]
            if d == 0:
                dmat[d, h] = np.where(dist >= 0, np.exp(np.maximum(dist, 0.0) * lg), 0.0)
                qsc[d, h] = np.exp((pos + 1.0) * lg)[:, None]
                ksc[d, h] = np.exp((CHUNK - 1.0 - pos) * lg)[:, None]
            else:
                dmat[d, h] = np.where(dist <= 0, np.exp(np.maximum(-dist, 0.0) * lg), 0.0)
                qsc[d, h] = np.exp((CHUNK - pos) * lg)[:, None]
                ksc[d, h] = np.exp(pos * lg)[:, None]
            cd[d, h] = np.exp(CHUNK * lg)
    return jnp.asarray(dmat), jnp.asarray(qsc), jnp.asarray(ksc), jnp.asarray(cd)


def _rotary_tables(t):
    half = NARROW_DK // 2
    freqs = ROPE_THETA ** (-jnp.arange(half, dtype=F32) / half)
    ang = jnp.arange(t, dtype=F32)[:, None] * freqs[None, :]
    cos, sin = jnp.cos(ang), jnp.sin(ang)
    cos_t = jnp.tile(cos, (1, LANES // half))
    sin_t = jnp.tile(jnp.concatenate([-sin, sin], axis=1), (1, LANES // NARROW_DK))
    return cos_t.astype(F32), sin_t.astype(F32)


def _scan_c(p3):
    bsz, t, _ = p3.shape
    nb = t // SCAN_ROWS
    tables = _retention_tables()

    def fwd(j):
        return pl.BlockSpec((None, SCAN_ROWS, 512), lambda b, n: (b, n, j))

    def bwd(j):
        return pl.BlockSpec((None, SCAN_ROWS, 512), lambda b, n: (b, nb - 1 - n, j))

    def const(shape):
        return pl.BlockSpec(shape, lambda b, n: (0,) * len(shape))

    out = jax.ShapeDtypeStruct((bsz, t, MIX_HALF), BF16)
    return pl.pallas_call(
        _scan_c_kernel,
        out_shape=(out, out),
        grid=(bsz, nb),
        in_specs=[fwd(CD_QKC), fwd(CD_VC), bwd(CD_QKC), bwd(CD_VC)] + [const(a.shape) for a in tables],
        out_specs=(pl.BlockSpec((None, SCAN_ROWS, MIX_HALF), lambda b, n: (b, n, 0)),
                   pl.BlockSpec((None, SCAN_ROWS, MIX_HALF), lambda b, n: (b, nb - 1 - n, 0))),
        scratch_shapes=[pltpu.VMEM((2 * N_HEADS, HEAD_DV, LANES), F32)],
        compiler_params=_params(2),
        name="scan_c",
    )(p3, p3, p3, p3, *tables)


def _conv_kernel(prev_ref, cur_ref, next_ref, w_ref, b_ref, o_ref, buf_ref):
    i = pl.program_id(1)
    last = pl.num_programs(1) - 1
    rows = cur_ref.shape[0]
    buf_ref[0:SUBLANES, :] = jnp.where(i == 0, 0.0, prev_ref[...])
    buf_ref[SUBLANES:SUBLANES + rows, :] = cur_ref[...]
    buf_ref[SUBLANES + rows:2 * SUBLANES + rows, :] = jnp.where(i == last, 0.0, next_ref[...])
    y = (buf_ref[SUBLANES - 1:SUBLANES - 1 + rows, :] * w_ref[0:1, :]
         + buf_ref[SUBLANES:SUBLANES + rows, :] * w_ref[1:2, :]
         + buf_ref[SUBLANES + 1:SUBLANES + 1 + rows, :] * w_ref[2:3, :]
         + b_ref[...])
    o_ref[...] = _silu(y)


def _conv_silu(p3, conv_w, conv_b):
    bsz, t, _ = p3.shape
    tile = min(CONV_TILE, t)
    nb = t // tile
    per = tile // SUBLANES
    outs = []
    for j in (CD_QD, CD_KD):
        lo = (j - CD_QD) * 512
        outs.append(pl.pallas_call(
            _conv_kernel,
            out_shape=jax.ShapeDtypeStruct((bsz, t, 512), F32),
            grid=(bsz, nb),
            in_specs=[pl.BlockSpec((None, SUBLANES, 512), lambda b, i, j=j: (b, jnp.maximum(i * per - 1, 0), j)),
                      pl.BlockSpec((None, tile, 512), lambda b, i, j=j: (b, i, j)),
                      pl.BlockSpec((None, SUBLANES, 512),
                                   lambda b, i, j=j: (b, jnp.minimum((i + 1) * per, t // SUBLANES - 1), j)),
                      pl.BlockSpec((3, 512), lambda b, i: (0, 0)),
                      pl.BlockSpec((1, 512), lambda b, i: (0, 0))],
            out_specs=pl.BlockSpec((None, tile, 512), lambda b, i: (b, i, 0)),
            scratch_shapes=[pltpu.VMEM((tile + 2 * SUBLANES, 512), F32)],
            compiler_params=_params(2),
            name="conv_silu",
        )(p3, p3, p3, conv_w[:, lo:lo + 512], conv_b[None, lo:lo + 512]))
    return outs


def _gate_lane_layout():
    lane = np.arange(LANES)
    r = lane % GATE_LANES
    return lane // (N_HEADS * GATE_LANES), (lane // GATE_LANES) % N_HEADS, r // 3, r % 3


def _gate_const_rows():
    _, _, cls, _ = _gate_lane_layout()
    return jnp.asarray(np.concatenate([np.tile((cls == k + 1).astype(np.float32)[None, :], (LANES, 1))
                                       for k in range(3)], axis=0), BF16)


def _pieces(x, piece):
    hi = x.astype(BF16).astype(F32)
    r1 = x - hi
    mid = r1.astype(BF16).astype(F32)
    lo = (r1 - mid).astype(BF16).astype(F32)
    return jnp.where(piece == 0, hi, jnp.where(piece == 1, mid, lo))


def _running_max(x, direction):
    row = lax.broadcasted_iota(jnp.int32, x.shape, 0)
    step = 1
    while step < CHUNK:
        if direction == 0:
            shifted = jnp.where(row >= step, pltpu.roll(x, step, 0), -jnp.inf)
        else:
            shifted = jnp.where(row < CHUNK - step, pltpu.roll(x, CHUNK - step, 0), -jnp.inf)
        x = jnp.maximum(x, shifted)
        step *= 2
    return x


def _scan_d_kernel(q_f, k_f, v_f, gi_f, gf_f, q_b, k_b, v_b, gi_b, gf_b, fgb_ref, const_ref, o_f, o_b, c_ref, m_ref):
    @pl.when(pl.program_id(1) == 0)
    def _():
        c_ref[...] = jnp.zeros_like(c_ref)
        m_ref[...] = jnp.zeros_like(m_ref)

    n_chunks = SCAN_ROWS // CHUNK
    dirs = ((q_f, k_f, v_f, gi_f, gf_f, o_f), (q_b, k_b, v_b, gi_b, gf_b, o_b))
    lane = lax.broadcasted_iota(jnp.int32, (1, LANES), 1)
    lane_dir = lane // (N_HEADS * GATE_LANES)
    lane_head = (lane // GATE_LANES) % N_HEADS
    cls = (lane % GATE_LANES) // 3
    piece = (lane % GATE_LANES) % 3
    ones_block = jnp.ones((CHUNK, HEAD_DV), BF16)

    def prepare(c, m_state):
        pre = []
        for d, (q_ref, k_ref, v_ref, gi_ref, gf_ref, o_ref) in enumerate(dirs):
            rows = pl.ds((c if d == 0 else n_chunks - 1 - c) * CHUNK, CHUNK)
            pre.append((rows, gi_ref[rows, :], _log_sigmoid(gf_ref[rows, :] + fgb_ref[...])))
        bcs = _chunk_cumsum([(pre[0][2], 0), (pre[1][2], 1)])
        units, m_next = [], []
        for d, (q_ref, k_ref, v_ref, gi_ref, gf_ref, o_ref) in enumerate(dirs):
            rows, gi, _ = pre[d]
            b = bcs[d]
            last = _last_row(d)
            mask = _tri(d)
            m_st = m_state[d]
            b_last = b[last:last + 1, :]
            cj = gi - b
            run = _running_max(cj, d)
            m_i = jnp.maximum(b + m_st, b + run)
            u = b - m_i
            m_new = b_last + jnp.maximum(m_st, run[last:last + 1, :])
            s_old = jnp.exp(b_last + m_st - m_new)
            m_next.append(m_new)
            quantity =jnp.where(cls == 0, u, jnp.where(cls == 1, u + m_st, jnp.where(
                cls == 2, -m_i, cj + (b_last - m_new))))
            lhs_all = jnp.where(cls == 4, 1.0, jnp.where(cls < 4, _pieces(quantity, piece), 0.0))
            lhs = jnp.concatenate([jnp.where((lane_dir == d) & (lane_head == h), lhs_all, 0.0)
                                   for h in range(N_HEADS)], axis=0).astype(BF16)
            key_rows = jnp.where(cls == 4, _pieces(cj, piece), jnp.where(cls == 0, 1.0, 0.0)).astype(BF16)
            expo = _nt(lhs, jnp.concatenate([const_ref[...], key_rows], axis=0))
            for h in range(N_HEADS):
                sl = slice(h * HEAD_DV, (h + 1) * HEAD_DV)
                q = q_ref[rows, sl] * (HEAD_DV ** -0.5)
                k = k_ref[rows, sl]
                v_ext = jnp.concatenate([v_ref[rows, sl].astype(BF16), ones_block], axis=1)
                lane0 = (d * N_HEADS + h) * GATE_LANES
                units.append(dict(u=d * N_HEADS + h, rows=rows, sl=sl, o_ref=o_ref, q=q, k=k, v_ext=v_ext,
                                  mask=mask, expo=expo[h * CHUNK:(h + 1) * CHUNK, :],
                                  s_old=s_old[:, lane0:lane0 + 1], qk=_nt(q.astype(BF16), k.astype(BF16))))
        for t in units:
            e = t.pop("expo")
            w = t.pop("qk") * jnp.exp(jnp.where(t.pop("mask"), e[:, 3 * LANES:3 * LANES + CHUNK], -jnp.inf))
            t["lhs2"] = jnp.concatenate([(t.pop("q") * jnp.exp(e[:, :LANES])).astype(BF16), w.astype(BF16)], axis=1)
            t["floor"] = jnp.exp(e[:, LANES:2 * LANES])
            t["upd"] = _tn((t.pop("k") * jnp.exp(e[:, 2 * LANES:3 * LANES])).astype(BF16), t["v_ext"])
        return units, m_next

    def finish(units):
        nums = [_mm(t["lhs2"], jnp.concatenate([c_ref[t["u"]].astype(BF16), t["v_ext"]], axis=0)) for t in units]
        for t, num in zip(units, nums):
            den = jnp.maximum(jnp.abs(num[:, HEAD_DV:]), t["floor"])
            t["o_ref"][t["rows"], t["sl"]] = (num[:, :HEAD_DV] / den).astype(BF16)
            c_ref[t["u"]] = t["s_old"] * c_ref[t["u"]] + t["upd"]

    m_state = [m_ref[d][0:1, :] for d in range(2)]
    pending, m_state = prepare(0, m_state)
    for c in range(n_chunks):
        following = None
        if c + 1 < n_chunks:
            following, m_state = prepare(c + 1, m_state)
        finish(pending)
        pending = following
    for d in range(2):
        m_ref[d] = jnp.broadcast_to(m_state[d], (SUBLANES, LANES))


def _scan_d(qd, kd, p3, fgb_row):
    bsz, t, _ = p3.shape
    nb = t // SCAN_ROWS
    const_rows = _gate_const_rows()

    def spec(j, w, flip):
        if flip:
            return pl.BlockSpec((None, SCAN_ROWS, w), lambda b, n: (b, nb - 1 - n, j))
        return pl.BlockSpec((None, SCAN_ROWS, w), lambda b, n: (b, n, j))

    out = jax.ShapeDtypeStruct((bsz, t, MIX_HALF), BF16)
    in_specs = []
    for flip in (False, True):
        in_specs += [spec(0, 512, flip), spec(0, 512, flip), spec(CD_VD, 512, flip),
                     spec(CD_GATE_I_BLOCK, LANES, flip), spec(CD_GATE_F_BLOCK, LANES, flip)]
    in_specs += [pl.BlockSpec((1, LANES), lambda b, n: (0, 0)), pl.BlockSpec(const_rows.shape, lambda b, n: (0, 0))]
    return pl.pallas_call(
        _scan_d_kernel,
        out_shape=(out, out),
        grid=(bsz, nb),
        in_specs=in_specs,
        out_specs=(spec(0, MIX_HALF, False), spec(0, MIX_HALF, True)),
        scratch_shapes=[pltpu.VMEM((2 * N_HEADS, HEAD_DV, 2 * HEAD_DV), F32),
                        pltpu.VMEM((2, SUBLANES, LANES), F32)],
        compiler_params=_params(2),
        name="scan_d",
    )(qd, kd, p3, p3, p3, qd, kd, p3, p3, p3, fgb_row, const_rows)


def _layer_norm(z, g, b):
    zc = z - jnp.mean(z, axis=-1, keepdims=True)
    var = jnp.mean(zc * zc, axis=-1, keepdims=True)
    return zc * lax.rsqrt(var + LN_EPS) * g + b


def _mix_out_kernel(o1f, o1b, o2f, o2b, g1, g2, x_ref, norm_ref, w_ref, lng, lnb, h_ref, hb_ref):
    parts = []
    for of, ob, g_ref, base in ((o1f, o1b, g1, 0), (o2f, o2b, g2, MIX_HALF)):
        o = of[...].astype(F32) + ob[...].astype(F32)
        gate = g_ref[...]
        for h in range(N_HEADS):
            sl = slice(h * HEAD_DV, (h + 1) * HEAD_DV)
            oh = o[:, sl]
            oh = oh * lax.rsqrt(jnp.mean(oh * oh, axis=-1, keepdims=True) + NORM_EPS)
            parts.append((oh * norm_ref[:, base + h * HEAD_DV:base + (h + 1) * HEAD_DV] * gate[:, sl]).astype(BF16))
    mixed = jnp.concatenate(parts, axis=1)
    z = ALPHA * x_ref[...] + _mm(mixed, w_ref[...])
    hn = _layer_norm(z, lng[...], lnb[...])
    h_ref[...] = hn
    hb_ref[...] = hn.astype(BF16)


def _mix_out(o1f, o1b, o2f, o2b, p2d, gate_blocks, x2d, norm, w_out, ln_g, ln_b):
    n = x2d.shape[0]
    half = pl.BlockSpec((WIDE_TILE, MIX_HALF), lambda i: (i, 0))
    full = pl.BlockSpec((WIDE_TILE, D_MODEL), lambda i: (i, 0))
    row = pl.BlockSpec((1, D_MODEL), lambda i: (0, 0))
    ga, gb = gate_blocks
    return pl.pallas_call(
        _mix_out_kernel,
        out_shape=(jax.ShapeDtypeStruct((n, D_MODEL), F32), jax.ShapeDtypeStruct((n, D_MODEL), BF16)),
        grid=(n // WIDE_TILE,),
        in_specs=[half, half, half, half,
                  pl.BlockSpec((WIDE_TILE, MIX_HALF), lambda i: (i, ga)),
                  pl.BlockSpec((WIDE_TILE, MIX_HALF), lambda i: (i, gb)),
                  full, row, pl.BlockSpec((D_MODEL, D_MODEL), lambda i: (0, 0)), row, row],
        out_specs=(full, full),
        compiler_params=_params(1),
        name="mix_out",
    )(o1f, o1b, o2f, o2b, p2d, p2d, x2d, norm, w_out, ln_g, ln_b)


def _route(h, rw_ref, rb_ref, idx_ref, gate_ref):
    rw = rw_ref[...]
    rw_hi = rw.astype(BF16)
    rw_r = rw - rw_hi.astype(F32)
    rw_mid = rw_r.astype(BF16)
    rw3 = jnp.concatenate([rw_hi, rw_mid, (rw_r - rw_mid.astype(F32)).astype(BF16)], axis=0)
    h_hi = h.astype(BF16)
    h_lo = (h - h_hi.astype(F32)).astype(BF16)
    terms = _nt(rw3, h_hi) + _nt(rw3, h_lo)
    logits = terms[:N_EXPERTS] + terms[N_EXPERTS:2 * N_EXPERTS] + terms[2 * N_EXPERTS:]
    rows = [logits[e:e + 1, :] for e in range(N_EXPERTS)]
    mx = functools.reduce(jnp.maximum, rows)
    ex = [jnp.exp(r - mx) for r in rows]
    tot = functools.reduce(lambda a, b: a + b, ex)
    score = [e / tot for e in ex]
    biased = [score[e] + rb_ref[e:e + 1, :] for e in range(N_EXPERTS)]

    def argmax_first(vals):
        best, idx = vals[0], jnp.zeros(vals[0].shape, jnp.int32)
        for j in range(1, len(vals)):
            upd = vals[j] > best
            best = jnp.where(upd, vals[j], best)
            idx = jnp.where(upd, j, idx)
        return best, idx

    def pick(vals, idx):
        out = vals[0]
        for j in range(1, len(vals)):
            out = jnp.where(idx == j, vals[j], out)
        return out

    group_scores = []
    for g in range(N_GROUPS):
        a = biased[g * EXPERTS_PER_GROUP:(g + 1) * EXPERTS_PER_GROUP]
        pairs = [a[i] + a[j] for i in range(EXPERTS_PER_GROUP) for j in range(i + 1, EXPERTS_PER_GROUP)]
        group_scores.append(functools.reduce(jnp.maximum, pairs))
    _, g_sel = argmax_first(group_scores)
    in_b = [pick([biased[g * EXPERTS_PER_GROUP + k] for g in range(N_GROUPS)], g_sel) for k in range(EXPERTS_PER_GROUP)]
    in_s = [pick([score[g * EXPERTS_PER_GROUP + k] for g in range(N_GROUPS)], g_sel) for k in range(EXPERTS_PER_GROUP)]
    _, i0 = argmax_first(in_b)
    _, i1 = argmax_first([jnp.where(i0 == k, -jnp.inf, in_b[k]) for k in range(EXPERTS_PER_GROUP)])
    s0, s1 = pick(in_s, i0), pick(in_s, i1)
    den = s0 + s1
    idx_ref[0:1, :] = g_sel * EXPERTS_PER_GROUP + i0
    idx_ref[1:2, :] = g_sel * EXPERTS_PER_GROUP + i1
    gate_ref[0:1, :] = s0 / den
    gate_ref[1:2, :] = s1 / den


def _router_kernel(h_ref, rw_ref, rb_ref, idx_ref, gate_ref):
    _route(h_ref[...], rw_ref, rb_ref, idx_ref, gate_ref)


def _router(h2d, rw_t, rb_col):
    n = h2d.shape[0]
    tile = min(ROUTER_TILE, n)
    routed = pl.BlockSpec((TOP_K, tile), lambda i: (0, i))
    return pl.pallas_call(
        _router_kernel,
        out_shape=(jax.ShapeDtypeStruct((TOP_K, n), jnp.int32), jax.ShapeDtypeStruct((TOP_K, n), F32)),
        grid=(n // tile,),
        in_specs=[pl.BlockSpec((tile, D_MODEL), lambda i: (i, 0)),
                  pl.BlockSpec(rw_t.shape, lambda i: (0, 0)),
                  pl.BlockSpec((N_EXPERTS, 1), lambda i: (0, 0))],
        out_specs=(routed, routed),
        compiler_params=_params(1),
        name="router",
    )(h2d, rw_t, rb_col)


def _plan_kernel(idx_ref, dest_ref, cnt_ref, su_ref, run_ref, start_ref):
    phase, i = pl.program_id(0), pl.program_id(1)
    tile = idx_ref.shape[1]
    e_iota = lax.broadcasted_iota(jnp.int32, (N_EXPERTS, tile), 0)
    oh0 = idx_ref[0:1, :] == e_iota
    oh1 = idx_ref[1:2, :] == e_iota
    member = jnp.where(oh0 | oh1, 1.0, 0.0)
    tile_count = jnp.broadcast_to(jnp.sum(member, axis=1, keepdims=True), (N_EXPERTS, LANES))

    @pl.when((phase == 0) & (i == 0))
    def _():
        run_ref[...] = jnp.zeros_like(run_ref)
        r = lax.broadcasted_iota(jnp.int32, (tile, tile), 0)
        c = lax.broadcasted_iota(jnp.int32, (tile, tile), 1)
        su_ref[...] = jnp.where(r < c, 1.0, 0.0).astype(BF16)

    @pl.when((phase == 1) & (i == 0))
    def _():
        counts = run_ref[...]
        cnt_ref[...] = counts.astype(jnp.int32)
        padded = jnp.ceil(counts * (1.0 / EXPERT_ROWS)) * EXPERT_ROWS
        r = lax.broadcasted_iota(jnp.int32, (N_EXPERTS, N_EXPERTS), 0)
        c = lax.broadcasted_iota(jnp.int32, (N_EXPERTS, N_EXPERTS), 1)
        start_ref[...] = jnp.dot(jnp.where(c < r, 1.0, 0.0), padded, precision=HIGHEST, preferred_element_type=F32)
        run_ref[...] = jnp.zeros_like(run_ref)

    @pl.when(phase == 1)
    def _():
        before = _mm(member.astype(BF16), su_ref[...])
        pos = start_ref[:, 0:1] + run_ref[:, 0:1] + before
        dest_ref[0:1, :] = jnp.sum(jnp.where(oh0, pos, 0.0), axis=0, keepdims=True).astype(jnp.int32)
        dest_ref[1:2, :] = jnp.sum(jnp.where(oh1, pos, 0.0), axis=0, keepdims=True).astype(jnp.int32)

    run_ref[...] += tile_count


def _plan(idx):
    n = idx.shape[1]
    tile = min(PLAN_TILE, n)
    return pl.pallas_call(
        _plan_kernel,
        out_shape=(jax.ShapeDtypeStruct((TOP_K, n), jnp.int32), jax.ShapeDtypeStruct((N_EXPERTS, LANES), jnp.int32)),
        grid=(2, n // tile),
        in_specs=[pl.BlockSpec((TOP_K, tile), lambda p, i: (0, i))],
        out_specs=(pl.BlockSpec((TOP_K, tile), lambda p, i: (0, i * p)),
                   pl.BlockSpec((N_EXPERTS, LANES), lambda p, i: (0, 0))),
        scratch_shapes=[pltpu.VMEM((tile, tile), BF16), pltpu.VMEM((N_EXPERTS, LANES), F32),
                        pltpu.VMEM((N_EXPERTS, LANES), F32)],
        compiler_params=_params(2),
        name="moe_plan",
    )(idx)


def _invert_kernel(pad_ref, dest_ref, tok_ref):
    n = dest_ref.shape[0] // TOP_K
    n_rows = tok_ref.shape[0]

    def filler(r):
        return r - jnp.where(r >= 2 * n, 2 * n, jnp.where(r >= n, n, 0))

    for e in range(N_EXPERTS):
        def clear(i, carry, lo=pad_ref[0, e], hi=pad_ref[1, e]):
            r = jnp.minimum(jnp.maximum(hi - 1 - i, lo), n_rows - 1)
            tok_ref[r] = filler(r)
            return carry

        lax.fori_loop(0, EXPERT_ROWS, clear, 0, unroll=16)

    def clear_tail(r, carry):
        tok_ref[r] = filler(r)
        return carry

    lax.fori_loop(pad_ref[0, N_EXPERTS], pad_ref[1, N_EXPERTS], clear_tail, 0)

    def place(t, carry):
        for k in range(TOP_K):
            tok_ref[dest_ref[k * n + t]] = t
        return carry

    lax.fori_loop(0, n, place, 0, unroll=16)


def _invert(pad_ranges, dest_flat, n_rows):
    smem = pl.BlockSpec(memory_space=pltpu.SMEM)
    return pl.pallas_call(
        _invert_kernel,
        out_shape=jax.ShapeDtypeStruct((n_rows,), jnp.int32),
        in_specs=[smem, smem],
        out_specs=smem,
        name="moe_invert",
    )(pad_ranges, dest_flat)


def _expert_kernel(be_ref, nv_ref, x_ref, w1_ref, w3_ref, w2_ref, y_ref, wb_ref):
    i = pl.program_id(0)
    valid = i < nv_ref[0]
    new_expert = (i == 0) | (be_ref[i] != be_ref[jnp.maximum(i - 1, 0)])

    @pl.when(valid & new_expert)
    def _():
        wb_ref[0] = w1_ref[...].astype(BF16)
        wb_ref[1] = w3_ref[...].astype(BF16)
        wb_ref[2] = w2_ref[...].astype(BF16)

    @pl.when(valid)
    def _():
        x = x_ref[...]
        hid = _silu(_mm(x, wb_ref[0])) * _mm(x, wb_ref[1])
        y_ref[...] = _mm(hid.astype(BF16), wb_ref[2]).astype(BF16)

    @pl.when(jnp.logical_not(valid))
    def _():
        y_ref[...] = jnp.zeros_like(y_ref)


def _experts(block_expert, n_valid, xs, w1, w3, w2, layer):
    n_rows = xs.shape[0]
    wspec = pl.BlockSpec((None, None, D_MODEL, D_MODEL), lambda i, be, nv: (layer, be[i], 0, 0))
    return pl.pallas_call(
        _expert_kernel,
        out_shape=jax.ShapeDtypeStruct((n_rows, D_MODEL), BF16),
        grid_spec=pltpu.PrefetchScalarGridSpec(
            num_scalar_prefetch=2,
            grid=(n_rows // EXPERT_ROWS,),
            in_specs=[pl.BlockSpec((EXPERT_ROWS, D_MODEL), lambda i, be, nv: (i, 0)), wspec, wspec, wspec],
            out_specs=pl.BlockSpec((EXPERT_ROWS, D_MODEL), lambda i, be, nv: (i, 0)),
            scratch_shapes=[pltpu.VMEM((3, D_MODEL, D_MODEL), BF16)]),
        compiler_params=_params(1),
        name="experts",
    )(block_expert, n_valid, xs, w1, w3, w2)


def _combine_kernel(h_ref, y0_ref, y1_ref, g_ref, lng, lnb, o_ref):
    g = g_ref[...]
    z = ALPHA * h_ref[...] + (g[:, 0:1] * y0_ref[...].astype(F32) + g[:, 1:2] * y1_ref[...].astype(F32))
    o_ref[...] = _layer_norm(z, lng[...], lnb[...])


def _combine(h2d, y_rows, gates_t, ln_g, ln_b):
    n = h2d.shape[0]
    nt = n // WIDE_TILE
    full = pl.BlockSpec((WIDE_TILE, D_MODEL), lambda i: (i, 0))
    row = pl.BlockSpec((1, D_MODEL), lambda i: (0, 0))
    return pl.pallas_call(
        _combine_kernel,
        out_shape=jax.ShapeDtypeStruct((n, D_MODEL), F32),
        grid=(nt,),
        in_specs=[full, full, pl.BlockSpec((WIDE_TILE, D_MODEL), lambda i: (nt + i, 0)),
                  pl.BlockSpec((WIDE_TILE, TOP_K), lambda i: (i, 0)), row, row],
        out_specs=full,
        compiler_params=_params(1),
        name="moe_combine",
    )(h2d, y_rows, y_rows, gates_t, ln_g, ln_b)


def _moe(h2d, h_bf16, idx, gates, w1, w3, w2, layer, ln_g, ln_b):
    n = h2d.shape[0]
    dest, counts = _plan(idx)
    n_rows = n * TOP_K + N_EXPERTS * EXPERT_ROWS
    n_blocks = n_rows // EXPERT_ROWS
    counts = counts[:, 0]
    padded = (counts + EXPERT_ROWS - 1) // EXPERT_ROWS * EXPERT_ROWS
    pad_end = jnp.cumsum(padded)
    block_start = jnp.arange(n_blocks, dtype=jnp.int32) * EXPERT_ROWS
    block_expert = jnp.minimum(jnp.sum((pad_end[None, :] <= block_start[:, None]).astype(jnp.int32), axis=1),
                               N_EXPERTS - 1)
    n_valid = (pad_end[-1:] // EXPERT_ROWS).astype(jnp.int32)
    pad_ranges = jnp.stack([jnp.concatenate([pad_end - padded + counts, pad_end[-1:]]),
                            jnp.concatenate([pad_end, jnp.full((1,), n_rows, jnp.int32)])]).astype(jnp.int32)
    dest_flat = dest.reshape(-1)
    xs = h_bf16[_invert(pad_ranges, dest_flat, n_rows)]
    y = _experts(block_expert, n_valid, xs, w1, w3, w2, layer)
    return _combine(h2d, y[dest_flat], gates.T, ln_g, ln_b)


def _pad_cols(w, b, main, width):
    k = w.shape[0]
    tail = w.shape[1] - main
    w_p = jnp.concatenate([w[:, :main], w[:, main:], jnp.zeros((k, width - main - tail), w.dtype)], axis=1)
    b_p = jnp.concatenate([b[:main], b[main:], jnp.zeros((width - main - tail,), b.dtype)])
    return w_p.astype(BF16), b_p[None, :].astype(F32)


def kernel(x, ab_w_in, ab_b_in, hgrn_lb, gla_gk_up, gla_gk_b, hgrn_norm, gla_norm, ab_w_out, cd_w_in, cd_b_in,
           mlstm_conv_w, mlstm_conv_b, mlstm_fgate_b, ret_norm, mlstm_norm, cd_w_out, ln_mix_g, ln_mix_b, ln_ffn_g,
           ln_ffn_b, router_w, router_b, moe_w1, moe_w3, moe_w2):
    bsz, t, d = x.shape
    n = bsz * t
    lower_bounds = jnp.cumsum(jax.nn.softmax(hgrn_lb.astype(F32), axis=0), axis=0)
    rw_t = router_w.T.astype(F32)
    rb_col = router_b.astype(F32)[:, None]
    h = x.reshape(n, d)
    for layer in range(DEPTH):
        j = layer // 2
        if layer % 2 == 0:
            w_p, b_p = _pad_cols(ab_w_in[j], ab_b_in[j], 8 * 512, AB_WIDTH)
            lb = lower_bounds[layer][None, :]
            p = _in_proj(h, w_p, b_p, _proj_ab_kernel, [lb], [pl.BlockSpec(lb.shape, lambda i: (0, 0))])
            p3 = p.reshape(bsz, t, AB_WIDTH)
            nk = N_HEADS * NARROW_DK
            u_pad = jnp.zeros((LANES, 2 * nk), F32)
            u_pad = u_pad.at[:GLA_LOWRANK, :nk].set(gla_gk_up[j, 0])
            u_pad = u_pad.at[GLA_LOWRANK:2 * GLA_LOWRANK, nk:].set(gla_gk_up[j, 1]).astype(BF16)
            gkb = jnp.concatenate([gla_gk_b[j, 0], gla_gk_b[j, 1]])[None, :].astype(F32)
            oa_f, ob_f, oa_b, ob_b = _scan_ab(p3, u_pad, gkb)
            norm = jnp.concatenate([hgrn_norm[j], gla_norm[j]])[None, :].astype(F32)
            outs = [a.reshape(n, MIX_HALF) for a in (oa_f, oa_b, ob_f, ob_b)]
            h, h_bf16 = _mix_out(*outs, p, (AB_GA, AB_GB), h, norm, ab_w_out[j].astype(BF16),
                                 ln_mix_g[layer][None, :], ln_mix_b[layer][None, :])
        else:
            lane_dir, lane_head, _, _ = _gate_lane_layout()
            col_i = 7 * 512 + lane_dir * 2 * N_HEADS + lane_head
            cols = np.concatenate([np.arange(7 * 512), col_i, col_i + N_HEADS])
            w_p, b_p = cd_w_in[j][:, cols].astype(BF16), cd_b_in[j][cols][None, :].astype(F32)
            pos = pl.BlockSpec((ROW_TILE, LANES), lambda i: (i % (t // ROW_TILE), 0))
            p = _in_proj(h, w_p, b_p, _proj_cd_kernel, list(_rotary_tables(t)), [pos, pos])
            p3 = p.reshape(bsz, t, CD_WIDTH)
            oc_f, oc_b = _scan_c(p3)
            qd, kd = _conv_silu(p3, mlstm_conv_w[j].astype(F32), mlstm_conv_b[j].astype(F32))
            fgb_row = mlstm_fgate_b[j].astype(F32)[lane_dir, lane_head][None, :]
            od_f, od_b = _scan_d(qd, kd, p3, fgb_row)
            norm = jnp.concatenate([ret_norm[j], mlstm_norm[j]])[None, :].astype(F32)
            outs = [a.reshape(n, MIX_HALF) for a in (oc_f, oc_b, od_f, od_b)]
            h, h_bf16 = _mix_out(*outs, p, (CD_GC, CD_OD), h, norm, cd_w_out[j].astype(BF16),
                                 ln_mix_g[layer][None, :], ln_mix_b[layer][None, :])
        idx, gates = _router(h, rw_t, rb_col)
        h = _moe(h, h_bf16, idx, gates, moe_w1, moe_w3, moe_w2, layer,
                 ln_ffn_g[layer][None, :], ln_ffn_b[layer][None, :])
    return h.reshape(bsz, t, d)
```

```python
import functools

import jax
import jax.numpy as jnp
import numpy as np
from jax import lax
from jax.experimental import pallas as pl
from jax.experimental.pallas import tpu as pltpu

F32 = jnp.float32
BF16 = jnp.bfloat16
HIGHEST = lax.Precision.HIGHEST

D_MODEL = 1024
DEPTH = 2
CHUNK = 64
MIX_HALF = D_MODEL // 2
N_HEADS = 4
HEAD_DV = MIX_HALF // N_HEADS
NARROW_DK = 64
GLA_LOWRANK = 16
GLA_GATE_NORMALIZER = 16.0
ROPE_THETA = 10000.0
N_EXPERTS = 16
N_GROUPS = 4
EXPERTS_PER_GROUP = N_EXPERTS // N_GROUPS
TOP_K = 2
ALPHA = (2.0 * DEPTH) ** 0.25
LN_EPS = 1e-5
NORM_EPS = 1e-6

LANES = 128
SUBLANES = 8
VMEM_LIMIT = 52 * 1024 * 1024

ROW_TILE = 512
WIDE_TILE = 512
CONV_TILE = 1024
SCAN_ROWS = 512
ROUTER_TILE = 2048
PLAN_TILE = 1024
EXPERT_ROWS = 512

AB_QA, AB_FF, AB_FB, AB_IA, AB_GA, AB_QKB, AB_VB, AB_GB = range(8)
AB_LR_BLOCK = 32
AB_WIDTH = 8 * 512 + LANES
CD_QKC, CD_VC, CD_GC, CD_QD, CD_KD, CD_VD, CD_OD = range(7)
CD_GATE_I_BLOCK = 28
CD_GATE_F_BLOCK = 29
CD_WIDTH = 7 * 512 + 2 * LANES
GATE_LANES = 16


def _params(n_axes, vmem=VMEM_LIMIT):
    return pltpu.CompilerParams(dimension_semantics=("arbitrary",) * n_axes, vmem_limit_bytes=vmem)


def _nt(a, b):
    return lax.dot_general(a, b, (((1,), (1,)), ((), ())), preferred_element_type=F32)


def _tn(a, b):
    return lax.dot_general(a, b, (((0,), (0,)), ((), ())), preferred_element_type=F32)


def _mm(a, b):
    return jnp.dot(a, b, preferred_element_type=F32)


def _sigmoid(x):
    return 1.0 / (1.0 + jnp.exp(-x))


def _silu(x):
    return x * _sigmoid(x)


def _log_sigmoid(x):
    return jnp.minimum(x, 0.0) - jnp.log(1.0 + jnp.exp(-jnp.abs(x)))


def _tri(direction):
    row = lax.broadcasted_iota(jnp.int32, (CHUNK, CHUNK), 0)
    col = lax.broadcasted_iota(jnp.int32, (CHUNK, CHUNK), 1)
    return (col <= row) if direction == 0 else (col >= row)


def _ref_row(direction):
    return CHUNK // 2 - 1 if direction == 0 else CHUNK // 2


def _last_row(direction):
    return CHUNK - 1 if direction == 0 else 0


def _proj_ab_kernel(x_ref, w_ref, b_ref, lb_ref, o_ref):
    p = _mm(x_ref[...].astype(BF16), w_ref[...]) + b_ref[...]
    o_ref[:, :512] = _silu(p[:, :512])
    lb = jnp.concatenate([lb_ref[...], lb_ref[...]], axis=1)
    o_ref[:, 512:1536] = lb + (1.0 - lb) * _sigmoid(p[:, 512:1536])
    o_ref[:, 1536:] = p[:, 1536:]
    for blk in (AB_GA, AB_GB):
        o_ref[:, blk * 512:(blk + 1) * 512] = _silu(p[:, blk * 512:(blk + 1) * 512])


def _proj_cd_kernel(x_ref, w_ref, b_ref, cos_ref, sin_ref, o_ref):
    p = _mm(x_ref[...].astype(BF16), w_ref[...]) + b_ref[...]
    cos, sin_signed = cos_ref[...], sin_ref[...]
    for j in range(4):
        rot = _rotary(p[:, j * LANES:(j + 1) * LANES], cos, sin_signed)
        o_ref[:, j * LANES:(j + 1) * LANES] = rot * (NARROW_DK ** -0.5) if j < 2 else rot
    o_ref[:, 512:] = p[:, 512:]
    o_ref[:, CD_GC * 512:(CD_GC + 1) * 512] = _silu(p[:, CD_GC * 512:(CD_GC + 1) * 512])
    o_ref[:, CD_OD * 512:(CD_OD + 1) * 512] = _sigmoid(p[:, CD_OD * 512:(CD_OD + 1) * 512])


def _in_proj(x2d, w, b, body, extras, extra_specs):
    n, k = x2d.shape
    m = w.shape[1]
    return pl.pallas_call(
        body,
        out_shape=jax.ShapeDtypeStruct((n, m), F32),
        grid=(n // ROW_TILE,),
        in_specs=[pl.BlockSpec((ROW_TILE, k), lambda i: (i, 0)),
                  pl.BlockSpec((k, m), lambda i: (0, 0)),
                  pl.BlockSpec((1, m), lambda i: (0, 0))] + extra_specs,
        out_specs=pl.BlockSpec((ROW_TILE, m), lambda i: (i, 0)),
        compiler_params=_params(1),
        name="in_proj",
    )(x2d, w, b, *extras)


def _pipelined_chunks(n_chunks, prepare, finish):
    pending = prepare(0)
    for c in range(n_chunks):
        following = prepare(c + 1) if c + 1 < n_chunks else None
        finish(pending)
        pending = following


def _units_prepare(units):
    return _units_prepare_staggered([lambda: units])


def _units_prepare_staggered(builders):
    def second(group):
        for u in group:
            u["lhs"] = jnp.concatenate([u["qi"], u["intra"](u.pop("scores")).astype(BF16)], axis=1)

    done, previous = [], None
    for build in builders:
        group = build()
        for u in group:
            u["scores"] = _nt(u["qe"], u["ke"])
        for u in group:
            u["v_t"] = u["v"].T
            u["upd"] = _mm(u["v_t"], u["ks"])
        if previous is not None:
            second(previous)
        previous = group
        done += group
    second(previous)
    return done


def _units_finish(units, st_ref, group=4):
    def close(batch):
        for u in batch:
            u["store"](u.pop("o").astype(BF16))
            st_ref[u["idx"]] = st_ref[u["idx"]] * u["dec"] + u["upd"]

    previous = None
    for g in range(0, len(units), group):
        batch = units[g:g + group]
        for u in batch:
            u["o"] = _nt(u["lhs"], jnp.concatenate([st_ref[u["idx"]].astype(BF16), u["v_t"]], axis=1))
        if previous is not None:
            close(previous)
        previous = batch
    close(previous)


def _chunk_cumsum(parts):
    cat = jnp.concatenate([x for x, _ in parts], axis=1)
    hi = cat.astype(BF16)
    lo = (cat - hi.astype(F32)).astype(BF16)
    width = cat.shape[1]
    inc = _mm(_tri(0).astype(BF16), jnp.concatenate([hi, lo], axis=1))
    inc = inc[:, :width] + inc[:, width:]
    out, off = [], 0
    for x, direction in parts:
        p = inc[:, off:off + x.shape[1]]
        off += x.shape[1]
        out.append(p if direction == 0 else p[CHUNK - 1:CHUNK, :] - p + x)
    return out


def _gated_operands(q, k, gc, direction):
    r, l = _ref_row(direction), _last_row(direction)
    g_ref = gc[r:r + 1, :]
    g_last = gc[l:l + 1, :]
    qe = q * jnp.exp(gc - g_ref)
    ke = k * jnp.exp(g_ref - gc)
    qi = qe * jnp.exp(g_ref)
    ks = ke * jnp.exp(g_last - g_ref)
    return [a.astype(BF16) for a in (qe, ke, qi, ks)], jnp.exp(g_last)


def _narrow_head_mask(h):
    lane = lax.broadcasted_iota(jnp.int32, (1, LANES), 1)
    return (lane // NARROW_DK) == (h % 2)


def _scan_ab_kernel(qa_f, fa_f, ia_f, qkb_f, vb_f, lr_f, qa_b, fa_b, ia_b, qkb_b, vb_b, lr_b,
                    u_ref, gkb_ref, oa_f, ob_f, oa_b, ob_b, st_ref):
    @pl.when(pl.program_id(1) == 0)
    def _():
        st_ref[...] = jnp.zeros_like(st_ref)

    n_chunks = SCAN_ROWS // CHUNK
    dirs = ((qa_f, fa_f, ia_f, qkb_f, vb_f, lr_f, oa_f, ob_f), (qa_b, fa_b, ia_b, qkb_b, vb_b, lr_b, oa_b, ob_b))

    nk = N_HEADS * NARROW_DK

    def prepare(c):
        pre = []
        for d, (qa, fa, ia, qkb, vb, lr, oa, ob) in enumerate(dirs):
            rows = pl.ds((c if d == 0 else n_chunks - 1 - c) * CHUNK, CHUNK)
            f = fa[rows, :]
            logits = _mm(lr[rows, :].astype(BF16), u_ref[:, d * nk:(d + 1) * nk]) + gkb_ref[:, d * nk:(d + 1) * nk]
            qk = qkb[rows, :]
            pre.append(dict(rows=rows, f=f, lf_a=jnp.log(f), lf_b=_log_sigmoid(logits) / GLA_GATE_NORMALIZER,
                            q_a=qa[rows, :], v_a=ia[rows, :].astype(BF16),
                            q_b=qk[:, :nk] * (NARROW_DK ** -0.5), k_b=qk[:, nk:], v_b=vb[rows, :].astype(BF16)))
        gcs = _chunk_cumsum([(pre[0]["lf_a"], 0), (pre[0]["lf_b"], 0), (pre[1]["lf_a"], 1), (pre[1]["lf_b"], 1)])
        def hgrn_group(d, oa):
            p, mask = pre[d], _tri(d)
            rows = p["rows"]
            (qe, ke, qi, ks), dec = _gated_operands(p["q_a"], 1.0 - p["f"], gcs[2 * d], d)
            group = []
            for h in range(N_HEADS):
                sl = slice(h * LANES, (h + 1) * LANES)

                def store(o, sl=sl):
                    oa[rows, sl] = o

                group.append(dict(qe=qe[:, sl], ke=ke[:, sl], qi=qi[:, sl], ks=ks[:, sl], v=p["v_a"][:, sl],
                                  dec=dec[:, sl], intra=lambda s: jnp.where(mask, s, 0.0),
                                  idx=d * 2 * N_HEADS + h, store=store))
            return group

        def gla_group(d, ob):
            p, mask = pre[d], _tri(d)
            rows = p["rows"]
            (qe, ke, qi, ks), dec = _gated_operands(p["q_b"], p["k_b"], gcs[2 * d + 1], d)
            group = []
            for h in range(N_HEADS):
                psl = slice((h // 2) * LANES, (h // 2 + 1) * LANES)
                hm = _narrow_head_mask(h)
                pick = lambda a, psl=psl, hm=hm: jnp.where(hm, a[:, psl], jnp.zeros((), BF16))
                vsl = slice(h * HEAD_DV, (h + 1) * HEAD_DV)

                def store(o, vsl=vsl):
                    ob[rows, vsl] = o

                group.append(dict(qe=pick(qe), ke=pick(ke), qi=pick(qi), ks=pick(ks), v=p["v_b"][:, vsl],
                                  dec=dec[:, psl], intra=lambda s: jnp.where(mask, s, 0.0),
                                  idx=d * 2 * N_HEADS + N_HEADS + h, store=store))
            return group

        builders = []
        for d, (qa, fa, ia, qkb, vb, lr, oa, ob) in enumerate(dirs):
            builders += [functools.partial(hgrn_group, d, oa), functools.partial(gla_group, d, ob)]
        return _units_prepare_staggered(builders)

    _pipelined_chunks(n_chunks, prepare, functools.partial(_units_finish, st_ref=st_ref))


def _scan_ab(p3, u_pad, gkb):
    bsz, t, _ = p3.shape
    nb = t // SCAN_ROWS

    def fwd(j, w=512):
        return pl.BlockSpec((None, SCAN_ROWS, w), lambda b, n: (b, n, j))

    def bwd(j, w=512):
        return pl.BlockSpec((None, SCAN_ROWS, w), lambda b, n: (b, nb - 1 - n, j))

    def const(shape):
        return pl.BlockSpec(shape, lambda b, n: (0,) * len(shape))

    out = jax.ShapeDtypeStruct((bsz, t, MIX_HALF), BF16)
    o_f = pl.BlockSpec((None, SCAN_ROWS, MIX_HALF), lambda b, n: (b, n, 0))
    o_b = pl.BlockSpec((None, SCAN_ROWS, MIX_HALF), lambda b, n: (b, nb - 1 - n, 0))
    return pl.pallas_call(
        _scan_ab_kernel,
        out_shape=(out, out, out, out),
        grid=(bsz, nb),
        in_specs=[fwd(AB_QA), fwd(AB_FF), fwd(AB_IA), fwd(AB_QKB), fwd(AB_VB), fwd(AB_LR_BLOCK, LANES),
                  bwd(AB_QA), bwd(AB_FB), bwd(AB_IA), bwd(AB_QKB), bwd(AB_VB), bwd(AB_LR_BLOCK, LANES),
                  const(u_pad.shape), const(gkb.shape)],
        out_specs=(o_f, o_f, o_b, o_b),
        scratch_shapes=[pltpu.VMEM((4 * N_HEADS, HEAD_DV, LANES), F32)],
        compiler_params=_params(2),
        name="scan_ab",
    )(*([p3] * 12), u_pad, gkb)


def _rotary(a, cos, sin_signed):
    lane = lax.broadcasted_iota(jnp.int32, (1, LANES), 1)
    first_half = (lane % NARROW_DK) < (NARROW_DK // 2)
    swapped = jnp.where(first_half, pltpu.roll(a, LANES - NARROW_DK // 2, 1), pltpu.roll(a, NARROW_DK // 2, 1))
    return a * cos + swapped * sin_signed


def _scan_c_kernel(qk_f, v_f, qk_b, v_b, dmat_ref, qsc_ref, ksc_ref, cd_ref, o_f, o_b, st_ref):
    @pl.when(pl.program_id(1) == 0)
    def _():
        st_ref[...] = jnp.zeros_like(st_ref)

    n_chunks = SCAN_ROWS // CHUNK
    dirs = ((qk_f, v_f, o_f), (qk_b, v_b, o_b))

    def prepare(c):
        units = []
        for d, (qk_ref, v_ref, o_ref) in enumerate(dirs):
            rows = pl.ds((c if d == 0 else n_chunks - 1 - c) * CHUNK, CHUNK)
            v = v_ref[rows, :].astype(BF16)
            for h in range(N_HEADS):
                hm = _narrow_head_mask(h)
                q = jnp.where(hm, qk_ref[rows, (h // 2) * LANES:(h // 2 + 1) * LANES], 0.0)
                k = jnp.where(hm, qk_ref[rows, (2 + h // 2) * LANES:(3 + h // 2) * LANES], 0.0)
                vsl = slice(h * HEAD_DV, (h + 1) * HEAD_DV)

                def store(o, o_ref=o_ref, rows=rows, vsl=vsl):
                    o_ref[rows, vsl] = o

                units.append(dict(qe=q.astype(BF16), ke=k.astype(BF16), qi=(q * qsc_ref[d, h]).astype(BF16),
                                  ks=(k * ksc_ref[d, h]).astype(BF16), v=v[:, vsl], dec=cd_ref[d, h][0:1, :],
                                  intra=lambda s, d=d, h=h: s * dmat_ref[d, h], idx=d * N_HEADS + h, store=store))
        return _units_prepare(units)

    _pipelined_chunks(n_chunks, prepare, functools.partial(_units_finish, st_ref=st_ref))


def _retention_tables():
    pos = np.arange(CHUNK, dtype=np.float64)
    dmat = np.zeros((2, N_HEADS, CHUNK, CHUNK), np.float32)
    qsc = np.zeros((2, N_HEADS, CHUNK, LANES), np.float32)
    ksc = np.zeros((2, N_HEADS, CHUNK, LANES), np.float32)
    cd = np.zeros((2, N_HEADS, SUBLANES, LANES), np.float32)
    for d in range(2):
        for h in range(N_HEADS):
            lg = np.log1p(-np.exp2(-(5.0 + 2.0 * h + d)))
            dist = pos[:, None] - pos[None, :]
            if d == 0:
                dmat[d, h] = np.where(dist >= 0, np.exp(np.maximum(dist, 0.0) * lg), 0.0)
                qsc[d, h] = np.exp((pos + 1.0) * lg)[:, None]
                ksc[d, h] = np.exp((CHUNK - 1.0 - pos) * lg)[:, None]
            else:
                dmat[d, h] = np.where(dist <= 0, np.exp(np.maximum(-dist, 0.0) * lg), 0.0)
                qsc[d, h] = np.exp((CHUNK - pos) * lg)[:, None]
                ksc[d, h] = np.exp(pos * lg)[:, None]
            cd[d, h] = np.exp(CHUNK * lg)
    return jnp.asarray(dmat), jnp.asarray(qsc), jnp.asarray(ksc), jnp.asarray(cd)


def _rotary_tables(t):
    half = NARROW_DK // 2
    freqs = ROPE_THETA ** (-jnp.arange(half, dtype=F32) / half)
    ang = jnp.arange(t, dtype=F32)[:, None] * freqs[None, :]
    cos, sin = jnp.cos(ang), jnp.sin(ang)
    cos_t = jnp.tile(cos, (1, LANES // half))
    sin_t = jnp.tile(jnp.concatenate([-sin, sin], axis=1), (1, LANES // NARROW_DK))
    return cos_t.astype(F32), sin_t.astype(F32)


def _scan_c(p3):
    bsz, t, _ = p3.shape
    nb = t // SCAN_ROWS
    tables = _retention_tables()

    def fwd(j):
        return pl.BlockSpec((None, SCAN_ROWS, 512), lambda b, n: (b, n, j))

    def bwd(j):
        return pl.BlockSpec((None, SCAN_ROWS, 512), lambda b, n: (b, nb - 1 - n, j))

    def const(shape):
        return pl.BlockSpec(shape, lambda b, n: (0,) * len(shape))

    out = jax.ShapeDtypeStruct((bsz, t, MIX_HALF), BF16)
    return pl.pallas_call(
        _scan_c_kernel,
        out_shape=(out, out),
        grid=(bsz, nb),
        in_specs=[fwd(CD_QKC), fwd(CD_VC), bwd(CD_QKC), bwd(CD_VC)] + [const(a.shape) for a in tables],
        out_specs=(pl.BlockSpec((None, SCAN_ROWS, MIX_HALF), lambda b, n: (b, n, 0)),
                   pl.BlockSpec((None, SCAN_ROWS, MIX_HALF), lambda b, n: (b, nb - 1 - n, 0))),
        scratch_shapes=[pltpu.VMEM((2 * N_HEADS, HEAD_DV, LANES), F32)],
        compiler_params=_params(2),
        name="scan_c",
    )(p3, p3, p3, p3, *tables)


def _conv_kernel(prev_ref, cur_ref, next_ref, w_ref, b_ref, o_ref, buf_ref):
    i = pl.program_id(1)
    last = pl.num_programs(1) - 1
    rows = cur_ref.shape[0]
    buf_ref[0:SUBLANES, :] = jnp.where(i == 0, 0.0, prev_ref[...])
    buf_ref[SUBLANES:SUBLANES + rows, :] = cur_ref[...]
    buf_ref[SUBLANES + rows:2 * SUBLANES + rows, :] = jnp.where(i == last, 0.0, next_ref[...])
    y = (buf_ref[SUBLANES - 1:SUBLANES - 1 + rows, :] * w_ref[0:1, :]
         + buf_ref[SUBLANES:SUBLANES + rows, :] * w_ref[1:2, :]
         + buf_ref[SUBLANES + 1:SUBLANES + 1 + rows, :] * w_ref[2:3, :]
         + b_ref[...])
    o_ref[...] = _silu(y)


def _conv_silu(p3, conv_w, conv_b):
    bsz, t, _ = p3.shape
    tile = min(CONV_TILE, t)
    nb = t // tile
    per = tile // SUBLANES
    outs = []
    for j in (CD_QD, CD_KD):
        lo = (j - CD_QD) * 512
        outs.append(pl.pallas_call(
            _conv_kernel,
            out_shape=jax.ShapeDtypeStruct((bsz, t, 512), F32),
            grid=(bsz, nb),
            in_specs=[pl.BlockSpec((None, SUBLANES, 512), lambda b, i, j=j: (b, jnp.maximum(i * per - 1, 0), j)),
                      pl.BlockSpec((None, tile, 512), lambda b, i, j=j: (b, i, j)),
                      pl.BlockSpec((None, SUBLANES, 512),
                                   lambda b, i, j=j: (b, jnp.minimum((i + 1) * per, t // SUBLANES - 1), j)),
                      pl.BlockSpec((3, 512), lambda b, i: (0, 0)),
                      pl.BlockSpec((1, 512), lambda b, i: (0, 0))],
            out_specs=pl.BlockSpec((None, tile, 512), lambda b, i: (b, i, 0)),
            scratch_shapes=[pltpu.VMEM((tile + 2 * SUBLANES, 512), F32)],
            compiler_params=_params(2),
            name="conv_silu",
        )(p3, p3, p3, conv_w[:, lo:lo + 512], conv_b[None, lo:lo + 512]))
    return outs


def _gate_lane_layout():
    lane = np.arange(LANES)
    r = lane % GATE_LANES
    return lane // (N_HEADS * GATE_LANES), (lane // GATE_LANES) % N_HEADS, r // 3, r % 3


def _gate_const_rows():
    _, _, cls, _ = _gate_lane_layout()
    return jnp.asarray(np.concatenate([np.tile((cls == k + 1).astype(np.float32)[None, :], (LANES, 1))
                                       for k in range(3)], axis=0), BF16)


def _pieces(x, piece):
    hi = x.astype(BF16).astype(F32)
    r1 = x - hi
    mid = r1.astype(BF16).astype(F32)
    lo = (r1 - mid).astype(BF16).astype(F32)
    return jnp.where(piece == 0, hi, jnp.where(piece == 1, mid, lo))


def _running_max(x, direction):
    row = lax.broadcasted_iota(jnp.int32, x.shape, 0)
    step = 1
    while step < CHUNK:
        if direction == 0:
            shifted = jnp.where(row >= step, pltpu.roll(x, step, 0), -jnp.inf)
        else:
            shifted = jnp.where(row < CHUNK - step, pltpu.roll(x, CHUNK - step, 0), -jnp.inf)
        x = jnp.maximum(x, shifted)
        step *= 2
    return x


def _scan_d_kernel(q_f, k_f, v_f, gi_f, gf_f, q_b, k_b, v_b, gi_b, gf_b, fgb_ref, const_ref, o_f, o_b, c_ref, m_ref):
    @pl.when(pl.program_id(1) == 0)
    def _():
        c_ref[...] = jnp.zeros_like(c_ref)
        m_ref[...] = jnp.zeros_like(m_ref)

    n_chunks = SCAN_ROWS // CHUNK
    dirs = ((q_f, k_f, v_f, gi_f, gf_f, o_f), (q_b, k_b, v_b, gi_b, gf_b, o_b))
    lane = lax.broadcasted_iota(jnp.int32, (1, LANES), 1)
    lane_dir = lane // (N_HEADS * GATE_LANES)
    lane_head = (lane // GATE_LANES) % N_HEADS
    cls = (lane % GATE_LANES) // 3
    piece = (lane % GATE_LANES) % 3
    ones_block = jnp.ones((CHUNK, HEAD_DV), BF16)

    def prepare(c, m_state):
        pre = []
        for d, (q_ref, k_ref, v_ref, gi_ref, gf_ref, o_ref) in enumerate(dirs):
            rows = pl.ds((c if d == 0 else n_chunks - 1 - c) * CHUNK, CHUNK)
            pre.append((rows, gi_ref[rows, :], _log_sigmoid(gf_ref[rows, :] + fgb_ref[...])))
        bcs = _chunk_cumsum([(pre[0][2], 0), (pre[1][2], 1)])
        units, m_next = [], []
        for d, (q_ref, k_ref, v_ref, gi_ref, gf_ref, o_ref) in enumerate(dirs):
            rows, gi, _ = pre[d]
            b = bcs[d]
            last = _last_row(d)
            mask = _tri(d)
            m_st = m_state[d]
            b_last = b[last:last + 1, :]
            cj = gi - b
            run = _running_max(cj, d)
            m_i = jnp.maximum(b + m_st, b + run)
            u = b - m_i
            m_new = b_last + jnp.maximum(m_st, run[last:last + 1, :])
            s_old = jnp.exp(b_last + m_st - m_new)
            m_next.append(m_new)
            quantity =jnp.where(cls == 0, u, jnp.where(cls == 1, u + m_st, jnp.where(
                cls == 2, -m_i, cj + (b_last - m_new))))
            lhs_all = jnp.where(cls == 4, 1.0, jnp.where(cls < 4, _pieces(quantity, piece), 0.0))
            lhs = jnp.concatenate([jnp.where((lane_dir == d) & (lane_head == h), lhs_all, 0.0)
                                   for h in range(N_HEADS)], axis=0).astype(BF16)
            key_rows = jnp.where(cls == 4, _pieces(cj, piece), jnp.where(cls == 0, 1.0, 0.0)).astype(BF16)
            expo = _nt(lhs, jnp.concatenate([const_ref[...], key_rows], axis=0))
            for h in range(N_HEADS):
                sl = slice(h * HEAD_DV, (h + 1) * HEAD_DV)
                q = q_ref[rows, sl] * (HEAD_DV ** -0.5)
                k = k_ref[rows, sl]
                v_ext = jnp.concatenate([v_ref[rows, sl].astype(BF16), ones_block], axis=1)
                lane0 = (d * N_HEADS + h) * GATE_LANES
                units.append(dict(u=d * N_HEADS + h, rows=rows, sl=sl, o_ref=o_ref, q=q, k=k, v_ext=v_ext,
                                  mask=mask, expo=expo[h * CHUNK:(h + 1) * CHUNK, :],
                                  s_old=s_old[:, lane0:lane0 + 1], qk=_nt(q.astype(BF16), k.astype(BF16))))
        for t in units:
            e = t.pop("expo")
            w = t.pop("qk") * jnp.exp(jnp.where(t.pop("mask"), e[:, 3 * LANES:3 * LANES + CHUNK], -jnp.inf))
            t["lhs2"] = jnp.concatenate([(t.pop("q") * jnp.exp(e[:, :LANES])).astype(BF16), w.astype(BF16)], axis=1)
            t["floor"] = jnp.exp(e[:, LANES:2 * LANES])
            t["upd"] = _tn((t.pop("k") * jnp.exp(e[:, 2 * LANES:3 * LANES])).astype(BF16), t["v_ext"])
        return units, m_next

    def finish(units):
        nums = [_mm(t["lhs2"], jnp.concatenate([c_ref[t["u"]].astype(BF16), t["v_ext"]], axis=0)) for t in units]
        for t, num in zip(units, nums):
            den = jnp.maximum(jnp.abs(num[:, HEAD_DV:]), t["floor"])
            t["o_ref"][t["rows"], t["sl"]] = (num[:, :HEAD_DV] / den).astype(BF16)
            c_ref[t["u"]] = t["s_old"] * c_ref[t["u"]] + t["upd"]

    m_state = [m_ref[d][0:1, :] for d in range(2)]
    pending, m_state = prepare(0, m_state)
    for c in range(n_chunks):
        following = None
        if c + 1 < n_chunks:
            following, m_state = prepare(c + 1, m_state)
        finish(pending)
        pending = following
    for d in range(2):
        m_ref[d] = jnp.broadcast_to(m_state[d], (SUBLANES, LANES))


def _scan_d(qd, kd, p3, fgb_row):
    bsz, t, _ = p3.shape
    nb = t // SCAN_ROWS
    const_rows = _gate_const_rows()

    def spec(j, w, flip):
        if flip:
            return pl.BlockSpec((None, SCAN_ROWS, w), lambda b, n: (b, nb - 1 - n, j))
        return pl.BlockSpec((None, SCAN_ROWS, w), lambda b, n: (b, n, j))

    out = jax.ShapeDtypeStruct((bsz, t, MIX_HALF), BF16)
    in_specs = []
    for flip in (False, True):
        in_specs += [spec(0, 512, flip), spec(0, 512, flip), spec(CD_VD, 512, flip),
                     spec(CD_GATE_I_BLOCK, LANES, flip), spec(CD_GATE_F_BLOCK, LANES, flip)]
    in_specs += [pl.BlockSpec((1, LANES), lambda b, n: (0, 0)), pl.BlockSpec(const_rows.shape, lambda b, n: (0, 0))]
    return pl.pallas_call(
        _scan_d_kernel,
        out_shape=(out, out),
        grid=(bsz, nb),
        in_specs=in_specs,
        out_specs=(spec(0, MIX_HALF, False), spec(0, MIX_HALF, True)),
        scratch_shapes=[pltpu.VMEM((2 * N_HEADS, HEAD_DV, 2 * HEAD_DV), F32),
                        pltpu.VMEM((2, SUBLANES, LANES), F32)],
        compiler_params=_params(2),
        name="scan_d",
    )(qd, kd, p3, p3, p3, qd, kd, p3, p3, p3, fgb_row, const_rows)


def _layer_norm(z, g, b):
    zc = z - jnp.mean(z, axis=-1, keepdims=True)
    var = jnp.mean(zc * zc, axis=-1, keepdims=True)
    return zc * lax.rsqrt(var + LN_EPS) * g + b


def _mix_out_kernel(o1f, o1b, o2f, o2b, g1, g2, x_ref, norm_ref, w_ref, lng, lnb, h_ref, hb_ref):
    parts = []
    for of, ob, g_ref, base in ((o1f, o1b, g1, 0), (o2f, o2b, g2, MIX_HALF)):
        o = of[...].astype(F32) + ob[...].astype(F32)
        gate = g_ref[...]
        for h in range(N_HEADS):
            sl = slice(h * HEAD_DV, (h + 1) * HEAD_DV)
            oh = o[:, sl]
            oh = oh * lax.rsqrt(jnp.mean(oh * oh, axis=-1, keepdims=True) + NORM_EPS)
            parts.append((oh * norm_ref[:, base + h * HEAD_DV:base + (h + 1) * HEAD_DV] * gate[:, sl]).astype(BF16))
    mixed = jnp.concatenate(parts, axis=1)
    z = ALPHA * x_ref[...] + _mm(mixed, w_ref[...])
    hn = _layer_norm(z, lng[...], lnb[...])
    h_ref[...] = hn
    hb_ref[...] = hn.astype(BF16)


def _mix_out(o1f, o1b, o2f, o2b, p2d, gate_blocks, x2d, norm, w_out, ln_g, ln_b):
    n = x2d.shape[0]
    half = pl.BlockSpec((WIDE_TILE, MIX_HALF), lambda i: (i, 0))
    full = pl.BlockSpec((WIDE_TILE, D_MODEL), lambda i: (i, 0))
    row = pl.BlockSpec((1, D_MODEL), lambda i: (0, 0))
    ga, gb = gate_blocks
    return pl.pallas_call(
        _mix_out_kernel,
        out_shape=(jax.ShapeDtypeStruct((n, D_MODEL), F32), jax.ShapeDtypeStruct((n, D_MODEL), BF16)),
        grid=(n // WIDE_TILE,),
        in_specs=[half, half, half, half,
                  pl.BlockSpec((WIDE_TILE, MIX_HALF), lambda i: (i, ga)),
                  pl.BlockSpec((WIDE_TILE, MIX_HALF), lambda i: (i, gb)),
                  full, row, pl.BlockSpec((D_MODEL, D_MODEL), lambda i: (0, 0)), row, row],
        out_specs=(full, full),
        compiler_params=_params(1),
        name="mix_out",
    )(o1f, o1b, o2f, o2b, p2d, p2d, x2d, norm, w_out, ln_g, ln_b)


def _route(h, rw_ref, rb_ref, idx_ref, gate_ref):
    rw = rw_ref[...]
    rw_hi = rw.astype(BF16)
    rw_r = rw - rw_hi.astype(F32)
    rw_mid = rw_r.astype(BF16)
    rw3 = jnp.concatenate([rw_hi, rw_mid, (rw_r - rw_mid.astype(F32)).astype(BF16)], axis=0)
    h_hi = h.astype(BF16)
    h_lo = (h - h_hi.astype(F32)).astype(BF16)
    terms = _nt(rw3, h_hi) + _nt(rw3, h_lo)
    logits = terms[:N_EXPERTS] + terms[N_EXPERTS:2 * N_EXPERTS] + terms[2 * N_EXPERTS:]
    rows = [logits[e:e + 1, :] for e in range(N_EXPERTS)]
    mx = functools.reduce(jnp.maximum, rows)
    ex = [jnp.exp(r - mx) for r in rows]
    tot = functools.reduce(lambda a, b: a + b, ex)
    score = [e / tot for e in ex]
    biased = [score[e] + rb_ref[e:e + 1, :] for e in range(N_EXPERTS)]

    def argmax_first(vals):
        best, idx = vals[0], jnp.zeros(vals[0].shape, jnp.int32)
        for j in range(1, len(vals)):
            upd = vals[j] > best
            best = jnp.where(upd, vals[j], best)
            idx = jnp.where(upd, j, idx)
        return best, idx

    def pick(vals, idx):
        out = vals[0]
        for j in range(1, len(vals)):
            out = jnp.where(idx == j, vals[j], out)
        return out

    group_scores = []
    for g in range(N_GROUPS):
        a = biased[g * EXPERTS_PER_GROUP:(g + 1) * EXPERTS_PER_GROUP]
        pairs = [a[i] + a[j] for i in range(EXPERTS_PER_GROUP) for j in range(i + 1, EXPERTS_PER_GROUP)]
        group_scores.append(functools.reduce(jnp.maximum, pairs))
    _, g_sel = argmax_first(group_scores)
    in_b = [pick([biased[g * EXPERTS_PER_GROUP + k] for g in range(N_GROUPS)], g_sel) for k in range(EXPERTS_PER_GROUP)]
    in_s = [pick([score[g * EXPERTS_PER_GROUP + k] for g in range(N_GROUPS)], g_sel) for k in range(EXPERTS_PER_GROUP)]
    _, i0 = argmax_first(in_b)
    _, i1 = argmax_first([jnp.where(i0 == k, -jnp.inf, in_b[k]) for k in range(EXPERTS_PER_GROUP)])
    s0, s1 = pick(in_s, i0), pick(in_s, i1)
    den = s0 + s1
    idx_ref[0:1, :] = g_sel * EXPERTS_PER_GROUP + i0
    idx_ref[1:2, :] = g_sel * EXPERTS_PER_GROUP + i1
    gate_ref[0:1, :] = s0 / den
    gate_ref[1:2, :] = s1 / den


def _router_kernel(h_ref, rw_ref, rb_ref, idx_ref, gate_ref):
    _route(h_ref[...], rw_ref, rb_ref, idx_ref, gate_ref)


def _router(h2d, rw_t, rb_col):
    n = h2d.shape[0]
    tile = min(ROUTER_TILE, n)
    routed = pl.BlockSpec((TOP_K, tile), lambda i: (0, i))
    return pl.pallas_call(
        _router_kernel,
        out_shape=(jax.ShapeDtypeStruct((TOP_K, n), jnp.int32), jax.ShapeDtypeStruct((TOP_K, n), F32)),
        grid=(n // tile,),
        in_specs=[pl.BlockSpec((tile, D_MODEL), lambda i: (i, 0)),
                  pl.BlockSpec(rw_t.shape, lambda i: (0, 0)),
                  pl.BlockSpec((N_EXPERTS, 1), lambda i: (0, 0))],
        out_specs=(routed, routed),
        compiler_params=_params(1),
        name="router",
    )(h2d, rw_t, rb_col)


def _plan_kernel(idx_ref, dest_ref, cnt_ref, su_ref, run_ref, start_ref):
    phase, i = pl.program_id(0), pl.program_id(1)
    tile = idx_ref.shape[1]
    e_iota = lax.broadcasted_iota(jnp.int32, (N_EXPERTS, tile), 0)
    oh0 = idx_ref[0:1, :] == e_iota
    oh1 = idx_ref[1:2, :] == e_iota
    member = jnp.where(oh0 | oh1, 1.0, 0.0)
    tile_count = jnp.broadcast_to(jnp.sum(member, axis=1, keepdims=True), (N_EXPERTS, LANES))

    @pl.when((phase == 0) & (i == 0))
    def _():
        run_ref[...] = jnp.zeros_like(run_ref)
        r = lax.broadcasted_iota(jnp.int32, (tile, tile), 0)
        c = lax.broadcasted_iota(jnp.int32, (tile, tile), 1)
        su_ref[...] = jnp.where(r < c, 1.0, 0.0).astype(BF16)

    @pl.when((phase == 1) & (i == 0))
    def _():
        counts = run_ref[...]
        cnt_ref[...] = counts.astype(jnp.int32)
        padded = jnp.ceil(counts * (1.0 / EXPERT_ROWS)) * EXPERT_ROWS
        r = lax.broadcasted_iota(jnp.int32, (N_EXPERTS, N_EXPERTS), 0)
        c = lax.broadcasted_iota(jnp.int32, (N_EXPERTS, N_EXPERTS), 1)
        start_ref[...] = jnp.dot(jnp.where(c < r, 1.0, 0.0), padded, precision=HIGHEST, preferred_element_type=F32)
        run_ref[...] = jnp.zeros_like(run_ref)

    @pl.when(phase == 1)
    def _():
        before = _mm(member.astype(BF16), su_ref[...])
        pos = start_ref[:, 0:1] + run_ref[:, 0:1] + before
        dest_ref[0:1, :] = jnp.sum(jnp.where(oh0, pos, 0.0), axis=0, keepdims=True).astype(jnp.int32)
        dest_ref[1:2, :] = jnp.sum(jnp.where(oh1, pos, 0.0), axis=0, keepdims=True).astype(jnp.int32)

    run_ref[...] += tile_count


def _plan(idx):
    n = idx.shape[1]
    tile = min(PLAN_TILE, n)
    return pl.pallas_call(
        _plan_kernel,
        out_shape=(jax.ShapeDtypeStruct((TOP_K, n), jnp.int32), jax.ShapeDtypeStruct((N_EXPERTS, LANES), jnp.int32)),
        grid=(2, n // tile),
        in_specs=[pl.BlockSpec((TOP_K, tile), lambda p, i: (0, i))],
        out_specs=(pl.BlockSpec((TOP_K, tile), lambda p, i: (0, i * p)),
                   pl.BlockSpec((N_EXPERTS, LANES), lambda p, i: (0, 0))),
        scratch_shapes=[pltpu.VMEM((tile, tile), BF16), pltpu.VMEM((N_EXPERTS, LANES), F32),
                        pltpu.VMEM((N_EXPERTS, LANES), F32)],
        compiler_params=_params(2),
        name="moe_plan",
    )(idx)


def _invert_kernel(pad_ref, dest_ref, tok_ref):
    n = dest_ref.shape[0] // TOP_K
    n_rows = tok_ref.shape[0]

    def filler(r):
        return r - jnp.where(r >= 2 * n, 2 * n, jnp.where(r >= n, n, 0))

    for e in range(N_EXPERTS):
        def clear(i, carry, lo=pad_ref[0, e], hi=pad_ref[1, e]):
            r = jnp.minimum(jnp.maximum(hi - 1 - i, lo), n_rows - 1)
            tok_ref[r] = filler(r)
            return carry

        lax.fori_loop(0, EXPERT_ROWS, clear, 0, unroll=16)

    def clear_tail(r, carry):
        tok_ref[r] = filler(r)
        return carry

    lax.fori_loop(pad_ref[0, N_EXPERTS], pad_ref[1, N_EXPERTS], clear_tail, 0)

    def place(t, carry):
        for k in range(TOP_K):
            tok_ref[dest_ref[k * n + t]] = t
        return carry

    lax.fori_loop(0, n, place, 0, unroll=16)


def _invert(pad_ranges, dest_flat, n_rows):
    smem = pl.BlockSpec(memory_space=pltpu.SMEM)
    return pl.pallas_call(
        _invert_kernel,
        out_shape=jax.ShapeDtypeStruct((n_rows,), jnp.int32),
        in_specs=[smem, smem],
        out_specs=smem,
        name="moe_invert",
    )(pad_ranges, dest_flat)


def _expert_kernel(be_ref, nv_ref, x_ref, w1_ref, w3_ref, w2_ref, y_ref, wb_ref):
    i = pl.program_id(0)
    valid = i < nv_ref[0]
    new_expert = (i == 0) | (be_ref[i] != be_ref[jnp.maximum(i - 1, 0)])

    @pl.when(valid & new_expert)
    def _():
        wb_ref[0] = w1_ref[...].astype(BF16)
        wb_ref[1] = w3_ref[...].astype(BF16)
        wb_ref[2] = w2_ref[...].astype(BF16)

    @pl.when(valid)
    def _():
        x = x_ref[...]
        hid = _silu(_mm(x, wb_ref[0])) * _mm(x, wb_ref[1])
        y_ref[...] = _mm(hid.astype(BF16), wb_ref[2]).astype(BF16)

    @pl.when(jnp.logical_not(valid))
    def _():
        y_ref[...] = jnp.zeros_like(y_ref)


def _experts(block_expert, n_valid, xs, w1, w3, w2, layer):
    n_rows = xs.shape[0]
    wspec = pl.BlockSpec((None, None, D_MODEL, D_MODEL), lambda i, be, nv: (layer, be[i], 0, 0))
    return pl.pallas_call(
        _expert_kernel,
        out_shape=jax.ShapeDtypeStruct((n_rows, D_MODEL), BF16),
        grid_spec=pltpu.PrefetchScalarGridSpec(
            num_scalar_prefetch=2,
            grid=(n_rows // EXPERT_ROWS,),
            in_specs=[pl.BlockSpec((EXPERT_ROWS, D_MODEL), lambda i, be, nv: (i, 0)), wspec, wspec, wspec],
            out_specs=pl.BlockSpec((EXPERT_ROWS, D_MODEL), lambda i, be, nv: (i, 0)),
            scratch_shapes=[pltpu.VMEM((3, D_MODEL, D_MODEL), BF16)]),
        compiler_params=_params(1),
        name="experts",
    )(block_expert, n_valid, xs, w1, w3, w2)


def _combine_kernel(h_ref, y0_ref, y1_ref, g_ref, lng, lnb, o_ref):
    g = g_ref[...]
    z = ALPHA * h_ref[...] + (g[:, 0:1] * y0_ref[...].astype(F32) + g[:, 1:2] * y1_ref[...].astype(F32))
    o_ref[...] = _layer_norm(z, lng[...], lnb[...])


def _combine(h2d, y_rows, gates_t, ln_g, ln_b):
    n = h2d.shape[0]
    nt = n // WIDE_TILE
    full = pl.BlockSpec((WIDE_TILE, D_MODEL), lambda i: (i, 0))
    row = pl.BlockSpec((1, D_MODEL), lambda i: (0, 0))
    return pl.pallas_call(
        _combine_kernel,
        out_shape=jax.ShapeDtypeStruct((n, D_MODEL), F32),
        grid=(nt,),
        in_specs=[full, full, pl.BlockSpec((WIDE_TILE, D_MODEL), lambda i: (nt + i, 0)),
                  pl.BlockSpec((WIDE_TILE, TOP_K), lambda i: (i, 0)), row, row],
        out_specs=full,
        compiler_params=_params(1),
        name="moe_combine",
    )(h2d, y_rows, y_rows, gates_t, ln_g, ln_b)


def _moe(h2d, h_bf16, idx, gates, w1, w3, w2, layer, ln_g, ln_b):
    n = h2d.shape[0]
    dest, counts = _plan(idx)
    n_rows = n * TOP_K + N_EXPERTS * EXPERT_ROWS
    n_blocks = n_rows // EXPERT_ROWS
    counts = counts[:, 0]
    padded = (counts + EXPERT_ROWS - 1) // EXPERT_ROWS * EXPERT_ROWS
    pad_end = jnp.cumsum(padded)
    block_start = jnp.arange(n_blocks, dtype=jnp.int32) * EXPERT_ROWS
    block_expert = jnp.minimum(jnp.sum((pad_end[None, :] <= block_start[:, None]).astype(jnp.int32), axis=1),
                               N_EXPERTS - 1)
    n_valid = (pad_end[-1:] // EXPERT_ROWS).astype(jnp.int32)
    pad_ranges = jnp.stack([jnp.concatenate([pad_end - padded + counts, pad_end[-1:]]),
                            jnp.concatenate([pad_end, jnp.full((1,), n_rows, jnp.int32)])]).astype(jnp.int32)
    dest_flat = dest.reshape(-1)
    xs = h_bf16[_invert(pad_ranges, dest_flat, n_rows)]
    y = _experts(block_expert, n_valid, xs, w1, w3, w2, layer)
    return _combine(h2d, y[dest_flat], gates.T, ln_g, ln_b)


def _pad_cols(w, b, main, width):
    k = w.shape[0]
    tail = w.shape[1] - main
    w_p = jnp.concatenate([w[:, :main], w[:, main:], jnp.zeros((k, width - main - tail), w.dtype)], axis=1)
    b_p = jnp.concatenate([b[:main], b[main:], jnp.zeros((width - main - tail,), b.dtype)])
    return w_p.astype(BF16), b_p[None, :].astype(F32)


def kernel(x, ab_w_in, ab_b_in, hgrn_lb, gla_gk_up, gla_gk_b, hgrn_norm, gla_norm, ab_w_out, cd_w_in, cd_b_in,
           mlstm_conv_w, mlstm_conv_b, mlstm_fgate_b, ret_norm, mlstm_norm, cd_w_out, ln_mix_g, ln_mix_b, ln_ffn_g,
           ln_ffn_b, router_w, router_b, moe_w1, moe_w3, moe_w2):
    bsz, t, d = x.shape
    n = bsz * t
    assert d == D_MODEL and t % SCAN_ROWS == 0 and t % min(CONV_TILE, t) == 0, (bsz, t, d)
    assert n % ROW_TILE == 0 and n % WIDE_TILE == 0 and n % min(ROUTER_TILE, n) == 0 and n % min(PLAN_TILE, n) == 0
    lower_bounds = jnp.cumsum(jax.nn.softmax(hgrn_lb.astype(F32), axis=0), axis=0)
    rw_t = router_w.T.astype(F32)
    rb_col = router_b.astype(F32)[:, None]
    h = x.reshape(n, d)
    for layer in range(DEPTH):
        j = layer // 2
        if layer % 2 == 0:
            w_p, b_p = _pad_cols(ab_w_in[j], ab_b_in[j], 8 * 512, AB_WIDTH)
            lb = lower_bounds[layer][None, :]
            p = _in_proj(h, w_p, b_p, _proj_ab_kernel, [lb], [pl.BlockSpec(lb.shape, lambda i: (0, 0))])
            p3 = p.reshape(bsz, t, AB_WIDTH)
            nk = N_HEADS * NARROW_DK
            u_pad = jnp.zeros((LANES, 2 * nk), F32)
            u_pad = u_pad.at[:GLA_LOWRANK, :nk].set(gla_gk_up[j, 0])
            u_pad = u_pad.at[GLA_LOWRANK:2 * GLA_LOWRANK, nk:].set(gla_gk_up[j, 1]).astype(BF16)
            gkb = jnp.concatenate([gla_gk_b[j, 0], gla_gk_b[j, 1]])[None, :].astype(F32)
            oa_f, ob_f, oa_b, ob_b = _scan_ab(p3, u_pad, gkb)
            norm = jnp.concatenate([hgrn_norm[j], gla_norm[j]])[None, :].astype(F32)
            outs = [a.reshape(n, MIX_HALF) for a in (oa_f, oa_b, ob_f, ob_b)]
            h, h_bf16 = _mix_out(*outs, p, (AB_GA, AB_GB), h, norm, ab_w_out[j].astype(BF16),
                                 ln_mix_g[layer][None, :], ln_mix_b[layer][None, :])
        else:
            lane_dir, lane_head, _, _ = _gate_lane_layout()
            col_i = 7 * 512 + lane_dir * 2 * N_HEADS + lane_head
            cols = np.concatenate([np.arange(7 * 512), col_i, col_i + N_HEADS])
            w_p, b_p = cd_w_in[j][:, cols].astype(BF16), cd_b_in[j][cols][None, :].astype(F32)
            pos = pl.BlockSpec((ROW_TILE, LANES), lambda i: (i % (t // ROW_TILE), 0))
            p = _in_proj(h, w_p, b_p, _proj_cd_kernel, list(_rotary_tables(t)), [pos, pos])
            p3 = p.reshape(bsz, t, CD_WIDTH)
            oc_f, oc_b = _scan_c(p3)
            qd, kd = _conv_silu(p3, mlstm_conv_w[j].astype(F32), mlstm_conv_b[j].astype(F32))
            fgb_row = mlstm_fgate_b[j].astype(F32)[lane_dir, lane_head][None, :]
            od_f, od_b = _scan_d(qd, kd, p3, fgb_row)
            norm = jnp.concatenate([ret_norm[j], mlstm_norm[j]])[None, :].astype(F32)
            outs = [a.reshape(n, MIX_HALF) for a in (oc_f, oc_b, od_f, od_b)]
            h, h_bf16 = _mix_out(*outs, p, (CD_GC, CD_OD), h, norm, cd_w_out[j].astype(BF16),
                                 ln_mix_g[layer][None, :], ln_mix_b[layer][None, :])
        idx, gates = _router(h, rw_t, rb_col)
        h = _moe(h, h_bf16, idx, gates, moe_w1, moe_w3, moe_w2, layer,
                 ln_ffn_g[layer][None, :], ln_ffn_b[layer][None, :])
    return h.reshape(bsz, t, d)
```

```python
import functools

import jax
import jax.numpy as jnp
import numpy as np
from jax import lax
from jax.experimental import pallas as pl
from jax.experimental.pallas import tpu as pltpu

F32 = jnp.float32
BF16 = jnp.bfloat16
HIGHEST = lax.Precision.HIGHEST

D_MODEL = 1024
DEPTH = 2
CHUNK = 64
MIX_HALF = D_MODEL // 2
N_HEADS = 4
HEAD_DV = MIX_HALF // N_HEADS
NARROW_DK = 64
GLA_LOWRANK = 16
GLA_GATE_NORMALIZER = 16.0
ROPE_THETA = 10000.0
N_EXPERTS = 16
N_GROUPS = 4
EXPERTS_PER_GROUP = N_EXPERTS // N_GROUPS
TOP_K = 2
ALPHA = (2.0 * DEPTH) ** 0.25
LN_EPS = 1e-5
NORM_EPS = 1e-6

LANES = 128
SUBLANES = 8
VMEM_LIMIT = 52 * 1024 * 1024

ROW_TILE = 512
WIDE_TILE = 512
CONV_TILE = 1024
SCAN_ROWS = 512
ROUTER_TILE = 2048
PLAN_TILE = 1024
INVERT_SPAN = 8192
EXPERT_ROWS = 512

AB_QA, AB_FF, AB_FB, AB_IA, AB_GA, AB_QKB, AB_VB, AB_GB = range(8)
AB_LR_BLOCK = 32
AB_WIDTH = 8 * 512 + LANES
CD_QKC, CD_VC, CD_GC, CD_QD, CD_KD, CD_VD, CD_OD = range(7)
CD_GATE_I_BLOCK = 28
CD_GATE_F_BLOCK = 29
CD_WIDTH = 7 * 512 + 2 * LANES
GATE_LANES = 16


def _params(n_axes, vmem=VMEM_LIMIT):
    return pltpu.CompilerParams(dimension_semantics=("arbitrary",) * n_axes, vmem_limit_bytes=vmem)


def _nt(a, b):
    return lax.dot_general(a, b, (((1,), (1,)), ((), ())), preferred_element_type=F32)


def _tn(a, b):
    return lax.dot_general(a, b, (((0,), (0,)), ((), ())), preferred_element_type=F32)


def _mm(a, b):
    return jnp.dot(a, b, preferred_element_type=F32)


def _sigmoid(x):
    return 1.0 / (1.0 + jnp.exp(-x))


def _silu(x):
    return x * _sigmoid(x)


def _log_sigmoid(x):
    return jnp.minimum(x, 0.0) - jnp.log(1.0 + jnp.exp(-jnp.abs(x)))


def _tri(direction):
    row = lax.broadcasted_iota(jnp.int32, (CHUNK, CHUNK), 0)
    col = lax.broadcasted_iota(jnp.int32, (CHUNK, CHUNK), 1)
    return (col <= row) if direction == 0 else (col >= row)


def _ref_row(direction):
    return CHUNK // 2 - 1 if direction == 0 else CHUNK // 2


def _last_row(direction):
    return CHUNK - 1 if direction == 0 else 0


def _proj_ab_kernel(x_ref, w_ref, b_ref, lb_ref, o_ref):
    p = _mm(x_ref[...].astype(BF16), w_ref[...]) + b_ref[...]
    o_ref[:, :512] = _silu(p[:, :512])
    lb = jnp.concatenate([lb_ref[...], lb_ref[...]], axis=1)
    o_ref[:, 512:1536] = lb + (1.0 - lb) * _sigmoid(p[:, 512:1536])
    o_ref[:, 1536:] = p[:, 1536:]
    for blk in (AB_GA, AB_GB):
        o_ref[:, blk * 512:(blk + 1) * 512] = _silu(p[:, blk * 512:(blk + 1) * 512])


def _proj_cd_kernel(x_ref, w_ref, b_ref, cos_ref, sin_ref, o_ref):
    p = _mm(x_ref[...].astype(BF16), w_ref[...]) + b_ref[...]
    cos, sin_signed = cos_ref[...], sin_ref[...]
    for j in range(4):
        rot = _rotary(p[:, j * LANES:(j + 1) * LANES], cos, sin_signed)
        o_ref[:, j * LANES:(j + 1) * LANES] = rot * (NARROW_DK ** -0.5) if j < 2 else rot
    o_ref[:, 512:] = p[:, 512:]
    o_ref[:, CD_GC * 512:(CD_GC + 1) * 512] = _silu(p[:, CD_GC * 512:(CD_GC + 1) * 512])
    o_ref[:, CD_OD * 512:(CD_OD + 1) * 512] = _sigmoid(p[:, CD_OD * 512:(CD_OD + 1) * 512])


def _in_proj(x2d, w, b, body, extras, extra_specs):
    n, k = x2d.shape
    m = w.shape[1]
    return pl.pallas_call(
        body,
        out_shape=jax.ShapeDtypeStruct((n, m), F32),
        grid=(n // ROW_TILE,),
        in_specs=[pl.BlockSpec((ROW_TILE, k), lambda i: (i, 0)),
                  pl.BlockSpec((k, m), lambda i: (0, 0)),
                  pl.BlockSpec((1, m), lambda i: (0, 0))] + extra_specs,
        out_specs=pl.BlockSpec((ROW_TILE, m), lambda i: (i, 0)),
        compiler_params=_params(1),
        name="in_proj",
    )(x2d, w, b, *extras)


def _pipelined_chunks(n_chunks, prepare, finish):
    pending = prepare(0)
    for c in range(n_chunks):
        following = prepare(c + 1) if c + 1 < n_chunks else None
        finish(pending)
        pending = following


def _units_prepare(units):
    return _units_prepare_staggered([lambda: units])


def _units_prepare_staggered(builders):
    def second(group):
        for u in group:
            u["lhs"] = jnp.concatenate([u["qi"], u["intra"](u.pop("scores")).astype(BF16)], axis=1)

    done, previous = [], None
    for build in builders:
        group = build()
        for u in group:
            u["scores"] = _nt(u["qe"], u["ke"])
        for u in group:
            u["v_t"] = u["v"].T
            u["upd"] = _mm(u["v_t"], u["ks"])
        if previous is not None:
            second(previous)
        previous = group
        done += group
    second(previous)
    return done


def _units_finish(units, st_ref, group=4):
    def close(batch):
        for u in batch:
            u["store"](u.pop("o").astype(BF16))
            st_ref[u["idx"]] = st_ref[u["idx"]] * u["dec"] + u["upd"]

    previous = None
    for g in range(0, len(units), group):
        batch = units[g:g + group]
        for u in batch:
            u["o"] = _nt(u["lhs"], jnp.concatenate([st_ref[u["idx"]].astype(BF16), u["v_t"]], axis=1))
        if previous is not None:
            close(previous)
        previous = batch
    close(previous)


def _chunk_cumsum(parts):
    cat = jnp.concatenate([x for x, _ in parts], axis=1)
    hi = cat.astype(BF16)
    lo = (cat - hi.astype(F32)).astype(BF16)
    width = cat.shape[1]
    inc = _mm(_tri(0).astype(BF16), jnp.concatenate([hi, lo], axis=1))
    inc = inc[:, :width] + inc[:, width:]
    out, off = [], 0
    for x, direction in parts:
        p = inc[:, off:off + x.shape[1]]
        off += x.shape[1]
        out.append(p if direction == 0 else p[CHUNK - 1:CHUNK, :] - p + x)
    return out


def _gated_operands(q, k, gc, direction):
    r, l = _ref_row(direction), _last_row(direction)
    g_ref = gc[r:r + 1, :]
    g_last = gc[l:l + 1, :]
    qe = q * jnp.exp(gc - g_ref)
    ke = k * jnp.exp(g_ref - gc)
    qi = qe * jnp.exp(g_ref)
    ks = ke * jnp.exp(g_last - g_ref)
    return [a.astype(BF16) for a in (qe, ke, qi, ks)], jnp.exp(g_last)


def _narrow_head_mask(h):
    lane = lax.broadcasted_iota(jnp.int32, (1, LANES), 1)
    return (lane // NARROW_DK) == (h % 2)


def _scan_ab_kernel(qa_f, fa_f, ia_f, qkb_f, vb_f, lr_f, qa_b, fa_b, ia_b, qkb_b, vb_b, lr_b,
                    u_ref, gkb_ref, oa_f, ob_f, oa_b, ob_b, st_ref):
    @pl.when(pl.program_id(1) == 0)
    def _():
        st_ref[...] = jnp.zeros_like(st_ref)

    n_chunks = SCAN_ROWS // CHUNK
    dirs = ((qa_f, fa_f, ia_f, qkb_f, vb_f, lr_f, oa_f, ob_f), (qa_b, fa_b, ia_b, qkb_b, vb_b, lr_b, oa_b, ob_b))

    nk = N_HEADS * NARROW_DK

    def prepare(c):
        pre = []
        for d, (qa, fa, ia, qkb, vb, lr, oa, ob) in enumerate(dirs):
            rows = pl.ds((c if d == 0 else n_chunks - 1 - c) * CHUNK, CHUNK)
            f = fa[rows, :]
            logits = _mm(lr[rows, :].astype(BF16), u_ref[:, d * nk:(d + 1) * nk]) + gkb_ref[:, d * nk:(d + 1) * nk]
            qk = qkb[rows, :]
            pre.append(dict(rows=rows, f=f, lf_a=jnp.log(f), lf_b=_log_sigmoid(logits) / GLA_GATE_NORMALIZER,
                            q_a=qa[rows, :], v_a=ia[rows, :].astype(BF16),
                            q_b=qk[:, :nk] * (NARROW_DK ** -0.5), k_b=qk[:, nk:], v_b=vb[rows, :].astype(BF16)))
        gcs = _chunk_cumsum([(pre[0]["lf_a"], 0), (pre[0]["lf_b"], 0), (pre[1]["lf_a"], 1), (pre[1]["lf_b"], 1)])
        def hgrn_group(d, oa):
            p, mask = pre[d], _tri(d)
            rows = p["rows"]
            (qe, ke, qi, ks), dec = _gated_operands(p["q_a"], 1.0 - p["f"], gcs[2 * d], d)
            group = []
            for h in range(N_HEADS):
                sl = slice(h * LANES, (h + 1) * LANES)

                def store(o, sl=sl):
                    oa[rows, sl] = o

                group.append(dict(qe=qe[:, sl], ke=ke[:, sl], qi=qi[:, sl], ks=ks[:, sl], v=p["v_a"][:, sl],
                                  dec=dec[:, sl], intra=lambda s: jnp.where(mask, s, 0.0),
                                  idx=d * 2 * N_HEADS + h, store=store))
            return group

        def gla_group(d, ob):
            p, mask = pre[d], _tri(d)
            rows = p["rows"]
            (qe, ke, qi, ks), dec = _gated_operands(p["q_b"], p["k_b"], gcs[2 * d + 1], d)
            group = []
            for h in range(N_HEADS):
                psl = slice((h // 2) * LANES, (h // 2 + 1) * LANES)
                hm = _narrow_head_mask(h)
                pick = lambda a, psl=psl, hm=hm: jnp.where(hm, a[:, psl], jnp.zeros((), BF16))
                vsl = slice(h * HEAD_DV, (h + 1) * HEAD_DV)

                def store(o, vsl=vsl):
                    ob[rows, vsl] = o

                group.append(dict(qe=pick(qe), ke=pick(ke), qi=pick(qi), ks=pick(ks), v=p["v_b"][:, vsl],
                                  dec=dec[:, psl], intra=lambda s: jnp.where(mask, s, 0.0),
                                  idx=d * 2 * N_HEADS + N_HEADS + h, store=store))
            return group

        builders = []
        for d, (qa, fa, ia, qkb, vb, lr, oa, ob) in enumerate(dirs):
            builders += [functools.partial(hgrn_group, d, oa), functools.partial(gla_group, d, ob)]
        return _units_prepare_staggered(builders)

    _pipelined_chunks(n_chunks, prepare, functools.partial(_units_finish, st_ref=st_ref))


def _scan_ab(p3, u_pad, gkb):
    bsz, t, _ = p3.shape
    nb = t // SCAN_ROWS

    def fwd(j, w=512):
        return pl.BlockSpec((None, SCAN_ROWS, w), lambda b, n: (b, n, j))

    def bwd(j, w=512):
        return pl.BlockSpec((None, SCAN_ROWS, w), lambda b, n: (b, nb - 1 - n, j))

    def const(shape):
        return pl.BlockSpec(shape, lambda b, n: (0,) * len(shape))

    out = jax.ShapeDtypeStruct((bsz, t, MIX_HALF), BF16)
    o_f = pl.BlockSpec((None, SCAN_ROWS, MIX_HALF), lambda b, n: (b, n, 0))
    o_b = pl.BlockSpec((None, SCAN_ROWS, MIX_HALF), lambda b, n: (b, nb - 1 - n, 0))
    return pl.pallas_call(
        _scan_ab_kernel,
        out_shape=(out, out, out, out),
        grid=(bsz, nb),
        in_specs=[fwd(AB_QA), fwd(AB_FF), fwd(AB_IA), fwd(AB_QKB), fwd(AB_VB), fwd(AB_LR_BLOCK, LANES),
                  bwd(AB_QA), bwd(AB_FB), bwd(AB_IA), bwd(AB_QKB), bwd(AB_VB), bwd(AB_LR_BLOCK, LANES),
                  const(u_pad.shape), const(gkb.shape)],
        out_specs=(o_f, o_f, o_b, o_b),
        scratch_shapes=[pltpu.VMEM((4 * N_HEADS, HEAD_DV, LANES), F32)],
        compiler_params=_params(2),
        name="scan_ab",
    )(*([p3] * 12), u_pad, gkb)


def _rotary(a, cos, sin_signed):
    lane = lax.broadcasted_iota(jnp.int32, (1, LANES), 1)
    first_half = (lane % NARROW_DK) < (NARROW_DK // 2)
    swapped = jnp.where(first_half, pltpu.roll(a, LANES - NARROW_DK // 2, 1), pltpu.roll(a, NARROW_DK // 2, 1))
    return a * cos + swapped * sin_signed


def _scan_c_kernel(qk_f, v_f, qk_b, v_b, dmat_ref, qsc_ref, ksc_ref, cd_ref, o_f, o_b, st_ref):
    @pl.when(pl.program_id(1) == 0)
    def _():
        st_ref[...] = jnp.zeros_like(st_ref)

    n_chunks = SCAN_ROWS // CHUNK
    dirs = ((qk_f, v_f, o_f), (qk_b, v_b, o_b))

    def prepare(c):
        units = []
        for d, (qk_ref, v_ref, o_ref) in enumerate(dirs):
            rows = pl.ds((c if d == 0 else n_chunks - 1 - c) * CHUNK, CHUNK)
            v = v_ref[rows, :].astype(BF16)
            for h in range(N_HEADS):
                hm = _narrow_head_mask(h)
                q = jnp.where(hm, qk_ref[rows, (h // 2) * LANES:(h // 2 + 1) * LANES], 0.0)
                k = jnp.where(hm, qk_ref[rows, (2 + h // 2) * LANES:(3 + h // 2) * LANES], 0.0)
                vsl = slice(h * HEAD_DV, (h + 1) * HEAD_DV)

                def store(o, o_ref=o_ref, rows=rows, vsl=vsl):
                    o_ref[rows, vsl] = o

                units.append(dict(qe=q.astype(BF16), ke=k.astype(BF16), qi=(q * qsc_ref[d, h]).astype(BF16),
                                  ks=(k * ksc_ref[d, h]).astype(BF16), v=v[:, vsl], dec=cd_ref[d, h][0:1, :],
                                  intra=lambda s, d=d, h=h: s * dmat_ref[d, h], idx=d * N_HEADS + h, store=store))
        return _units_prepare(units)

    _pipelined_chunks(n_chunks, prepare, functools.partial(_units_finish, st_ref=st_ref))


def _retention_tables():
    pos = np.arange(CHUNK, dtype=np.float64)
    dmat = np.zeros((2, N_HEADS, CHUNK, CHUNK), np.float32)
    qsc = np.zeros((2, N_HEADS, CHUNK, LANES), np.float32)
    ksc = np.zeros((2, N_HEADS, CHUNK, LANES), np.float32)
    cd = np.zeros((2, N_HEADS, SUBLANES, LANES), np.float32)
    for d in range(2):
        for h in range(N_HEADS):
            lg = np.log1p(-np.exp2(-(5.0 + 2.0 * h + d)))
            dist = pos[:, None] - pos[None, :]
            if d == 0:
                dmat[d, h] = np.where(dist >= 0, np.exp(np.maximum(dist, 0.0) * lg), 0.0)
                qsc[d, h] = np.exp((pos + 1.0) * lg)[:, None]
                ksc[d, h] = np.exp((CHUNK - 1.0 - pos) * lg)[:, None]
            else:
                dmat[d, h] = np.where(dist <= 0, np.exp(np.maximum(-dist, 0.0) * lg), 0.0)
                qsc[d, h] = np.exp((CHUNK - pos) * lg)[:, None]
                ksc[d, h] = np.exp(pos * lg)[:, None]
            cd[d, h] = np.exp(CHUNK * lg)
    return jnp.asarray(dmat), jnp.asarray(qsc), jnp.asarray(ksc), jnp.asarray(cd)


def _rotary_tables(t):
    half = NARROW_DK // 2
    freqs = ROPE_THETA ** (-jnp.arange(half, dtype=F32) / half)
    ang = jnp.arange(t, dtype=F32)[:, None] * freqs[None, :]
    cos, sin = jnp.cos(ang), jnp.sin(ang)
    cos_t = jnp.tile(cos, (1, LANES // half))
    sin_t = jnp.tile(jnp.concatenate([-sin, sin], axis=1), (1, LANES // NARROW_DK))
    return cos_t.astype(F32), sin_t.astype(F32)


def _scan_c(p3):
    bsz, t, _ = p3.shape
    nb = t // SCAN_ROWS
    tables = _retention_tables()

    def fwd(j):
        return pl.BlockSpec((None, SCAN_ROWS, 512), lambda b, n: (b, n, j))

    def bwd(j):
        return pl.BlockSpec((None, SCAN_ROWS, 512), lambda b, n: (b, nb - 1 - n, j))

    def const(shape):
        return pl.BlockSpec(shape, lambda b, n: (0,) * len(shape))

    out = jax.ShapeDtypeStruct((bsz, t, MIX_HALF), BF16)
    return pl.pallas_call(
        _scan_c_kernel,
        out_shape=(out, out),
        grid=(bsz, nb),
        in_specs=[fwd(CD_QKC), fwd(CD_VC), bwd(CD_QKC), bwd(CD_VC)] + [const(a.shape) for a in tables],
        out_specs=(pl.BlockSpec((None, SCAN_ROWS, MIX_HALF), lambda b, n: (b, n, 0)),
                   pl.BlockSpec((None, SCAN_ROWS, MIX_HALF), lambda b, n: (b, nb - 1 - n, 0))),
        scratch_shapes=[pltpu.VMEM((2 * N_HEADS, HEAD_DV, LANES), F32)],
        compiler_params=_params(2),
        name="scan_c",
    )(p3, p3, p3, p3, *tables)


def _conv_kernel(prev_ref, cur_ref, next_ref, w_ref, b_ref, o_ref, buf_ref):
    i = pl.program_id(1)
    last = pl.num_programs(1) - 1
    rows = cur_ref.shape[0]
    buf_ref[0:SUBLANES, :] = jnp.where(i == 0, 0.0, prev_ref[...])
    buf_ref[SUBLANES:SUBLANES + rows, :] = cur_ref[...]
    buf_ref[SUBLANES + rows:2 * SUBLANES + rows, :] = jnp.where(i == last, 0.0, next_ref[...])
    y = (buf_ref[SUBLANES - 1:SUBLANES - 1 + rows, :] * w_ref[0:1, :]
         + buf_ref[SUBLANES:SUBLANES + rows, :] * w_ref[1:2, :]
         + buf_ref[SUBLANES + 1:SUBLANES + 1 + rows, :] * w_ref[2:3, :]
         + b_ref[...])
    o_ref[...] = _silu(y)


def _conv_silu(p3, conv_w, conv_b):
    bsz, t, _ = p3.shape
    tile = min(CONV_TILE, t)
    nb = t // tile
    per = tile // SUBLANES
    outs = []
    for j in (CD_QD, CD_KD):
        lo = (j - CD_QD) * 512
        outs.append(pl.pallas_call(
            _conv_kernel,
            out_shape=jax.ShapeDtypeStruct((bsz, t, 512), F32),
            grid=(bsz, nb),
            in_specs=[pl.BlockSpec((None, SUBLANES, 512), lambda b, i, j=j: (b, jnp.maximum(i * per - 1, 0), j)),
                      pl.BlockSpec((None, tile, 512), lambda b, i, j=j: (b, i, j)),
                      pl.BlockSpec((None, SUBLANES, 512),
                                   lambda b, i, j=j: (b, jnp.minimum((i + 1) * per, t // SUBLANES - 1), j)),
                      pl.BlockSpec((3, 512), lambda b, i: (0, 0)),
                      pl.BlockSpec((1, 512), lambda b, i: (0, 0))],
            out_specs=pl.BlockSpec((None, tile, 512), lambda b, i: (b, i, 0)),
            scratch_shapes=[pltpu.VMEM((tile + 2 * SUBLANES, 512), F32)],
            compiler_params=_params(2),
            name="conv_silu",
        )(p3, p3, p3, conv_w[:, lo:lo + 512], conv_b[None, lo:lo + 512]))
    return outs


def _gate_lane_layout():
    lane = np.arange(LANES)
    r = lane % GATE_LANES
    return lane // (N_HEADS * GATE_LANES), (lane // GATE_LANES) % N_HEADS, r // 3, r % 3


def _gate_const_rows():
    _, _, cls, _ = _gate_lane_layout()
    return jnp.asarray(np.concatenate([np.tile((cls == k + 1).astype(np.float32)[None, :], (LANES, 1))
                                       for k in range(3)], axis=0), BF16)


def _pieces(x, piece):
    hi = x.astype(BF16).astype(F32)
    r1 = x - hi
    mid = r1.astype(BF16).astype(F32)
    lo = (r1 - mid).astype(BF16).astype(F32)
    return jnp.where(piece == 0, hi, jnp.where(piece == 1, mid, lo))


def _running_max(x, direction):
    row = lax.broadcasted_iota(jnp.int32, x.shape, 0)
    step = 1
    while step < CHUNK:
        if direction == 0:
            shifted = jnp.where(row >= step, pltpu.roll(x, step, 0), -jnp.inf)
        else:
            shifted = jnp.where(row < CHUNK - step, pltpu.roll(x, CHUNK - step, 0), -jnp.inf)
        x = jnp.maximum(x, shifted)
        step *= 2
    return x


def _scan_d_kernel(q_f, k_f, v_f, gi_f, gf_f, q_b, k_b, v_b, gi_b, gf_b, fgb_ref, const_ref, o_f, o_b, c_ref, m_ref):
    @pl.when(pl.program_id(1) == 0)
    def _():
        c_ref[...] = jnp.zeros_like(c_ref)
        m_ref[...] = jnp.zeros_like(m_ref)

    n_chunks = SCAN_ROWS // CHUNK
    dirs = ((q_f, k_f, v_f, gi_f, gf_f, o_f), (q_b, k_b, v_b, gi_b, gf_b, o_b))
    lane = lax.broadcasted_iota(jnp.int32, (1, LANES), 1)
    lane_dir = lane // (N_HEADS * GATE_LANES)
    lane_head = (lane // GATE_LANES) % N_HEADS
    cls = (lane % GATE_LANES) // 3
    piece = (lane % GATE_LANES) % 3
    ones_block = jnp.ones((CHUNK, HEAD_DV), BF16)

    def prepare(c, m_state):
        pre = []
        for d, (q_ref, k_ref, v_ref, gi_ref, gf_ref, o_ref) in enumerate(dirs):
            rows = pl.ds((c if d == 0 else n_chunks - 1 - c) * CHUNK, CHUNK)
            pre.append((rows, gi_ref[rows, :], _log_sigmoid(gf_ref[rows, :] + fgb_ref[...])))
        bcs = _chunk_cumsum([(pre[0][2], 0), (pre[1][2], 1)])
        units, m_next = [], []
        for d, (q_ref, k_ref, v_ref, gi_ref, gf_ref, o_ref) in enumerate(dirs):
            rows, gi, _ = pre[d]
            b = bcs[d]
            last = _last_row(d)
            mask = _tri(d)
            m_st = m_state[d]
            b_last = b[last:last + 1, :]
            cj = gi - b
            run = _running_max(cj, d)
            m_i = jnp.maximum(b + m_st, b + run)
            u = b - m_i
            m_new = b_last + jnp.maximum(m_st, run[last:last + 1, :])
            s_old = jnp.exp(b_last + m_st - m_new)
            m_next.append(m_new)
            quantity =jnp.where(cls == 0, u, jnp.where(cls == 1, u + m_st, jnp.where(
                cls == 2, -m_i, cj + (b_last - m_new))))
            lhs_all = jnp.where(cls == 4, 1.0, jnp.where(cls < 4, _pieces(quantity, piece), 0.0))
            lhs = jnp.concatenate([jnp.where((lane_dir == d) & (lane_head == h), lhs_all, 0.0)
                                   for h in range(N_HEADS)], axis=0).astype(BF16)
            key_rows = jnp.where(cls == 4, _pieces(cj, piece), jnp.where(cls == 0, 1.0, 0.0)).astype(BF16)
            expo = _nt(lhs, jnp.concatenate([const_ref[...], key_rows], axis=0))
            for h in range(N_HEADS):
                sl = slice(h * HEAD_DV, (h + 1) * HEAD_DV)
                q = q_ref[rows, sl] * (HEAD_DV ** -0.5)
                k = k_ref[rows, sl]
                v_ext = jnp.concatenate([v_ref[rows, sl].astype(BF16), ones_block], axis=1)
                lane0 = (d * N_HEADS + h) * GATE_LANES
                units.append(dict(u=d * N_HEADS + h, rows=rows, sl=sl, o_ref=o_ref, q=q, k=k, v_ext=v_ext,
                                  mask=mask, expo=expo[h * CHUNK:(h + 1) * CHUNK, :],
                                  s_old=s_old[:, lane0:lane0 + 1], qk=_nt(q.astype(BF16), k.astype(BF16))))
        for t in units:
            e = t.pop("expo")
            w = t.pop("qk") * jnp.exp(jnp.where(t.pop("mask"), e[:, 3 * LANES:3 * LANES + CHUNK], -jnp.inf))
            t["lhs2"] = jnp.concatenate([(t.pop("q") * jnp.exp(e[:, :LANES])).astype(BF16), w.astype(BF16)], axis=1)
            t["floor"] = jnp.exp(e[:, LANES:2 * LANES])
            t["upd"] = _tn((t.pop("k") * jnp.exp(e[:, 2 * LANES:3 * LANES])).astype(BF16), t["v_ext"])
        return units, m_next

    def finish(units):
        nums = [_mm(t["lhs2"], jnp.concatenate([c_ref[t["u"]].astype(BF16), t["v_ext"]], axis=0)) for t in units]
        for t, num in zip(units, nums):
            den = jnp.maximum(jnp.abs(num[:, HEAD_DV:]), t["floor"])
            t["o_ref"][t["rows"], t["sl"]] = (num[:, :HEAD_DV] / den).astype(BF16)
            c_ref[t["u"]] = t["s_old"] * c_ref[t["u"]] + t["upd"]

    m_state = [m_ref[d][0:1, :] for d in range(2)]
    pending, m_state = prepare(0, m_state)
    for c in range(n_chunks):
        following = None
        if c + 1 < n_chunks:
            following, m_state = prepare(c + 1, m_state)
        finish(pending)
        pending = following
    for d in range(2):
        m_ref[d] = jnp.broadcast_to(m_state[d], (SUBLANES, LANES))


def _scan_d(qd, kd, p3, fgb_row):
    bsz, t, _ = p3.shape
    nb = t // SCAN_ROWS
    const_rows = _gate_const_rows()

    def spec(j, w, flip):
        if flip:
            return pl.BlockSpec((None, SCAN_ROWS, w), lambda b, n: (b, nb - 1 - n, j))
        return pl.BlockSpec((None, SCAN_ROWS, w), lambda b, n: (b, n, j))

    out = jax.ShapeDtypeStruct((bsz, t, MIX_HALF), BF16)
    in_specs = []
    for flip in (False, True):
        in_specs += [spec(0, 512, flip), spec(0, 512, flip), spec(CD_VD, 512, flip),
                     spec(CD_GATE_I_BLOCK, LANES, flip), spec(CD_GATE_F_BLOCK, LANES, flip)]
    in_specs += [pl.BlockSpec((1, LANES), lambda b, n: (0, 0)), pl.BlockSpec(const_rows.shape, lambda b, n: (0, 0))]
    return pl.pallas_call(
        _scan_d_kernel,
        out_shape=(out, out),
        grid=(bsz, nb),
        in_specs=in_specs,
        out_specs=(spec(0, MIX_HALF, False), spec(0, MIX_HALF, True)),
        scratch_shapes=[pltpu.VMEM((2 * N_HEADS, HEAD_DV, 2 * HEAD_DV), F32),
                        pltpu.VMEM((2, SUBLANES, LANES), F32)],
        compiler_params=_params(2),
        name="scan_d",
    )(qd, kd, p3, p3, p3, qd, kd, p3, p3, p3, fgb_row, const_rows)


def _layer_norm(z, g, b):
    zc = z - jnp.mean(z, axis=-1, keepdims=True)
    var = jnp.mean(zc * zc, axis=-1, keepdims=True)
    return zc * lax.rsqrt(var + LN_EPS) * g + b


def _mix_out_kernel(o1f, o1b, o2f, o2b, g1, g2, x_ref, norm_ref, w_ref, lng, lnb, h_ref, hb_ref):
    parts = []
    for of, ob, g_ref, base in ((o1f, o1b, g1, 0), (o2f, o2b, g2, MIX_HALF)):
        o = of[...].astype(F32) + ob[...].astype(F32)
        gate = g_ref[...]
        for h in range(N_HEADS):
            sl = slice(h * HEAD_DV, (h + 1) * HEAD_DV)
            oh = o[:, sl]
            oh = oh * lax.rsqrt(jnp.mean(oh * oh, axis=-1, keepdims=True) + NORM_EPS)
            parts.append((oh * norm_ref[:, base + h * HEAD_DV:base + (h + 1) * HEAD_DV] * gate[:, sl]).astype(BF16))
    mixed = jnp.concatenate(parts, axis=1)
    z = ALPHA * x_ref[...] + _mm(mixed, w_ref[...])
    hn = _layer_norm(z, lng[...], lnb[...])
    h_ref[...] = hn
    hb_ref[...] = hn.astype(BF16)


def _mix_out(o1f, o1b, o2f, o2b, p2d, gate_blocks, x2d, norm, w_out, ln_g, ln_b):
    n = x2d.shape[0]
    half = pl.BlockSpec((WIDE_TILE, MIX_HALF), lambda i: (i, 0))
    full = pl.BlockSpec((WIDE_TILE, D_MODEL), lambda i: (i, 0))
    row = pl.BlockSpec((1, D_MODEL), lambda i: (0, 0))
    ga, gb = gate_blocks
    return pl.pallas_call(
        _mix_out_kernel,
        out_shape=(jax.ShapeDtypeStruct((n, D_MODEL), F32), jax.ShapeDtypeStruct((n, D_MODEL), BF16)),
        grid=(n // WIDE_TILE,),
        in_specs=[half, half, half, half,
                  pl.BlockSpec((WIDE_TILE, MIX_HALF), lambda i: (i, ga)),
                  pl.BlockSpec((WIDE_TILE, MIX_HALF), lambda i: (i, gb)),
                  full, row, pl.BlockSpec((D_MODEL, D_MODEL), lambda i: (0, 0)), row, row],
        out_specs=(full, full),
        compiler_params=_params(1),
        name="mix_out",
    )(o1f, o1b, o2f, o2b, p2d, p2d, x2d, norm, w_out, ln_g, ln_b)


def _route(h, rw_ref, rb_ref, idx_ref, gate_ref):
    rw = rw_ref[...]
    rw_hi = rw.astype(BF16)
    rw_r = rw - rw_hi.astype(F32)
    rw_mid = rw_r.astype(BF16)
    rw3 = jnp.concatenate([rw_hi, rw_mid, (rw_r - rw_mid.astype(F32)).astype(BF16)], axis=0)
    h_hi = h.astype(BF16)
    h_lo = (h - h_hi.astype(F32)).astype(BF16)
    terms = _nt(rw3, h_hi) + _nt(rw3, h_lo)
    logits = terms[:N_EXPERTS] + terms[N_EXPERTS:2 * N_EXPERTS] + terms[2 * N_EXPERTS:]
    rows = [logits[e:e + 1, :] for e in range(N_EXPERTS)]
    mx = functools.reduce(jnp.maximum, rows)
    ex = [jnp.exp(r - mx) for r in rows]
    tot = functools.reduce(lambda a, b: a + b, ex)
    score = [e / tot for e in ex]
    biased = [score[e] + rb_ref[e:e + 1, :] for e in range(N_EXPERTS)]

    def argmax_first(vals):
        best, idx = vals[0], jnp.zeros(vals[0].shape, jnp.int32)
        for j in range(1, len(vals)):
            upd = vals[j] > best
            best = jnp.where(upd, vals[j], best)
            idx = jnp.where(upd, j, idx)
        return best, idx

    def pick(vals, idx):
        out = vals[0]
        for j in range(1, len(vals)):
            out = jnp.where(idx == j, vals[j], out)
        return out

    group_scores = []
    for g in range(N_GROUPS):
        a = biased[g * EXPERTS_PER_GROUP:(g + 1) * EXPERTS_PER_GROUP]
        pairs = [a[i] + a[j] for i in range(EXPERTS_PER_GROUP) for j in range(i + 1, EXPERTS_PER_GROUP)]
        group_scores.append(functools.reduce(jnp.maximum, pairs))
    _, g_sel = argmax_first(group_scores)
    in_b = [pick([biased[g * EXPERTS_PER_GROUP + k] for g in range(N_GROUPS)], g_sel) for k in range(EXPERTS_PER_GROUP)]
    in_s = [pick([score[g * EXPERTS_PER_GROUP + k] for g in range(N_GROUPS)], g_sel) for k in range(EXPERTS_PER_GROUP)]
    _, i0 = argmax_first(in_b)
    _, i1 = argmax_first([jnp.where(i0 == k, -jnp.inf, in_b[k]) for k in range(EXPERTS_PER_GROUP)])
    s0, s1 = pick(in_s, i0), pick(in_s, i1)
    den = s0 + s1
    idx_ref[0:1, :] = g_sel * EXPERTS_PER_GROUP + i0
    idx_ref[1:2, :] = g_sel * EXPERTS_PER_GROUP + i1
    gate_ref[0:1, :] = s0 / den
    gate_ref[1:2, :] = s1 / den


def _router_kernel(h_ref, rw_ref, rb_ref, idx_ref, gate_ref):
    _route(h_ref[...], rw_ref, rb_ref, idx_ref, gate_ref)


def _router(h2d, rw_t, rb_col):
    n = h2d.shape[0]
    tile = min(ROUTER_TILE, n)
    routed = pl.BlockSpec((TOP_K, tile), lambda i: (0, i))
    return pl.pallas_call(
        _router_kernel,
        out_shape=(jax.ShapeDtypeStruct((TOP_K, n), jnp.int32), jax.ShapeDtypeStruct((TOP_K, n), F32)),
        grid=(n // tile,),
        in_specs=[pl.BlockSpec((tile, D_MODEL), lambda i: (i, 0)),
                  pl.BlockSpec(rw_t.shape, lambda i: (0, 0)),
                  pl.BlockSpec((N_EXPERTS, 1), lambda i: (0, 0))],
        out_specs=(routed, routed),
        compiler_params=_params(1),
        name="router",
    )(h2d, rw_t, rb_col)


def _plan_kernel(idx_ref, dest_ref, cnt_ref, su_ref, run_ref, start_ref):
    phase, i = pl.program_id(0), pl.program_id(1)
    tile = idx_ref.shape[1]
    e_iota = lax.broadcasted_iota(jnp.int32, (N_EXPERTS, tile), 0)
    oh0 = idx_ref[0:1, :] == e_iota
    oh1 = idx_ref[1:2, :] == e_iota
    member = jnp.where(oh0 | oh1, 1.0, 0.0)
    tile_count = jnp.broadcast_to(jnp.sum(member, axis=1, keepdims=True), (N_EXPERTS, LANES))

    @pl.when((phase == 0) & (i == 0))
    def _():
        run_ref[...] = jnp.zeros_like(run_ref)
        r = lax.broadcasted_iota(jnp.int32, (tile, tile), 0)
        c = lax.broadcasted_iota(jnp.int32, (tile, tile), 1)
        su_ref[...] = jnp.where(r < c, 1.0, 0.0).astype(BF16)

    @pl.when((phase == 1) & (i == 0))
    def _():
        counts = run_ref[...]
        cnt_ref[...] = counts.astype(jnp.int32)
        padded = jnp.ceil(counts * (1.0 / EXPERT_ROWS)) * EXPERT_ROWS
        r = lax.broadcasted_iota(jnp.int32, (N_EXPERTS, N_EXPERTS), 0)
        c = lax.broadcasted_iota(jnp.int32, (N_EXPERTS, N_EXPERTS), 1)
        start_ref[...] = jnp.dot(jnp.where(c < r, 1.0, 0.0), padded, precision=HIGHEST, preferred_element_type=F32)
        run_ref[...] = jnp.zeros_like(run_ref)

    @pl.when(phase == 1)
    def _():
        before = _mm(member.astype(BF16), su_ref[...])
        pos = start_ref[:, 0:1] + run_ref[:, 0:1] + before
        dest_ref[0:1, :] = jnp.sum(jnp.where(oh0, pos, 0.0), axis=0, keepdims=True).astype(jnp.int32)
        dest_ref[1:2, :] = jnp.sum(jnp.where(oh1, pos, 0.0), axis=0, keepdims=True).astype(jnp.int32)

    run_ref[...] += tile_count


def _plan(idx):
    n = idx.shape[1]
    tile = min(PLAN_TILE, n)
    return pl.pallas_call(
        _plan_kernel,
        out_shape=(jax.ShapeDtypeStruct((TOP_K, n), jnp.int32), jax.ShapeDtypeStruct((N_EXPERTS, LANES), jnp.int32)),
        grid=(2, n // tile),
        in_specs=[pl.BlockSpec((TOP_K, tile), lambda p, i: (0, i))],
        out_specs=(pl.BlockSpec((TOP_K, tile), lambda p, i: (0, i * p)),
                   pl.BlockSpec((N_EXPERTS, LANES), lambda p, i: (0, 0))),
        scratch_shapes=[pltpu.VMEM((tile, tile), BF16), pltpu.VMEM((N_EXPERTS, LANES), F32),
                        pltpu.VMEM((N_EXPERTS, LANES), F32)],
        compiler_params=_params(2),
        name="moe_plan",
    )(idx)


def _invert_kernel(pad_ref, dest_ref, tok_ref, *, n):
    step = pl.program_id(0)
    span = dest_ref.shape[0]
    n_rows = tok_ref.shape[0]

    def filler(r):
        return r - jnp.where(r >= 2 * n, 2 * n, jnp.where(r >= n, n, 0))

    @pl.when(step == 0)
    def _():
        for e in range(N_EXPERTS):
            def clear(i, carry, lo=pad_ref[0, e], hi=pad_ref[1, e]):
                r = jnp.minimum(jnp.maximum(hi - 1 - i, lo), n_rows - 1)
                tok_ref[r] = filler(r)
                return carry

            lax.fori_loop(0, EXPERT_ROWS, clear, 0, unroll=16)

        def clear_tail(r, carry):
            tok_ref[r] = filler(r)
            return carry

        lax.fori_loop(pad_ref[0, N_EXPERTS], pad_ref[1, N_EXPERTS], clear_tail, 0)

    def place(i, carry):
        a = step * span + i
        tok_ref[dest_ref[i]] = jnp.where(a >= n, a - n, a)
        return carry

    lax.fori_loop(0, span, place, 0, unroll=16)


def _invert(pad_ranges, dest_flat, n_rows):
    n_assign = dest_flat.shape[0]
    span = min(INVERT_SPAN, n_assign)
    return pl.pallas_call(
        functools.partial(_invert_kernel, n=n_assign // TOP_K),
        out_shape=jax.ShapeDtypeStruct((n_rows,), jnp.int32),
        grid=(n_assign // span,),
        in_specs=[pl.BlockSpec(pad_ranges.shape, lambda i: (0, 0), memory_space=pltpu.SMEM),
                  pl.BlockSpec((span,), lambda i: (i,), memory_space=pltpu.SMEM)],
        out_specs=pl.BlockSpec((n_rows,), lambda i: (0,), memory_space=pltpu.SMEM),
        compiler_params=_params(1),
        name="moe_invert",
    )(pad_ranges, dest_flat)


def _expert_kernel(be_ref, nv_ref, x_ref, w1_ref, w3_ref, w2_ref, y_ref, wb_ref):
    i = pl.program_id(0)
    valid = i < nv_ref[0]
    new_expert = (i == 0) | (be_ref[i] != be_ref[jnp.maximum(i - 1, 0)])

    @pl.when(valid & new_expert)
    def _():
        wb_ref[0] = w1_ref[...].astype(BF16)
        wb_ref[1] = w3_ref[...].astype(BF16)
        wb_ref[2] = w2_ref[...].astype(BF16)

    @pl.when(valid)
    def _():
        x = x_ref[...]
        hid = _silu(_mm(x, wb_ref[0])) * _mm(x, wb_ref[1])
        y_ref[...] = _mm(hid.astype(BF16), wb_ref[2]).astype(BF16)

    @pl.when(jnp.logical_not(valid))
    def _():
        y_ref[...] = jnp.zeros_like(y_ref)


def _experts(block_expert, n_valid, xs, w1, w3, w2, layer):
    n_rows = xs.shape[0]
    wspec = pl.BlockSpec((None, None, D_MODEL, D_MODEL), lambda i, be, nv: (layer, be[i], 0, 0))
    return pl.pallas_call(
        _expert_kernel,
        out_shape=jax.ShapeDtypeStruct((n_rows, D_MODEL), BF16),
        grid_spec=pltpu.PrefetchScalarGridSpec(
            num_scalar_prefetch=2,
            grid=(n_rows // EXPERT_ROWS,),
            in_specs=[pl.BlockSpec((EXPERT_ROWS, D_MODEL), lambda i, be, nv: (i, 0)), wspec, wspec, wspec],
            out_specs=pl.BlockSpec((EXPERT_ROWS, D_MODEL), lambda i, be, nv: (i, 0)),
            scratch_shapes=[pltpu.VMEM((3, D_MODEL, D_MODEL), BF16)]),
        compiler_params=_params(1),
        name="experts",
    )(block_expert, n_valid, xs, w1, w3, w2)


def _combine_kernel(h_ref, y0_ref, y1_ref, g_ref, lng, lnb, o_ref):
    g = g_ref[...]
    z = ALPHA * h_ref[...] + (g[:, 0:1] * y0_ref[...].astype(F32) + g[:, 1:2] * y1_ref[...].astype(F32))
    o_ref[...] = _layer_norm(z, lng[...], lnb[...])


def _combine(h2d, y_rows, gates_t, ln_g, ln_b):
    n = h2d.shape[0]
    nt = n // WIDE_TILE
    full = pl.BlockSpec((WIDE_TILE, D_MODEL), lambda i: (i, 0))
    row = pl.BlockSpec((1, D_MODEL), lambda i: (0, 0))
    return pl.pallas_call(
        _combine_kernel,
        out_shape=jax.ShapeDtypeStruct((n, D_MODEL), F32),
        grid=(nt,),
        in_specs=[full, full, pl.BlockSpec((WIDE_TILE, D_MODEL), lambda i: (nt + i, 0)),
                  pl.BlockSpec((WIDE_TILE, TOP_K), lambda i: (i, 0)), row, row],
        out_specs=full,
        compiler_params=_params(1),
        name="moe_combine",
    )(h2d, y_rows, y_rows, gates_t, ln_g, ln_b)


def _moe(h2d, h_bf16, idx, gates, w1, w3, w2, layer, ln_g, ln_b):
    n = h2d.shape[0]
    dest, counts = _plan(idx)
    n_rows = n * TOP_K + N_EXPERTS * EXPERT_ROWS
    n_blocks = n_rows // EXPERT_ROWS
    counts = counts[:, 0]
    padded = (counts + EXPERT_ROWS - 1) // EXPERT_ROWS * EXPERT_ROWS
    pad_end = jnp.cumsum(padded)
    block_start = jnp.arange(n_blocks, dtype=jnp.int32) * EXPERT_ROWS
    block_expert = jnp.minimum(jnp.sum((pad_end[None, :] <= block_start[:, None]).astype(jnp.int32), axis=1),
                               N_EXPERTS - 1)
    n_valid = (pad_end[-1:] // EXPERT_ROWS).astype(jnp.int32)
    pad_ranges = jnp.stack([jnp.concatenate([pad_end - padded + counts, pad_end[-1:]]),
                            jnp.concatenate([pad_end, jnp.full((1,), n_rows, jnp.int32)])]).astype(jnp.int32)
    dest_flat = dest.reshape(-1)
    xs = h_bf16[_invert(pad_ranges, dest_flat, n_rows)]
    y = _experts(block_expert, n_valid, xs, w1, w3, w2, layer)
    return _combine(h2d, y[dest_flat], gates.T, ln_g, ln_b)


def _pad_cols(w, b, main, width):
    k = w.shape[0]
    tail = w.shape[1] - main
    w_p = jnp.concatenate([w[:, :main], w[:, main:], jnp.zeros((k, width - main - tail), w.dtype)], axis=1)
    b_p = jnp.concatenate([b[:main], b[main:], jnp.zeros((width - main - tail,), b.dtype)])
    return w_p.astype(BF16), b_p[None, :].astype(F32)


def kernel(x, ab_w_in, ab_b_in, hgrn_lb, gla_gk_up, gla_gk_b, hgrn_norm, gla_norm, ab_w_out, cd_w_in, cd_b_in,
           mlstm_conv_w, mlstm_conv_b, mlstm_fgate_b, ret_norm, mlstm_norm, cd_w_out, ln_mix_g, ln_mix_b, ln_ffn_g,
           ln_ffn_b, router_w, router_b, moe_w1, moe_w3, moe_w2):
    bsz, t, d = x.shape
    n = bsz * t
    assert d == D_MODEL and t % SCAN_ROWS == 0 and t % min(CONV_TILE, t) == 0, (bsz, t, d)
    assert n % ROW_TILE == 0 and n % WIDE_TILE == 0 and n % min(ROUTER_TILE, n) == 0 and n % min(PLAN_TILE, n) == 0
    lower_bounds = jnp.cumsum(jax.nn.softmax(hgrn_lb.astype(F32), axis=0), axis=0)
    rw_t = router_w.T.astype(F32)
    rb_col = router_b.astype(F32)[:, None]
    h = x.reshape(n, d)
    for layer in range(DEPTH):
        j = layer // 2
        if layer % 2 == 0:
            w_p, b_p = _pad_cols(ab_w_in[j], ab_b_in[j], 8 * 512, AB_WIDTH)
            lb = lower_bounds[layer][None, :]
            p = _in_proj(h, w_p, b_p, _proj_ab_kernel, [lb], [pl.BlockSpec(lb.shape, lambda i: (0, 0))])
            p3 = p.reshape(bsz, t, AB_WIDTH)
            nk = N_HEADS * NARROW_DK
            u_pad = jnp.zeros((LANES, 2 * nk), F32)
            u_pad = u_pad.at[:GLA_LOWRANK, :nk].set(gla_gk_up[j, 0])
            u_pad = u_pad.at[GLA_LOWRANK:2 * GLA_LOWRANK, nk:].set(gla_gk_up[j, 1]).astype(BF16)
            gkb = jnp.concatenate([gla_gk_b[j, 0], gla_gk_b[j, 1]])[None, :].astype(F32)
            oa_f, ob_f, oa_b, ob_b = _scan_ab(p3, u_pad, gkb)
            norm = jnp.concatenate([hgrn_norm[j], gla_norm[j]])[None, :].astype(F32)
            outs = [a.reshape(n, MIX_HALF) for a in (oa_f, oa_b, ob_f, ob_b)]
            h, h_bf16 = _mix_out(*outs, p, (AB_GA, AB_GB), h, norm, ab_w_out[j].astype(BF16),
                                 ln_mix_g[layer][None, :], ln_mix_b[layer][None, :])
        else:
            lane_dir, lane_head, _, _ = _gate_lane_layout()
            col_i = 7 * 512 + lane_dir * 2 * N_HEADS + lane_head
            gate_cols = np.concatenate([col_i, col_i + N_HEADS])
            w_p = jnp.concatenate([cd_w_in[j][:, :7 * 512], cd_w_in[j][:, gate_cols]], axis=1).astype(BF16)
            b_p = jnp.concatenate([cd_b_in[j][:7 * 512], cd_b_in[j][gate_cols]])[None, :].astype(F32)
            pos = pl.BlockSpec((ROW_TILE, LANES), lambda i: (i % (t // ROW_TILE), 0))
            p = _in_proj(h, w_p, b_p, _proj_cd_kernel, list(_rotary_tables(t)), [pos, pos])
            p3 = p.reshape(bsz, t, CD_WIDTH)
            oc_f, oc_b = _scan_c(p3)
            qd, kd = _conv_silu(p3, mlstm_conv_w[j].astype(F32), mlstm_conv_b[j].astype(F32))
            fgb_row = mlstm_fgate_b[j].astype(F32)[lane_dir, lane_head][None, :]
            od_f, od_b = _scan_d(qd, kd, p3, fgb_row)
            norm = jnp.concatenate([ret_norm[j], mlstm_norm[j]])[None, :].astype(F32)
            outs = [a.reshape(n, MIX_HALF) for a in (oc_f, oc_b, od_f, od_b)]
            h, h_bf16 = _mix_out(*outs, p, (CD_GC, CD_OD), h, norm, cd_w_out[j].astype(BF16),
                                 ln_mix_g[layer][None, :], ln_mix_b[layer][None, :])
        idx, gates = _router(h, rw_t, rb_col)
        h = _moe(h, h_bf16, idx, gates, moe_w1, moe_w3, moe_w2, layer,
                 ln_ffn_g[layer][None, :], ln_ffn_b[layer][None, :])
    return h.reshape(bsz, t, d)
```

```python
import functools

import jax
import jax.numpy as jnp
import numpy as np
from jax import lax
from jax.experimental import pallas as pl
from jax.experimental.pallas import tpu as pltpu

F32 = jnp.float32
BF16 = jnp.bfloat16
HIGHEST = lax.Precision.HIGHEST

D_MODEL = 1024
DEPTH = 2
CHUNK = 64
MIX_HALF = D_MODEL // 2
N_HEADS = 4
HEAD_DV = MIX_HALF // N_HEADS
NARROW_DK = 64
GLA_LOWRANK = 16
GLA_GATE_NORMALIZER = 16.0
ROPE_THETA = 10000.0
N_EXPERTS = 16
N_GROUPS = 4
EXPERTS_PER_GROUP = N_EXPERTS // N_GROUPS
TOP_K = 2
ALPHA = (2.0 * DEPTH) ** 0.25
LN_EPS = 1e-5
NORM_EPS = 1e-6

LANES = 128
SUBLANES = 8
VMEM_LIMIT = 52 * 1024 * 1024

ROW_TILE = 512
WIDE_TILE = 512
COMBINE_TILE = 1024
CONV_TILE = 2048
SCAN_ROWS = 512
ROUTER_TILE = 2048
PLAN_TILE = 1024
EXPERT_ROWS = 512

AB_QA, AB_FF, AB_FB, AB_IA, AB_GA, AB_QKB, AB_VB, AB_GB = range(8)
AB_LR_BLOCK = 32
AB_WIDTH = 8 * 512 + LANES
CD_QKC, CD_VC, CD_GC, CD_QD, CD_KD, CD_VD, CD_OD = range(7)
CD_GATE_I_BLOCK = 28
CD_GATE_F_BLOCK = 29
CD_WIDTH = 7 * 512 + 2 * LANES
GATE_LANES = 16


def _params(n_axes, vmem=VMEM_LIMIT):
    return pltpu.CompilerParams(dimension_semantics=("arbitrary",) * n_axes, vmem_limit_bytes=vmem)


def _nt(a, b):
    return lax.dot_general(a, b, (((1,), (1,)), ((), ())), preferred_element_type=F32)


def _tn(a, b):
    return lax.dot_general(a, b, (((0,), (0,)), ((), ())), preferred_element_type=F32)


def _mm(a, b):
    return jnp.dot(a, b, preferred_element_type=F32)


def _sigmoid(x):
    return 1.0 / (1.0 + jnp.exp(-x))


def _silu(x):
    return x * _sigmoid(x)


def _log_sigmoid(x):
    return jnp.minimum(x, 0.0) - jnp.log(1.0 + jnp.exp(-jnp.abs(x)))


def _tri(direction):
    row = lax.broadcasted_iota(jnp.int32, (CHUNK, CHUNK), 0)
    col = lax.broadcasted_iota(jnp.int32, (CHUNK, CHUNK), 1)
    return (col <= row) if direction == 0 else (col >= row)


def _ref_row(direction):
    return CHUNK // 2 - 1 if direction == 0 else CHUNK // 2


def _last_row(direction):
    return CHUNK - 1 if direction == 0 else 0


def _proj_ab_kernel(x_ref, w_ref, b_ref, lb_ref, o_ref):
    p = _mm(x_ref[...].astype(BF16), w_ref[...]) + b_ref[...]
    o_ref[:, :512] = _silu(p[:, :512])
    lb = jnp.concatenate([lb_ref[...], lb_ref[...]], axis=1)
    o_ref[:, 512:1536] = lb + (1.0 - lb) * _sigmoid(p[:, 512:1536])
    o_ref[:, 1536:] = p[:, 1536:]
    for blk in (AB_GA, AB_GB):
        o_ref[:, blk * 512:(blk + 1) * 512] = _silu(p[:, blk * 512:(blk + 1) * 512])


def _proj_cd_kernel(x_ref, w_ref, b_ref, cos_ref, sin_ref, o_ref):
    p = _mm(x_ref[...].astype(BF16), w_ref[...]) + b_ref[...]
    cos, sin_signed = cos_ref[...], sin_ref[...]
    for j in range(4):
        rot = _rotary(p[:, j * LANES:(j + 1) * LANES], cos, sin_signed)
        o_ref[:, j * LANES:(j + 1) * LANES] = rot * (NARROW_DK ** -0.5) if j < 2 else rot
    o_ref[:, 512:] = p[:, 512:]
    o_ref[:, CD_GC * 512:(CD_GC + 1) * 512] = _silu(p[:, CD_GC * 512:(CD_GC + 1) * 512])
    o_ref[:, CD_OD * 512:(CD_OD + 1) * 512] = _sigmoid(p[:, CD_OD * 512:(CD_OD + 1) * 512])


def _in_proj(x2d, w, b, body, extras, extra_specs):
    n, k = x2d.shape
    m = w.shape[1]
    return pl.pallas_call(
        body,
        out_shape=jax.ShapeDtypeStruct((n, m), F32),
        grid=(n // ROW_TILE,),
        in_specs=[pl.BlockSpec((ROW_TILE, k), lambda i: (i, 0)),
                  pl.BlockSpec((k, m), lambda i: (0, 0)),
                  pl.BlockSpec((1, m), lambda i: (0, 0))] + extra_specs,
        out_specs=pl.BlockSpec((ROW_TILE, m), lambda i: (i, 0)),
        compiler_params=_params(1),
        name="in_proj",
    )(x2d, w, b, *extras)


def _pipelined_chunks(n_chunks, prepare, finish):
    pending = prepare(0)
    for c in range(n_chunks):
        following = prepare(c + 1) if c + 1 < n_chunks else None
        finish(pending)
        pending = following


def _units_prepare(units):
    return _units_prepare_staggered([lambda: units])


def _units_prepare_staggered(builders):
    def second(group):
        for u in group:
            u["lhs"] = jnp.concatenate([u["qi"], u["intra"](u.pop("scores")).astype(BF16)], axis=1)

    done, previous = [], None
    for build in builders:
        group = build()
        for u in group:
            u["scores"] = _nt(u["qe"], u["ke"])
        for u in group:
            u["v_t"] = u["v"].T
            u["upd"] = _mm(u["v_t"], u["ks"])
        if previous is not None:
            second(previous)
        previous = group
        done += group
    second(previous)
    return done


def _units_finish(units, st_ref, group=4):
    def close(batch):
        for u in batch:
            u["store"](u.pop("o").astype(BF16))
            st_ref[u["idx"]] = st_ref[u["idx"]] * u["dec"] + u["upd"]

    previous = None
    for g in range(0, len(units), group):
        batch = units[g:g + group]
        for u in batch:
            u["o"] = _nt(u["lhs"], jnp.concatenate([st_ref[u["idx"]].astype(BF16), u["v_t"]], axis=1))
        if previous is not None:
            close(previous)
        previous = batch
    close(previous)


def _chunk_cumsum(parts):
    cat = jnp.concatenate([x for x, _ in parts], axis=1)
    hi = cat.astype(BF16)
    lo = (cat - hi.astype(F32)).astype(BF16)
    width = cat.shape[1]
    inc = _mm(_tri(0).astype(BF16), jnp.concatenate([hi, lo], axis=1))
    inc = inc[:, :width] + inc[:, width:]
    out, off = [], 0
    for x, direction in parts:
        p = inc[:, off:off + x.shape[1]]
        off += x.shape[1]
        out.append(p if direction == 0 else p[CHUNK - 1:CHUNK, :] - p + x)
    return out


def _gated_operands(q, k, gc, direction):
    r, l = _ref_row(direction), _last_row(direction)
    g_ref = gc[r:r + 1, :]
    g_last = gc[l:l + 1, :]
    qe = q * jnp.exp(gc - g_ref)
    ke = k * jnp.exp(g_ref - gc)
    qi = qe * jnp.exp(g_ref)
    ks = ke * jnp.exp(g_last - g_ref)
    return [a.astype(BF16) for a in (qe, ke, qi, ks)], jnp.exp(g_last)


def _narrow_head_mask(h):
    lane = lax.broadcasted_iota(jnp.int32, (1, LANES), 1)
    return (lane // NARROW_DK) == (h % 2)


def _scan_ab_kernel(qa_f, fa_f, ia_f, qkb_f, vb_f, lr_f, qa_b, fa_b, ia_b, qkb_b, vb_b, lr_b,
                    u_ref, gkb_ref, oa_f, ob_f, oa_b, ob_b, st_ref):
    @pl.when(pl.program_id(1) == 0)
    def _():
        st_ref[...] = jnp.zeros_like(st_ref)

    n_chunks = SCAN_ROWS // CHUNK
    dirs = ((qa_f, fa_f, ia_f, qkb_f, vb_f, lr_f, oa_f, ob_f), (qa_b, fa_b, ia_b, qkb_b, vb_b, lr_b, oa_b, ob_b))

    nk = N_HEADS * NARROW_DK

    def prepare(c):
        pre = []
        for d, (qa, fa, ia, qkb, vb, lr, oa, ob) in enumerate(dirs):
            rows = pl.ds((c if d == 0 else n_chunks - 1 - c) * CHUNK, CHUNK)
            f = fa[rows, :]
            logits = _mm(lr[rows, :].astype(BF16), u_ref[:, d * nk:(d + 1) * nk]) + gkb_ref[:, d * nk:(d + 1) * nk]
            qk = qkb[rows, :]
            pre.append(dict(rows=rows, f=f, lf_a=jnp.log(f), lf_b=_log_sigmoid(logits) / GLA_GATE_NORMALIZER,
                            q_a=qa[rows, :], v_a=ia[rows, :].astype(BF16),
                            q_b=qk[:, :nk] * (NARROW_DK ** -0.5), k_b=qk[:, nk:], v_b=vb[rows, :].astype(BF16)))
        gcs = _chunk_cumsum([(pre[0]["lf_a"], 0), (pre[0]["lf_b"], 0), (pre[1]["lf_a"], 1), (pre[1]["lf_b"], 1)])
        def hgrn_group(d, oa):
            p, mask = pre[d], _tri(d)
            rows = p["rows"]
            (qe, ke, qi, ks), dec = _gated_operands(p["q_a"], 1.0 - p["f"], gcs[2 * d], d)
            group = []
            for h in range(N_HEADS):
                sl = slice(h * LANES, (h + 1) * LANES)

                def store(o, sl=sl):
                    oa[rows, sl] = o

                group.append(dict(qe=qe[:, sl], ke=ke[:, sl], qi=qi[:, sl], ks=ks[:, sl], v=p["v_a"][:, sl],
                                  dec=dec[:, sl], intra=lambda s: jnp.where(mask, s, 0.0),
                                  idx=d * 2 * N_HEADS + h, store=store))
            return group

        def gla_group(d, ob):
            p, mask = pre[d], _tri(d)
            rows = p["rows"]
            (qe, ke, qi, ks), dec = _gated_operands(p["q_b"], p["k_b"], gcs[2 * d + 1], d)
            group = []
            for h in range(N_HEADS):
                psl = slice((h // 2) * LANES, (h // 2 + 1) * LANES)
                hm = _narrow_head_mask(h)
                pick = lambda a, psl=psl, hm=hm: jnp.where(hm, a[:, psl], jnp.zeros((), BF16))
                vsl = slice(h * HEAD_DV, (h + 1) * HEAD_DV)

                def store(o, vsl=vsl):
                    ob[rows, vsl] = o

                group.append(dict(qe=pick(qe), ke=pick(ke), qi=pick(qi), ks=pick(ks), v=p["v_b"][:, vsl],
                                  dec=dec[:, psl], intra=lambda s: jnp.where(mask, s, 0.0),
                                  idx=d * 2 * N_HEADS + N_HEADS + h, store=store))
            return group

        builders = []
        for d, (qa, fa, ia, qkb, vb, lr, oa, ob) in enumerate(dirs):
            builders += [functools.partial(hgrn_group, d, oa), functools.partial(gla_group, d, ob)]
        return _units_prepare_staggered(builders)

    _pipelined_chunks(n_chunks, prepare, functools.partial(_units_finish, st_ref=st_ref))


def _scan_ab(p3, u_pad, gkb):
    bsz, t, _ = p3.shape
    nb = t // SCAN_ROWS

    def fwd(j, w=512):
        return pl.BlockSpec((None, SCAN_ROWS, w), lambda b, n: (b, n, j))

    def bwd(j, w=512):
        return pl.BlockSpec((None, SCAN_ROWS, w), lambda b, n: (b, nb - 1 - n, j))

    def const(shape):
        return pl.BlockSpec(shape, lambda b, n: (0,) * len(shape))

    out = jax.ShapeDtypeStruct((bsz, t, MIX_HALF), BF16)
    o_f = pl.BlockSpec((None, SCAN_ROWS, MIX_HALF), lambda b, n: (b, n, 0))
    o_b = pl.BlockSpec((None, SCAN_ROWS, MIX_HALF), lambda b, n: (b, nb - 1 - n, 0))
    return pl.pallas_call(
        _scan_ab_kernel,
        out_shape=(out, out, out, out),
        grid=(bsz, nb),
        in_specs=[fwd(AB_QA), fwd(AB_FF), fwd(AB_IA), fwd(AB_QKB), fwd(AB_VB), fwd(AB_LR_BLOCK, LANES),
                  bwd(AB_QA), bwd(AB_FB), bwd(AB_IA), bwd(AB_QKB), bwd(AB_VB), bwd(AB_LR_BLOCK, LANES),
                  const(u_pad.shape), const(gkb.shape)],
        out_specs=(o_f, o_f, o_b, o_b),
        scratch_shapes=[pltpu.VMEM((4 * N_HEADS, HEAD_DV, LANES), F32)],
        compiler_params=_params(2),
        name="scan_ab",
    )(*([p3] * 12), u_pad, gkb)


def _rotary(a, cos, sin_signed):
    lane = lax.broadcasted_iota(jnp.int32, (1, LANES), 1)
    first_half = (lane % NARROW_DK) < (NARROW_DK // 2)
    swapped = jnp.where(first_half, pltpu.roll(a, LANES - NARROW_DK // 2, 1), pltpu.roll(a, NARROW_DK // 2, 1))
    return a * cos + swapped * sin_signed


def _scan_c_kernel(qk_f, v_f, qk_b, v_b, dmat_ref, qsc_ref, ksc_ref, cd_ref, o_f, o_b, st_ref):
    @pl.when(pl.program_id(1) == 0)
    def _():
        st_ref[...] = jnp.zeros_like(st_ref)

    n_chunks = SCAN_ROWS // CHUNK
    dirs = ((qk_f, v_f, o_f), (qk_b, v_b, o_b))

    def prepare(c):
        units = []
        for d, (qk_ref, v_ref, o_ref) in enumerate(dirs):
            rows = pl.ds((c if d == 0 else n_chunks - 1 - c) * CHUNK, CHUNK)
            v = v_ref[rows, :].astype(BF16)
            for h in range(N_HEADS):
                hm = _narrow_head_mask(h)
                q = jnp.where(hm, qk_ref[rows, (h // 2) * LANES:(h // 2 + 1) * LANES], 0.0)
                k = jnp.where(hm, qk_ref[rows, (2 + h // 2) * LANES:(3 + h // 2) * LANES], 0.0)
                vsl = slice(h * HEAD_DV, (h + 1) * HEAD_DV)

                def store(o, o_ref=o_ref, rows=rows, vsl=vsl):
                    o_ref[rows, vsl] = o

                units.append(dict(qe=q.astype(BF16), ke=k.astype(BF16), qi=(q * qsc_ref[d, h]).astype(BF16),
                                  ks=(k * ksc_ref[d, h]).astype(BF16), v=v[:, vsl], dec=cd_ref[d, h][0:1, :],
                                  intra=lambda s, d=d, h=h: s * dmat_ref[d, h], idx=d * N_HEADS + h, store=store))
        return _units_prepare(units)

    _pipelined_chunks(n_chunks, prepare, functools.partial(_units_finish, st_ref=st_ref))


def _retention_tables():
    pos = np.arange(CHUNK, dtype=np.float64)
    dmat = np.zeros((2, N_HEADS, CHUNK, CHUNK), np.float32)
    qsc = np.zeros((2, N_HEADS, CHUNK, LANES), np.float32)
    ksc = np.zeros((2, N_HEADS, CHUNK, LANES), np.float32)
    cd = np.zeros((2, N_HEADS, SUBLANES, LANES), np.float32)
    for d in range(2):
        for h in range(N_HEADS):
            lg = np.log1p(-np.exp2(-(5.0 + 2.0 * h + d)))
            dist = pos[:, None] - pos[None, :]
            if d == 0:
                dmat[d, h] = np.where(dist >= 0, np.exp(np.maximum(dist, 0.0) * lg), 0.0)
                qsc[d, h] = np.exp((pos + 1.0) * lg)[:, None]
                ksc[d, h] = np.exp((CHUNK - 1.0 - pos) * lg)[:, None]
            else:
                dmat[d, h] = np.where(dist <= 0, np.exp(np.maximum(-dist, 0.0) * lg), 0.0)
                qsc[d, h] = np.exp((CHUNK - pos) * lg)[:, None]
                ksc[d, h] = np.exp(pos * lg)[:, None]
            cd[d, h] = np.exp(CHUNK * lg)
    return jnp.asarray(dmat), jnp.asarray(qsc), jnp.asarray(ksc), jnp.asarray(cd)


def _rotary_tables(t):
    half = NARROW_DK // 2
    freqs = ROPE_THETA ** (-jnp.arange(half, dtype=F32) / half)
    ang = jnp.arange(t, dtype=F32)[:, None] * freqs[None, :]
    cos, sin = jnp.cos(ang), jnp.sin(ang)
    cos_t = jnp.tile(cos, (1, LANES // half))
    sin_t = jnp.tile(jnp.concatenate([-sin, sin], axis=1), (1, LANES // NARROW_DK))
    return cos_t.astype(F32), sin_t.astype(F32)


def _scan_c(p3):
    bsz, t, _ = p3.shape
    nb = t // SCAN_ROWS
    tables = _retention_tables()

    def fwd(j):
        return pl.BlockSpec((None, SCAN_ROWS, 512), lambda b, n: (b, n, j))

    def bwd(j):
        return pl.BlockSpec((None, SCAN_ROWS, 512), lambda b, n: (b, nb - 1 - n, j))

    def const(shape):
        return pl.BlockSpec(shape, lambda b, n: (0,) * len(shape))

    out = jax.ShapeDtypeStruct((bsz, t, MIX_HALF), BF16)
    return pl.pallas_call(
        _scan_c_kernel,
        out_shape=(out, out),
        grid=(bsz, nb),
        in_specs=[fwd(CD_QKC), fwd(CD_VC), bwd(CD_QKC), bwd(CD_VC)] + [const(a.shape) for a in tables],
        out_specs=(pl.BlockSpec((None, SCAN_ROWS, MIX_HALF), lambda b, n: (b, n, 0)),
                   pl.BlockSpec((None, SCAN_ROWS, MIX_HALF), lambda b, n: (b, nb - 1 - n, 0))),
        scratch_shapes=[pltpu.VMEM((2 * N_HEADS, HEAD_DV, LANES), F32)],
        compiler_params=_params(2),
        name="scan_c",
    )(p3, p3, p3, p3, *tables)


def _conv_kernel(prev_ref, cur_ref, next_ref, w_ref, b_ref, o_ref, buf_ref):
    i = pl.program_id(1)
    last = pl.num_programs(1) - 1
    rows = cur_ref.shape[0]
    buf_ref[0:SUBLANES, :] = jnp.where(i == 0, 0.0, prev_ref[...])
    buf_ref[SUBLANES:SUBLANES + rows, :] = cur_ref[...]
    buf_ref[SUBLANES + rows:2 * SUBLANES + rows, :] = jnp.where(i == last, 0.0, next_ref[...])
    y = (buf_ref[SUBLANES - 1:SUBLANES - 1 + rows, :] * w_ref[0:1, :]
         + buf_ref[SUBLANES:SUBLANES + rows, :] * w_ref[1:2, :]
         + buf_ref[SUBLANES + 1:SUBLANES + 1 + rows, :] * w_ref[2:3, :]
         + b_ref[...])
    o_ref[...] = _silu(y)


def _conv_silu(p3, conv_w, conv_b):
    bsz, t, _ = p3.shape
    tile = min(CONV_TILE, t)
    nb = t // tile
    per = tile // SUBLANES
    outs = []
    for j in (CD_QD, CD_KD):
        lo = (j - CD_QD) * 512
        outs.append(pl.pallas_call(
            _conv_kernel,
            out_shape=jax.ShapeDtypeStruct((bsz, t, 512), F32),
            grid=(bsz, nb),
            in_specs=[pl.BlockSpec((None, SUBLANES, 512), lambda b, i, j=j: (b, jnp.maximum(i * per - 1, 0), j)),
                      pl.BlockSpec((None, tile, 512), lambda b, i, j=j: (b, i, j)),
                      pl.BlockSpec((None, SUBLANES, 512),
                                   lambda b, i, j=j: (b, jnp.minimum((i + 1) * per, t // SUBLANES - 1), j)),
                      pl.BlockSpec((3, 512), lambda b, i: (0, 0)),
                      pl.BlockSpec((1, 512), lambda b, i: (0, 0))],
            out_specs=pl.BlockSpec((None, tile, 512), lambda b, i: (b, i, 0)),
            scratch_shapes=[pltpu.VMEM((tile + 2 * SUBLANES, 512), F32)],
            compiler_params=_params(2),
            name="conv_silu",
        )(p3, p3, p3, conv_w[:, lo:lo + 512], conv_b[None, lo:lo + 512]))
    return outs


def _gate_lane_layout():
    lane = np.arange(LANES)
    r = lane % GATE_LANES
    return lane // (N_HEADS * GATE_LANES), (lane // GATE_LANES) % N_HEADS, r // 3, r % 3


def _gate_const_rows():
    _, _, cls, _ = _gate_lane_layout()
    return jnp.asarray(np.concatenate([np.tile((cls == k + 1).astype(np.float32)[None, :], (LANES, 1))
                                       for k in range(3)], axis=0), BF16)


def _pieces(x, piece):
    hi = x.astype(BF16).astype(F32)
    r1 = x - hi
    mid = r1.astype(BF16).astype(F32)
    lo = (r1 - mid).astype(BF16).astype(F32)
    return jnp.where(piece == 0, hi, jnp.where(piece == 1, mid, lo))


def _running_max(x, direction):
    row = lax.broadcasted_iota(jnp.int32, x.shape, 0)
    step = 1
    while step < CHUNK:
        if direction == 0:
            shifted = jnp.where(row >= step, pltpu.roll(x, step, 0), -jnp.inf)
        else:
            shifted = jnp.where(row < CHUNK - step, pltpu.roll(x, CHUNK - step, 0), -jnp.inf)
        x = jnp.maximum(x, shifted)
        step *= 2
    return x


def _scan_d_kernel(q_f, k_f, v_f, gi_f, gf_f, q_b, k_b, v_b, gi_b, gf_b, fgb_ref, const_ref, o_f, o_b, c_ref, m_ref):
    @pl.when(pl.program_id(1) == 0)
    def _():
        c_ref[...] = jnp.zeros_like(c_ref)
        m_ref[...] = jnp.zeros_like(m_ref)

    n_chunks = SCAN_ROWS // CHUNK
    dirs = ((q_f, k_f, v_f, gi_f, gf_f, o_f), (q_b, k_b, v_b, gi_b, gf_b, o_b))
    lane = lax.broadcasted_iota(jnp.int32, (1, LANES), 1)
    lane_dir = lane // (N_HEADS * GATE_LANES)
    lane_head = (lane // GATE_LANES) % N_HEADS
    cls = (lane % GATE_LANES) // 3
    piece = (lane % GATE_LANES) % 3
    ones_block = jnp.ones((CHUNK, HEAD_DV), BF16)

    def prepare(c, m_state):
        pre = []
        for d, (q_ref, k_ref, v_ref, gi_ref, gf_ref, o_ref) in enumerate(dirs):
            rows = pl.ds((c if d == 0 else n_chunks - 1 - c) * CHUNK, CHUNK)
            pre.append((rows, gi_ref[rows, :], _log_sigmoid(gf_ref[rows, :] + fgb_ref[...])))
        bcs = _chunk_cumsum([(pre[0][2], 0), (pre[1][2], 1)])
        units, m_next = [], []
        for d, (q_ref, k_ref, v_ref, gi_ref, gf_ref, o_ref) in enumerate(dirs):
            rows, gi, _ = pre[d]
            b = bcs[d]
            last = _last_row(d)
            mask = _tri(d)
            m_st = m_state[d]
            b_last = b[last:last + 1, :]
            cj = gi - b
            run = _running_max(cj, d)
            m_i = jnp.maximum(b + m_st, b + run)
            u = b - m_i
            m_new = b_last + jnp.maximum(m_st, run[last:last + 1, :])
            s_old = jnp.exp(b_last + m_st - m_new)
            m_next.append(m_new)
            quantity =jnp.where(cls == 0, u, jnp.where(cls == 1, u + m_st, jnp.where(
                cls == 2, -m_i, cj + (b_last - m_new))))
            lhs_all = jnp.where(cls == 4, 1.0, jnp.where(cls < 4, _pieces(quantity, piece), 0.0))
            lhs = jnp.concatenate([jnp.where((lane_dir == d) & (lane_head == h), lhs_all, 0.0)
                                   for h in range(N_HEADS)], axis=0).astype(BF16)
            key_rows = jnp.where(cls == 4, _pieces(cj, piece), jnp.where(cls == 0, 1.0, 0.0)).astype(BF16)
            expo = _nt(lhs, jnp.concatenate([const_ref[...], key_rows], axis=0))
            for h in range(N_HEADS):
                sl = slice(h * HEAD_DV, (h + 1) * HEAD_DV)
                q = q_ref[rows, sl] * (HEAD_DV ** -0.5)
                k = k_ref[rows, sl]
                v_ext = jnp.concatenate([v_ref[rows, sl].astype(BF16), ones_block], axis=1)
                lane0 = (d * N_HEADS + h) * GATE_LANES
                units.append(dict(u=d * N_HEADS + h, rows=rows, sl=sl, o_ref=o_ref, q=q, k=k, v_ext=v_ext,
                                  mask=mask, expo=expo[h * CHUNK:(h + 1) * CHUNK, :],
                                  s_old=s_old[:, lane0:lane0 + 1], qk=_nt(q.astype(BF16), k.astype(BF16))))
        for t in units:
            e = t.pop("expo")
            w = t.pop("qk") * jnp.exp(jnp.where(t.pop("mask"), e[:, 3 * LANES:3 * LANES + CHUNK], -jnp.inf))
            t["lhs2"] = jnp.concatenate([(t.pop("q") * jnp.exp(e[:, :LANES])).astype(BF16), w.astype(BF16)], axis=1)
            t["floor"] = jnp.exp(e[:, LANES:2 * LANES])
            t["upd"] = _tn((t.pop("k") * jnp.exp(e[:, 2 * LANES:3 * LANES])).astype(BF16), t["v_ext"])
        return units, m_next

    def finish(units):
        nums = [_mm(t["lhs2"], jnp.concatenate([c_ref[t["u"]].astype(BF16), t["v_ext"]], axis=0)) for t in units]
        for t, num in zip(units, nums):
            den = jnp.maximum(jnp.abs(num[:, HEAD_DV:]), t["floor"])
            t["o_ref"][t["rows"], t["sl"]] = (num[:, :HEAD_DV] / den).astype(BF16)
            c_ref[t["u"]] = t["s_old"] * c_ref[t["u"]] + t["upd"]

    m_state = [m_ref[d][0:1, :] for d in range(2)]
    pending, m_state = prepare(0, m_state)
    for c in range(n_chunks):
        following = None
        if c + 1 < n_chunks:
            following, m_state = prepare(c + 1, m_state)
        finish(pending)
        pending = following
    for d in range(2):
        m_ref[d] = jnp.broadcast_to(m_state[d], (SUBLANES, LANES))


def _scan_d(qd, kd, p3, fgb_row):
    bsz, t, _ = p3.shape
    nb = t // SCAN_ROWS
    const_rows = _gate_const_rows()

    def spec(j, w, flip):
        if flip:
            return pl.BlockSpec((None, SCAN_ROWS, w), lambda b, n: (b, nb - 1 - n, j))
        return pl.BlockSpec((None, SCAN_ROWS, w), lambda b, n: (b, n, j))

    out = jax.ShapeDtypeStruct((bsz, t, MIX_HALF), BF16)
    in_specs = []
    for flip in (False, True):
        in_specs += [spec(0, 512, flip), spec(0, 512, flip), spec(CD_VD, 512, flip),
                     spec(CD_GATE_I_BLOCK, LANES, flip), spec(CD_GATE_F_BLOCK, LANES, flip)]
    in_specs += [pl.BlockSpec((1, LANES), lambda b, n: (0, 0)), pl.BlockSpec(const_rows.shape, lambda b, n: (0, 0))]
    return pl.pallas_call(
        _scan_d_kernel,
        out_shape=(out, out),
        grid=(bsz, nb),
        in_specs=in_specs,
        out_specs=(spec(0, MIX_HALF, False), spec(0, MIX_HALF, True)),
        scratch_shapes=[pltpu.VMEM((2 * N_HEADS, HEAD_DV, 2 * HEAD_DV), F32),
                        pltpu.VMEM((2, SUBLANES, LANES), F32)],
        compiler_params=_params(2),
        name="scan_d",
    )(qd, kd, p3, p3, p3, qd, kd, p3, p3, p3, fgb_row, const_rows)


def _layer_norm(z, g, b):
    zc = z - jnp.mean(z, axis=-1, keepdims=True)
    var = jnp.mean(zc * zc, axis=-1, keepdims=True)
    return zc * lax.rsqrt(var + LN_EPS) * g + b


def _mix_out_kernel(o1f, o1b, o2f, o2b, g1, g2, x_ref, norm_ref, w_ref, lng, lnb, h_ref, hb_ref):
    parts = []
    for of, ob, g_ref, base in ((o1f, o1b, g1, 0), (o2f, o2b, g2, MIX_HALF)):
        o = of[...].astype(F32) + ob[...].astype(F32)
        gate = g_ref[...]
        for h in range(N_HEADS):
            sl = slice(h * HEAD_DV, (h + 1) * HEAD_DV)
            oh = o[:, sl]
            oh = oh * lax.rsqrt(jnp.mean(oh * oh, axis=-1, keepdims=True) + NORM_EPS)
            parts.append((oh * norm_ref[:, base + h * HEAD_DV:base + (h + 1) * HEAD_DV] * gate[:, sl]).astype(BF16))
    mixed = jnp.concatenate(parts, axis=1)
    z = ALPHA * x_ref[...] + _mm(mixed, w_ref[...])
    hn = _layer_norm(z, lng[...], lnb[...])
    h_ref[...] = hn
    hb_ref[...] = hn.astype(BF16)


def _mix_out(o1f, o1b, o2f, o2b, p2d, gate_blocks, x2d, norm, w_out, ln_g, ln_b):
    n = x2d.shape[0]
    half = pl.BlockSpec((WIDE_TILE, MIX_HALF), lambda i: (i, 0))
    full = pl.BlockSpec((WIDE_TILE, D_MODEL), lambda i: (i, 0))
    row = pl.BlockSpec((1, D_MODEL), lambda i: (0, 0))
    ga, gb = gate_blocks
    return pl.pallas_call(
        _mix_out_kernel,
        out_shape=(jax.ShapeDtypeStruct((n, D_MODEL), F32), jax.ShapeDtypeStruct((n, D_MODEL), BF16)),
        grid=(n // WIDE_TILE,),
        in_specs=[half, half, half, half,
                  pl.BlockSpec((WIDE_TILE, MIX_HALF), lambda i: (i, ga)),
                  pl.BlockSpec((WIDE_TILE, MIX_HALF), lambda i: (i, gb)),
                  full, row, pl.BlockSpec((D_MODEL, D_MODEL), lambda i: (0, 0)), row, row],
        out_specs=(full, full),
        compiler_params=_params(1),
        name="mix_out",
    )(o1f, o1b, o2f, o2b, p2d, p2d, x2d, norm, w_out, ln_g, ln_b)


def _route(h, rw_ref, rb_ref, idx_ref, gate_ref):
    rw = rw_ref[...]
    rw_hi = rw.astype(BF16)
    rw_r = rw - rw_hi.astype(F32)
    rw_mid = rw_r.astype(BF16)
    rw3 = jnp.concatenate([rw_hi, rw_mid, (rw_r - rw_mid.astype(F32)).astype(BF16)], axis=0)
    h_hi = h.astype(BF16)
    h_lo = (h - h_hi.astype(F32)).astype(BF16)
    terms = _nt(rw3, h_hi) + _nt(rw3, h_lo)
    logits = terms[:N_EXPERTS] + terms[N_EXPERTS:2 * N_EXPERTS] + terms[2 * N_EXPERTS:]
    rows = [logits[e:e + 1, :] for e in range(N_EXPERTS)]
    mx = functools.reduce(jnp.maximum, rows)
    ex = [jnp.exp(r - mx) for r in rows]
    tot = functools.reduce(lambda a, b: a + b, ex)
    score = [e / tot for e in ex]
    biased = [score[e] + rb_ref[e:e + 1, :] for e in range(N_EXPERTS)]

    def argmax_first(vals):
        best, idx = vals[0], jnp.zeros(vals[0].shape, jnp.int32)
        for j in range(1, len(vals)):
            upd = vals[j] > best
            best = jnp.where(upd, vals[j], best)
            idx = jnp.where(upd, j, idx)
        return best, idx

    def pick(vals, idx):
        out = vals[0]
        for j in range(1, len(vals)):
            out = jnp.where(idx == j, vals[j], out)
        return out

    group_scores = []
    for g in range(N_GROUPS):
        a = biased[g * EXPERTS_PER_GROUP:(g + 1) * EXPERTS_PER_GROUP]
        pairs = [a[i] + a[j] for i in range(EXPERTS_PER_GROUP) for j in range(i + 1, EXPERTS_PER_GROUP)]
        group_scores.append(functools.reduce(jnp.maximum, pairs))
    _, g_sel = argmax_first(group_scores)
    in_b = [pick([biased[g * EXPERTS_PER_GROUP + k] for g in range(N_GROUPS)], g_sel) for k in range(EXPERTS_PER_GROUP)]
    in_s = [pick([score[g * EXPERTS_PER_GROUP + k] for g in range(N_GROUPS)], g_sel) for k in range(EXPERTS_PER_GROUP)]
    _, i0 = argmax_first(in_b)
    _, i1 = argmax_first([jnp.where(i0 == k, -jnp.inf, in_b[k]) for k in range(EXPERTS_PER_GROUP)])
    s0, s1 = pick(in_s, i0), pick(in_s, i1)
    den = s0 + s1
    idx_ref[0:1, :] = g_sel * EXPERTS_PER_GROUP + i0
    idx_ref[1:2, :] = g_sel * EXPERTS_PER_GROUP + i1
    gate_ref[0:1, :] = s0 / den
    gate_ref[1:2, :] = s1 / den


def _router_kernel(h_ref, rw_ref, rb_ref, idx_ref, gate_ref):
    _route(h_ref[...], rw_ref, rb_ref, idx_ref, gate_ref)


def _router(h2d, rw_t, rb_col):
    n = h2d.shape[0]
    tile = min(ROUTER_TILE, n)
    routed = pl.BlockSpec((TOP_K, tile), lambda i: (0, i))
    return pl.pallas_call(
        _router_kernel,
        out_shape=(jax.ShapeDtypeStruct((TOP_K, n), jnp.int32), jax.ShapeDtypeStruct((TOP_K, n), F32)),
        grid=(n // tile,),
        in_specs=[pl.BlockSpec((tile, D_MODEL), lambda i: (i, 0)),
                  pl.BlockSpec(rw_t.shape, lambda i: (0, 0)),
                  pl.BlockSpec((N_EXPERTS, 1), lambda i: (0, 0))],
        out_specs=(routed, routed),
        compiler_params=_params(1),
        name="router",
    )(h2d, rw_t, rb_col)


def _plan_kernel(idx_ref, dest_ref, cnt_ref, su_ref, run_ref, start_ref):
    phase, i = pl.program_id(0), pl.program_id(1)
    tile = idx_ref.shape[1]
    e_iota = lax.broadcasted_iota(jnp.int32, (N_EXPERTS, tile), 0)
    oh0 = idx_ref[0:1, :] == e_iota
    oh1 = idx_ref[1:2, :] == e_iota
    member = jnp.where(oh0 | oh1, 1.0, 0.0)
    tile_count = jnp.broadcast_to(jnp.sum(member, axis=1, keepdims=True), (N_EXPERTS, LANES))

    @pl.when((phase == 0) & (i == 0))
    def _():
        run_ref[...] = jnp.zeros_like(run_ref)
        r = lax.broadcasted_iota(jnp.int32, (tile, tile), 0)
        c = lax.broadcasted_iota(jnp.int32, (tile, tile), 1)
        su_ref[...] = jnp.where(r < c, 1.0, 0.0).astype(BF16)

    @pl.when((phase == 1) & (i == 0))
    def _():
        counts = run_ref[...]
        cnt_ref[...] = counts.astype(jnp.int32)
        padded = jnp.ceil(counts * (1.0 / EXPERT_ROWS)) * EXPERT_ROWS
        r = lax.broadcasted_iota(jnp.int32, (N_EXPERTS, N_EXPERTS), 0)
        c = lax.broadcasted_iota(jnp.int32, (N_EXPERTS, N_EXPERTS), 1)
        start_ref[...] = jnp.dot(jnp.where(c < r, 1.0, 0.0), padded, precision=HIGHEST, preferred_element_type=F32)
        run_ref[...] = jnp.zeros_like(run_ref)

    @pl.when(phase == 1)
    def _():
        before = _mm(member.astype(BF16), su_ref[...])
        pos = start_ref[:, 0:1] + run_ref[:, 0:1] + before
        dest_ref[0:1, :] = jnp.sum(jnp.where(oh0, pos, 0.0), axis=0, keepdims=True).astype(jnp.int32)
        dest_ref[1:2, :] = jnp.sum(jnp.where(oh1, pos, 0.0), axis=0, keepdims=True).astype(jnp.int32)

    run_ref[...] += tile_count


def _plan(idx):
    n = idx.shape[1]
    tile = min(PLAN_TILE, n)
    return pl.pallas_call(
        _plan_kernel,
        out_shape=(jax.ShapeDtypeStruct((TOP_K, n), jnp.int32), jax.ShapeDtypeStruct((N_EXPERTS, LANES), jnp.int32)),
        grid=(2, n // tile),
        in_specs=[pl.BlockSpec((TOP_K, tile), lambda p, i: (0, i))],
        out_specs=(pl.BlockSpec((TOP_K, tile), lambda p, i: (0, i * p)),
                   pl.BlockSpec((N_EXPERTS, LANES), lambda p, i: (0, 0))),
        scratch_shapes=[pltpu.VMEM((tile, tile), BF16), pltpu.VMEM((N_EXPERTS, LANES), F32),
                        pltpu.VMEM((N_EXPERTS, LANES), F32)],
        compiler_params=_params(2),
        name="moe_plan",
    )(idx)


def _invert_kernel(pad_ref, dest_ref, tok_ref):
    n = dest_ref.shape[0] // TOP_K
    n_rows = tok_ref.shape[0]

    def filler(r):
        return r - jnp.where(r >= 2 * n, 2 * n, jnp.where(r >= n, n, 0))

    for e in range(N_EXPERTS):
        def clear(i, carry, lo=pad_ref[0, e], hi=pad_ref[1, e]):
            r = jnp.minimum(jnp.maximum(hi - 1 - i, lo), n_rows - 1)
            tok_ref[r] = filler(r)
            return carry

        lax.fori_loop(0, EXPERT_ROWS, clear, 0, unroll=16)

    def clear_tail(r, carry):
        tok_ref[r] = filler(r)
        return carry

    lax.fori_loop(pad_ref[0, N_EXPERTS], pad_ref[1, N_EXPERTS], clear_tail, 0)

    def place(t, carry):
        for k in range(TOP_K):
            tok_ref[dest_ref[k * n + t]] = t
        return carry

    lax.fori_loop(0, n, place, 0, unroll=16)


def _invert(pad_ranges, dest_flat, n_rows):
    smem = pl.BlockSpec(memory_space=pltpu.SMEM)
    return pl.pallas_call(
        _invert_kernel,
        out_shape=jax.ShapeDtypeStruct((n_rows,), jnp.int32),
        in_specs=[smem, smem],
        out_specs=smem,
        name="moe_invert",
    )(pad_ranges, dest_flat)


def _expert_kernel(be_ref, nv_ref, x_ref, w1_ref, w3_ref, w2_ref, y_ref, wb_ref):
    i = pl.program_id(0)
    valid = i < nv_ref[0]
    new_expert = (i == 0) | (be_ref[i] != be_ref[jnp.maximum(i - 1, 0)])

    @pl.when(valid & new_expert)
    def _():
        wb_ref[0] = w1_ref[...].astype(BF16)
        wb_ref[1] = w3_ref[...].astype(BF16)
        wb_ref[2] = w2_ref[...].astype(BF16)

    @pl.when(valid)
    def _():
        x = x_ref[...]
        hid = _silu(_mm(x, wb_ref[0])) * _mm(x, wb_ref[1])
        y_ref[...] = _mm(hid.astype(BF16), wb_ref[2]).astype(BF16)

    @pl.when(jnp.logical_not(valid))
    def _():
        y_ref[...] = jnp.zeros_like(y_ref)


def _experts(block_expert, n_valid, xs, w1, w3, w2, layer):
    n_rows = xs.shape[0]
    wspec = pl.BlockSpec((None, None, D_MODEL, D_MODEL), lambda i, be, nv: (layer, be[i], 0, 0))
    return pl.pallas_call(
        _expert_kernel,
        out_shape=jax.ShapeDtypeStruct((n_rows, D_MODEL), BF16),
        grid_spec=pltpu.PrefetchScalarGridSpec(
            num_scalar_prefetch=2,
            grid=(n_rows // EXPERT_ROWS,),
            in_specs=[pl.BlockSpec((EXPERT_ROWS, D_MODEL), lambda i, be, nv: (i, 0)), wspec, wspec, wspec],
            out_specs=pl.BlockSpec((EXPERT_ROWS, D_MODEL), lambda i, be, nv: (i, 0)),
            scratch_shapes=[pltpu.VMEM((3, D_MODEL, D_MODEL), BF16)]),
        compiler_params=_params(1),
        name="experts",
    )(block_expert, n_valid, xs, w1, w3, w2)


def _combine_kernel(h_ref, y0_ref, y1_ref, g_ref, lng, lnb, o_ref):
    g = g_ref[...]
    z = ALPHA * h_ref[...] + (g[:, 0:1] * y0_ref[...].astype(F32) + g[:, 1:2] * y1_ref[...].astype(F32))
    o_ref[...] = _layer_norm(z, lng[...], lnb[...])


def _combine(h2d, y_rows, gates_t, ln_g, ln_b):
    n = h2d.shape[0]
    tile = min(COMBINE_TILE, n)
    nt = n // tile
    full = pl.BlockSpec((tile, D_MODEL), lambda i: (i, 0))
    row = pl.BlockSpec((1, D_MODEL), lambda i: (0, 0))
    return pl.pallas_call(
        _combine_kernel,
        out_shape=jax.ShapeDtypeStruct((n, D_MODEL), F32),
        grid=(nt,),
        in_specs=[full, full, pl.BlockSpec((tile, D_MODEL), lambda i: (nt + i, 0)),
                  pl.BlockSpec((tile, TOP_K), lambda i: (i, 0)), row, row],
        out_specs=full,
        compiler_params=_params(1),
        name="moe_combine",
    )(h2d, y_rows, y_rows, gates_t, ln_g, ln_b)


def _moe(h2d, h_bf16, idx, gates, w1, w3, w2, layer, ln_g, ln_b):
    n = h2d.shape[0]
    dest, counts = _plan(idx)
    n_rows = n * TOP_K + N_EXPERTS * EXPERT_ROWS
    n_blocks = n_rows // EXPERT_ROWS
    counts = counts[:, 0]
    padded = (counts + EXPERT_ROWS - 1) // EXPERT_ROWS * EXPERT_ROWS
    pad_end = jnp.cumsum(padded)
    block_start = jnp.arange(n_blocks, dtype=jnp.int32) * EXPERT_ROWS
    block_expert = jnp.minimum(jnp.sum((pad_end[None, :] <= block_start[:, None]).astype(jnp.int32), axis=1),
                               N_EXPERTS - 1)
    n_valid = (pad_end[-1:] // EXPERT_ROWS).astype(jnp.int32)
    pad_ranges = jnp.stack([jnp.concatenate([pad_end - padded + counts, pad_end[-1:]]),
                            jnp.concatenate([pad_end, jnp.full((1,), n_rows, jnp.int32)])]).astype(jnp.int32)
    dest_flat = dest.reshape(-1)
    xs = h_bf16[_invert(pad_ranges, dest_flat, n_rows)]
    y = _experts(block_expert, n_valid, xs, w1, w3, w2, layer)
    return _combine(h2d, y[dest_flat], gates.T, ln_g, ln_b)


def _pad_cols(w, b, main, width):
    k = w.shape[0]
    tail = w.shape[1] - main
    w_p = jnp.concatenate([w[:, :main], w[:, main:], jnp.zeros((k, width - main - tail), w.dtype)], axis=1)
    b_p = jnp.concatenate([b[:main], b[main:], jnp.zeros((width - main - tail,), b.dtype)])
    return w_p.astype(BF16), b_p[None, :].astype(F32)


def kernel(x, ab_w_in, ab_b_in, hgrn_lb, gla_gk_up, gla_gk_b, hgrn_norm, gla_norm, ab_w_out, cd_w_in, cd_b_in,
           mlstm_conv_w, mlstm_conv_b, mlstm_fgate_b, ret_norm, mlstm_norm, cd_w_out, ln_mix_g, ln_mix_b, ln_ffn_g,
           ln_ffn_b, router_w, router_b, moe_w1, moe_w3, moe_w2):
    bsz, t, d = x.shape
    n = bsz * t
    assert d == D_MODEL and t % SCAN_ROWS == 0 and t % min(CONV_TILE, t) == 0, (bsz, t, d)
    assert n % ROW_TILE == 0 and n % WIDE_TILE == 0 and n % min(ROUTER_TILE, n) == 0 and n % min(PLAN_TILE, n) == 0
    assert n % min(COMBINE_TILE, n) == 0
    lower_bounds = jnp.cumsum(jax.nn.softmax(hgrn_lb.astype(F32), axis=0), axis=0)
    rw_t = router_w.T.astype(F32)
    rb_col = router_b.astype(F32)[:, None]
    h = x.reshape(n, d)
    for layer in range(DEPTH):
        j = layer // 2
        if layer % 2 == 0:
            w_p, b_p = _pad_cols(ab_w_in[j], ab_b_in[j], 8 * 512, AB_WIDTH)
            lb = lower_bounds[layer][None, :]
            p = _in_proj(h, w_p, b_p, _proj_ab_kernel, [lb], [pl.BlockSpec(lb.shape, lambda i: (0, 0))])
            p3 = p.reshape(bsz, t, AB_WIDTH)
            nk = N_HEADS * NARROW_DK
            u_pad = jnp.zeros((LANES, 2 * nk), F32)
            u_pad = u_pad.at[:GLA_LOWRANK, :nk].set(gla_gk_up[j, 0])
            u_pad = u_pad.at[GLA_LOWRANK:2 * GLA_LOWRANK, nk:].set(gla_gk_up[j, 1]).astype(BF16)
            gkb = jnp.concatenate([gla_gk_b[j, 0], gla_gk_b[j, 1]])[None, :].astype(F32)
            oa_f, ob_f, oa_b, ob_b = _scan_ab(p3, u_pad, gkb)
            norm = jnp.concatenate([hgrn_norm[j], gla_norm[j]])[None, :].astype(F32)
            outs = [a.reshape(n, MIX_HALF) for a in (oa_f, oa_b, ob_f, ob_b)]
            h, h_bf16 = _mix_out(*outs, p, (AB_GA, AB_GB), h, norm, ab_w_out[j].astype(BF16),
                                 ln_mix_g[layer][None, :], ln_mix_b[layer][None, :])
        else:
            lane_dir, lane_head, _, _ = _gate_lane_layout()
            col_i = 7 * 512 + lane_dir * 2 * N_HEADS + lane_head
            cols = np.concatenate([np.arange(7 * 512), col_i, col_i + N_HEADS])
            w_p, b_p = cd_w_in[j][:, cols].astype(BF16), cd_b_in[j][cols][None, :].astype(F32)
            pos = pl.BlockSpec((ROW_TILE, LANES), lambda i: (i % (t // ROW_TILE), 0))
            p = _in_proj(h, w_p, b_p, _proj_cd_kernel, list(_rotary_tables(t)), [pos, pos])
            p3 = p.reshape(bsz, t, CD_WIDTH)
            oc_f, oc_b = _scan_c(p3)
            qd, kd = _conv_silu(p3, mlstm_conv_w[j].astype(F32), mlstm_conv_b[j].astype(F32))
            fgb_row = mlstm_fgate_b[j].astype(F32)[lane_dir, lane_head][None, :]
            od_f, od_b = _scan_d(qd, kd, p3, fgb_row)
            norm = jnp.concatenate([ret_norm[j], mlstm_norm[j]])[None, :].astype(F32)
            outs = [a.reshape(n, MIX_HALF) for a in (oc_f, oc_b, od_f, od_b)]
            h, h_bf16 = _mix_out(*outs, p, (CD_GC, CD_OD), h, norm, cd_w_out[j].astype(BF16),
                                 ln_mix_g[layer][None, :], ln_mix_b[layer][None, :])
        idx, gates = _router(h, rw_t, rb_col)
        h = _moe(h, h_bf16, idx, gates, moe_w1, moe_w3, moe_w2, layer,
                 ln_ffn_g[layer][None, :], ln_ffn_b[layer][None, :])
    return h.reshape(bsz, t, d)
```

```python
import functools

import jax
import jax.numpy as jnp
import numpy as np
from jax import lax
from jax.experimental import pallas as pl
from jax.experimental.pallas import tpu as pltpu

F32 = jnp.float32
BF16 = jnp.bfloat16
HIGHEST = lax.Precision.HIGHEST

D_MODEL = 1024
DEPTH = 2
CHUNK = 64
MIX_HALF = D_MODEL // 2
N_HEADS = 4
HEAD_DV = MIX_HALF // N_HEADS
NARROW_DK = 64
GLA_LOWRANK = 16
GLA_GATE_NORMALIZER = 16.0
ROPE_THETA = 10000.0
N_EXPERTS = 16
N_GROUPS = 4
EXPERTS_PER_GROUP = N_EXPERTS // N_GROUPS
TOP_K = 2
ALPHA = (2.0 * DEPTH) ** 0.25
LN_EPS = 1e-5
NORM_EPS = 1e-6

LANES = 128
SUBLANES = 8
VMEM_LIMIT = 52 * 1024 * 1024

ROW_TILE = 512
WIDE_TILE = 512
COMBINE_TILE = 1024
CONV_TILE = 2048
SCAN_ROWS = 512
ROUTER_TILE = 2048
PLAN_TILE = 1024
EXPERT_ROWS = 512

AB_QA, AB_FF, AB_FB, AB_IA, AB_GA, AB_QKB, AB_VB, AB_GB = range(8)
AB_LR_BLOCK = 32
AB_WIDTH = 8 * 512 + LANES
CD_QKC, CD_VC, CD_GC, CD_QD, CD_KD, CD_VD, CD_OD = range(7)
CD_GATE_I_BLOCK = 28
CD_GATE_F_BLOCK = 29
CD_WIDTH = 7 * 512 + 2 * LANES
GATE_LANES = 16


def _params(n_axes, vmem=VMEM_LIMIT):
    return pltpu.CompilerParams(dimension_semantics=("arbitrary",) * n_axes, vmem_limit_bytes=vmem)


def _nt(a, b):
    return lax.dot_general(a, b, (((1,), (1,)), ((), ())), preferred_element_type=F32)


def _tn(a, b):
    return lax.dot_general(a, b, (((0,), (0,)), ((), ())), preferred_element_type=F32)


def _mm(a, b):
    return jnp.dot(a, b, preferred_element_type=F32)


def _sigmoid(x):
    return 1.0 / (1.0 + jnp.exp(-x))


def _silu(x):
    return x * _sigmoid(x)


def _log_sigmoid(x):
    return jnp.minimum(x, 0.0) - jnp.log(1.0 + jnp.exp(-jnp.abs(x)))


def _tri(direction):
    row = lax.broadcasted_iota(jnp.int32, (CHUNK, CHUNK), 0)
    col = lax.broadcasted_iota(jnp.int32, (CHUNK, CHUNK), 1)
    return (col <= row) if direction == 0 else (col >= row)


def _ref_row(direction):
    return CHUNK // 2 - 1 if direction == 0 else CHUNK // 2


def _last_row(direction):
    return CHUNK - 1 if direction == 0 else 0


def _proj_ab_kernel(x_ref, w_ref, b_ref, lb_ref, o_ref, g_ref):
    p = _mm(x_ref[...].astype(BF16), w_ref[...]) + b_ref[...]
    o_ref[:, :512] = _silu(p[:, :512])
    lb = jnp.concatenate([lb_ref[...], lb_ref[...]], axis=1)
    o_ref[:, 512:1536] = lb + (1.0 - lb) * _sigmoid(p[:, 512:1536])
    o_ref[:, 1536:] = p[:, 1536:]
    for half, blk in enumerate((AB_GA, AB_GB)):
        g_ref[:, half * 512:(half + 1) * 512] = _silu(p[:, blk * 512:(blk + 1) * 512]).astype(BF16)


def _proj_cd_kernel(x_ref, w_ref, b_ref, cos_ref, sin_ref, o_ref, g_ref):
    p = _mm(x_ref[...].astype(BF16), w_ref[...]) + b_ref[...]
    cos, sin_signed = cos_ref[...], sin_ref[...]
    for j in range(4):
        rot = _rotary(p[:, j * LANES:(j + 1) * LANES], cos, sin_signed)
        o_ref[:, j * LANES:(j + 1) * LANES] = rot * (NARROW_DK ** -0.5) if j < 2 else rot
    o_ref[:, 512:] = p[:, 512:]
    g_ref[:, :512] = _silu(p[:, CD_GC * 512:(CD_GC + 1) * 512]).astype(BF16)
    g_ref[:, 512:] = _sigmoid(p[:, CD_OD * 512:(CD_OD + 1) * 512]).astype(BF16)


def _in_proj(x2d, w, b, body, extras, extra_specs):
    n, k = x2d.shape
    m = w.shape[1]
    return pl.pallas_call(
        body,
        out_shape=(jax.ShapeDtypeStruct((n, m), F32), jax.ShapeDtypeStruct((n, D_MODEL), BF16)),
        grid=(n // ROW_TILE,),
        in_specs=[pl.BlockSpec((ROW_TILE, k), lambda i: (i, 0)),
                  pl.BlockSpec((k, m), lambda i: (0, 0)),
                  pl.BlockSpec((1, m), lambda i: (0, 0))] + extra_specs,
        out_specs=(pl.BlockSpec((ROW_TILE, m), lambda i: (i, 0)), pl.BlockSpec((ROW_TILE, D_MODEL), lambda i: (i, 0))),
        compiler_params=_params(1),
        name="in_proj",
    )(x2d, w, b, *extras)


def _pipelined_chunks(n_chunks, prepare, finish):
    pending = prepare(0)
    for c in range(n_chunks):
        following = prepare(c + 1) if c + 1 < n_chunks else None
        finish(pending)
        pending = following


def _units_prepare(units):
    return _units_prepare_staggered([lambda: units])


def _units_prepare_staggered(builders):
    def second(group):
        for u in group:
            u["lhs"] = jnp.concatenate([u["qi"], u["intra"](u.pop("scores")).astype(BF16)], axis=1)

    done, previous = [], None
    for build in builders:
        group = build()
        for u in group:
            u["scores"] = _nt(u["qe"], u["ke"])
        for u in group:
            u["v_t"] = u["v"].T
            u["upd"] = _mm(u["v_t"], u["ks"])
        if previous is not None:
            second(previous)
        previous = group
        done += group
    second(previous)
    return done


def _units_finish(units, st_ref, group=4):
    def close(batch):
        for u in batch:
            u["store"](u.pop("o").astype(BF16))
            st_ref[u["idx"]] = st_ref[u["idx"]] * u["dec"] + u["upd"]

    previous = None
    for g in range(0, len(units), group):
        batch = units[g:g + group]
        for u in batch:
            u["o"] = _nt(u["lhs"], jnp.concatenate([st_ref[u["idx"]].astype(BF16), u["v_t"]], axis=1))
        if previous is not None:
            close(previous)
        previous = batch
    close(previous)


def _chunk_cumsum(parts):
    cat = jnp.concatenate([x for x, _ in parts], axis=1)
    hi = cat.astype(BF16)
    lo = (cat - hi.astype(F32)).astype(BF16)
    width = cat.shape[1]
    inc = _mm(_tri(0).astype(BF16), jnp.concatenate([hi, lo], axis=1))
    inc = inc[:, :width] + inc[:, width:]
    out, off = [], 0
    for x, direction in parts:
        p = inc[:, off:off + x.shape[1]]
        off += x.shape[1]
        out.append(p if direction == 0 else p[CHUNK - 1:CHUNK, :] - p + x)
    return out


def _gated_operands(q, k, gc, direction):
    r, l = _ref_row(direction), _last_row(direction)
    g_ref = gc[r:r + 1, :]
    g_last = gc[l:l + 1, :]
    qe = q * jnp.exp(gc - g_ref)
    ke = k * jnp.exp(g_ref - gc)
    qi = qe * jnp.exp(g_ref)
    ks = ke * jnp.exp(g_last - g_ref)
    return [a.astype(BF16) for a in (qe, ke, qi, ks)], jnp.exp(g_last)


def _narrow_head_mask(h):
    lane = lax.broadcasted_iota(jnp.int32, (1, LANES), 1)
    return (lane // NARROW_DK) == (h % 2)


def _scan_ab_kernel(qa_f, fa_f, ia_f, qkb_f, vb_f, lr_f, qa_b, fa_b, ia_b, qkb_b, vb_b, lr_b,
                    u_ref, gkb_ref, oa_f, ob_f, oa_b, ob_b, st_ref):
    @pl.when(pl.program_id(1) == 0)
    def _():
        st_ref[...] = jnp.zeros_like(st_ref)

    n_chunks = SCAN_ROWS // CHUNK
    dirs = ((qa_f, fa_f, ia_f, qkb_f, vb_f, lr_f, oa_f, ob_f), (qa_b, fa_b, ia_b, qkb_b, vb_b, lr_b, oa_b, ob_b))

    nk = N_HEADS * NARROW_DK

    def prepare(c):
        pre = []
        for d, (qa, fa, ia, qkb, vb, lr, oa, ob) in enumerate(dirs):
            rows = pl.ds((c if d == 0 else n_chunks - 1 - c) * CHUNK, CHUNK)
            f = fa[rows, :]
            logits = _mm(lr[rows, :].astype(BF16), u_ref[:, d * nk:(d + 1) * nk]) + gkb_ref[:, d * nk:(d + 1) * nk]
            qk = qkb[rows, :]
            pre.append(dict(rows=rows, f=f, lf_a=jnp.log(f), lf_b=_log_sigmoid(logits) / GLA_GATE_NORMALIZER,
                            q_a=qa[rows, :], v_a=ia[rows, :].astype(BF16),
                            q_b=qk[:, :nk] * (NARROW_DK ** -0.5), k_b=qk[:, nk:], v_b=vb[rows, :].astype(BF16)))
        gcs = _chunk_cumsum([(pre[0]["lf_a"], 0), (pre[0]["lf_b"], 0), (pre[1]["lf_a"], 1), (pre[1]["lf_b"], 1)])
        def hgrn_group(d, oa):
            p, mask = pre[d], _tri(d)
            rows = p["rows"]
            (qe, ke, qi, ks), dec = _gated_operands(p["q_a"], 1.0 - p["f"], gcs[2 * d], d)
            group = []
            for h in range(N_HEADS):
                sl = slice(h * LANES, (h + 1) * LANES)

                def store(o, sl=sl):
                    oa[rows, sl] = o

                group.append(dict(qe=qe[:, sl], ke=ke[:, sl], qi=qi[:, sl], ks=ks[:, sl], v=p["v_a"][:, sl],
                                  dec=dec[:, sl], intra=lambda s: jnp.where(mask, s, 0.0),
                                  idx=d * 2 * N_HEADS + h, store=store))
            return group

        def gla_group(d, ob):
            p, mask = pre[d], _tri(d)
            rows = p["rows"]
            (qe, ke, qi, ks), dec = _gated_operands(p["q_b"], p["k_b"], gcs[2 * d + 1], d)
            group = []
            for h in range(N_HEADS):
                psl = slice((h // 2) * LANES, (h // 2 + 1) * LANES)
                hm = _narrow_head_mask(h)
                pick = lambda a, psl=psl, hm=hm: jnp.where(hm, a[:, psl], jnp.zeros((), BF16))
                vsl = slice(h * HEAD_DV, (h + 1) * HEAD_DV)

                def store(o, vsl=vsl):
                    ob[rows, vsl] = o

                group.append(dict(qe=pick(qe), ke=pick(ke), qi=pick(qi), ks=pick(ks), v=p["v_b"][:, vsl],
                                  dec=dec[:, psl], intra=lambda s: jnp.where(mask, s, 0.0),
                                  idx=d * 2 * N_HEADS + N_HEADS + h, store=store))
            return group

        builders = []
        for d, (qa, fa, ia, qkb, vb, lr, oa, ob) in enumerate(dirs):
            builders += [functools.partial(hgrn_group, d, oa), functools.partial(gla_group, d, ob)]
        return _units_prepare_staggered(builders)

    _pipelined_chunks(n_chunks, prepare, functools.partial(_units_finish, st_ref=st_ref))


def _scan_ab(p3, u_pad, gkb):
    bsz, t, _ = p3.shape
    nb = t // SCAN_ROWS

    def fwd(j, w=512):
        return pl.BlockSpec((None, SCAN_ROWS, w), lambda b, n: (b, n, j))

    def bwd(j, w=512):
        return pl.BlockSpec((None, SCAN_ROWS, w), lambda b, n: (b, nb - 1 - n, j))

    def const(shape):
        return pl.BlockSpec(shape, lambda b, n: (0,) * len(shape))

    out = jax.ShapeDtypeStruct((bsz, t, MIX_HALF), BF16)
    o_f = pl.BlockSpec((None, SCAN_ROWS, MIX_HALF), lambda b, n: (b, n, 0))
    o_b = pl.BlockSpec((None, SCAN_ROWS, MIX_HALF), lambda b, n: (b, nb - 1 - n, 0))
    return pl.pallas_call(
        _scan_ab_kernel,
        out_shape=(out, out, out, out),
        grid=(bsz, nb),
        in_specs=[fwd(AB_QA), fwd(AB_FF), fwd(AB_IA), fwd(AB_QKB), fwd(AB_VB), fwd(AB_LR_BLOCK, LANES),
                  bwd(AB_QA), bwd(AB_FB), bwd(AB_IA), bwd(AB_QKB), bwd(AB_VB), bwd(AB_LR_BLOCK, LANES),
                  const(u_pad.shape), const(gkb.shape)],
        out_specs=(o_f, o_f, o_b, o_b),
        scratch_shapes=[pltpu.VMEM((4 * N_HEADS, HEAD_DV, LANES), F32)],
        compiler_params=_params(2),
        name="scan_ab",
    )(*([p3] * 12), u_pad, gkb)


def _rotary(a, cos, sin_signed):
    lane = lax.broadcasted_iota(jnp.int32, (1, LANES), 1)
    first_half = (lane % NARROW_DK) < (NARROW_DK // 2)
    swapped = jnp.where(first_half, pltpu.roll(a, LANES - NARROW_DK // 2, 1), pltpu.roll(a, NARROW_DK // 2, 1))
    return a * cos + swapped * sin_signed


def _scan_c_kernel(qk_f, v_f, qk_b, v_b, dmat_ref, qsc_ref, ksc_ref, cd_ref, o_f, o_b, st_ref):
    @pl.when(pl.program_id(1) == 0)
    def _():
        st_ref[...] = jnp.zeros_like(st_ref)

    n_chunks = SCAN_ROWS // CHUNK
    dirs = ((qk_f, v_f, o_f), (qk_b, v_b, o_b))

    def prepare(c):
        units = []
        for d, (qk_ref, v_ref, o_ref) in enumerate(dirs):
            rows = pl.ds((c if d == 0 else n_chunks - 1 - c) * CHUNK, CHUNK)
            v = v_ref[rows, :].astype(BF16)
            for h in range(N_HEADS):
                hm = _narrow_head_mask(h)
                q = jnp.where(hm, qk_ref[rows, (h // 2) * LANES:(h // 2 + 1) * LANES], 0.0)
                k = jnp.where(hm, qk_ref[rows, (2 + h // 2) * LANES:(3 + h // 2) * LANES], 0.0)
                vsl = slice(h * HEAD_DV, (h + 1) * HEAD_DV)

                def store(o, o_ref=o_ref, rows=rows, vsl=vsl):
                    o_ref[rows, vsl] = o

                units.append(dict(qe=q.astype(BF16), ke=k.astype(BF16), qi=(q * qsc_ref[d, h]).astype(BF16),
                                  ks=(k * ksc_ref[d, h]).astype(BF16), v=v[:, vsl], dec=cd_ref[d, h][0:1, :],
                                  intra=lambda s, d=d, h=h: s * dmat_ref[d, h], idx=d * N_HEADS + h, store=store))
        return _units_prepare(units)

    _pipelined_chunks(n_chunks, prepare, functools.partial(_units_finish, st_ref=st_ref))


def _retention_tables():
    pos = np.arange(CHUNK, dtype=np.float64)
    dmat = np.zeros((2, N_HEADS, CHUNK, CHUNK), np.float32)
    qsc = np.zeros((2, N_HEADS, CHUNK, LANES), np.float32)
    ksc = np.zeros((2, N_HEADS, CHUNK, LANES), np.float32)
    cd = np.zeros((2, N_HEADS, SUBLANES, LANES), np.float32)
    for d in range(2):
        for h in range(N_HEADS):
            lg = np.log1p(-np.exp2(-(5.0 + 2.0 * h + d)))
            dist = pos[:, None] - pos[None, :]
            if d == 0:
                dmat[d, h] = np.where(dist >= 0, np.exp(np.maximum(dist, 0.0) * lg), 0.0)
                qsc[d, h] = np.exp((pos + 1.0) * lg)[:, None]
                ksc[d, h] = np.exp((CHUNK - 1.0 - pos) * lg)[:, None]
            else:
                dmat[d, h] = np.where(dist <= 0, np.exp(np.maximum(-dist, 0.0) * lg), 0.0)
                qsc[d, h] = np.exp((CHUNK - pos) * lg)[:, None]
                ksc[d, h] = np.exp(pos * lg)[:, None]
            cd[d, h] = np.exp(CHUNK * lg)
    return jnp.asarray(dmat), jnp.asarray(qsc), jnp.asarray(ksc), jnp.asarray(cd)


def _rotary_tables(t):
    half = NARROW_DK // 2
    freqs = ROPE_THETA ** (-jnp.arange(half, dtype=F32) / half)
    ang = jnp.arange(t, dtype=F32)[:, None] * freqs[None, :]
    cos, sin = jnp.cos(ang), jnp.sin(ang)
    cos_t = jnp.tile(cos, (1, LANES // half))
    sin_t = jnp.tile(jnp.concatenate([-sin, sin], axis=1), (1, LANES // NARROW_DK))
    return cos_t.astype(F32), sin_t.astype(F32)


def _scan_c(p3):
    bsz, t, _ = p3.shape
    nb = t // SCAN_ROWS
    tables = _retention_tables()

    def fwd(j):
        return pl.BlockSpec((None, SCAN_ROWS, 512), lambda b, n: (b, n, j))

    def bwd(j):
        return pl.BlockSpec((None, SCAN_ROWS, 512), lambda b, n: (b, nb - 1 - n, j))

    def const(shape):
        return pl.BlockSpec(shape, lambda b, n: (0,) * len(shape))

    out = jax.ShapeDtypeStruct((bsz, t, MIX_HALF), BF16)
    return pl.pallas_call(
        _scan_c_kernel,
        out_shape=(out, out),
        grid=(bsz, nb),
        in_specs=[fwd(CD_QKC), fwd(CD_VC), bwd(CD_QKC), bwd(CD_VC)] + [const(a.shape) for a in tables],
        out_specs=(pl.BlockSpec((None, SCAN_ROWS, MIX_HALF), lambda b, n: (b, n, 0)),
                   pl.BlockSpec((None, SCAN_ROWS, MIX_HALF), lambda b, n: (b, nb - 1 - n, 0))),
        scratch_shapes=[pltpu.VMEM((2 * N_HEADS, HEAD_DV, LANES), F32)],
        compiler_params=_params(2),
        name="scan_c",
    )(p3, p3, p3, p3, *tables)


def _conv_kernel(prev_ref, cur_ref, next_ref, w_ref, b_ref, o_ref, buf_ref):
    i = pl.program_id(1)
    last = pl.num_programs(1) - 1
    rows = cur_ref.shape[0]
    buf_ref[0:SUBLANES, :] = jnp.where(i == 0, 0.0, prev_ref[...])
    buf_ref[SUBLANES:SUBLANES + rows, :] = cur_ref[...]
    buf_ref[SUBLANES + rows:2 * SUBLANES + rows, :] = jnp.where(i == last, 0.0, next_ref[...])
    y = (buf_ref[SUBLANES - 1:SUBLANES - 1 + rows, :] * w_ref[0:1, :]
         + buf_ref[SUBLANES:SUBLANES + rows, :] * w_ref[1:2, :]
         + buf_ref[SUBLANES + 1:SUBLANES + 1 + rows, :] * w_ref[2:3, :]
         + b_ref[...])
    o_ref[...] = _silu(y)


def _conv_silu(p3, conv_w, conv_b):
    bsz, t, _ = p3.shape
    tile = min(CONV_TILE, t)
    nb = t // tile
    per = tile // SUBLANES
    outs = []
    for j in (CD_QD, CD_KD):
        lo = (j - CD_QD) * 512
        outs.append(pl.pallas_call(
            _conv_kernel,
            out_shape=jax.ShapeDtypeStruct((bsz, t, 512), F32),
            grid=(bsz, nb),
            in_specs=[pl.BlockSpec((None, SUBLANES, 512), lambda b, i, j=j: (b, jnp.maximum(i * per - 1, 0), j)),
                      pl.BlockSpec((None, tile, 512), lambda b, i, j=j: (b, i, j)),
                      pl.BlockSpec((None, SUBLANES, 512),
                                   lambda b, i, j=j: (b, jnp.minimum((i + 1) * per, t // SUBLANES - 1), j)),
                      pl.BlockSpec((3, 512), lambda b, i: (0, 0)),
                      pl.BlockSpec((1, 512), lambda b, i: (0, 0))],
            out_specs=pl.BlockSpec((None, tile, 512), lambda b, i: (b, i, 0)),
            scratch_shapes=[pltpu.VMEM((tile + 2 * SUBLANES, 512), F32)],
            compiler_params=_params(2),
            name="conv_silu",
        )(p3, p3, p3, conv_w[:, lo:lo + 512], conv_b[None, lo:lo + 512]))
    return outs


def _gate_lane_layout():
    lane = np.arange(LANES)
    r = lane % GATE_LANES
    return lane // (N_HEADS * GATE_LANES), (lane // GATE_LANES) % N_HEADS, r // 3, r % 3


def _gate_const_rows():
    _, _, cls, _ = _gate_lane_layout()
    return jnp.asarray(np.concatenate([np.tile((cls == k + 1).astype(np.float32)[None, :], (LANES, 1))
                                       for k in range(3)], axis=0), BF16)


def _pieces(x, piece):
    hi = x.astype(BF16).astype(F32)
    r1 = x - hi
    mid = r1.astype(BF16).astype(F32)
    lo = (r1 - mid).astype(BF16).astype(F32)
    return jnp.where(piece == 0, hi, jnp.where(piece == 1, mid, lo))


def _running_max(x, direction):
    row = lax.broadcasted_iota(jnp.int32, x.shape, 0)
    step = 1
    while step < CHUNK:
        if direction == 0:
            shifted = jnp.where(row >= step, pltpu.roll(x, step, 0), -jnp.inf)
        else:
            shifted = jnp.where(row < CHUNK - step, pltpu.roll(x, CHUNK - step, 0), -jnp.inf)
        x = jnp.maximum(x, shifted)
        step *= 2
    return x


def _scan_d_kernel(q_f, k_f, v_f, gi_f, gf_f, q_b, k_b, v_b, gi_b, gf_b, fgb_ref, const_ref, o_f, o_b, c_ref, m_ref):
    @pl.when(pl.program_id(1) == 0)
    def _():
        c_ref[...] = jnp.zeros_like(c_ref)
        m_ref[...] = jnp.zeros_like(m_ref)

    n_chunks = SCAN_ROWS // CHUNK
    dirs = ((q_f, k_f, v_f, gi_f, gf_f, o_f), (q_b, k_b, v_b, gi_b, gf_b, o_b))
    lane = lax.broadcasted_iota(jnp.int32, (1, LANES), 1)
    lane_dir = lane // (N_HEADS * GATE_LANES)
    lane_head = (lane // GATE_LANES) % N_HEADS
    cls = (lane % GATE_LANES) // 3
    piece = (lane % GATE_LANES) % 3
    ones_block = jnp.ones((CHUNK, HEAD_DV), BF16)

    def prepare(c, m_state):
        pre = []
        for d, (q_ref, k_ref, v_ref, gi_ref, gf_ref, o_ref) in enumerate(dirs):
            rows = pl.ds((c if d == 0 else n_chunks - 1 - c) * CHUNK, CHUNK)
            pre.append((rows, gi_ref[rows, :], _log_sigmoid(gf_ref[rows, :] + fgb_ref[...])))
        bcs = _chunk_cumsum([(pre[0][2], 0), (pre[1][2], 1)])
        units, m_next = [], []
        for d, (q_ref, k_ref, v_ref, gi_ref, gf_ref, o_ref) in enumerate(dirs):
            rows, gi, _ = pre[d]
            b = bcs[d]
            last = _last_row(d)
            mask = _tri(d)
            m_st = m_state[d]
            b_last = b[last:last + 1, :]
            cj = gi - b
            run = _running_max(cj, d)
            m_i = jnp.maximum(b + m_st, b + run)
            u = b - m_i
            m_new = b_last + jnp.maximum(m_st, run[last:last + 1, :])
            s_old = jnp.exp(b_last + m_st - m_new)
            m_next.append(m_new)
            quantity =jnp.where(cls == 0, u, jnp.where(cls == 1, u + m_st, jnp.where(
                cls == 2, -m_i, cj + (b_last - m_new))))
            lhs_all = jnp.where(cls == 4, 1.0, jnp.where(cls < 4, _pieces(quantity, piece), 0.0))
            lhs = jnp.concatenate([jnp.where((lane_dir == d) & (lane_head == h), lhs_all, 0.0)
                                   for h in range(N_HEADS)], axis=0).astype(BF16)
            key_rows = jnp.where(cls == 4, _pieces(cj, piece), jnp.where(cls == 0, 1.0, 0.0)).astype(BF16)
            expo = _nt(lhs, jnp.concatenate([const_ref[...], key_rows], axis=0))
            for h in range(N_HEADS):
                sl = slice(h * HEAD_DV, (h + 1) * HEAD_DV)
                q = q_ref[rows, sl] * (HEAD_DV ** -0.5)
                k = k_ref[rows, sl]
                v_ext = jnp.concatenate([v_ref[rows, sl].astype(BF16), ones_block], axis=1)
                lane0 = (d * N_HEADS + h) * GATE_LANES
                units.append(dict(u=d * N_HEADS + h, rows=rows, sl=sl, o_ref=o_ref, q=q, k=k, v_ext=v_ext,
                                  mask=mask, expo=expo[h * CHUNK:(h + 1) * CHUNK, :],
                                  s_old=s_old[:, lane0:lane0 + 1], qk=_nt(q.astype(BF16), k.astype(BF16))))
        for t in units:
            e = t.pop("expo")
            w = t.pop("qk") * jnp.exp(jnp.where(t.pop("mask"), e[:, 3 * LANES:3 * LANES + CHUNK], -jnp.inf))
            t["lhs2"] = jnp.concatenate([(t.pop("q") * jnp.exp(e[:, :LANES])).astype(BF16), w.astype(BF16)], axis=1)
            t["floor"] = jnp.exp(e[:, LANES:2 * LANES])
            t["upd"] = _tn((t.pop("k") * jnp.exp(e[:, 2 * LANES:3 * LANES])).astype(BF16), t["v_ext"])
        return units, m_next

    def finish(units):
        nums = [_mm(t["lhs2"], jnp.concatenate([c_ref[t["u"]].astype(BF16), t["v_ext"]], axis=0)) for t in units]
        for t, num in zip(units, nums):
            den = jnp.maximum(jnp.abs(num[:, HEAD_DV:]), t["floor"])
            t["o_ref"][t["rows"], t["sl"]] = (num[:, :HEAD_DV] / den).astype(BF16)
            c_ref[t["u"]] = t["s_old"] * c_ref[t["u"]] + t["upd"]

    m_state = [m_ref[d][0:1, :] for d in range(2)]
    pending, m_state = prepare(0, m_state)
    for c in range(n_chunks):
        following = None
        if c + 1 < n_chunks:
            following, m_state = prepare(c + 1, m_state)
        finish(pending)
        pending = following
    for d in range(2):
        m_ref[d] = jnp.broadcast_to(m_state[d], (SUBLANES, LANES))


def _scan_d(qd, kd, p3, fgb_row):
    bsz, t, _ = p3.shape
    nb = t // SCAN_ROWS
    const_rows = _gate_const_rows()

    def spec(j, w, flip):
        if flip:
            return pl.BlockSpec((None, SCAN_ROWS, w), lambda b, n: (b, nb - 1 - n, j))
        return pl.BlockSpec((None, SCAN_ROWS, w), lambda b, n: (b, n, j))

    out = jax.ShapeDtypeStruct((bsz, t, MIX_HALF), BF16)
    in_specs = []
    for flip in (False, True):
        in_specs += [spec(0, 512, flip), spec(0, 512, flip), spec(CD_VD, 512, flip),
                     spec(CD_GATE_I_BLOCK, LANES, flip), spec(CD_GATE_F_BLOCK, LANES, flip)]
    in_specs += [pl.BlockSpec((1, LANES), lambda b, n: (0, 0)), pl.BlockSpec(const_rows.shape, lambda b, n: (0, 0))]
    return pl.pallas_call(
        _scan_d_kernel,
        out_shape=(out, out),
        grid=(bsz, nb),
        in_specs=in_specs,
        out_specs=(spec(0, MIX_HALF, False), spec(0, MIX_HALF, True)),
        scratch_shapes=[pltpu.VMEM((2 * N_HEADS, HEAD_DV, 2 * HEAD_DV), F32),
                        pltpu.VMEM((2, SUBLANES, LANES), F32)],
        compiler_params=_params(2),
        name="scan_d",
    )(qd, kd, p3, p3, p3, qd, kd, p3, p3, p3, fgb_row, const_rows)


def _layer_norm(z, g, b):
    zc = z - jnp.mean(z, axis=-1, keepdims=True)
    var = jnp.mean(zc * zc, axis=-1, keepdims=True)
    return zc * lax.rsqrt(var + LN_EPS) * g + b


def _mix_out_kernel(o1f, o1b, o2f, o2b, g1, g2, x_ref, norm_ref, w_ref, lng, lnb, h_ref, hb_ref):
    parts = []
    for of, ob, g_ref, base in ((o1f, o1b, g1, 0), (o2f, o2b, g2, MIX_HALF)):
        o = of[...].astype(F32) + ob[...].astype(F32)
        gate = g_ref[...].astype(F32)
        for h in range(N_HEADS):
            sl = slice(h * HEAD_DV, (h + 1) * HEAD_DV)
            oh = o[:, sl]
            oh = oh * lax.rsqrt(jnp.mean(oh * oh, axis=-1, keepdims=True) + NORM_EPS)
            parts.append((oh * norm_ref[:, base + h * HEAD_DV:base + (h + 1) * HEAD_DV] * gate[:, sl]).astype(BF16))
    mixed = jnp.concatenate(parts, axis=1)
    z = ALPHA * x_ref[...] + _mm(mixed, w_ref[...])
    hn = _layer_norm(z, lng[...], lnb[...])
    h_ref[...] = hn
    hb_ref[...] = hn.astype(BF16)


def _mix_out(o1f, o1b, o2f, o2b, gates, x2d, norm, w_out, ln_g, ln_b):
    n = x2d.shape[0]
    half = pl.BlockSpec((WIDE_TILE, MIX_HALF), lambda i: (i, 0))
    full = pl.BlockSpec((WIDE_TILE, D_MODEL), lambda i: (i, 0))
    row = pl.BlockSpec((1, D_MODEL), lambda i: (0, 0))
    return pl.pallas_call(
        _mix_out_kernel,
        out_shape=(jax.ShapeDtypeStruct((n, D_MODEL), F32), jax.ShapeDtypeStruct((n, D_MODEL), BF16)),
        grid=(n // WIDE_TILE,),
        in_specs=[half, half, half, half, half, pl.BlockSpec((WIDE_TILE, MIX_HALF), lambda i: (i, 1)),
                  full, row, pl.BlockSpec((D_MODEL, D_MODEL), lambda i: (0, 0)), row, row],
        out_specs=(full, full),
        compiler_params=_params(1),
        name="mix_out",
    )(o1f, o1b, o2f, o2b, gates, gates, x2d, norm, w_out, ln_g, ln_b)


def _route(h, rw_ref, rb_ref, idx_ref, gate_ref):
    rw = rw_ref[...]
    rw_hi = rw.astype(BF16)
    rw_r = rw - rw_hi.astype(F32)
    rw_mid = rw_r.astype(BF16)
    rw3 = jnp.concatenate([rw_hi, rw_mid, (rw_r - rw_mid.astype(F32)).astype(BF16)], axis=0)
    h_hi = h.astype(BF16)
    h_lo = (h - h_hi.astype(F32)).astype(BF16)
    terms = _nt(rw3, h_hi) + _nt(rw3, h_lo)
    logits = terms[:N_EXPERTS] + terms[N_EXPERTS:2 * N_EXPERTS] + terms[2 * N_EXPERTS:]
    rows = [logits[e:e + 1, :] for e in range(N_EXPERTS)]
    mx = functools.reduce(jnp.maximum, rows)
    ex = [jnp.exp(r - mx) for r in rows]
    tot = functools.reduce(lambda a, b: a + b, ex)
    score = [e / tot for e in ex]
    biased = [score[e] + rb_ref[e:e + 1, :] for e in range(N_EXPERTS)]

    def argmax_first(vals):
        best, idx = vals[0], jnp.zeros(vals[0].shape, jnp.int32)
        for j in range(1, len(vals)):
            upd = vals[j] > best
            best = jnp.where(upd, vals[j], best)
            idx = jnp.where(upd, j, idx)
        return best, idx

    def pick(vals, idx):
        out = vals[0]
        for j in range(1, len(vals)):
            out = jnp.where(idx == j, vals[j], out)
        return out

    group_scores = []
    for g in range(N_GROUPS):
        a = biased[g * EXPERTS_PER_GROUP:(g + 1) * EXPERTS_PER_GROUP]
        pairs = [a[i] + a[j] for i in range(EXPERTS_PER_GROUP) for j in range(i + 1, EXPERTS_PER_GROUP)]
        group_scores.append(functools.reduce(jnp.maximum, pairs))
    _, g_sel = argmax_first(group_scores)
    in_b = [pick([biased[g * EXPERTS_PER_GROUP + k] for g in range(N_GROUPS)], g_sel) for k in range(EXPERTS_PER_GROUP)]
    in_s = [pick([score[g * EXPERTS_PER_GROUP + k] for g in range(N_GROUPS)], g_sel) for k in range(EXPERTS_PER_GROUP)]
    _, i0 = argmax_first(in_b)
    _, i1 = argmax_first([jnp.where(i0 == k, -jnp.inf, in_b[k]) for k in range(EXPERTS_PER_GROUP)])
    s0, s1 = pick(in_s, i0), pick(in_s, i1)
    den = s0 + s1
    idx_ref[0:1, :] = g_sel * EXPERTS_PER_GROUP + i0
    idx_ref[1:2, :] = g_sel * EXPERTS_PER_GROUP + i1
    gate_ref[0:1, :] = s0 / den
    gate_ref[1:2, :] = s1 / den


def _router_kernel(h_ref, rw_ref, rb_ref, idx_ref, gate_ref):
    _route(h_ref[...], rw_ref, rb_ref, idx_ref, gate_ref)


def _router(h2d, rw_t, rb_col):
    n = h2d.shape[0]
    tile = min(ROUTER_TILE, n)
    routed = pl.BlockSpec((TOP_K, tile), lambda i: (0, i))
    return pl.pallas_call(
        _router_kernel,
        out_shape=(jax.ShapeDtypeStruct((TOP_K, n), jnp.int32), jax.ShapeDtypeStruct((TOP_K, n), F32)),
        grid=(n // tile,),
        in_specs=[pl.BlockSpec((tile, D_MODEL), lambda i: (i, 0)),
                  pl.BlockSpec(rw_t.shape, lambda i: (0, 0)),
                  pl.BlockSpec((N_EXPERTS, 1), lambda i: (0, 0))],
        out_specs=(routed, routed),
        compiler_params=_params(1),
        name="router",
    )(h2d, rw_t, rb_col)


def _plan_kernel(idx_ref, dest_ref, cnt_ref, su_ref, run_ref, start_ref):
    phase, i = pl.program_id(0), pl.program_id(1)
    tile = idx_ref.shape[1]
    e_iota = lax.broadcasted_iota(jnp.int32, (N_EXPERTS, tile), 0)
    oh0 = idx_ref[0:1, :] == e_iota
    oh1 = idx_ref[1:2, :] == e_iota
    member = jnp.where(oh0 | oh1, 1.0, 0.0)
    tile_count = jnp.broadcast_to(jnp.sum(member, axis=1, keepdims=True), (N_EXPERTS, LANES))

    @pl.when((phase == 0) & (i == 0))
    def _():
        run_ref[...] = jnp.zeros_like(run_ref)
        r = lax.broadcasted_iota(jnp.int32, (tile, tile), 0)
        c = lax.broadcasted_iota(jnp.int32, (tile, tile), 1)
        su_ref[...] = jnp.where(r < c, 1.0, 0.0).astype(BF16)

    @pl.when((phase == 1) & (i == 0))
    def _():
        counts = run_ref[...]
        cnt_ref[...] = counts.astype(jnp.int32)
        padded = jnp.ceil(counts * (1.0 / EXPERT_ROWS)) * EXPERT_ROWS
        r = lax.broadcasted_iota(jnp.int32, (N_EXPERTS, N_EXPERTS), 0)
        c = lax.broadcasted_iota(jnp.int32, (N_EXPERTS, N_EXPERTS), 1)
        start_ref[...] = jnp.dot(jnp.where(c < r, 1.0, 0.0), padded, precision=HIGHEST, preferred_element_type=F32)
        run_ref[...] = jnp.zeros_like(run_ref)

    @pl.when(phase == 1)
    def _():
        before = _mm(member.astype(BF16), su_ref[...])
        pos = start_ref[:, 0:1] + run_ref[:, 0:1] + before
        dest_ref[0:1, :] = jnp.sum(jnp.where(oh0, pos, 0.0), axis=0, keepdims=True).astype(jnp.int32)
        dest_ref[1:2, :] = jnp.sum(jnp.where(oh1, pos, 0.0), axis=0, keepdims=True).astype(jnp.int32)

    run_ref[...] += tile_count


def _plan(idx):
    n = idx.shape[1]
    tile = min(PLAN_TILE, n)
    return pl.pallas_call(
        _plan_kernel,
        out_shape=(jax.ShapeDtypeStruct((TOP_K, n), jnp.int32), jax.ShapeDtypeStruct((N_EXPERTS, LANES), jnp.int32)),
        grid=(2, n // tile),
        in_specs=[pl.BlockSpec((TOP_K, tile), lambda p, i: (0, i))],
        out_specs=(pl.BlockSpec((TOP_K, tile), lambda p, i: (0, i * p)),
                   pl.BlockSpec((N_EXPERTS, LANES), lambda p, i: (0, 0))),
        scratch_shapes=[pltpu.VMEM((tile, tile), BF16), pltpu.VMEM((N_EXPERTS, LANES), F32),
                        pltpu.VMEM((N_EXPERTS, LANES), F32)],
        compiler_params=_params(2),
        name="moe_plan",
    )(idx)


def _invert_kernel(pad_ref, dest_ref, tok_ref):
    n = dest_ref.shape[0] // TOP_K
    n_rows = tok_ref.shape[0]

    def filler(r):
        return r - jnp.where(r >= 2 * n, 2 * n, jnp.where(r >= n, n, 0))

    for e in range(N_EXPERTS):
        def clear(i, carry, lo=pad_ref[0, e], hi=pad_ref[1, e]):
            r = jnp.minimum(jnp.maximum(hi - 1 - i, lo), n_rows - 1)
            tok_ref[r] = filler(r)
            return carry

        lax.fori_loop(0, EXPERT_ROWS, clear, 0, unroll=16)

    def clear_tail(r, carry):
        tok_ref[r] = filler(r)
        return carry

    lax.fori_loop(pad_ref[0, N_EXPERTS], pad_ref[1, N_EXPERTS], clear_tail, 0)

    def place(t, carry):
        for k in range(TOP_K):
            tok_ref[dest_ref[k * n + t]] = t
        return carry

    lax.fori_loop(0, n, place, 0, unroll=16)


def _invert(pad_ranges, dest_flat, n_rows):
    smem = pl.BlockSpec(memory_space=pltpu.SMEM)
    return pl.pallas_call(
        _invert_kernel,
        out_shape=jax.ShapeDtypeStruct((n_rows,), jnp.int32),
        in_specs=[smem, smem],
        out_specs=smem,
        name="moe_invert",
    )(pad_ranges, dest_flat)


def _expert_kernel(be_ref, nv_ref, x_ref, w1_ref, w3_ref, w2_ref, y_ref, wb_ref):
    i = pl.program_id(0)
    valid = i < nv_ref[0]
    new_expert = (i == 0) | (be_ref[i] != be_ref[jnp.maximum(i - 1, 0)])

    @pl.when(valid & new_expert)
    def _():
        wb_ref[0] = w1_ref[...].astype(BF16)
        wb_ref[1] = w3_ref[...].astype(BF16)
        wb_ref[2] = w2_ref[...].astype(BF16)

    @pl.when(valid)
    def _():
        x = x_ref[...]
        hid = _silu(_mm(x, wb_ref[0])) * _mm(x, wb_ref[1])
        y_ref[...] = _mm(hid.astype(BF16), wb_ref[2]).astype(BF16)

    @pl.when(jnp.logical_not(valid))
    def _():
        y_ref[...] = jnp.zeros_like(y_ref)


def _experts(block_expert, n_valid, xs, w1, w3, w2, layer):
    n_rows = xs.shape[0]
    wspec = pl.BlockSpec((None, None, D_MODEL, D_MODEL), lambda i, be, nv: (layer, be[i], 0, 0))
    return pl.pallas_call(
        _expert_kernel,
        out_shape=jax.ShapeDtypeStruct((n_rows, D_MODEL), BF16),
        grid_spec=pltpu.PrefetchScalarGridSpec(
            num_scalar_prefetch=2,
            grid=(n_rows // EXPERT_ROWS,),
            in_specs=[pl.BlockSpec((EXPERT_ROWS, D_MODEL), lambda i, be, nv: (i, 0)), wspec, wspec, wspec],
            out_specs=pl.BlockSpec((EXPERT_ROWS, D_MODEL), lambda i, be, nv: (i, 0)),
            scratch_shapes=[pltpu.VMEM((3, D_MODEL, D_MODEL), BF16)]),
        compiler_params=_params(1),
        name="experts",
    )(block_expert, n_valid, xs, w1, w3, w2)


def _combine_kernel(h_ref, y0_ref, y1_ref, g_ref, lng, lnb, o_ref):
    g = g_ref[...]
    z = ALPHA * h_ref[...] + (g[:, 0:1] * y0_ref[...].astype(F32) + g[:, 1:2] * y1_ref[...].astype(F32))
    o_ref[...] = _layer_norm(z, lng[...], lnb[...])


def _combine(h2d, y_rows, gates_t, ln_g, ln_b):
    n = h2d.shape[0]
    tile = min(COMBINE_TILE, n)
    nt = n // tile
    full = pl.BlockSpec((tile, D_MODEL), lambda i: (i, 0))
    row = pl.BlockSpec((1, D_MODEL), lambda i: (0, 0))
    return pl.pallas_call(
        _combine_kernel,
        out_shape=jax.ShapeDtypeStruct((n, D_MODEL), F32),
        grid=(nt,),
        in_specs=[full, full, pl.BlockSpec((tile, D_MODEL), lambda i: (nt + i, 0)),
                  pl.BlockSpec((tile, TOP_K), lambda i: (i, 0)), row, row],
        out_specs=full,
        compiler_params=_params(1),
        name="moe_combine",
    )(h2d, y_rows, y_rows, gates_t, ln_g, ln_b)


def _moe(h2d, h_bf16, idx, gates, w1, w3, w2, layer, ln_g, ln_b):
    n = h2d.shape[0]
    dest, counts = _plan(idx)
    n_rows = n * TOP_K + N_EXPERTS * EXPERT_ROWS
    n_blocks = n_rows // EXPERT_ROWS
    counts = counts[:, 0]
    padded = (counts + EXPERT_ROWS - 1) // EXPERT_ROWS * EXPERT_ROWS
    pad_end = jnp.cumsum(padded)
    block_start = jnp.arange(n_blocks, dtype=jnp.int32) * EXPERT_ROWS
    block_expert = jnp.minimum(jnp.sum((pad_end[None, :] <= block_start[:, None]).astype(jnp.int32), axis=1),
                               N_EXPERTS - 1)
    n_valid = (pad_end[-1:] // EXPERT_ROWS).astype(jnp.int32)
    pad_ranges = jnp.stack([jnp.concatenate([pad_end - padded + counts, pad_end[-1:]]),
                            jnp.concatenate([pad_end, jnp.full((1,), n_rows, jnp.int32)])]).astype(jnp.int32)
    dest_flat = dest.reshape(-1)
    xs = h_bf16[_invert(pad_ranges, dest_flat, n_rows)]
    y = _experts(block_expert, n_valid, xs, w1, w3, w2, layer)
    return _combine(h2d, y[dest_flat], gates.T, ln_g, ln_b)


def _pad_cols(w, b, main, width):
    k = w.shape[0]
    tail = w.shape[1] - main
    w_p = jnp.concatenate([w[:, :main], w[:, main:], jnp.zeros((k, width - main - tail), w.dtype)], axis=1)
    b_p = jnp.concatenate([b[:main], b[main:], jnp.zeros((width - main - tail,), b.dtype)])
    return w_p.astype(BF16), b_p[None, :].astype(F32)


def kernel(x, ab_w_in, ab_b_in, hgrn_lb, gla_gk_up, gla_gk_b, hgrn_norm, gla_norm, ab_w_out, cd_w_in, cd_b_in,
           mlstm_conv_w, mlstm_conv_b, mlstm_fgate_b, ret_norm, mlstm_norm, cd_w_out, ln_mix_g, ln_mix_b, ln_ffn_g,
           ln_ffn_b, router_w, router_b, moe_w1, moe_w3, moe_w2):
    bsz, t, d = x.shape
    n = bsz * t
    assert d == D_MODEL and t % SCAN_ROWS == 0 and t % min(CONV_TILE, t) == 0, (bsz, t, d)
    assert n % ROW_TILE == 0 and n % WIDE_TILE == 0 and n % min(ROUTER_TILE, n) == 0 and n % min(PLAN_TILE, n) == 0
    assert n % min(COMBINE_TILE, n) == 0
    lower_bounds = jnp.cumsum(jax.nn.softmax(hgrn_lb.astype(F32), axis=0), axis=0)
    rw_t = router_w.T.astype(F32)
    rb_col = router_b.astype(F32)[:, None]
    h = x.reshape(n, d)
    for layer in range(DEPTH):
        j = layer // 2
        if layer % 2 == 0:
            w_p, b_p = _pad_cols(ab_w_in[j], ab_b_in[j], 8 * 512, AB_WIDTH)
            lb = lower_bounds[layer][None, :]
            p, out_gates = _in_proj(h, w_p, b_p, _proj_ab_kernel, [lb], [pl.BlockSpec(lb.shape, lambda i: (0, 0))])
            p3 = p.reshape(bsz, t, AB_WIDTH)
            nk = N_HEADS * NARROW_DK
            u_pad = jnp.zeros((LANES, 2 * nk), F32)
            u_pad = u_pad.at[:GLA_LOWRANK, :nk].set(gla_gk_up[j, 0])
            u_pad = u_pad.at[GLA_LOWRANK:2 * GLA_LOWRANK, nk:].set(gla_gk_up[j, 1]).astype(BF16)
            gkb = jnp.concatenate([gla_gk_b[j, 0], gla_gk_b[j, 1]])[None, :].astype(F32)
            oa_f, ob_f, oa_b, ob_b = _scan_ab(p3, u_pad, gkb)
            norm = jnp.concatenate([hgrn_norm[j], gla_norm[j]])[None, :].astype(F32)
            outs = [a.reshape(n, MIX_HALF) for a in (oa_f, oa_b, ob_f, ob_b)]
            h, h_bf16 = _mix_out(*outs, out_gates, h, norm, ab_w_out[j].astype(BF16),
                                 ln_mix_g[layer][None, :], ln_mix_b[layer][None, :])
        else:
            lane_dir, lane_head, _, _ = _gate_lane_layout()
            col_i = 7 * 512 + lane_dir * 2 * N_HEADS + lane_head
            cols = np.concatenate([np.arange(7 * 512), col_i, col_i + N_HEADS])
            w_p, b_p = cd_w_in[j][:, cols].astype(BF16), cd_b_in[j][cols][None, :].astype(F32)
            pos = pl.BlockSpec((ROW_TILE, LANES), lambda i: (i % (t // ROW_TILE), 0))
            p, out_gates = _in_proj(h, w_p, b_p, _proj_cd_kernel, list(_rotary_tables(t)), [pos, pos])
            p3 = p.reshape(bsz, t, CD_WIDTH)
            oc_f, oc_b = _scan_c(p3)
            qd, kd = _conv_silu(p3, mlstm_conv_w[j].astype(F32), mlstm_conv_b[j].astype(F32))
            fgb_row = mlstm_fgate_b[j].astype(F32)[lane_dir, lane_head][None, :]
            od_f, od_b = _scan_d(qd, kd, p3, fgb_row)
            norm = jnp.concatenate([ret_norm[j], mlstm_norm[j]])[None, :].astype(F32)
            outs = [a.reshape(n, MIX_HALF) for a in (oc_f, oc_b, od_f, od_b)]
            h, h_bf16 = _mix_out(*outs, out_gates, h, norm, cd_w_out[j].astype(BF16),
                                 ln_mix_g[layer][None, :], ln_mix_b[layer][None, :])
        idx, gates = _router(h, rw_t, rb_col)
        h = _moe(h, h_bf16, idx, gates, moe_w1, moe_w3, moe_w2, layer,
                 ln_ffn_g[layer][None, :], ln_ffn_b[layer][None, :])
    return h.reshape(bsz, t, d)
```

```python
import functools

import jax
import jax.numpy as jnp
import numpy as np
from jax import lax
from jax.experimental import pallas as pl
from jax.experimental.pallas import tpu as pltpu

F32 = jnp.float32
BF16 = jnp.bfloat16
HIGHEST = lax.Precision.HIGHEST

D_MODEL = 1024
DEPTH = 2
CHUNK = 64
MIX_HALF = D_MODEL // 2
N_HEADS = 4
HEAD_DV = MIX_HALF // N_HEADS
NARROW_DK = 64
GLA_LOWRANK = 16
GLA_GATE_NORMALIZER = 16.0
ROPE_THETA = 10000.0
N_EXPERTS = 16
N_GROUPS = 4
EXPERTS_PER_GROUP = N_EXPERTS // N_GROUPS
TOP_K = 2
ALPHA = (2.0 * DEPTH) ** 0.25
LN_EPS = 1e-5
NORM_EPS = 1e-6

LANES = 128
SUBLANES = 8
VMEM_LIMIT = 52 * 1024 * 1024

ROW_TILE = 512
WIDE_TILE = 512
COMBINE_TILE = 1024
CONV_TILE = 2048
SCAN_ROWS = 512
ROUTER_TILE = 2048
PLAN_TILE = 1024
EXPERT_ROWS = 512
EXPERT_HIDDEN_SPLIT = 2

AB_QA, AB_FF, AB_FB, AB_IA, AB_GA, AB_QKB, AB_VB, AB_GB = range(8)
AB_LR_BLOCK = 32
AB_WIDTH = 8 * 512 + LANES
CD_QKC, CD_VC, CD_GC, CD_QD, CD_KD, CD_VD, CD_OD = range(7)
CD_GATE_I_BLOCK = 28
CD_GATE_F_BLOCK = 29
CD_WIDTH = 7 * 512 + 2 * LANES
GATE_LANES = 16


def _params(n_axes, vmem=VMEM_LIMIT):
    return pltpu.CompilerParams(dimension_semantics=("arbitrary",) * n_axes, vmem_limit_bytes=vmem)


def _nt(a, b):
    return lax.dot_general(a, b, (((1,), (1,)), ((), ())), preferred_element_type=F32)


def _tn(a, b):
    return lax.dot_general(a, b, (((0,), (0,)), ((), ())), preferred_element_type=F32)


def _mm(a, b):
    return jnp.dot(a, b, preferred_element_type=F32)


def _sigmoid(x):
    return 1.0 / (1.0 + jnp.exp(-x))


def _silu(x):
    return x * _sigmoid(x)


def _log_sigmoid(x):
    return jnp.minimum(x, 0.0) - jnp.log(1.0 + jnp.exp(-jnp.abs(x)))


def _tri(direction):
    row = lax.broadcasted_iota(jnp.int32, (CHUNK, CHUNK), 0)
    col = lax.broadcasted_iota(jnp.int32, (CHUNK, CHUNK), 1)
    return (col <= row) if direction == 0 else (col >= row)


def _ref_row(direction):
    return CHUNK // 2 - 1 if direction == 0 else CHUNK // 2


def _last_row(direction):
    return CHUNK - 1 if direction == 0 else 0


def _proj_ab_kernel(x_ref, w_ref, b_ref, lb_ref, o_ref, g_ref):
    p = _mm(x_ref[...].astype(BF16), w_ref[...]) + b_ref[...]
    o_ref[:, :512] = _silu(p[:, :512])
    lb = jnp.concatenate([lb_ref[...], lb_ref[...]], axis=1)
    o_ref[:, 512:1536] = lb + (1.0 - lb) * _sigmoid(p[:, 512:1536])
    o_ref[:, 1536:] = p[:, 1536:]
    for half, blk in enumerate((AB_GA, AB_GB)):
        g_ref[:, half * 512:(half + 1) * 512] = _silu(p[:, blk * 512:(blk + 1) * 512]).astype(BF16)


def _proj_cd_kernel(x_ref, w_ref, b_ref, cos_ref, sin_ref, o_ref, g_ref):
    p = _mm(x_ref[...].astype(BF16), w_ref[...]) + b_ref[...]
    cos, sin_signed = cos_ref[...], sin_ref[...]
    for j in range(4):
        rot = _rotary(p[:, j * LANES:(j + 1) * LANES], cos, sin_signed)
        o_ref[:, j * LANES:(j + 1) * LANES] = rot * (NARROW_DK ** -0.5) if j < 2 else rot
    o_ref[:, 512:] = p[:, 512:]
    g_ref[:, :512] = _silu(p[:, CD_GC * 512:(CD_GC + 1) * 512]).astype(BF16)
    g_ref[:, 512:] = _sigmoid(p[:, CD_OD * 512:(CD_OD + 1) * 512]).astype(BF16)


def _in_proj(x2d, w, b, body, extras, extra_specs):
    n, k = x2d.shape
    m = w.shape[1]
    return pl.pallas_call(
        body,
        out_shape=(jax.ShapeDtypeStruct((n, m), F32), jax.ShapeDtypeStruct((n, D_MODEL), BF16)),
        grid=(n // ROW_TILE,),
        in_specs=[pl.BlockSpec((ROW_TILE, k), lambda i: (i, 0)),
                  pl.BlockSpec((k, m), lambda i: (0, 0)),
                  pl.BlockSpec((1, m), lambda i: (0, 0))] + extra_specs,
        out_specs=(pl.BlockSpec((ROW_TILE, m), lambda i: (i, 0)), pl.BlockSpec((ROW_TILE, D_MODEL), lambda i: (i, 0))),
        compiler_params=_params(1),
        name="in_proj",
    )(x2d, w, b, *extras)


def _pipelined_chunks(n_chunks, prepare, finish):
    pending = prepare(0)
    for c in range(n_chunks):
        following = prepare(c + 1) if c + 1 < n_chunks else None
        finish(pending)
        pending = following


def _units_prepare(units):
    return _units_prepare_staggered([lambda: units])


def _units_prepare_staggered(builders):
    def second(group):
        for u in group:
            u["lhs"] = jnp.concatenate([u["qi"], u["intra"](u.pop("scores")).astype(BF16)], axis=1)

    done, previous = [], None
    for build in builders:
        group = build()
        for u in group:
            u["scores"] = _nt(u["qe"], u["ke"])
        for u in group:
            u["v_t"] = u["v"].T
            u["upd"] = _mm(u["v_t"], u["ks"])
        if previous is not None:
            second(previous)
        previous = group
        done += group
    second(previous)
    return done


def _units_finish(units, st_ref, group=4):
    def close(batch):
        for u in batch:
            u["store"](u.pop("o").astype(BF16))
            st_ref[u["idx"]] = st_ref[u["idx"]] * u["dec"] + u["upd"]

    previous = None
    for g in range(0, len(units), group):
        batch = units[g:g + group]
        for u in batch:
            u["o"] = _nt(u["lhs"], jnp.concatenate([st_ref[u["idx"]].astype(BF16), u["v_t"]], axis=1))
        if previous is not None:
            close(previous)
        previous = batch
    close(previous)


def _chunk_cumsum(parts):
    cat = jnp.concatenate([x for x, _ in parts], axis=1)
    hi = cat.astype(BF16)
    lo = (cat - hi.astype(F32)).astype(BF16)
    width = cat.shape[1]
    inc = _mm(_tri(0).astype(BF16), jnp.concatenate([hi, lo], axis=1))
    inc = inc[:, :width] + inc[:, width:]
    out, off = [], 0
    for x, direction in parts:
        p = inc[:, off:off + x.shape[1]]
        off += x.shape[1]
        out.append(p if direction == 0 else p[CHUNK - 1:CHUNK, :] - p + x)
    return out


def _gated_operands(q, k, gc, direction):
    r, l = _ref_row(direction), _last_row(direction)
    g_ref = gc[r:r + 1, :]
    g_last = gc[l:l + 1, :]
    qe = q * jnp.exp(gc - g_ref)
    ke = k * jnp.exp(g_ref - gc)
    qi = qe * jnp.exp(g_ref)
    ks = ke * jnp.exp(g_last - g_ref)
    return [a.astype(BF16) for a in (qe, ke, qi, ks)], jnp.exp(g_last)


def _narrow_head_mask(h):
    lane = lax.broadcasted_iota(jnp.int32, (1, LANES), 1)
    return (lane // NARROW_DK) == (h % 2)


def _scan_ab_kernel(qa_f, fa_f, ia_f, qkb_f, vb_f, lr_f, qa_b, fa_b, ia_b, qkb_b, vb_b, lr_b,
                    u_ref, gkb_ref, oa_f, ob_f, oa_b, ob_b, st_ref):
    @pl.when(pl.program_id(1) == 0)
    def _():
        st_ref[...] = jnp.zeros_like(st_ref)

    n_chunks = SCAN_ROWS // CHUNK
    dirs = ((qa_f, fa_f, ia_f, qkb_f, vb_f, lr_f, oa_f, ob_f), (qa_b, fa_b, ia_b, qkb_b, vb_b, lr_b, oa_b, ob_b))

    nk = N_HEADS * NARROW_DK

    def prepare(c):
        pre = []
        for d, (qa, fa, ia, qkb, vb, lr, oa, ob) in enumerate(dirs):
            rows = pl.ds((c if d == 0 else n_chunks - 1 - c) * CHUNK, CHUNK)
            f = fa[rows, :]
            logits = _mm(lr[rows, :].astype(BF16), u_ref[:, d * nk:(d + 1) * nk]) + gkb_ref[:, d * nk:(d + 1) * nk]
            qk = qkb[rows, :]
            pre.append(dict(rows=rows, f=f, lf_a=jnp.log(f), lf_b=_log_sigmoid(logits) / GLA_GATE_NORMALIZER,
                            q_a=qa[rows, :], v_a=ia[rows, :].astype(BF16),
                            q_b=qk[:, :nk] * (NARROW_DK ** -0.5), k_b=qk[:, nk:], v_b=vb[rows, :].astype(BF16)))
        gcs = _chunk_cumsum([(pre[0]["lf_a"], 0), (pre[0]["lf_b"], 0), (pre[1]["lf_a"], 1), (pre[1]["lf_b"], 1)])
        def hgrn_group(d, oa):
            p, mask = pre[d], _tri(d)
            rows = p["rows"]
            (qe, ke, qi, ks), dec = _gated_operands(p["q_a"], 1.0 - p["f"], gcs[2 * d], d)
            group = []
            for h in range(N_HEADS):
                sl = slice(h * LANES, (h + 1) * LANES)

                def store(o, sl=sl):
                    oa[rows, sl] = o

                group.append(dict(qe=qe[:, sl], ke=ke[:, sl], qi=qi[:, sl], ks=ks[:, sl], v=p["v_a"][:, sl],
                                  dec=dec[:, sl], intra=lambda s: jnp.where(mask, s, 0.0),
                                  idx=d * 2 * N_HEADS + h, store=store))
            return group

        def gla_group(d, ob):
            p, mask = pre[d], _tri(d)
            rows = p["rows"]
            (qe, ke, qi, ks), dec = _gated_operands(p["q_b"], p["k_b"], gcs[2 * d + 1], d)
            group = []
            for h in range(N_HEADS):
                psl = slice((h // 2) * LANES, (h // 2 + 1) * LANES)
                hm = _narrow_head_mask(h)
                pick = lambda a, psl=psl, hm=hm: jnp.where(hm, a[:, psl], jnp.zeros((), BF16))
                vsl = slice(h * HEAD_DV, (h + 1) * HEAD_DV)

                def store(o, vsl=vsl):
                    ob[rows, vsl] = o

                group.append(dict(qe=pick(qe), ke=pick(ke), qi=pick(qi), ks=pick(ks), v=p["v_b"][:, vsl],
                                  dec=dec[:, psl], intra=lambda s: jnp.where(mask, s, 0.0),
                                  idx=d * 2 * N_HEADS + N_HEADS + h, store=store))
            return group

        builders = []
        for d, (qa, fa, ia, qkb, vb, lr, oa, ob) in enumerate(dirs):
            builders += [functools.partial(hgrn_group, d, oa), functools.partial(gla_group, d, ob)]
        return _units_prepare_staggered(builders)

    _pipelined_chunks(n_chunks, prepare, functools.partial(_units_finish, st_ref=st_ref))


def _scan_ab(p3, u_pad, gkb):
    bsz, t, _ = p3.shape
    nb = t // SCAN_ROWS

    def fwd(j, w=512):
        return pl.BlockSpec((None, SCAN_ROWS, w), lambda b, n: (b, n, j))

    def bwd(j, w=512):
        return pl.BlockSpec((None, SCAN_ROWS, w), lambda b, n: (b, nb - 1 - n, j))

    def const(shape):
        return pl.BlockSpec(shape, lambda b, n: (0,) * len(shape))

    out = jax.ShapeDtypeStruct((bsz, t, MIX_HALF), BF16)
    o_f = pl.BlockSpec((None, SCAN_ROWS, MIX_HALF), lambda b, n: (b, n, 0))
    o_b = pl.BlockSpec((None, SCAN_ROWS, MIX_HALF), lambda b, n: (b, nb - 1 - n, 0))
    return pl.pallas_call(
        _scan_ab_kernel,
        out_shape=(out, out, out, out),
        grid=(bsz, nb),
        in_specs=[fwd(AB_QA), fwd(AB_FF), fwd(AB_IA), fwd(AB_QKB), fwd(AB_VB), fwd(AB_LR_BLOCK, LANES),
                  bwd(AB_QA), bwd(AB_FB), bwd(AB_IA), bwd(AB_QKB), bwd(AB_VB), bwd(AB_LR_BLOCK, LANES),
                  const(u_pad.shape), const(gkb.shape)],
        out_specs=(o_f, o_f, o_b, o_b),
        scratch_shapes=[pltpu.VMEM((4 * N_HEADS, HEAD_DV, LANES), F32)],
        compiler_params=_params(2),
        name="scan_ab",
    )(*([p3] * 12), u_pad, gkb)


def _rotary(a, cos, sin_signed):
    lane = lax.broadcasted_iota(jnp.int32, (1, LANES), 1)
    first_half = (lane % NARROW_DK) < (NARROW_DK // 2)
    swapped = jnp.where(first_half, pltpu.roll(a, LANES - NARROW_DK // 2, 1), pltpu.roll(a, NARROW_DK // 2, 1))
    return a * cos + swapped * sin_signed


def _scan_c_kernel(qk_f, v_f, qk_b, v_b, dmat_ref, qsc_ref, ksc_ref, cd_ref, o_f, o_b, st_ref):
    @pl.when(pl.program_id(1) == 0)
    def _():
        st_ref[...] = jnp.zeros_like(st_ref)

    n_chunks = SCAN_ROWS // CHUNK
    dirs = ((qk_f, v_f, o_f), (qk_b, v_b, o_b))

    def prepare(c):
        units = []
        for d, (qk_ref, v_ref, o_ref) in enumerate(dirs):
            rows = pl.ds((c if d == 0 else n_chunks - 1 - c) * CHUNK, CHUNK)
            v = v_ref[rows, :].astype(BF16)
            for h in range(N_HEADS):
                hm = _narrow_head_mask(h)
                q = jnp.where(hm, qk_ref[rows, (h // 2) * LANES:(h // 2 + 1) * LANES], 0.0)
                k = jnp.where(hm, qk_ref[rows, (2 + h // 2) * LANES:(3 + h // 2) * LANES], 0.0)
                vsl = slice(h * HEAD_DV, (h + 1) * HEAD_DV)

                def store(o, o_ref=o_ref, rows=rows, vsl=vsl):
                    o_ref[rows, vsl] = o

                units.append(dict(qe=q.astype(BF16), ke=k.astype(BF16), qi=(q * qsc_ref[d, h]).astype(BF16),
                                  ks=(k * ksc_ref[d, h]).astype(BF16), v=v[:, vsl], dec=cd_ref[d, h][0:1, :],
                                  intra=lambda s, d=d, h=h: s * dmat_ref[d, h], idx=d * N_HEADS + h, store=store))
        return _units_prepare(units)

    _pipelined_chunks(n_chunks, prepare, functools.partial(_units_finish, st_ref=st_ref))


def _retention_tables():
    pos = np.arange(CHUNK, dtype=np.float64)
    dmat = np.zeros((2, N_HEADS, CHUNK, CHUNK), np.float32)
    qsc = np.zeros((2, N_HEADS, CHUNK, LANES), np.float32)
    ksc = np.zeros((2, N_HEADS, CHUNK, LANES), np.float32)
    cd = np.zeros((2, N_HEADS, SUBLANES, LANES), np.float32)
    for d in range(2):
        for h in range(N_HEADS):
            lg = np.log1p(-np.exp2(-(5.0 + 2.0 * h + d)))
            dist = pos[:, None] - pos[None, :]
            if d == 0:
                dmat[d, h] = np.where(dist >= 0, np.exp(np.maximum(dist, 0.0) * lg), 0.0)
                qsc[d, h] = np.exp((pos + 1.0) * lg)[:, None]
                ksc[d, h] = np.exp((CHUNK - 1.0 - pos) * lg)[:, None]
            else:
                dmat[d, h] = np.where(dist <= 0, np.exp(np.maximum(-dist, 0.0) * lg), 0.0)
                qsc[d, h] = np.exp((CHUNK - pos) * lg)[:, None]
                ksc[d, h] = np.exp(pos * lg)[:, None]
            cd[d, h] = np.exp(CHUNK * lg)
    return jnp.asarray(dmat), jnp.asarray(qsc), jnp.asarray(ksc), jnp.asarray(cd)


def _rotary_tables(t):
    half = NARROW_DK // 2
    freqs = ROPE_THETA ** (-jnp.arange(half, dtype=F32) / half)
    ang = jnp.arange(t, dtype=F32)[:, None] * freqs[None, :]
    cos, sin = jnp.cos(ang), jnp.sin(ang)
    cos_t = jnp.tile(cos, (1, LANES // half))
    sin_t = jnp.tile(jnp.concatenate([-sin, sin], axis=1), (1, LANES // NARROW_DK))
    return cos_t.astype(F32), sin_t.astype(F32)


def _scan_c(p3):
    bsz, t, _ = p3.shape
    nb = t // SCAN_ROWS
    tables = _retention_tables()

    def fwd(j):
        return pl.BlockSpec((None, SCAN_ROWS, 512), lambda b, n: (b, n, j))

    def bwd(j):
        return pl.BlockSpec((None, SCAN_ROWS, 512), lambda b, n: (b, nb - 1 - n, j))

    def const(shape):
        return pl.BlockSpec(shape, lambda b, n: (0,) * len(shape))

    out = jax.ShapeDtypeStruct((bsz, t, MIX_HALF), BF16)
    return pl.pallas_call(
        _scan_c_kernel,
        out_shape=(out, out),
        grid=(bsz, nb),
        in_specs=[fwd(CD_QKC), fwd(CD_VC), bwd(CD_QKC), bwd(CD_VC)] + [const(a.shape) for a in tables],
        out_specs=(pl.BlockSpec((None, SCAN_ROWS, MIX_HALF), lambda b, n: (b, n, 0)),
                   pl.BlockSpec((None, SCAN_ROWS, MIX_HALF), lambda b, n: (b, nb - 1 - n, 0))),
        scratch_shapes=[pltpu.VMEM((2 * N_HEADS, HEAD_DV, LANES), F32)],
        compiler_params=_params(2),
        name="scan_c",
    )(p3, p3, p3, p3, *tables)


def _conv_kernel(prev_ref, cur_ref, next_ref, w_ref, b_ref, o_ref, buf_ref):
    i = pl.program_id(1)
    last = pl.num_programs(1) - 1
    rows = cur_ref.shape[0]
    buf_ref[0:SUBLANES, :] = jnp.where(i == 0, 0.0, prev_ref[...])
    buf_ref[SUBLANES:SUBLANES + rows, :] = cur_ref[...]
    buf_ref[SUBLANES + rows:2 * SUBLANES + rows, :] = jnp.where(i == last, 0.0, next_ref[...])
    y = (buf_ref[SUBLANES - 1:SUBLANES - 1 + rows, :] * w_ref[0:1, :]
         + buf_ref[SUBLANES:SUBLANES + rows, :] * w_ref[1:2, :]
         + buf_ref[SUBLANES + 1:SUBLANES + 1 + rows, :] * w_ref[2:3, :]
         + b_ref[...])
    o_ref[...] = _silu(y)


def _conv_silu(p3, conv_w, conv_b):
    bsz, t, _ = p3.shape
    tile = min(CONV_TILE, t)
    nb = t // tile
    per = tile // SUBLANES
    outs = []
    for j in (CD_QD, CD_KD):
        lo = (j - CD_QD) * 512
        outs.append(pl.pallas_call(
            _conv_kernel,
            out_shape=jax.ShapeDtypeStruct((bsz, t, 512), F32),
            grid=(bsz, nb),
            in_specs=[pl.BlockSpec((None, SUBLANES, 512), lambda b, i, j=j: (b, jnp.maximum(i * per - 1, 0), j)),
                      pl.BlockSpec((None, tile, 512), lambda b, i, j=j: (b, i, j)),
                      pl.BlockSpec((None, SUBLANES, 512),
                                   lambda b, i, j=j: (b, jnp.minimum((i + 1) * per, t // SUBLANES - 1), j)),
                      pl.BlockSpec((3, 512), lambda b, i: (0, 0)),
                      pl.BlockSpec((1, 512), lambda b, i: (0, 0))],
            out_specs=pl.BlockSpec((None, tile, 512), lambda b, i: (b, i, 0)),
            scratch_shapes=[pltpu.VMEM((tile + 2 * SUBLANES, 512), F32)],
            compiler_params=_params(2),
            name="conv_silu",
        )(p3, p3, p3, conv_w[:, lo:lo + 512], conv_b[None, lo:lo + 512]))
    return outs


def _gate_lane_layout():
    lane = np.arange(LANES)
    r = lane % GATE_LANES
    return lane // (N_HEADS * GATE_LANES), (lane // GATE_LANES) % N_HEADS, r // 3, r % 3


def _gate_const_rows():
    _, _, cls, _ = _gate_lane_layout()
    return jnp.asarray(np.concatenate([np.tile((cls == k + 1).astype(np.float32)[None, :], (LANES, 1))
                                       for k in range(3)], axis=0), BF16)


def _pieces(x, piece):
    hi = x.astype(BF16).astype(F32)
    r1 = x - hi
    mid = r1.astype(BF16).astype(F32)
    lo = (r1 - mid).astype(BF16).astype(F32)
    return jnp.where(piece == 0, hi, jnp.where(piece == 1, mid, lo))


def _running_max(x, direction):
    row = lax.broadcasted_iota(jnp.int32, x.shape, 0)
    step = 1
    while step < CHUNK:
        if direction == 0:
            shifted = jnp.where(row >= step, pltpu.roll(x, step, 0), -jnp.inf)
        else:
            shifted = jnp.where(row < CHUNK - step, pltpu.roll(x, CHUNK - step, 0), -jnp.inf)
        x = jnp.maximum(x, shifted)
        step *= 2
    return x


def _scan_d_kernel(q_f, k_f, v_f, gi_f, gf_f, q_b, k_b, v_b, gi_b, gf_b, fgb_ref, const_ref, o_f, o_b, c_ref, m_ref):
    @pl.when(pl.program_id(1) == 0)
    def _():
        c_ref[...] = jnp.zeros_like(c_ref)
        m_ref[...] = jnp.zeros_like(m_ref)

    n_chunks = SCAN_ROWS // CHUNK
    dirs = ((q_f, k_f, v_f, gi_f, gf_f, o_f), (q_b, k_b, v_b, gi_b, gf_b, o_b))
    lane = lax.broadcasted_iota(jnp.int32, (1, LANES), 1)
    lane_dir = lane // (N_HEADS * GATE_LANES)
    lane_head = (lane // GATE_LANES) % N_HEADS
    cls = (lane % GATE_LANES) // 3
    piece = (lane % GATE_LANES) % 3
    ones_block = jnp.ones((CHUNK, HEAD_DV), BF16)

    def prepare(c, m_state):
        pre = []
        for d, (q_ref, k_ref, v_ref, gi_ref, gf_ref, o_ref) in enumerate(dirs):
            rows = pl.ds((c if d == 0 else n_chunks - 1 - c) * CHUNK, CHUNK)
            pre.append((rows, gi_ref[rows, :], _log_sigmoid(gf_ref[rows, :] + fgb_ref[...])))
        bcs = _chunk_cumsum([(pre[0][2], 0), (pre[1][2], 1)])
        units, m_next = [], []
        for d, (q_ref, k_ref, v_ref, gi_ref, gf_ref, o_ref) in enumerate(dirs):
            rows, gi, _ = pre[d]
            b = bcs[d]
            last = _last_row(d)
            mask = _tri(d)
            m_st = m_state[d]
            b_last = b[last:last + 1, :]
            cj = gi - b
            run = _running_max(cj, d)
            m_i = jnp.maximum(b + m_st, b + run)
            u = b - m_i
            m_new = b_last + jnp.maximum(m_st, run[last:last + 1, :])
            s_old = jnp.exp(b_last + m_st - m_new)
            m_next.append(m_new)
            quantity =jnp.where(cls == 0, u, jnp.where(cls == 1, u + m_st, jnp.where(
                cls == 2, -m_i, cj + (b_last - m_new))))
            lhs_all = jnp.where(cls == 4, 1.0, jnp.where(cls < 4, _pieces(quantity, piece), 0.0))
            lhs = jnp.concatenate([jnp.where((lane_dir == d) & (lane_head == h), lhs_all, 0.0)
                                   for h in range(N_HEADS)], axis=0).astype(BF16)
            key_rows = jnp.where(cls == 4, _pieces(cj, piece), jnp.where(cls == 0, 1.0, 0.0)).astype(BF16)
            expo = _nt(lhs, jnp.concatenate([const_ref[...], key_rows], axis=0))
            for h in range(N_HEADS):
                sl = slice(h * HEAD_DV, (h + 1) * HEAD_DV)
                q = q_ref[rows, sl] * (HEAD_DV ** -0.5)
                k = k_ref[rows, sl]
                v_ext = jnp.concatenate([v_ref[rows, sl].astype(BF16), ones_block], axis=1)
                lane0 = (d * N_HEADS + h) * GATE_LANES
                units.append(dict(u=d * N_HEADS + h, rows=rows, sl=sl, o_ref=o_ref, q=q, k=k, v_ext=v_ext,
                                  mask=mask, expo=expo[h * CHUNK:(h + 1) * CHUNK, :],
                                  s_old=s_old[:, lane0:lane0 + 1], qk=_nt(q.astype(BF16), k.astype(BF16))))
        for t in units:
            e = t.pop("expo")
            w = t.pop("qk") * jnp.exp(jnp.where(t.pop("mask"), e[:, 3 * LANES:3 * LANES + CHUNK], -jnp.inf))
            t["lhs2"] = jnp.concatenate([(t.pop("q") * jnp.exp(e[:, :LANES])).astype(BF16), w.astype(BF16)], axis=1)
            t["floor"] = jnp.exp(e[:, LANES:2 * LANES])
            t["upd"] = _tn((t.pop("k") * jnp.exp(e[:, 2 * LANES:3 * LANES])).astype(BF16), t["v_ext"])
        return units, m_next

    def finish(units):
        nums = [_mm(t["lhs2"], jnp.concatenate([c_ref[t["u"]].astype(BF16), t["v_ext"]], axis=0)) for t in units]
        for t, num in zip(units, nums):
            den = jnp.maximum(jnp.abs(num[:, HEAD_DV:]), t["floor"])
            t["o_ref"][t["rows"], t["sl"]] = (num[:, :HEAD_DV] / den).astype(BF16)
            c_ref[t["u"]] = t["s_old"] * c_ref[t["u"]] + t["upd"]

    m_state = [m_ref[d][0:1, :] for d in range(2)]
    pending, m_state = prepare(0, m_state)
    for c in range(n_chunks):
        following = None
        if c + 1 < n_chunks:
            following, m_state = prepare(c + 1, m_state)
        finish(pending)
        pending = following
    for d in range(2):
        m_ref[d] = jnp.broadcast_to(m_state[d], (SUBLANES, LANES))


def _scan_d(qd, kd, p3, fgb_row):
    bsz, t, _ = p3.shape
    nb = t // SCAN_ROWS
    const_rows = _gate_const_rows()

    def spec(j, w, flip):
        if flip:
            return pl.BlockSpec((None, SCAN_ROWS, w), lambda b, n: (b, nb - 1 - n, j))
        return pl.BlockSpec((None, SCAN_ROWS, w), lambda b, n: (b, n, j))

    out = jax.ShapeDtypeStruct((bsz, t, MIX_HALF), BF16)
    in_specs = []
    for flip in (False, True):
        in_specs += [spec(0, 512, flip), spec(0, 512, flip), spec(CD_VD, 512, flip),
                     spec(CD_GATE_I_BLOCK, LANES, flip), spec(CD_GATE_F_BLOCK, LANES, flip)]
    in_specs += [pl.BlockSpec((1, LANES), lambda b, n: (0, 0)), pl.BlockSpec(const_rows.shape, lambda b, n: (0, 0))]
    return pl.pallas_call(
        _scan_d_kernel,
        out_shape=(out, out),
        grid=(bsz, nb),
        in_specs=in_specs,
        out_specs=(spec(0, MIX_HALF, False), spec(0, MIX_HALF, True)),
        scratch_shapes=[pltpu.VMEM((2 * N_HEADS, HEAD_DV, 2 * HEAD_DV), F32),
                        pltpu.VMEM((2, SUBLANES, LANES), F32)],
        compiler_params=_params(2),
        name="scan_d",
    )(qd, kd, p3, p3, p3, qd, kd, p3, p3, p3, fgb_row, const_rows)


def _layer_norm(z, g, b):
    zc = z - jnp.mean(z, axis=-1, keepdims=True)
    var = jnp.mean(zc * zc, axis=-1, keepdims=True)
    return zc * lax.rsqrt(var + LN_EPS) * g + b


def _mix_out_kernel(o1f, o1b, o2f, o2b, g1, g2, x_ref, norm_ref, w_ref, lng, lnb, h_ref, hb_ref):
    parts = []
    for of, ob, g_ref, base in ((o1f, o1b, g1, 0), (o2f, o2b, g2, MIX_HALF)):
        o = of[...].astype(F32) + ob[...].astype(F32)
        gate = g_ref[...].astype(F32)
        for h in range(N_HEADS):
            sl = slice(h * HEAD_DV, (h + 1) * HEAD_DV)
            oh = o[:, sl]
            oh = oh * lax.rsqrt(jnp.mean(oh * oh, axis=-1, keepdims=True) + NORM_EPS)
            parts.append((oh * norm_ref[:, base + h * HEAD_DV:base + (h + 1) * HEAD_DV] * gate[:, sl]).astype(BF16))
    mixed = jnp.concatenate(parts, axis=1)
    z = ALPHA * x_ref[...] + _mm(mixed, w_ref[...])
    hn = _layer_norm(z, lng[...], lnb[...])
    h_ref[...] = hn
    hb_ref[...] = hn.astype(BF16)


def _mix_out(o1f, o1b, o2f, o2b, gates, x2d, norm, w_out, ln_g, ln_b):
    n = x2d.shape[0]
    half = pl.BlockSpec((WIDE_TILE, MIX_HALF), lambda i: (i, 0))
    full = pl.BlockSpec((WIDE_TILE, D_MODEL), lambda i: (i, 0))
    row = pl.BlockSpec((1, D_MODEL), lambda i: (0, 0))
    return pl.pallas_call(
        _mix_out_kernel,
        out_shape=(jax.ShapeDtypeStruct((n, D_MODEL), F32), jax.ShapeDtypeStruct((n, D_MODEL), BF16)),
        grid=(n // WIDE_TILE,),
        in_specs=[half, half, half, half, half, pl.BlockSpec((WIDE_TILE, MIX_HALF), lambda i: (i, 1)),
                  full, row, pl.BlockSpec((D_MODEL, D_MODEL), lambda i: (0, 0)), row, row],
        out_specs=(full, full),
        compiler_params=_params(1),
        name="mix_out",
    )(o1f, o1b, o2f, o2b, gates, gates, x2d, norm, w_out, ln_g, ln_b)


def _route(h, rw_ref, rb_ref, idx_ref, gate_ref):
    rw = rw_ref[...]
    rw_hi = rw.astype(BF16)
    rw_r = rw - rw_hi.astype(F32)
    rw_mid = rw_r.astype(BF16)
    rw3 = jnp.concatenate([rw_hi, rw_mid, (rw_r - rw_mid.astype(F32)).astype(BF16)], axis=0)
    h_hi = h.astype(BF16)
    h_lo = (h - h_hi.astype(F32)).astype(BF16)
    terms = _nt(rw3, h_hi) + _nt(rw3, h_lo)
    logits = terms[:N_EXPERTS] + terms[N_EXPERTS:2 * N_EXPERTS] + terms[2 * N_EXPERTS:]
    rows = [logits[e:e + 1, :] for e in range(N_EXPERTS)]
    mx = functools.reduce(jnp.maximum, rows)
    ex = [jnp.exp(r - mx) for r in rows]
    tot = functools.reduce(lambda a, b: a + b, ex)
    score = [e / tot for e in ex]
    biased = [score[e] + rb_ref[e:e + 1, :] for e in range(N_EXPERTS)]

    def argmax_first(vals):
        best, idx = vals[0], jnp.zeros(vals[0].shape, jnp.int32)
        for j in range(1, len(vals)):
            upd = vals[j] > best
            best = jnp.where(upd, vals[j], best)
            idx = jnp.where(upd, j, idx)
        return best, idx

    def pick(vals, idx):
        out = vals[0]
        for j in range(1, len(vals)):
            out = jnp.where(idx == j, vals[j], out)
        return out

    group_scores = []
    for g in range(N_GROUPS):
        a = biased[g * EXPERTS_PER_GROUP:(g + 1) * EXPERTS_PER_GROUP]
        pairs = [a[i] + a[j] for i in range(EXPERTS_PER_GROUP) for j in range(i + 1, EXPERTS_PER_GROUP)]
        group_scores.append(functools.reduce(jnp.maximum, pairs))
    _, g_sel = argmax_first(group_scores)
    in_b = [pick([biased[g * EXPERTS_PER_GROUP + k] for g in range(N_GROUPS)], g_sel) for k in range(EXPERTS_PER_GROUP)]
    in_s = [pick([score[g * EXPERTS_PER_GROUP + k] for g in range(N_GROUPS)], g_sel) for k in range(EXPERTS_PER_GROUP)]
    _, i0 = argmax_first(in_b)
    _, i1 = argmax_first([jnp.where(i0 == k, -jnp.inf, in_b[k]) for k in range(EXPERTS_PER_GROUP)])
    s0, s1 = pick(in_s, i0), pick(in_s, i1)
    den = s0 + s1
    idx_ref[0:1, :] = g_sel * EXPERTS_PER_GROUP + i0
    idx_ref[1:2, :] = g_sel * EXPERTS_PER_GROUP + i1
    gate_ref[0:1, :] = s0 / den
    gate_ref[1:2, :] = s1 / den


def _router_kernel(h_ref, rw_ref, rb_ref, idx_ref, gate_ref):
    _route(h_ref[...], rw_ref, rb_ref, idx_ref, gate_ref)


def _router(h2d, rw_t, rb_col):
    n = h2d.shape[0]
    tile = min(ROUTER_TILE, n)
    routed = pl.BlockSpec((TOP_K, tile), lambda i: (0, i))
    return pl.pallas_call(
        _router_kernel,
        out_shape=(jax.ShapeDtypeStruct((TOP_K, n), jnp.int32), jax.ShapeDtypeStruct((TOP_K, n), F32)),
        grid=(n // tile,),
        in_specs=[pl.BlockSpec((tile, D_MODEL), lambda i: (i, 0)),
                  pl.BlockSpec(rw_t.shape, lambda i: (0, 0)),
                  pl.BlockSpec((N_EXPERTS, 1), lambda i: (0, 0))],
        out_specs=(routed, routed),
        compiler_params=_params(1),
        name="router",
    )(h2d, rw_t, rb_col)


def _plan_kernel(idx_ref, dest_ref, cnt_ref, su_ref, run_ref, start_ref):
    phase, i = pl.program_id(0), pl.program_id(1)
    tile = idx_ref.shape[1]
    e_iota = lax.broadcasted_iota(jnp.int32, (N_EXPERTS, tile), 0)
    oh0 = idx_ref[0:1, :] == e_iota
    oh1 = idx_ref[1:2, :] == e_iota
    member = jnp.where(oh0 | oh1, 1.0, 0.0)
    tile_count = jnp.broadcast_to(jnp.sum(member, axis=1, keepdims=True), (N_EXPERTS, LANES))

    @pl.when((phase == 0) & (i == 0))
    def _():
        run_ref[...] = jnp.zeros_like(run_ref)
        r = lax.broadcasted_iota(jnp.int32, (tile, tile), 0)
        c = lax.broadcasted_iota(jnp.int32, (tile, tile), 1)
        su_ref[...] = jnp.where(r < c, 1.0, 0.0).astype(BF16)

    @pl.when((phase == 1) & (i == 0))
    def _():
        counts = run_ref[...]
        cnt_ref[...] = counts.astype(jnp.int32)
        padded = jnp.ceil(counts * (1.0 / EXPERT_ROWS)) * EXPERT_ROWS
        r = lax.broadcasted_iota(jnp.int32, (N_EXPERTS, N_EXPERTS), 0)
        c = lax.broadcasted_iota(jnp.int32, (N_EXPERTS, N_EXPERTS), 1)
        start_ref[...] = jnp.dot(jnp.where(c < r, 1.0, 0.0), padded, precision=HIGHEST, preferred_element_type=F32)
        run_ref[...] = jnp.zeros_like(run_ref)

    @pl.when(phase == 1)
    def _():
        before = _mm(member.astype(BF16), su_ref[...])
        pos = start_ref[:, 0:1] + run_ref[:, 0:1] + before
        dest_ref[0:1, :] = jnp.sum(jnp.where(oh0, pos, 0.0), axis=0, keepdims=True).astype(jnp.int32)
        dest_ref[1:2, :] = jnp.sum(jnp.where(oh1, pos, 0.0), axis=0, keepdims=True).astype(jnp.int32)

    run_ref[...] += tile_count


def _plan(idx):
    n = idx.shape[1]
    tile = min(PLAN_TILE, n)
    return pl.pallas_call(
        _plan_kernel,
        out_shape=(jax.ShapeDtypeStruct((TOP_K, n), jnp.int32), jax.ShapeDtypeStruct((N_EXPERTS, LANES), jnp.int32)),
        grid=(2, n // tile),
        in_specs=[pl.BlockSpec((TOP_K, tile), lambda p, i: (0, i))],
        out_specs=(pl.BlockSpec((TOP_K, tile), lambda p, i: (0, i * p)),
                   pl.BlockSpec((N_EXPERTS, LANES), lambda p, i: (0, 0))),
        scratch_shapes=[pltpu.VMEM((tile, tile), BF16), pltpu.VMEM((N_EXPERTS, LANES), F32),
                        pltpu.VMEM((N_EXPERTS, LANES), F32)],
        compiler_params=_params(2),
        name="moe_plan",
    )(idx)


def _invert_kernel(pad_ref, dest_ref, tok_ref):
    n = dest_ref.shape[0] // TOP_K
    n_rows = tok_ref.shape[0]

    def filler(r):
        return r - jnp.where(r >= 2 * n, 2 * n, jnp.where(r >= n, n, 0))

    for e in range(N_EXPERTS):
        def clear(i, carry, lo=pad_ref[0, e], hi=pad_ref[1, e]):
            r = jnp.minimum(jnp.maximum(hi - 1 - i, lo), n_rows - 1)
            tok_ref[r] = filler(r)
            return carry

        lax.fori_loop(0, EXPERT_ROWS, clear, 0, unroll=16)

    def clear_tail(r, carry):
        tok_ref[r] = filler(r)
        return carry

    lax.fori_loop(pad_ref[0, N_EXPERTS], pad_ref[1, N_EXPERTS], clear_tail, 0)

    def place(t, carry):
        for k in range(TOP_K):
            tok_ref[dest_ref[k * n + t]] = t
        return carry

    lax.fori_loop(0, n, place, 0, unroll=16)


def _invert(pad_ranges, dest_flat, n_rows):
    smem = pl.BlockSpec(memory_space=pltpu.SMEM)
    return pl.pallas_call(
        _invert_kernel,
        out_shape=jax.ShapeDtypeStruct((n_rows,), jnp.int32),
        in_specs=[smem, smem],
        out_specs=smem,
        name="moe_invert",
    )(pad_ranges, dest_flat)


def _expert_kernel(be_ref, nv_ref, x_ref, w1_ref, w3_ref, w2_ref, y_ref, wb_ref):
    i = pl.program_id(0)
    valid = i < nv_ref[0]
    new_expert = (i == 0) | (be_ref[i] != be_ref[jnp.maximum(i - 1, 0)])

    @pl.when(valid & new_expert)
    def _():
        wb_ref[0] = w1_ref[...].astype(BF16)
        wb_ref[1] = w3_ref[...].astype(BF16)
        wb_ref[2] = w2_ref[...].astype(BF16)

    @pl.when(valid)
    def _():
        x = x_ref[...]
        y = None
        for c in range(EXPERT_HIDDEN_SPLIT):
            cols = slice(c * D_MODEL // EXPERT_HIDDEN_SPLIT, (c + 1) * D_MODEL // EXPERT_HIDDEN_SPLIT)
            hid = _silu(_mm(x, wb_ref[0, :, cols])) * _mm(x, wb_ref[1, :, cols])
            part = _mm(hid.astype(BF16), wb_ref[2, cols, :])
            y = part if y is None else y + part
        y_ref[...] = y.astype(BF16)

    @pl.when(jnp.logical_not(valid))
    def _():
        y_ref[...] = jnp.zeros_like(y_ref)


def _experts(block_expert, n_valid, xs, w1, w3, w2, layer):
    n_rows = xs.shape[0]
    wspec = pl.BlockSpec((None, None, D_MODEL, D_MODEL), lambda i, be, nv: (layer, be[i], 0, 0))
    return pl.pallas_call(
        _expert_kernel,
        out_shape=jax.ShapeDtypeStruct((n_rows, D_MODEL), BF16),
        grid_spec=pltpu.PrefetchScalarGridSpec(
            num_scalar_prefetch=2,
            grid=(n_rows // EXPERT_ROWS,),
            in_specs=[pl.BlockSpec((EXPERT_ROWS, D_MODEL), lambda i, be, nv: (i, 0)), wspec, wspec, wspec],
            out_specs=pl.BlockSpec((EXPERT_ROWS, D_MODEL), lambda i, be, nv: (i, 0)),
            scratch_shapes=[pltpu.VMEM((3, D_MODEL, D_MODEL), BF16)]),
        compiler_params=_params(1),
        name="experts",
    )(block_expert, n_valid, xs, w1, w3, w2)


def _combine_kernel(h_ref, y0_ref, y1_ref, g_ref, lng, lnb, o_ref):
    g = g_ref[...]
    z = ALPHA * h_ref[...] + (g[:, 0:1] * y0_ref[...].astype(F32) + g[:, 1:2] * y1_ref[...].astype(F32))
    o_ref[...] = _layer_norm(z, lng[...], lnb[...])


def _combine(h2d, y_rows, gates_t, ln_g, ln_b):
    n = h2d.shape[0]
    tile = min(COMBINE_TILE, n)
    nt = n // tile
    full = pl.BlockSpec((tile, D_MODEL), lambda i: (i, 0))
    row = pl.BlockSpec((1, D_MODEL), lambda i: (0, 0))
    return pl.pallas_call(
        _combine_kernel,
        out_shape=jax.ShapeDtypeStruct((n, D_MODEL), F32),
        grid=(nt,),
        in_specs=[full, full, pl.BlockSpec((tile, D_MODEL), lambda i: (nt + i, 0)),
                  pl.BlockSpec((tile, TOP_K), lambda i: (i, 0)), row, row],
        out_specs=full,
        compiler_params=_params(1),
        name="moe_combine",
    )(h2d, y_rows, y_rows, gates_t, ln_g, ln_b)


def _moe(h2d, h_bf16, idx, gates, w1, w3, w2, layer, ln_g, ln_b):
    n = h2d.shape[0]
    dest, counts = _plan(idx)
    n_rows = n * TOP_K + N_EXPERTS * EXPERT_ROWS
    n_blocks = n_rows // EXPERT_ROWS
    counts = counts[:, 0]
    padded = (counts + EXPERT_ROWS - 1) // EXPERT_ROWS * EXPERT_ROWS
    pad_end = jnp.cumsum(padded)
    block_start = jnp.arange(n_blocks, dtype=jnp.int32) * EXPERT_ROWS
    block_expert = jnp.minimum(jnp.sum((pad_end[None, :] <= block_start[:, None]).astype(jnp.int32), axis=1),
                               N_EXPERTS - 1)
    n_valid = (pad_end[-1:] // EXPERT_ROWS).astype(jnp.int32)
    pad_ranges = jnp.stack([jnp.concatenate([pad_end - padded + counts, pad_end[-1:]]),
                            jnp.concatenate([pad_end, jnp.full((1,), n_rows, jnp.int32)])]).astype(jnp.int32)
    dest_flat = dest.reshape(-1)
    xs = h_bf16[_invert(pad_ranges, dest_flat, n_rows)]
    y = _experts(block_expert, n_valid, xs, w1, w3, w2, layer)
    return _combine(h2d, y[dest_flat], gates.T, ln_g, ln_b)


def _pad_cols(w, b, main, width):
    k = w.shape[0]
    tail = w.shape[1] - main
    w_p = jnp.concatenate([w[:, :main], w[:, main:], jnp.zeros((k, width - main - tail), w.dtype)], axis=1)
    b_p = jnp.concatenate([b[:main], b[main:], jnp.zeros((width - main - tail,), b.dtype)])
    return w_p.astype(BF16), b_p[None, :].astype(F32)


def kernel(x, ab_w_in, ab_b_in, hgrn_lb, gla_gk_up, gla_gk_b, hgrn_norm, gla_norm, ab_w_out, cd_w_in, cd_b_in,
           mlstm_conv_w, mlstm_conv_b, mlstm_fgate_b, ret_norm, mlstm_norm, cd_w_out, ln_mix_g, ln_mix_b, ln_ffn_g,
           ln_ffn_b, router_w, router_b, moe_w1, moe_w3, moe_w2):
    bsz, t, d = x.shape
    n = bsz * t
    assert d == D_MODEL and t % SCAN_ROWS == 0 and t % min(CONV_TILE, t) == 0, (bsz, t, d)
    assert n % ROW_TILE == 0 and n % WIDE_TILE == 0 and n % min(ROUTER_TILE, n) == 0 and n % min(PLAN_TILE, n) == 0
    assert n % min(COMBINE_TILE, n) == 0
    lower_bounds = jnp.cumsum(jax.nn.softmax(hgrn_lb.astype(F32), axis=0), axis=0)
    rw_t = router_w.T.astype(F32)
    rb_col = router_b.astype(F32)[:, None]
    h = x.reshape(n, d)
    for layer in range(DEPTH):
        j = layer // 2
        if layer % 2 == 0:
            w_p, b_p = _pad_cols(ab_w_in[j], ab_b_in[j], 8 * 512, AB_WIDTH)
            lb = lower_bounds[layer][None, :]
            p, out_gates = _in_proj(h, w_p, b_p, _proj_ab_kernel, [lb], [pl.BlockSpec(lb.shape, lambda i: (0, 0))])
            p3 = p.reshape(bsz, t, AB_WIDTH)
            nk = N_HEADS * NARROW_DK
            u_pad = jnp.zeros((LANES, 2 * nk), F32)
            u_pad = u_pad.at[:GLA_LOWRANK, :nk].set(gla_gk_up[j, 0])
            u_pad = u_pad.at[GLA_LOWRANK:2 * GLA_LOWRANK, nk:].set(gla_gk_up[j, 1]).astype(BF16)
            gkb = jnp.concatenate([gla_gk_b[j, 0], gla_gk_b[j, 1]])[None, :].astype(F32)
            oa_f, ob_f, oa_b, ob_b = _scan_ab(p3, u_pad, gkb)
            norm = jnp.concatenate([hgrn_norm[j], gla_norm[j]])[None, :].astype(F32)
            outs = [a.reshape(n, MIX_HALF) for a in (oa_f, oa_b, ob_f, ob_b)]
            h, h_bf16 = _mix_out(*outs, out_gates, h, norm, ab_w_out[j].astype(BF16),
                                 ln_mix_g[layer][None, :], ln_mix_b[layer][None, :])
        else:
            lane_dir, lane_head, _, _ = _gate_lane_layout()
            col_i = 7 * 512 + lane_dir * 2 * N_HEADS + lane_head
            cols = np.concatenate([np.arange(7 * 512), col_i, col_i + N_HEADS])
            w_p, b_p = cd_w_in[j][:, cols].astype(BF16), cd_b_in[j][cols][None, :].astype(F32)
            pos = pl.BlockSpec((ROW_TILE, LANES), lambda i: (i % (t // ROW_TILE), 0))
            p, out_gates = _in_proj(h, w_p, b_p, _proj_cd_kernel, list(_rotary_tables(t)), [pos, pos])
            p3 = p.reshape(bsz, t, CD_WIDTH)
            oc_f, oc_b = _scan_c(p3)
            qd, kd = _conv_silu(p3, mlstm_conv_w[j].astype(F32), mlstm_conv_b[j].astype(F32))
            fgb_row = mlstm_fgate_b[j].astype(F32)[lane_dir, lane_head][None, :]
            od_f, od_b = _scan_d(qd, kd, p3, fgb_row)
            norm = jnp.concatenate([ret_norm[j], mlstm_norm[j]])[None, :].astype(F32)
            outs = [a.reshape(n, MIX_HALF) for a in (oc_f, oc_b, od_f, od_b)]
            h, h_bf16 = _mix_out(*outs, out_gates, h, norm, cd_w_out[j].astype(BF16),
                                 ln_mix_g[layer][None, :], ln_mix_b[layer][None, :])
        idx, gates = _router(h, rw_t, rb_col)
        h = _moe(h, h_bf16, idx, gates, moe_w1, moe_w3, moe_w2, layer,
                 ln_ffn_g[layer][None, :], ln_ffn_b[layer][None, :])
    return h.reshape(bsz, t, d)
```
